```python
import math
import jax, jax.numpy as jnp
from jax import lax
import numpy as np

D_MODEL = 1024
BATCH = 2
SEQ = 16384
DEPTH = 2

N_A_LAYERS = DEPTH // 2
N_B_LAYERS = DEPTH - N_A_LAYERS
N_DENSE = (DEPTH + 1) // 2
N_MOE = DEPTH // 2

D_RNN = D_MODEL
N_LRU_BLOCKS = 8
LRU_BW = D_RNN // N_LRU_BLOCKS
CONV_W = 4
LRU_C = 8.0

N_HEADS = 16
HEAD_DIM = 64
D_ATTN = N_HEADS * HEAD_DIM
Q_BLOCK = 128

D_FF = 2816
N_EXPERTS = 8
TOP_K = 2
D_EXP = 3584

EPS = 1e-6

kernel_name = "yoco_rglru_fox_moe_hybrid"


def _rmsnorm(x, g):
    x32 = x.astype(jnp.float32)
    y = x32 * lax.rsqrt(jnp.mean(x32 * x32, axis=-1, keepdims=True) + EPS)
    return (y * g.astype(jnp.float32)).astype(x.dtype)


def _swiglu(xn, w_gu, w_down):
    gate, up = jnp.split(xn @ w_gu, 2, axis=-1)
    return (jax.nn.silu(gate) * up) @ w_down


def _lru_combine(left, right):
    a1, b1 = left
    a2, b2 = right
    return a1 * a2, a2 * b1 + b2


def _rg_lru_block(xn, w_in, conv_w, conv_b, gate_w, gate_b, lam, w_out):
    B, S, _ = xn.shape
    xb, gb = jnp.split(xn @ w_in, 2, axis=-1)
    xc = lax.conv_general_dilated(
        xb, conv_w[:, None, :], window_strides=(1,), padding=[(CONV_W - 1, 0)],
        dimension_numbers=('NWC', 'WIO', 'NWC'), feature_group_count=D_RNN) + conv_b
    xh = xc.reshape(B, S, N_LRU_BLOCKS, LRU_BW)
    gates = jnp.einsum('bsnd,gnde->gbsne', xh, gate_w).reshape(2, B, S, D_RNN) + gate_b[:, None, None, :]
    gates = gates.astype(jnp.float32)
    r = jax.nn.sigmoid(gates[0])
    i = jax.nn.sigmoid(gates[1])
    log_a = LRU_C * r * jax.nn.log_sigmoid(lam.astype(jnp.float32))
    a = jnp.exp(log_a)
    b = jnp.sqrt(-jnp.expm1(2.0 * log_a)) * (i * xc.astype(jnp.float32))
    _, h = lax.associative_scan(_lru_combine, (a, b), axis=1)
    y = jax.nn.gelu(gb) * h.astype(xn.dtype)
    return y @ w_out


def _shared_kv(x, kv_norm_g, w_kvf, b_f):
    B, S, _ = x.shape
    kvf = _rmsnorm(x, kv_norm_g) @ w_kvf
    k = kvf[..., :D_ATTN].reshape(B, S, N_HEADS, HEAD_DIM).transpose(0, 2, 1, 3)
    v = kvf[..., D_ATTN:2 * D_ATTN].reshape(B, S, N_HEADS, HEAD_DIM).transpose(0, 2, 1, 3)
    log_f = jax.nn.log_sigmoid((kvf[..., 2 * D_ATTN:] + b_f).astype(jnp.float32))
    cf = jnp.cumsum(log_f, axis=1).transpose(0, 2, 1)
    return k, v, cf


def _fox_attention(xn, w_qg, w_o, k, v, cf):
    B, S, _ = xn.shape
    q, g = jnp.split(xn @ w_qg, 2, axis=-1)
    q = q.reshape(B, S, N_HEADS, HEAD_DIM).transpose(0, 2, 1, 3) * (HEAD_DIM ** -0.5)
    k_pos = jnp.arange(S)

    def _block(blk):
        start = blk * Q_BLOCK
        qb = lax.dynamic_slice_in_dim(q, start, Q_BLOCK, axis=2)
        cq = lax.dynamic_slice_in_dim(cf, start, Q_BLOCK, axis=2)
        s = jnp.einsum('bhqd,bhkd->bhqk', qb, k).astype(jnp.float32)
        s = s + cq[..., :, None] - cf[..., None, :]
        q_pos = start + jnp.arange(Q_BLOCK)
        s = jnp.where(k_pos[None, :] <= q_pos[:, None], s, -jnp.inf)
        p = jax.nn.softmax(s, axis=-1)
        return jnp.einsum('bhqk,bhkd->bhqd', p.astype(v.dtype), v)

    o = lax.map(_block, jnp.arange(S // Q_BLOCK))
    o = o.transpose(1, 0, 3, 2, 4).reshape(B, S, D_ATTN)
    o = o * jax.nn.sigmoid(g)
    return o @ w_o


def _moe(xn, router, w_gu, w_down):
    logits = (xn @ router).astype(jnp.float32)
    top_v, top_i = lax.top_k(logits, TOP_K)
    w = jax.nn.softmax(top_v, axis=-1)
    gates = jnp.sum(w[..., None] * jax.nn.one_hot(top_i, N_EXPERTS, dtype=jnp.float32), axis=-2)
    y = jnp.zeros_like(xn)
    for e in range(N_EXPERTS):
        y = y + gates[..., e:e + 1].astype(xn.dtype) * _swiglu(xn, w_gu[e], w_down[e])
    return y


def setup_inputs(seed: int = 0) -> dict:
    key = jax.random.key(seed)
    ks = jax.random.split(key, 24)
    f32 = jnp.float32

    def nrm(k, shape, fan_in):
        return jax.random.normal(k, shape, f32) * (fan_in ** -0.5)

    x = jax.random.normal(ks[0], (BATCH, SEQ, D_MODEL), f32)
    norm_g = 1.0 + 0.02 * jax.random.normal(ks[1], (DEPTH, 2, D_MODEL), f32)
    lru_w_in = nrm(ks[2], (N_A_LAYERS, D_MODEL, 2 * D_RNN), D_MODEL)
    lru_conv_w = nrm(ks[3], (N_A_LAYERS, CONV_W, D_RNN), CONV_W)
    lru_conv_b = 0.02 * jax.random.normal(ks[4], (N_A_LAYERS, D_RNN), f32)
    lru_gate_w = nrm(ks[5], (N_A_LAYERS, 2, N_LRU_BLOCKS, LRU_BW, LRU_BW), LRU_BW)
    lru_gate_b = 0.1 * jax.random.normal(ks[6], (N_A_LAYERS, 2, D_RNN), f32)
    u = jax.random.uniform(ks[7], (N_A_LAYERS, D_RNN), f32, minval=0.9, maxval=0.999)
    s = u ** (1.0 / LRU_C)
    lru_lambda = jnp.log(s) - jnp.log1p(-s)
    lru_w_out = nrm(ks[8], (N_A_LAYERS, D_RNN, D_MODEL), D_RNN)
    kv_norm_g = 1.0 + 0.02 * jax.random.normal(ks[9], (D_MODEL,), f32)
    w_kvf = nrm(ks[10], (D_MODEL, 2 * D_ATTN + N_HEADS), D_MODEL)
    b_f = 3.0 + 0.5 * jax.random.normal(ks[11], (N_HEADS,), f32)
    fox_w_qg = nrm(ks[12], (N_B_LAYERS, D_MODEL, 2 * D_ATTN), D_MODEL)
    fox_w_o = nrm(ks[13], (N_B_LAYERS, D_ATTN, D_MODEL), D_ATTN)
    mlp_w_gu = nrm(ks[14], (N_DENSE, D_MODEL, 2 * D_FF), D_MODEL)
    mlp_w_down = nrm(ks[15], (N_DENSE, D_FF, D_MODEL), D_FF)
    moe_router = nrm(ks[16], (N_MOE, D_MODEL, N_EXPERTS), D_MODEL)
    moe_w_gu = nrm(ks[17], (N_MOE, N_EXPERTS, D_MODEL, 2 * D_EXP), D_MODEL)
    moe_w_down = nrm(ks[18], (N_MOE, N_EXPERTS, D_EXP, D_MODEL), D_EXP)
    final_norm_g = 1.0 + 0.02 * jax.random.normal(ks[19], (D_MODEL,), f32)
    return {"x": x, "norm_g": norm_g, "lru_w_in": lru_w_in, "lru_conv_w": lru_conv_w,
            "lru_conv_b": lru_conv_b, "lru_gate_w": lru_gate_w, "lru_gate_b": lru_gate_b,
            "lru_lambda": lru_lambda, "lru_w_out": lru_w_out, "kv_norm_g": kv_norm_g,
            "w_kvf": w_kvf, "b_f": b_f, "fox_w_qg": fox_w_qg, "fox_w_o": fox_w_o,
            "mlp_w_gu": mlp_w_gu, "mlp_w_down": mlp_w_down, "moe_router": moe_router,
            "moe_w_gu": moe_w_gu, "moe_w_down": moe_w_down, "final_norm_g": final_norm_g}


def reference(x, norm_g, lru_w_in, lru_conv_w, lru_conv_b, lru_gate_w, lru_gate_b, lru_lambda,
              lru_w_out, kv_norm_g, w_kvf, b_f, fox_w_qg, fox_w_o, mlp_w_gu, mlp_w_down,
              moe_router, moe_w_gu, moe_w_down, final_norm_g):
    k = v = cf = None
    for l in range(DEPTH):
        xn = _rmsnorm(x, norm_g[l, 0])
        if l < N_A_LAYERS:
            a = l
            x = x + _rg_lru_block(xn, lru_w_in[a], lru_conv_w[a], lru_conv_b[a], lru_gate_w[a],
                                  lru_gate_b[a], lru_lambda[a], lru_w_out[a])
        else:
            if l == N_A_LAYERS:
                k, v, cf = _shared_kv(_rmsnorm(x, 1.0) if False else x, kv_norm_g, w_kvf, b_f) if False else _shared_kv(x, kv_norm_g, w_kvf, b_f)
                xn = _rmsnorm(x, norm_g[l, 0])
            bi = l - N_A_LAYERS
            x = x + _fox_attention(xn, fox_w_qg[bi], fox_w_o[bi], k, v, cf)
        hn = _rmsnorm(x, norm_g[l, 1])
        if l % 2 == 0:
            x = x + _swiglu(hn, mlp_w_gu[l // 2], mlp_w_down[l // 2])
        else:
            x = x + _moe(hn, moe_router[l // 2], moe_w_gu[l // 2], moe_w_down[l // 2])
    return _rmsnorm(x, final_norm_g)
```

```python
import functools

import jax
import jax.numpy as jnp
from jax import lax
from jax.experimental import pallas as pl
from jax.experimental.pallas import tpu as pltpu

EPS = 1e-6
LRU_C = 8.0
N_LRU_BLOCKS = 8
CONV_W = 4
N_HEADS = 16
HEAD_DIM = 64
N_EXPERTS = 8
LANES = 128
SUBLANES = 8
NEG_BIG = -1e30

F32 = jnp.float32
BF16 = jnp.bfloat16
NT_DIMS = (((1,), (1,)), ((), ()))


def _rms(x, g):
    ms = jnp.mean(x * x, axis=-1, keepdims=True)
    return x * lax.rsqrt(ms + EPS) * g


def _log_sigmoid(x):
    return jnp.minimum(x, 0.0) - jnp.log1p(jnp.exp(-jnp.abs(x)))


def _gelu_tanh(x):
    c = 0.7978845608028654
    return 0.5 * x * (1.0 + jnp.tanh(c * (x + 0.044715 * (x * x * x))))


def _silu(x):
    return x * jax.nn.sigmoid(x)


def _lru_kernel(x_ref, g_ref, win_ref, cw_ref, cb_ref, gw_ref, gb_ref, lam_ref, wout_ref,
                o_ref, xbuf, hcar):
    t = pl.program_id(1)
    tt = x_ref.shape[1]
    d = x_ref.shape[2]
    bw = d // N_LRU_BLOCKS

    @pl.when(t == 0)
    def _():
        xbuf[0:SUBLANES, :] = jnp.zeros((SUBLANES, d), F32)
        hcar[...] = jnp.zeros_like(hcar)

    x = x_ref[0]
    xn = _rms(x, g_ref[...]).astype(BF16)
    xg = jnp.dot(xn, win_ref[...], preferred_element_type=F32)
    xb = xg[:, :d]
    gbr = xg[:, d:]

    xbuf[SUBLANES:SUBLANES + tt, :] = xb
    cw = cw_ref[...]
    xc = cb_ref[...] + cw[CONV_W - 1:CONV_W, :] * xb
    for j in range(CONV_W - 1):
        xc = xc + cw[j:j + 1, :] * xbuf[pl.ds(SUBLANES - (CONV_W - 1) + j, tt), :]
    xbuf[0:SUBLANES, :] = xbuf[tt:tt + SUBLANES, :]

    xcb = xc.astype(BF16)
    rl, il = [], []
    for n in range(N_LRU_BLOCKS):
        gl = jnp.dot(xcb[:, n * bw:(n + 1) * bw], gw_ref[n], preferred_element_type=F32)
        rl.append(gl[:, :bw])
        il.append(gl[:, bw:])
    gb = gb_ref[...]
    r = jax.nn.sigmoid(jnp.concatenate(rl, axis=1) + gb[0:1, :])
    i = jax.nn.sigmoid(jnp.concatenate(il, axis=1) + gb[1:2, :])
    log_a = LRU_C * r * _log_sigmoid(lam_ref[...])
    a = jnp.exp(log_a)
    th = jnp.tanh(log_a)
    b = jnp.sqrt(-2.0 * th / (1.0 - th)) * (i * xc)

    row = lax.broadcasted_iota(jnp.int32, (tt, 1), 0)
    sh = 1
    while sh < tt:
        keep = row >= sh
        a_sh = jnp.where(keep, pltpu.roll(a, sh, 0), 1.0)
        b_sh = jnp.where(keep, pltpu.roll(b, sh, 0), 0.0)
        b = a * b_sh + b
        a = a * a_sh
        sh *= 2
    h = b + a * hcar[...]
    hcar[...] = h[tt - 1:tt, :]

    y = (_gelu_tanh(gbr) * h).astype(BF16)
    o_ref[0] = x + jnp.dot(y, wout_ref[...], preferred_element_type=F32)


def _lru_layer(x, g, w_in, conv_w, conv_b, gate_w, gate_b, lam, w_out, *, tt):
    bsz, s, d = x.shape
    bw = d // N_LRU_BLOCKS
    gw = jnp.concatenate([gate_w[0], gate_w[1]], axis=-1).astype(BF16)
    const = lambda *shape: pl.BlockSpec(shape, lambda b, t: (0,) * len(shape))
    return pl.pallas_call(
        _lru_kernel,
        grid=(bsz, s // tt),
        in_specs=[
            pl.BlockSpec((1, tt, d), lambda b, t: (b, t, 0)),
            const(1, d), const(d, 2 * d), const(CONV_W, d), const(1, d),
            const(N_LRU_BLOCKS, bw, 2 * bw), const(2, d), const(1, d), const(d, d),
        ],
        out_specs=pl.BlockSpec((1, tt, d), lambda b, t: (b, t, 0)),
        out_shape=jax.ShapeDtypeStruct((bsz, s, d), F32),
        scratch_shapes=[pltpu.VMEM((tt + 2 * SUBLANES, d), F32), pltpu.VMEM((1, d), F32)],
        compiler_params=pltpu.CompilerParams(dimension_semantics=("parallel", "arbitrary")),
        name="lru_layer",
    )(x, g.reshape(1, d), w_in.astype(BF16), conv_w, conv_b.reshape(1, d), gw, gate_b,
      lam.reshape(1, d), w_out.astype(BF16))


def _mlp_kernel(x_ref, g_ref, wg_ref, wu_ref, wd_ref, o_ref, hn_ref, acc_ref):
    k = pl.program_id(1)

    @pl.when(k == 0)
    def _():
        hn_ref[...] = _rms(x_ref[...], g_ref[...]).astype(BF16)
        acc_ref[...] = jnp.zeros_like(acc_ref)

    hn = hn_ref[...]
    gate = jnp.dot(hn, wg_ref[...], preferred_element_type=F32)
    up = jnp.dot(hn, wu_ref[...], preferred_element_type=F32)
    h = (_silu(gate) * up).astype(BF16)
    acc_ref[...] += jnp.dot(h, wd_ref[...], preferred_element_type=F32)

    @pl.when(k == pl.num_programs(1) - 1)
    def _():
        o_ref[...] = x_ref[...] + acc_ref[...]


def _mlp(x2d, g, w_gu, w_down, *, tm, tf):
    t, d = x2d.shape
    dff = w_down.shape[0]
    nk = dff // tf
    w_gu = w_gu.astype(BF16)
    return pl.pallas_call(
        _mlp_kernel,
        grid=(t // tm, nk),
        in_specs=[
            pl.BlockSpec((tm, d), lambda i, k: (i, 0)),
            pl.BlockSpec((1, d), lambda i, k: (0, 0)),
            pl.BlockSpec((d, tf), lambda i, k: (0, k)),
            pl.BlockSpec((d, tf), lambda i, k: (0, nk + k)),
            pl.BlockSpec((tf, d), lambda i, k: (k, 0)),
        ],
        out_specs=pl.BlockSpec((tm, d), lambda i, k: (i, 0)),
        out_shape=jax.ShapeDtypeStruct((t, d), F32),
        scratch_shapes=[pltpu.VMEM((tm, d), BF16), pltpu.VMEM((tm, d), F32)],
        compiler_params=pltpu.CompilerParams(dimension_semantics=("parallel", "arbitrary")),
        name="mlp",
    )(x2d, g.reshape(1, d), w_gu, w_gu, w_down.astype(BF16))


def _split3(x):
    p0 = x.astype(BF16)
    r1 = x - p0.astype(F32)
    p1 = r1.astype(BF16)
    p2 = (r1 - p1.astype(F32)).astype(BF16)
    return p0, p1, p2


def _kvf_kernel(x_ref, g_ref, wkvT_ref, wfT_ref, wf_ref, bfc_ref, bfr_ref,
                kT_ref, vT_ref, cfr_ref, cfc_ref, car_r, car_c):
    t = pl.program_id(1)
    tk = x_ref.shape[1]
    da = kT_ref.shape[2]

    @pl.when(t == 0)
    def _():
        car_r[...] = jnp.zeros_like(car_r)
        car_c[...] = jnp.zeros_like(car_c)

    xn = _rms(x_ref[0], g_ref[...]).astype(BF16)
    kvT = lax.dot_general(wkvT_ref[...], xn, NT_DIMS, preferred_element_type=F32)
    kT_ref[0, 0] = kvT[:da].astype(BF16)
    vT_ref[0, 0] = kvT[da:].astype(BF16)

    ri = lax.broadcasted_iota(jnp.int32, (tk, tk), 0)
    ci = lax.broadcasted_iota(jnp.int32, (tk, tk), 1)
    upper = (ri <= ci).astype(BF16)
    lower = (ci <= ri).astype(BF16)

    lf_r = _log_sigmoid(lax.dot_general(wfT_ref[...], xn, NT_DIMS, preferred_element_type=F32) + bfc_ref[...])
    cs_r = car_r[...]
    for p in _split3(lf_r):
        cs_r = cs_r + jnp.dot(p, upper, preferred_element_type=F32)
    cfr_ref[0, 0] = cs_r
    car_r[...] = cs_r[:, tk - 1:tk]

    lf_c = _log_sigmoid(jnp.dot(xn, wf_ref[...], preferred_element_type=F32) + bfr_ref[...])
    cs_c = car_c[...]
    for p in _split3(lf_c):
        cs_c = cs_c + jnp.dot(lower, p, preferred_element_type=F32)
    cfc_ref[0] = cs_c
    car_c[...] = cs_c[tk - 1:tk, :]


def _kvf(x, g, w_kvf, b_f, *, tk):
    bsz, s, d = x.shape
    da = N_HEADS * HEAD_DIM
    nk = s // tk
    wkvT = w_kvf[:, :2 * da].T.astype(BF16)
    wf = w_kvf[:, 2 * da:].astype(BF16)
    wfT = wf.T
    const = lambda *shape: pl.BlockSpec(shape, lambda b, t: (0,) * len(shape))
    return pl.pallas_call(
        _kvf_kernel,
        grid=(bsz, nk),
        in_specs=[
            pl.BlockSpec((1, tk, d), lambda b, t: (b, t, 0)),
            const(1, d), const(2 * da, d), const(N_HEADS, d), const(d, N_HEADS),
            const(N_HEADS, 1), const(1, N_HEADS),
        ],
        out_specs=[
            pl.BlockSpec((1, 1, da, tk), lambda b, t: (b, t, 0, 0)),
            pl.BlockSpec((1, 1, da, tk), lambda b, t: (b, t, 0, 0)),
            pl.BlockSpec((1, 1, N_HEADS, tk), lambda b, t: (b, t, 0, 0)),
            pl.BlockSpec((1, tk, N_HEADS), lambda b, t: (b, t, 0)),
        ],
        out_shape=[
            jax.ShapeDtypeStruct((bsz, nk, da, tk), BF16),
            jax.ShapeDtypeStruct((bsz, nk, da, tk), BF16),
            jax.ShapeDtypeStruct((bsz, nk, N_HEADS, tk), F32),
            jax.ShapeDtypeStruct((bsz, s, N_HEADS), F32),
        ],
        scratch_shapes=[pltpu.VMEM((N_HEADS, 1), F32), pltpu.VMEM((1, N_HEADS), F32)],
        compiler_params=pltpu.CompilerParams(dimension_semantics=("parallel", "arbitrary")),
        name="kvf",
    )(x, g.reshape(1, d), wkvT, wfT, wf, b_f.reshape(N_HEADS, 1), b_f.reshape(1, N_HEADS))


def _qg_kernel(x_ref, g_ref, w_ref, q_ref, gate_ref):
    da = q_ref.shape[1]
    xn = _rms(x_ref[...], g_ref[...]).astype(BF16)
    qg = jnp.dot(xn, w_ref[...], preferred_element_type=F32)
    q_ref[...] = (qg[:, :da] * (HEAD_DIM ** -0.5)).astype(BF16)
    gate_ref[...] = jax.nn.sigmoid(qg[:, da:]).astype(BF16)


def _qg(x2d, g, w_qg, *, tm):
    t, d = x2d.shape
    da = N_HEADS * HEAD_DIM
    return pl.pallas_call(
        _qg_kernel,
        grid=(t // tm,),
        in_specs=[
            pl.BlockSpec((tm, d), lambda i: (i, 0)),
            pl.BlockSpec((1, d), lambda i: (0, 0)),
            pl.BlockSpec((d, 2 * da), lambda i: (0, 0)),
        ],
        out_specs=[pl.BlockSpec((tm, da), lambda i: (i, 0)), pl.BlockSpec((tm, da), lambda i: (i, 0))],
        out_shape=[jax.ShapeDtypeStruct((t, da), BF16), jax.ShapeDtypeStruct((t, da), BF16)],
        compiler_params=pltpu.CompilerParams(dimension_semantics=("parallel",)),
        name="qg",
    )(x2d, g.reshape(1, d), w_qg.astype(BF16))


def _attn_kernel(q_ref, kT_ref, vT_ref, cfr_ref, cfc_ref, gate_ref, o_ref):
    hp = pl.program_id(1)
    i = pl.program_id(2)
    tq = q_ref.shape[1]
    tk = kT_ref.shape[3]
    q2 = q_ref[0]
    head_lane = lax.broadcasted_iota(jnp.int32, (tq, N_HEADS), 1)
    cfc = cfc_ref[0]
    rq = lax.broadcasted_iota(jnp.int32, (tq, tk), 0)
    ck_pos = lax.broadcasted_iota(jnp.int32, (tq, tk), 1)

    outs = []
    for h in range(2):
        hid = 2 * hp + h
        qh = q2[:, h * HEAD_DIM:(h + 1) * HEAD_DIM]
        cq = jnp.sum(jnp.where(head_lane == hid, cfc, 0.0), axis=-1, keepdims=True)

        def block(j, carry, masked):
            m, l, acc = carry
            kT = kT_ref[0, j, h * HEAD_DIM:(h + 1) * HEAD_DIM, :]
            s = jnp.dot(qh, kT, preferred_element_type=F32)
            s = s + cq - cfr_ref[0, j, pl.ds(hid, 1), :]
            if masked:
                s = jnp.where(ck_pos <= rq, s, NEG_BIG)
            m_new = jnp.maximum(m, jnp.max(s, axis=-1, keepdims=True))
            alpha = jnp.exp(m - m_new)
            p = jnp.exp(s - m_new)
            l = alpha * l + jnp.sum(p, axis=-1, keepdims=True)
            vT = vT_ref[0, j, h * HEAD_DIM:(h + 1) * HEAD_DIM, :]
            pv = lax.dot_general(p.astype(BF16), vT, NT_DIMS, preferred_element_type=F32)
            return m_new, l, alpha * acc + pv

        init = (jnp.full((tq, 1), NEG_BIG, F32), jnp.zeros((tq, 1), F32), jnp.zeros((tq, HEAD_DIM), F32))
        carry = lax.fori_loop(0, i, functools.partial(block, masked=False), init)
        m, l, acc = block(i, carry, masked=True)
        outs.append(acc / l)
    o = jnp.concatenate(outs, axis=1) * gate_ref[0].astype(F32)
    o_ref[0] = o.astype(BF16)


def _attention(q, kT, vT, cfr, cfc, gate, *, tq):
    bsz, s, da = q.shape
    nk, tk = kT.shape[1], kT.shape[3]
    assert tq == tk
    hw = 2 * HEAD_DIM
    return pl.pallas_call(
        _attn_kernel,
        grid=(bsz, da // hw, s // tq),
        in_specs=[
            pl.BlockSpec((1, tq, hw), lambda b, hp, i: (b, i, hp)),
            pl.BlockSpec((1, nk, hw, tk), lambda b, hp, i: (b, 0, hp, 0)),
            pl.BlockSpec((1, nk, hw, tk), lambda b, hp, i: (b, 0, hp, 0)),
            pl.BlockSpec((1, nk, N_HEADS, tk), lambda b, hp, i: (b, 0, 0, 0)),
            pl.BlockSpec((1, tq, N_HEADS), lambda b, hp, i: (b, i, 0)),
            pl.BlockSpec((1, tq, hw), lambda b, hp, i: (b, i, hp)),
        ],
        out_specs=pl.BlockSpec((1, tq, hw), lambda b, hp, i: (b, i, hp)),
        out_shape=jax.ShapeDtypeStruct((bsz, s, da), BF16),
        compiler_params=pltpu.CompilerParams(dimension_semantics=("parallel", "parallel", "arbitrary")),
        name="fox_attention",
    )(q, kT, vT, cfr, cfc, gate)


def _oproj_kernel(x_ref, o_ref, w_ref, out_ref):
    out_ref[...] = x_ref[...] + jnp.dot(o_ref[...], w_ref[...], preferred_element_type=F32)


def _oproj(x2d, o2d, w_o, *, tm):
    t, d = x2d.shape
    da = o2d.shape[1]
    return pl.pallas_call(
        _oproj_kernel,
        grid=(t // tm,),
        in_specs=[
            pl.BlockSpec((tm, d), lambda i: (i, 0)),
            pl.BlockSpec((tm, da), lambda i: (i, 0)),
            pl.BlockSpec((da, d), lambda i: (0, 0)),
        ],
        out_specs=pl.BlockSpec((tm, d), lambda i: (i, 0)),
        out_shape=jax.ShapeDtypeStruct((t, d), F32),
        compiler_params=pltpu.CompilerParams(dimension_semantics=("parallel",)),
        name="attn_out_proj",
    )(x2d, o2d, w_o.astype(BF16))


def _moe_kernel(x_ref, g_ref, rt_ref, wg_ref, wu_ref, wd_ref, gf_ref, o_ref, hn_ref, gates_ref, acc_ref):
    e = pl.program_id(1)
    k = pl.program_id(2)
    tm = x_ref.shape[0]

    @pl.when((e == 0) & (k == 0))
    def _():
        hn = _rms(x_ref[...], g_ref[...])
        hn_ref[...] = hn.astype(BF16)
        acc_ref[...] = jnp.zeros_like(acc_ref)
        h0, h1, _ = _split3(hn)
        rt = rt_ref[...]
        r0, r1, _ = _split3(rt)
        logits = (jnp.dot(h0, r0, preferred_element_type=F32) + jnp.dot(h0, r1, preferred_element_type=F32)
                  + jnp.dot(h1, r0, preferred_element_type=F32))
        lane = lax.broadcasted_iota(jnp.int32, (tm, N_EXPERTS), 1)
        m1 = jnp.max(logits, axis=-1, keepdims=True)
        i1 = jnp.min(jnp.where(logits == m1, lane, N_EXPERTS), axis=-1, keepdims=True)
        first = lane == i1
        rest = jnp.where(first, -jnp.inf, logits)
        m2 = jnp.max(rest, axis=-1, keepdims=True)
        i2 = jnp.min(jnp.where(rest == m2, lane, N_EXPERTS), axis=-1, keepdims=True)
        second = lane == i2
        e2 = jnp.exp(m2 - m1)
        w1 = 1.0 / (1.0 + e2)
        w2 = e2 / (1.0 + e2)
        gates_ref[...] = jnp.where(first, w1, 0.0) + jnp.where(second, w2, 0.0)

    hn = hn_ref[...]
    gate = jnp.dot(hn, wg_ref[0], preferred_element_type=F32)
    up = jnp.dot(hn, wu_ref[0], preferred_element_type=F32)
    h = (_silu(gate) * up).astype(BF16)
    y = jnp.dot(h, wd_ref[0], preferred_element_type=F32)
    lane = lax.broadcasted_iota(jnp.int32, (tm, N_EXPERTS), 1)
    ge = jnp.sum(jnp.where(lane == e, gates_ref[...], 0.0), axis=-1, keepdims=True)
    acc_ref[...] += ge * y

    @pl.when((e == pl.num_programs(1) - 1) & (k == pl.num_programs(2) - 1))
    def _():
        o_ref[...] = _rms(x_ref[...] + acc_ref[...], gf_ref[...])


def _moe(x2d, g, router, w_gu, w_down, g_final, *, tm, tf):
    t, d = x2d.shape
    ne, dexp = w_down.shape[0], w_down.shape[1]
    nk = dexp // tf
    w_gu = w_gu.astype(BF16)
    return pl.pallas_call(
        _moe_kernel,
        grid=(t // tm, ne, nk),
        in_specs=[
            pl.BlockSpec((tm, d), lambda i, e, k: (i, 0)),
            pl.BlockSpec((1, d), lambda i, e, k: (0, 0)),
            pl.BlockSpec((d, ne), lambda i, e, k: (0, 0)),
            pl.BlockSpec((1, d, tf), lambda i, e, k: (e, 0, k)),
            pl.BlockSpec((1, d, tf), lambda i, e, k: (e, 0, nk + k)),
            pl.BlockSpec((1, tf, d), lambda i, e, k: (e, k, 0)),
            pl.BlockSpec((1, d), lambda i, e, k: (0, 0)),
        ],
        out_specs=pl.BlockSpec((tm, d), lambda i, e, k: (i, 0)),
        out_shape=jax.ShapeDtypeStruct((t, d), F32),
        scratch_shapes=[pltpu.VMEM((tm, d), BF16), pltpu.VMEM((tm, ne), F32), pltpu.VMEM((tm, d), F32)],
        compiler_params=pltpu.CompilerParams(dimension_semantics=("parallel", "arbitrary", "arbitrary")),
        name="moe",
    )(x2d, g.reshape(1, d), router, w_gu, w_gu, w_down.astype(BF16), g_final.reshape(1, d))


def _tile(n, pref):
    t = min(n, pref)
    assert n % t == 0
    return t


def kernel(x, norm_g, lru_w_in, lru_conv_w, lru_conv_b, lru_gate_w, lru_gate_b, lru_lambda, lru_w_out,
           kv_norm_g, w_kvf, b_f, fox_w_qg, fox_w_o, mlp_w_gu, mlp_w_down, moe_router, moe_w_gu,
           moe_w_down, final_norm_g):
    bsz, s, d = x.shape
    t = bsz * s
    tm = _tile(t, 512)
    ts = _tile(s, 512)

    x = _lru_layer(x, norm_g[0, 0], lru_w_in[0], lru_conv_w[0], lru_conv_b[0], lru_gate_w[0],
                   lru_gate_b[0], lru_lambda[0], lru_w_out[0], tt=_tile(s, 256))
    dff = mlp_w_down.shape[1]
    x = _mlp(x.reshape(t, d), norm_g[0, 1], mlp_w_gu[0], mlp_w_down[0], tm=tm, tf=dff // 2).reshape(bsz, s, d)

    kT, vT, cfr, cfc = _kvf(x, kv_norm_g, w_kvf, b_f, tk=ts)
    q, gate = _qg(x.reshape(t, d), norm_g[1, 0], fox_w_qg[0], tm=tm)
    o = _attention(q.reshape(bsz, s, -1), kT, vT, cfr, cfc, gate.reshape(bsz, s, -1), tq=ts)
    x2d = _oproj(x.reshape(t, d), o.reshape(t, -1), fox_w_o[0], tm=tm)

    dexp = moe_w_down.shape[2]
    out = _moe(x2d, norm_g[1, 1], moe_router[0], moe_w_gu[0], moe_w_down[0], final_norm_g, tm=tm, tf=dexp // 2)
    return out.reshape(bsz, s, d)
```

```python
import functools
import math

import jax
import jax.numpy as jnp
from jax import lax
from jax.experimental import pallas as pl
from jax.experimental.pallas import tpu as pltpu

EPS = 1e-6
LRU_C = 8.0
N_LRU_BLOCKS = 8
CONV_W = 4
N_HEADS = 16
HEAD_DIM = 64
N_EXPERTS = 8
LANES = 128
SUBLANES = 8
HEAD_GROUP = 128
BIAS_Q = HEAD_DIM
BIAS_K = HEAD_DIM + 3
V_GROUP = 80
NEG_BIG = -1e30
LOG2E = math.log2(math.e)
SKIP_GAP_LOG2 = 160.0
NORM_SLACK = 1.01

F32 = jnp.float32
BF16 = jnp.bfloat16
NT_DIMS = (((1,), (1,)), ((), ()))


def _rms(x, g):
    ms = jnp.mean(x * x, axis=-1, keepdims=True)
    return x * lax.rsqrt(ms + EPS) * g


def _log_sigmoid(x):
    return jnp.minimum(x, 0.0) - jnp.log1p(jnp.exp(-jnp.abs(x)))


def _gelu_tanh(x):
    c = 0.7978845608028654
    return 0.5 * x * (1.0 + jnp.tanh(c * (x + 0.044715 * (x * x * x))))


def _silu(x):
    return x * jax.nn.sigmoid(x)


def _split3(x):
    p0 = x.astype(BF16)
    r1 = x - p0.astype(F32)
    p1 = r1.astype(BF16)
    p2 = (r1 - p1.astype(F32)).astype(BF16)
    return p0, p1, p2


def _const_spec(shape, n_grid):
    zeros = (0,) * len(shape)
    if n_grid == 1:
        return pl.BlockSpec(shape, lambda i: zeros)
    if n_grid == 2:
        return pl.BlockSpec(shape, lambda i, j: zeros)
    return pl.BlockSpec(shape, lambda i, j, k: zeros)


def _lru_kernel(x_ref, g_ref, win_ref, cw_ref, cb_ref, gw_ref, gb_ref, lam_ref, wout_ref,
                o_ref, xbuf, hcar):
    t = pl.program_id(1)
    tt = x_ref.shape[1]
    d = x_ref.shape[2]
    bw = d // N_LRU_BLOCKS

    @pl.when(t == 0)
    def _():
        xbuf[0:SUBLANES, :] = jnp.zeros((SUBLANES, d), F32)
        hcar[...] = jnp.zeros_like(hcar)

    x = x_ref[0]
    xn = _rms(x, g_ref[...]).astype(BF16)
    xg = jnp.dot(xn, win_ref[...], preferred_element_type=F32)
    xb = xg[:, :d]
    gbr = xg[:, d:]

    xbuf[SUBLANES:SUBLANES + tt, :] = xb
    cw = cw_ref[...]
    xc = cb_ref[...] + cw[CONV_W - 1:CONV_W, :] * xb
    for j in range(CONV_W - 1):
        xc = xc + cw[j:j + 1, :] * xbuf[pl.ds(SUBLANES - (CONV_W - 1) + j, tt), :]
    xbuf[0:SUBLANES, :] = xbuf[tt:tt + SUBLANES, :]

    xcb = xc.astype(BF16)
    rl, il = [], []
    for n in range(N_LRU_BLOCKS):
        gl = jnp.dot(xcb[:, n * bw:(n + 1) * bw], gw_ref[n], preferred_element_type=F32)
        rl.append(gl[:, :bw])
        il.append(gl[:, bw:])
    gb = gb_ref[...]
    r = jax.nn.sigmoid(jnp.concatenate(rl, axis=1) + gb[0:1, :])
    i = jax.nn.sigmoid(jnp.concatenate(il, axis=1) + gb[1:2, :])
    log_a = LRU_C * r * _log_sigmoid(lam_ref[...])
    a = jnp.exp(log_a)
    th = jnp.tanh(log_a)
    b = jnp.sqrt(-2.0 * th / (1.0 - th)) * (i * xc)

    row = lax.broadcasted_iota(jnp.int32, (tt, 1), 0)
    sh = 1
    while sh < tt:
        keep = row >= sh
        a_sh = jnp.where(keep, pltpu.roll(a, sh, 0), 1.0)
        b_sh = jnp.where(keep, pltpu.roll(b, sh, 0), 0.0)
        b = a * b_sh + b
        a = a * a_sh
        sh *= 2
    h = b + a * hcar[...]
    hcar[...] = h[tt - 1:tt, :]

    y = (_gelu_tanh(gbr) * h).astype(BF16)
    o_ref[0] = x + jnp.dot(y, wout_ref[...], preferred_element_type=F32)


def _lru_layer(x, g, w_in, conv_w, conv_b, gate_w, gate_b, lam, w_out, *, tt):
    bsz, s, d = x.shape
    bw = d // N_LRU_BLOCKS
    gw = jnp.concatenate([gate_w[0], gate_w[1]], axis=-1).astype(BF16)
    const = lambda *shape: _const_spec(shape, 2)
    return pl.pallas_call(
        _lru_kernel,
        grid=(bsz, s // tt),
        in_specs=[
            pl.BlockSpec((1, tt, d), lambda b, t: (b, t, 0)),
            const(1, d), const(d, 2 * d), const(CONV_W, d), const(1, d),
            const(N_LRU_BLOCKS, bw, 2 * bw), const(2, d), const(1, d), const(d, d),
        ],
        out_specs=pl.BlockSpec((1, tt, d), lambda b, t: (b, t, 0)),
        out_shape=jax.ShapeDtypeStruct((bsz, s, d), F32),
        scratch_shapes=[pltpu.VMEM((tt + 2 * SUBLANES, d), F32), pltpu.VMEM((1, d), F32)],
        compiler_params=pltpu.CompilerParams(dimension_semantics=("parallel", "arbitrary")),
        name="lru_layer",
    )(x, g.reshape(1, d), w_in.astype(BF16), conv_w, conv_b.reshape(1, d), gw, gate_b,
      lam.reshape(1, d), w_out.astype(BF16))


def _mlp_kernel(x_ref, g_ref, wg_ref, wu_ref, wd_ref, o_ref, hn_ref, acc_ref):
    k = pl.program_id(1)

    @pl.when(k == 0)
    def _():
        hn_ref[...] = _rms(x_ref[...], g_ref[...]).astype(BF16)
        acc_ref[...] = jnp.zeros_like(acc_ref)

    hn = hn_ref[...]
    gate = jnp.dot(hn, wg_ref[...], preferred_element_type=F32)
    up = jnp.dot(hn, wu_ref[...], preferred_element_type=F32)
    h = (_silu(gate) * up).astype(BF16)
    acc_ref[...] += jnp.dot(h, wd_ref[...], preferred_element_type=F32)

    @pl.when(k == pl.num_programs(1) - 1)
    def _():
        o_ref[...] = x_ref[...] + acc_ref[...]


def _mlp(x2d, g, w_gu, w_down, *, tm, tf):
    t, d = x2d.shape
    dff = w_down.shape[0]
    nk = dff // tf
    w_gu = w_gu.astype(BF16)
    return pl.pallas_call(
        _mlp_kernel,
        grid=(t // tm, nk),
        in_specs=[
            pl.BlockSpec((tm, d), lambda i, k: (i, 0)),
            pl.BlockSpec((1, d), lambda i, k: (0, 0)),
            pl.BlockSpec((d, tf), lambda i, k: (0, k)),
            pl.BlockSpec((d, tf), lambda i, k: (0, nk + k)),
            pl.BlockSpec((tf, d), lambda i, k: (k, 0)),
        ],
        out_specs=pl.BlockSpec((tm, d), lambda i, k: (i, 0)),
        out_shape=jax.ShapeDtypeStruct((t, d), F32),
        scratch_shapes=[pltpu.VMEM((tm, d), BF16), pltpu.VMEM((tm, d), F32)],
        compiler_params=pltpu.CompilerParams(dimension_semantics=("parallel", "arbitrary")),
        name="mlp",
    )(x2d, g.reshape(1, d), w_gu, w_gu, w_down.astype(BF16))


def _head_selectors():
    hg = N_HEADS * HEAD_GROUP
    head = jnp.arange(hg) // HEAD_GROUP
    lane = jnp.arange(hg) % HEAD_GROUP
    col = jnp.arange(LANES)
    def sel(base, p):
        return ((col[:, None] == head[None, :]) & (lane[None, :] == base + p)).astype(BF16)
    sel_k = jnp.concatenate([-sel(BIAS_K, p) for p in range(3)], axis=0)
    sel_q = jnp.concatenate([sel(BIAS_Q, p).T for p in range(3)], axis=1)
    ones_k = ((lane >= BIAS_Q) & (lane < BIAS_Q + 3)).astype(F32).reshape(1, hg)
    ones_q = ((lane >= BIAS_K) & (lane < BIAS_K + 3)).astype(F32).reshape(hg, 1)
    head_sum = ((col[None, :] == head[:, None]) & (lane[:, None] < HEAD_DIM)).astype(BF16)
    return sel_k, sel_q, ones_k, ones_q, head_sum


def _kv_kernel(x_ref, g_ref, wk_ref, wvT_ref, onesv_ref, wf_ref, wfT_ref, bfr_ref, bfc_ref, selk_ref, onesk_ref,
               hsum_ref, ka_ref, vT_ref, c2r_ref, kn_ref, car_r, car_c):
    t = pl.program_id(1)
    tk = x_ref.shape[1]

    @pl.when(t == 0)
    def _():
        car_r[...] = jnp.zeros_like(car_r)
        car_c[...] = jnp.zeros_like(car_c)

    xn = _rms(x_ref[0], g_ref[...]).astype(BF16)
    vT = lax.dot_general(wvT_ref[...], xn, NT_DIMS, preferred_element_type=F32) + onesv_ref[...]
    vT_ref[0, 0] = vT.astype(BF16)
    kf = jnp.dot(xn, wk_ref[...], preferred_element_type=F32)

    ksq = kf * kf
    ksq_hi = ksq.astype(BF16)
    ksq_lo = (ksq - ksq_hi.astype(F32)).astype(BF16)
    hs = hsum_ref[...]
    kss = jnp.dot(ksq_hi, hs, preferred_element_type=F32) + jnp.dot(ksq_lo, hs, preferred_element_type=F32)
    kn = jnp.sqrt(jnp.max(kss, axis=0, keepdims=True))
    kn_ref[0, 0] = jnp.broadcast_to(kn, (SUBLANES, LANES))

    ri = lax.broadcasted_iota(jnp.int32, (tk, tk), 0)
    ci = lax.broadcasted_iota(jnp.int32, (tk, tk), 1)
    upper = (ri <= ci).astype(BF16)
    lower = (ci <= ri).astype(BF16)

    lf_c = _log_sigmoid(jnp.dot(xn, wf_ref[...], preferred_element_type=F32) + bfr_ref[...])
    cs3 = jnp.dot(lower, jnp.concatenate(_split3(lf_c), axis=1), preferred_element_type=F32)
    cs_c = car_c[...] + (cs3[:, :LANES] + cs3[:, LANES:2 * LANES] + cs3[:, 2 * LANES:])
    car_c[...] = cs_c[tk - 1:tk, :]
    pieces = jnp.concatenate(_split3(cs_c * LOG2E), axis=1)
    bias = jnp.dot(pieces, selk_ref[...], preferred_element_type=F32)
    ka_ref[0] = (kf + bias + onesk_ref[...]).astype(BF16)

    lf_r = _log_sigmoid(lax.dot_general(wfT_ref[...], xn, NT_DIMS, preferred_element_type=F32) + bfc_ref[...])
    cs3 = jnp.dot(jnp.concatenate(_split3(lf_r), axis=0), upper, preferred_element_type=F32)
    cs_r = car_r[...] + (cs3[:LANES] + cs3[LANES:2 * LANES] + cs3[2 * LANES:])
    car_r[...] = cs_r[:, tk - 1:tk]
    c2r_ref[0] = cs_r * LOG2E


def _kv_proj(x, g, w_kvf, b_f, consts, *, tk):
    bsz, s, d = x.shape
    da = N_HEADS * HEAD_DIM
    hg = N_HEADS * HEAD_GROUP
    dv = N_HEADS * V_GROUP
    nk = s // tk
    sel_k, _, ones_k, _, head_sum = consts
    wk = w_kvf[:, :da].reshape(d, N_HEADS, HEAD_DIM)
    wk = jnp.pad(wk, ((0, 0), (0, 0), (0, HEAD_GROUP - HEAD_DIM))).reshape(d, hg).astype(BF16)
    wv = w_kvf[:, da:2 * da].reshape(d, N_HEADS, HEAD_DIM)
    wvT = jnp.pad(wv, ((0, 0), (0, 0), (0, V_GROUP - HEAD_DIM))).reshape(d, dv).T.astype(BF16)
    ones_v = (jnp.arange(dv) % V_GROUP == HEAD_DIM).astype(F32).reshape(dv, 1)
    wf = jnp.pad(w_kvf[:, 2 * da:], ((0, 0), (0, LANES - N_HEADS))).astype(BF16)
    bfr = jnp.pad(b_f, (0, LANES - N_HEADS)).reshape(1, LANES)
    const = lambda *shape: _const_spec(shape, 2)
    return pl.pallas_call(
        _kv_kernel,
        grid=(bsz, nk),
        in_specs=[
            pl.BlockSpec((1, tk, d), lambda b, t: (b, t, 0)),
            const(1, d), const(d, hg), const(dv, d), const(dv, 1), const(d, LANES), const(LANES, d),
            const(1, LANES), const(LANES, 1), const(3 * LANES, hg), const(1, hg), const(hg, LANES),
        ],
        out_specs=[
            pl.BlockSpec((1, tk, hg), lambda b, t: (b, t, 0)),
            pl.BlockSpec((1, 1, dv, tk), lambda b, t: (b, t, 0, 0)),
            pl.BlockSpec((1, LANES, tk), lambda b, t: (b, 0, t)),
            pl.BlockSpec((1, 1, SUBLANES, LANES), lambda b, t: (b, t, 0, 0)),
        ],
        out_shape=[
            jax.ShapeDtypeStruct((bsz, s, hg), BF16),
            jax.ShapeDtypeStruct((bsz, nk, dv, tk), BF16),
            jax.ShapeDtypeStruct((bsz, LANES, s), F32),
            jax.ShapeDtypeStruct((bsz, nk, SUBLANES, LANES), F32),
        ],
        scratch_shapes=[pltpu.VMEM((LANES, 1), F32), pltpu.VMEM((1, LANES), F32)],
        compiler_params=pltpu.CompilerParams(dimension_semantics=("parallel", "arbitrary")),
        name="kv_proj",
    )(x, g.reshape(1, d), wk, wvT, ones_v, wf, wf.T, bfr, bfr.reshape(LANES, 1), sel_k, ones_k, head_sum)


def _q_kernel(x_ref, g_ref, wqT_ref, wg_ref, c2r_ref, selq_ref, onesq_ref, hsumT_ref, qa_ref, gate_ref, qn_ref):
    xn = _rms(x_ref[0], g_ref[...]).astype(BF16)
    gate_ref[0] = jax.nn.sigmoid(jnp.dot(xn, wg_ref[...], preferred_element_type=F32)).astype(BF16)
    qf = lax.dot_general(wqT_ref[...], xn, NT_DIMS, preferred_element_type=F32) * (LOG2E * HEAD_DIM ** -0.5)

    qsq = qf * qf
    qsq_hi = qsq.astype(BF16)
    qsq_lo = (qsq - qsq_hi.astype(F32)).astype(BF16)
    hsT = hsumT_ref[...]
    qss = jnp.dot(hsT, qsq_hi, preferred_element_type=F32) + jnp.dot(hsT, qsq_lo, preferred_element_type=F32)
    qn = jnp.sqrt(jnp.max(qss, axis=1, keepdims=True))
    qn_ref[0, 0] = jnp.broadcast_to(qn, (LANES, LANES))

    pieces = jnp.concatenate(_split3(c2r_ref[0]), axis=0)
    bias = jnp.dot(selq_ref[...], pieces, preferred_element_type=F32)
    qa_ref[0] = (qf + bias + onesq_ref[...]).astype(BF16)


def _q_proj(x, g, w_qg, c2r, consts, *, tq):
    bsz, s, d = x.shape
    da = N_HEADS * HEAD_DIM
    hg = N_HEADS * HEAD_GROUP
    nq = s // tq
    _, sel_q, _, ones_q, head_sum = consts
    wq = w_qg[:, :da].reshape(d, N_HEADS, HEAD_DIM)
    wqT = jnp.pad(wq, ((0, 0), (0, 0), (0, HEAD_GROUP - HEAD_DIM))).reshape(d, hg).T.astype(BF16)
    const = lambda *shape: _const_spec(shape, 2)
    return pl.pallas_call(
        _q_kernel,
        grid=(bsz, nq),
        in_specs=[
            pl.BlockSpec((1, tq, d), lambda b, t: (b, t, 0)),
            const(1, d), const(hg, d), const(d, da),
            pl.BlockSpec((1, LANES, tq), lambda b, t: (b, 0, t)),
            const(hg, 3 * LANES), const(hg, 1), const(LANES, hg),
        ],
        out_specs=[
            pl.BlockSpec((1, hg, tq), lambda b, t: (b, 0, t)),
            pl.BlockSpec((1, tq, da), lambda b, t: (b, t, 0)),
            pl.BlockSpec((1, 1, LANES, LANES), lambda b, t: (b, t, 0, 0)),
        ],
        out_shape=[
            jax.ShapeDtypeStruct((bsz, hg, s), BF16),
            jax.ShapeDtypeStruct((bsz, s, da), BF16),
            jax.ShapeDtypeStruct((bsz, nq, LANES, LANES), F32),
        ],
        compiler_params=pltpu.CompilerParams(dimension_semantics=("parallel", "parallel")),
        name="q_proj",
    )(x, g.reshape(1, d), wqT, w_qg[:, da:].astype(BF16), c2r, sel_q, ones_q, head_sum.T)


def _attn_kernel(jlo_ref, qa_ref, ka_ref, vT_ref, gate_ref, o_ref, m0, m1, acc0, acc1):
    b = pl.program_id(0)
    hp = pl.program_id(1)
    i = pl.program_id(2)
    tq = qa_ref.shape[2]
    tk = vT_ref.shape[3]
    j_lo = jlo_ref[(b * pl.num_programs(1) + hp) * pl.num_programs(2) + i]
    m_refs, acc_refs = (m0, m1), (acc0, acc1)

    for h in range(2):
        m_refs[h][...] = jnp.full((1, tq), NEG_BIG, F32)
        acc_refs[h][...] = jnp.zeros((V_GROUP, tq), F32)

    def process(blocks):
        units = [(j, masked, h) for (j, masked) in blocks for h in range(2)]
        scores = []
        for j, _, h in units:
            k_blk = ka_ref[0, pl.ds(pl.multiple_of(j * tk, tk), tk), h * HEAD_GROUP:(h + 1) * HEAD_GROUP]
            qT = qa_ref[0, h * HEAD_GROUP:(h + 1) * HEAD_GROUP, :]
            scores.append(jnp.dot(k_blk, qT, preferred_element_type=F32))
        for (j, masked, h), s in zip(units, scores):
            if masked:
                kv_pos = lax.broadcasted_iota(jnp.int32, (tk, tq), 0)
                q_pos = lax.broadcasted_iota(jnp.int32, (tk, tq), 1)
                s = jnp.where(kv_pos <= q_pos, s, NEG_BIG)
            m_old = m_refs[h][...]
            m_new = jnp.maximum(m_old, jnp.max(s, axis=0, keepdims=True))
            alpha = jnp.exp2(m_old - m_new)
            p = jnp.exp2(s - m_new).astype(BF16)
            m_refs[h][...] = m_new
            vT = vT_ref[0, j, h * V_GROUP:(h + 1) * V_GROUP, :]
            acc_refs[h][...] = alpha * acc_refs[h][...] + jnp.dot(vT, p, preferred_element_type=F32)

    n_full = i - j_lo

    def body(n, carry):
        j = j_lo + 2 * n
        process([(j, False), (j + 1, False)])
        return carry

    lax.fori_loop(0, n_full // 2, body, 0)

    @pl.when(n_full % 2 == 1)
    def _():
        process([(i - 1, False), (i, True)])

    @pl.when(n_full % 2 == 0)
    def _():
        process([(i, True)])

    oT = jnp.concatenate([acc[0:HEAD_DIM, :] / acc[HEAD_DIM:HEAD_DIM + 1, :] for acc in (acc0[...], acc1[...])],
                         axis=0)
    o_ref[0] = (oT.T * gate_ref[0].astype(F32)).astype(BF16)


def _skip_table(qn, kn, c2r, tq):
    bsz, nq = qn.shape[0], qn.shape[1]
    qn = qn[:, :, :N_HEADS, 0].transpose(0, 2, 1) * NORM_SLACK
    kn = kn[:, :, 0, :N_HEADS].transpose(0, 2, 1) * NORM_SLACK
    c2 = c2r[:, :N_HEADS, :]
    c_first = c2[:, :, ::tq]
    c_last = c2[:, :, tq - 1::tq]
    upper = qn[..., :, None] * kn[..., None, :] + c_first[..., :, None] - c_last[..., None, :]
    lower = -(qn * kn)[..., :, None]
    skip = (upper - lower) < -SKIP_GAP_LOG2
    nk = skip.shape[-1]
    jlo = jnp.min(jnp.where(skip, nk, jnp.arange(nk, dtype=jnp.int32)), axis=-1)
    jlo = jnp.minimum(jlo, jnp.arange(nq, dtype=jnp.int32))
    jlo = jnp.min(jlo.reshape(bsz, N_HEADS // 2, 2, nq), axis=2)
    return jlo.reshape(-1).astype(jnp.int32)


def _attention(jlo, qa, ka, vT, gate, *, tq):
    bsz, hg, s = qa.shape
    nk, tk = vT.shape[1], vT.shape[3]
    da = N_HEADS * HEAD_DIM
    assert tq == tk
    pair = 2 * HEAD_GROUP
    hw = 2 * HEAD_DIM
    grid_spec = pltpu.PrefetchScalarGridSpec(
        num_scalar_prefetch=1,
        grid=(bsz, N_HEADS // 2, s // tq),
        in_specs=[
            pl.BlockSpec((1, pair, tq), lambda b, hp, i, jlo: (b, hp, i)),
            pl.BlockSpec((1, s, pair), lambda b, hp, i, jlo: (b, 0, hp)),
            pl.BlockSpec((1, nk, 2 * V_GROUP, tk), lambda b, hp, i, jlo: (b, 0, hp, 0)),
            pl.BlockSpec((1, tq, hw), lambda b, hp, i, jlo: (b, i, hp)),
        ],
        out_specs=pl.BlockSpec((1, tq, hw), lambda b, hp, i, jlo: (b, i, hp)),
        scratch_shapes=[pltpu.VMEM((1, tq), F32)] * 2 + [pltpu.VMEM((V_GROUP, tq), F32)] * 2,
    )
    return pl.pallas_call(
        _attn_kernel,
        grid_spec=grid_spec,
        out_shape=jax.ShapeDtypeStruct((bsz, s, da), BF16),
        compiler_params=pltpu.CompilerParams(dimension_semantics=("parallel", "parallel", "arbitrary")),
        name="fox_attention",
    )(jlo, qa, ka, vT, gate)


def _oproj_kernel(x_ref, o_ref, w_ref, out_ref):
    out_ref[...] = x_ref[...] + jnp.dot(o_ref[...], w_ref[...], preferred_element_type=F32)


def _oproj(x2d, o2d, w_o, *, tm):
    t, d = x2d.shape
    da = o2d.shape[1]
    return pl.pallas_call(
        _oproj_kernel,
        grid=(t // tm,),
        in_specs=[
            pl.BlockSpec((tm, d), lambda i: (i, 0)),
            pl.BlockSpec((tm, da), lambda i: (i, 0)),
            pl.BlockSpec((da, d), lambda i: (0, 0)),
        ],
        out_specs=pl.BlockSpec((tm, d), lambda i: (i, 0)),
        out_shape=jax.ShapeDtypeStruct((t, d), F32),
        compiler_params=pltpu.CompilerParams(dimension_semantics=("parallel",)),
        name="attn_out_proj",
    )(x2d, o2d, w_o.astype(BF16))


def _moe_kernel(x_ref, g_ref, rt_ref, wg_ref, wu_ref, wd_ref, gf_ref, o_ref, hn_ref, gates_ref, acc_ref):
    e = pl.program_id(1)
    k = pl.program_id(2)
    tm = x_ref.shape[0]

    @pl.when((e == 0) & (k == 0))
    def _():
        hn = _rms(x_ref[...], g_ref[...])
        hn_ref[...] = hn.astype(BF16)
        acc_ref[...] = jnp.zeros_like(acc_ref)
        h0, h1, _ = _split3(hn)
        rt = rt_ref[...]
        r0, r1, _ = _split3(rt)
        logits = (jnp.dot(h0, r0, preferred_element_type=F32) + jnp.dot(h0, r1, preferred_element_type=F32)
                  + jnp.dot(h1, r0, preferred_element_type=F32))
        lane = lax.broadcasted_iota(jnp.int32, (tm, N_EXPERTS), 1)
        m1 = jnp.max(logits, axis=-1, keepdims=True)
        i1 = jnp.min(jnp.where(logits == m1, lane, N_EXPERTS), axis=-1, keepdims=True)
        first = lane == i1
        rest = jnp.where(first, -jnp.inf, logits)
        m2 = jnp.max(rest, axis=-1, keepdims=True)
        i2 = jnp.min(jnp.where(rest == m2, lane, N_EXPERTS), axis=-1, keepdims=True)
        second = lane == i2
        e2 = jnp.exp(m2 - m1)
        w1 = 1.0 / (1.0 + e2)
        w2 = e2 / (1.0 + e2)
        gates_ref[...] = jnp.where(first, w1, 0.0) + jnp.where(second, w2, 0.0)

    hn = hn_ref[...]
    gate = jnp.dot(hn, wg_ref[0], preferred_element_type=F32)
    up = jnp.dot(hn, wu_ref[0], preferred_element_type=F32)
    h = (_silu(gate) * up).astype(BF16)
    y = jnp.dot(h, wd_ref[0], preferred_element_type=F32)
    lane = lax.broadcasted_iota(jnp.int32, (tm, N_EXPERTS), 1)
    ge = jnp.sum(jnp.where(lane == e, gates_ref[...], 0.0), axis=-1, keepdims=True)
    acc_ref[...] += ge * y

    @pl.when((e == pl.num_programs(1) - 1) & (k == pl.num_programs(2) - 1))
    def _():
        o_ref[...] = _rms(x_ref[...] + acc_ref[...], gf_ref[...])


def _moe(x2d, g, router, w_gu, w_down, g_final, *, tm, tf):
    t, d = x2d.shape
    ne, dexp = w_down.shape[0], w_down.shape[1]
    nk = dexp // tf
    w_gu = w_gu.astype(BF16)
    return pl.pallas_call(
        _moe_kernel,
        grid=(t // tm, ne, nk),
        in_specs=[
            pl.BlockSpec((tm, d), lambda i, e, k: (i, 0)),
            pl.BlockSpec((1, d), lambda i, e, k: (0, 0)),
            pl.BlockSpec((d, ne), lambda i, e, k: (0, 0)),
            pl.BlockSpec((1, d, tf), lambda i, e, k: (e, 0, k)),
            pl.BlockSpec((1, d, tf), lambda i, e, k: (e, 0, nk + k)),
            pl.BlockSpec((1, tf, d), lambda i, e, k: (e, k, 0)),
            pl.BlockSpec((1, d), lambda i, e, k: (0, 0)),
        ],
        out_specs=pl.BlockSpec((tm, d), lambda i, e, k: (i, 0)),
        out_shape=jax.ShapeDtypeStruct((t, d), F32),
        scratch_shapes=[pltpu.VMEM((tm, d), BF16), pltpu.VMEM((tm, ne), F32), pltpu.VMEM((tm, d), F32)],
        compiler_params=pltpu.CompilerParams(dimension_semantics=("parallel", "arbitrary", "arbitrary")),
        name="moe",
    )(x2d, g.reshape(1, d), router, w_gu, w_gu, w_down.astype(BF16), g_final.reshape(1, d))


def _tile(n, pref):
    t = min(n, pref)
    assert n % t == 0
    return t


def kernel(x, norm_g, lru_w_in, lru_conv_w, lru_conv_b, lru_gate_w, lru_gate_b, lru_lambda, lru_w_out,
           kv_norm_g, w_kvf, b_f, fox_w_qg, fox_w_o, mlp_w_gu, mlp_w_down, moe_router, moe_w_gu,
           moe_w_down, final_norm_g):
    bsz, s, d = x.shape
    t = bsz * s
    tm = _tile(t, 512)
    ts = _tile(s, 512)

    x = _lru_layer(x, norm_g[0, 0], lru_w_in[0], lru_conv_w[0], lru_conv_b[0], lru_gate_w[0],
                   lru_gate_b[0], lru_lambda[0], lru_w_out[0], tt=_tile(s, 256))
    dff = mlp_w_down.shape[1]
    x = _mlp(x.reshape(t, d), norm_g[0, 1], mlp_w_gu[0], mlp_w_down[0], tm=tm, tf=dff // 2).reshape(bsz, s, d)

    consts = _head_selectors()
    ka, vT, c2r, kn = _kv_proj(x, kv_norm_g, w_kvf, b_f, consts, tk=ts)
    qa, gate, qn = _q_proj(x, norm_g[1, 0], fox_w_qg[0], c2r, consts, tq=ts)
    jlo = _skip_table(qn, kn, c2r, ts)
    o = _attention(jlo, qa, ka, vT, gate, tq=ts)
    x2d = _oproj(x.reshape(t, d), o.reshape(t, -1), fox_w_o[0], tm=tm)

    dexp = moe_w_down.shape[2]
    out = _moe(x2d, norm_g[1, 1], moe_router[0], moe_w_gu[0], moe_w_down[0], final_norm_g, tm=tm, tf=dexp // 2)
    return out.reshape(bsz, s, d)
```

```python
import functools
import math

import jax
import jax.numpy as jnp
from jax import lax
from jax.experimental import pallas as pl
from jax.experimental.pallas import tpu as pltpu
from jax.experimental.pallas import tpu_sc as plsc

EPS = 1e-6
LRU_C = 8.0
N_LRU_BLOCKS = 8
CONV_W = 4
N_HEADS = 16
HEAD_DIM = 64
N_EXPERTS = 8
LANES = 128
SUBLANES = 8
HEAD_GROUP = 128
BIAS_Q = HEAD_DIM
BIAS_K = HEAD_DIM + 3
V_GROUP = 80
NEG_BIG = -1e30
LOG2E = math.log2(math.e)
SKIP_GAP_LOG2 = 160.0
NORM_SLACK = 1.01
SC_GATHER_WINDOW = 128

F32 = jnp.float32
BF16 = jnp.bfloat16
NT_DIMS = (((1,), (1,)), ((), ()))


def _rms(x, g):
    ms = jnp.mean(x * x, axis=-1, keepdims=True)
    return x * lax.rsqrt(ms + EPS) * g


def _log_sigmoid(x):
    return jnp.minimum(x, 0.0) - jnp.log1p(jnp.exp(-jnp.abs(x)))


def _gelu_tanh(x):
    c = 0.7978845608028654
    return 0.5 * x * (1.0 + jnp.tanh(c * (x + 0.044715 * (x * x * x))))


def _silu(x):
    return x * jax.nn.sigmoid(x)


def _split3(x):
    p0 = x.astype(BF16)
    r1 = x - p0.astype(F32)
    p1 = r1.astype(BF16)
    p2 = (r1 - p1.astype(F32)).astype(BF16)
    return p0, p1, p2


def _const_spec(shape, n_grid):
    zeros = (0,) * len(shape)
    if n_grid == 1:
        return pl.BlockSpec(shape, lambda i: zeros)
    if n_grid == 2:
        return pl.BlockSpec(shape, lambda i, j: zeros)
    return pl.BlockSpec(shape, lambda i, j, k: zeros)


def _lru_kernel(x_ref, g_ref, win_ref, cw_ref, cb_ref, gw_ref, gb_ref, lam_ref, wout_ref,
                o_ref, xbuf, hcar):
    t = pl.program_id(1)
    tt = x_ref.shape[1]
    d = x_ref.shape[2]
    bw = d // N_LRU_BLOCKS

    @pl.when(t == 0)
    def _():
        xbuf[0:SUBLANES, :] = jnp.zeros((SUBLANES, d), F32)
        hcar[...] = jnp.zeros_like(hcar)

    x = x_ref[0]
    xn = _rms(x, g_ref[...]).astype(BF16)
    xg = jnp.dot(xn, win_ref[...], preferred_element_type=F32)
    xb = xg[:, :d]
    gbr = xg[:, d:]

    xbuf[SUBLANES:SUBLANES + tt, :] = xb
    cw = cw_ref[...]
    xc = cb_ref[...] + cw[CONV_W - 1:CONV_W, :] * xb
    for j in range(CONV_W - 1):
        xc = xc + cw[j:j + 1, :] * xbuf[pl.ds(SUBLANES - (CONV_W - 1) + j, tt), :]
    xbuf[0:SUBLANES, :] = xbuf[tt:tt + SUBLANES, :]

    xcb = xc.astype(BF16)
    rl, il = [], []
    for n in range(N_LRU_BLOCKS):
        gl = jnp.dot(xcb[:, n * bw:(n + 1) * bw], gw_ref[n], preferred_element_type=F32)
        rl.append(gl[:, :bw])
        il.append(gl[:, bw:])
    gb = gb_ref[...]
    r = jax.nn.sigmoid(jnp.concatenate(rl, axis=1) + gb[0:1, :])
    i = jax.nn.sigmoid(jnp.concatenate(il, axis=1) + gb[1:2, :])
    log_a = LRU_C * r * _log_sigmoid(lam_ref[...])
    a = jnp.exp(log_a)
    th = jnp.tanh(log_a)
    b = jnp.sqrt(-2.0 * th / (1.0 - th)) * (i * xc)

    row = lax.broadcasted_iota(jnp.int32, (tt, 1), 0)
    sh = 1
    while sh < tt:
        keep = row >= sh
        a_sh = jnp.where(keep, pltpu.roll(a, sh, 0), 1.0)
        b_sh = jnp.where(keep, pltpu.roll(b, sh, 0), 0.0)
        b = a * b_sh + b
        a = a * a_sh
        sh *= 2
    h = b + a * hcar[...]
    hcar[...] = h[tt - 1:tt, :]

    y = (_gelu_tanh(gbr) * h).astype(BF16)
    o_ref[0] = x + jnp.dot(y, wout_ref[...], preferred_element_type=F32)


def _lru_layer(x, g, w_in, conv_w, conv_b, gate_w, gate_b, lam, w_out, *, tt):
    bsz, s, d = x.shape
    bw = d // N_LRU_BLOCKS
    gw = jnp.concatenate([gate_w[0], gate_w[1]], axis=-1).astype(BF16)
    const = lambda *shape: _const_spec(shape, 2)
    return pl.pallas_call(
        _lru_kernel,
        grid=(bsz, s // tt),
        in_specs=[
            pl.BlockSpec((1, tt, d), lambda b, t: (b, t, 0)),
            const(1, d), const(d, 2 * d), const(CONV_W, d), const(1, d),
            const(N_LRU_BLOCKS, bw, 2 * bw), const(2, d), const(1, d), const(d, d),
        ],
        out_specs=pl.BlockSpec((1, tt, d), lambda b, t: (b, t, 0)),
        out_shape=jax.ShapeDtypeStruct((bsz, s, d), F32),
        scratch_shapes=[pltpu.VMEM((tt + 2 * SUBLANES, d), F32), pltpu.VMEM((1, d), F32)],
        compiler_params=pltpu.CompilerParams(dimension_semantics=("parallel", "arbitrary")),
        name="lru_layer",
    )(x, g.reshape(1, d), w_in.astype(BF16), conv_w, conv_b.reshape(1, d), gw, gate_b,
      lam.reshape(1, d), w_out.astype(BF16))


def _mlp_kernel(x_ref, g_ref, wg_ref, wu_ref, wd_ref, o_ref, hn_ref, acc_ref):
    k = pl.program_id(1)

    @pl.when(k == 0)
    def _():
        hn_ref[...] = _rms(x_ref[...], g_ref[...]).astype(BF16)
        acc_ref[...] = jnp.zeros_like(acc_ref)

    hn = hn_ref[...]
    gate = jnp.dot(hn, wg_ref[...], preferred_element_type=F32)
    up = jnp.dot(hn, wu_ref[...], preferred_element_type=F32)
    h = (_silu(gate) * up).astype(BF16)
    acc_ref[...] += jnp.dot(h, wd_ref[...], preferred_element_type=F32)

    @pl.when(k == pl.num_programs(1) - 1)
    def _():
        o_ref[...] = x_ref[...] + acc_ref[...]


def _mlp(x2d, g, w_gu, w_down, *, tm, tf):
    t, d = x2d.shape
    dff = w_down.shape[0]
    nk = dff // tf
    w_gu = w_gu.astype(BF16)
    return pl.pallas_call(
        _mlp_kernel,
        grid=(t // tm, nk),
        in_specs=[
            pl.BlockSpec((tm, d), lambda i, k: (i, 0)),
            pl.BlockSpec((1, d), lambda i, k: (0, 0)),
            pl.BlockSpec((d, tf), lambda i, k: (0, k)),
            pl.BlockSpec((d, tf), lambda i, k: (0, nk + k)),
            pl.BlockSpec((tf, d), lambda i, k: (k, 0)),
        ],
        out_specs=pl.BlockSpec((tm, d), lambda i, k: (i, 0)),
        out_shape=jax.ShapeDtypeStruct((t, d), F32),
        scratch_shapes=[pltpu.VMEM((tm, d), BF16), pltpu.VMEM((tm, d), F32)],
        compiler_params=pltpu.CompilerParams(dimension_semantics=("parallel", "arbitrary")),
        name="mlp",
    )(x2d, g.reshape(1, d), w_gu, w_gu, w_down.astype(BF16))


def _head_selectors():
    hg = N_HEADS * HEAD_GROUP
    head = jnp.arange(hg) // HEAD_GROUP
    lane = jnp.arange(hg) % HEAD_GROUP
    col = jnp.arange(LANES)
    def sel(base, p):
        return ((col[:, None] == head[None, :]) & (lane[None, :] == base + p)).astype(BF16)
    sel_k = jnp.concatenate([-sel(BIAS_K, p) for p in range(3)], axis=0)
    sel_q = jnp.concatenate([sel(BIAS_Q, p).T for p in range(3)], axis=1)
    ones_k = ((lane >= BIAS_Q) & (lane < BIAS_Q + 3)).astype(F32).reshape(1, hg)
    ones_q = ((lane >= BIAS_K) & (lane < BIAS_K + 3)).astype(F32).reshape(hg, 1)
    head_sum = ((col[None, :] == head[:, None]) & (lane[:, None] < HEAD_DIM)).astype(BF16)
    return sel_k, sel_q, ones_k, ones_q, head_sum


def _kv_kernel(x_ref, g_ref, wk_ref, wvT_ref, onesv_ref, wf_ref, wfT_ref, bfr_ref, bfc_ref, selk_ref, onesk_ref,
               hsum_ref, ka_ref, vT_ref, c2r_ref, kn_ref, car_r, car_c):
    t = pl.program_id(1)
    tk = x_ref.shape[1]

    @pl.when(t == 0)
    def _():
        car_r[...] = jnp.zeros_like(car_r)
        car_c[...] = jnp.zeros_like(car_c)

    xn = _rms(x_ref[0], g_ref[...]).astype(BF16)
    vT = lax.dot_general(wvT_ref[...], xn, NT_DIMS, preferred_element_type=F32) + onesv_ref[...]
    vT_ref[0, 0] = vT.astype(BF16)
    kf = jnp.dot(xn, wk_ref[...], preferred_element_type=F32)

    ksq = kf * kf
    ksq_hi = ksq.astype(BF16)
    ksq_lo = (ksq - ksq_hi.astype(F32)).astype(BF16)
    hs = hsum_ref[...]
    kss = jnp.dot(ksq_hi, hs, preferred_element_type=F32) + jnp.dot(ksq_lo, hs, preferred_element_type=F32)
    kn = jnp.sqrt(jnp.max(kss, axis=0, keepdims=True))
    kn_ref[0, 0] = jnp.broadcast_to(kn, (SUBLANES, LANES))

    ri = lax.broadcasted_iota(jnp.int32, (tk, tk), 0)
    ci = lax.broadcasted_iota(jnp.int32, (tk, tk), 1)
    upper = (ri <= ci).astype(BF16)
    lower = (ci <= ri).astype(BF16)

    lf_c = _log_sigmoid(jnp.dot(xn, wf_ref[...], preferred_element_type=F32) + bfr_ref[...])
    cs3 = jnp.dot(lower, jnp.concatenate(_split3(lf_c), axis=1), preferred_element_type=F32)
    cs_c = car_c[...] + (cs3[:, :LANES] + cs3[:, LANES:2 * LANES] + cs3[:, 2 * LANES:])
    car_c[...] = cs_c[tk - 1:tk, :]
    pieces = jnp.concatenate(_split3(cs_c * LOG2E), axis=1)
    bias = jnp.dot(pieces, selk_ref[...], preferred_element_type=F32)
    ka_ref[0] = (kf + bias + onesk_ref[...]).astype(BF16)

    lf_r = _log_sigmoid(lax.dot_general(wfT_ref[...], xn, NT_DIMS, preferred_element_type=F32) + bfc_ref[...])
    cs3 = jnp.dot(jnp.concatenate(_split3(lf_r), axis=0), upper, preferred_element_type=F32)
    cs_r = car_r[...] + (cs3[:LANES] + cs3[LANES:2 * LANES] + cs3[2 * LANES:])
    car_r[...] = cs_r[:, tk - 1:tk]
    c2r_ref[0] = cs_r * LOG2E


def _kv_proj(x, g, w_kvf, b_f, consts, *, tk):
    bsz, s, d = x.shape
    da = N_HEADS * HEAD_DIM
    hg = N_HEADS * HEAD_GROUP
    dv = N_HEADS * V_GROUP
    nk = s // tk
    sel_k, _, ones_k, _, head_sum = consts
    wk = w_kvf[:, :da].reshape(d, N_HEADS, HEAD_DIM)
    wk = jnp.pad(wk, ((0, 0), (0, 0), (0, HEAD_GROUP - HEAD_DIM))).reshape(d, hg).astype(BF16)
    wv = w_kvf[:, da:2 * da].reshape(d, N_HEADS, HEAD_DIM)
    wvT = jnp.pad(wv, ((0, 0), (0, 0), (0, V_GROUP - HEAD_DIM))).reshape(d, dv).T.astype(BF16)
    ones_v = (jnp.arange(dv) % V_GROUP == HEAD_DIM).astype(F32).reshape(dv, 1)
    wf = jnp.pad(w_kvf[:, 2 * da:], ((0, 0), (0, LANES - N_HEADS))).astype(BF16)
    bfr = jnp.pad(b_f, (0, LANES - N_HEADS)).reshape(1, LANES)
    const = lambda *shape: _const_spec(shape, 2)
    return pl.pallas_call(
        _kv_kernel,
        grid=(bsz, nk),
        in_specs=[
            pl.BlockSpec((1, tk, d), lambda b, t: (b, t, 0)),
            const(1, d), const(d, hg), const(dv, d), const(dv, 1), const(d, LANES), const(LANES, d),
            const(1, LANES), const(LANES, 1), const(3 * LANES, hg), const(1, hg), const(hg, LANES),
        ],
        out_specs=[
            pl.BlockSpec((1, tk, hg), lambda b, t: (b, t, 0)),
            pl.BlockSpec((1, 1, dv, tk), lambda b, t: (b, t, 0, 0)),
            pl.BlockSpec((1, LANES, tk), lambda b, t: (b, 0, t)),
            pl.BlockSpec((1, 1, SUBLANES, LANES), lambda b, t: (b, t, 0, 0)),
        ],
        out_shape=[
            jax.ShapeDtypeStruct((bsz, s, hg), BF16),
            jax.ShapeDtypeStruct((bsz, nk, dv, tk), BF16),
            jax.ShapeDtypeStruct((bsz, LANES, s), F32),
            jax.ShapeDtypeStruct((bsz, nk, SUBLANES, LANES), F32),
        ],
        scratch_shapes=[pltpu.VMEM((LANES, 1), F32), pltpu.VMEM((1, LANES), F32)],
        compiler_params=pltpu.CompilerParams(dimension_semantics=("parallel", "arbitrary")),
        name="kv_proj",
    )(x, g.reshape(1, d), wk, wvT, ones_v, wf, wf.T, bfr, bfr.reshape(LANES, 1), sel_k, ones_k, head_sum)


def _q_kernel(x_ref, g_ref, wqT_ref, wg_ref, c2r_ref, selq_ref, onesq_ref, hsumT_ref, qa_ref, gate_ref, qn_ref):
    xn = _rms(x_ref[0], g_ref[...]).astype(BF16)
    gate_ref[0] = jax.nn.sigmoid(jnp.dot(xn, wg_ref[...], preferred_element_type=F32)).astype(BF16)
    qf = lax.dot_general(wqT_ref[...], xn, NT_DIMS, preferred_element_type=F32) * (LOG2E * HEAD_DIM ** -0.5)

    qsq = qf * qf
    qsq_hi = qsq.astype(BF16)
    qsq_lo = (qsq - qsq_hi.astype(F32)).astype(BF16)
    hsT = hsumT_ref[...]
    qss = jnp.dot(hsT, qsq_hi, preferred_element_type=F32) + jnp.dot(hsT, qsq_lo, preferred_element_type=F32)
    qn = jnp.sqrt(jnp.max(qss, axis=1, keepdims=True))
    qn_ref[0, 0] = jnp.broadcast_to(qn, (LANES, LANES))

    pieces = jnp.concatenate(_split3(c2r_ref[0]), axis=0)
    bias = jnp.dot(selq_ref[...], pieces, preferred_element_type=F32)
    qa_ref[0] = (qf + bias + onesq_ref[...]).astype(BF16)


def _q_proj(x, g, w_qg, c2r, consts, *, tq):
    bsz, s, d = x.shape
    da = N_HEADS * HEAD_DIM
    hg = N_HEADS * HEAD_GROUP
    nq = s // tq
    _, sel_q, _, ones_q, head_sum = consts
    wq = w_qg[:, :da].reshape(d, N_HEADS, HEAD_DIM)
    wqT = jnp.pad(wq, ((0, 0), (0, 0), (0, HEAD_GROUP - HEAD_DIM))).reshape(d, hg).T.astype(BF16)
    const = lambda *shape: _const_spec(shape, 2)
    return pl.pallas_call(
        _q_kernel,
        grid=(bsz, nq),
        in_specs=[
            pl.BlockSpec((1, tq, d), lambda b, t: (b, t, 0)),
            const(1, d), const(hg, d), const(d, da),
            pl.BlockSpec((1, LANES, tq), lambda b, t: (b, 0, t)),
            const(hg, 3 * LANES), const(hg, 1), const(LANES, hg),
        ],
        out_specs=[
            pl.BlockSpec((1, hg, tq), lambda b, t: (b, 0, t)),
            pl.BlockSpec((1, tq, da), lambda b, t: (b, t, 0)),
            pl.BlockSpec((1, 1, LANES, LANES), lambda b, t: (b, t, 0, 0)),
        ],
        out_shape=[
            jax.ShapeDtypeStruct((bsz, hg, s), BF16),
            jax.ShapeDtypeStruct((bsz, s, da), BF16),
            jax.ShapeDtypeStruct((bsz, nq, LANES, LANES), F32),
        ],
        compiler_params=pltpu.CompilerParams(dimension_semantics=("parallel", "parallel")),
        name="q_proj",
    )(x, g.reshape(1, d), wqT, w_qg[:, da:].astype(BF16), c2r, sel_q, ones_q, head_sum.T)


def _attn_kernel(jlo_ref, qa_ref, ka_ref, vT_ref, gate_ref, o_ref, m0, m1, acc0, acc1):
    b = pl.program_id(0)
    hp = pl.program_id(1)
    i = pl.program_id(2)
    tq = qa_ref.shape[2]
    tk = vT_ref.shape[3]
    j_lo = jlo_ref[(b * pl.num_programs(1) + hp) * pl.num_programs(2) + i]
    m_refs, acc_refs = (m0, m1), (acc0, acc1)

    for h in range(2):
        m_refs[h][...] = jnp.full((1, tq), NEG_BIG, F32)
        acc_refs[h][...] = jnp.zeros((V_GROUP, tq), F32)

    def process(blocks):
        units = [(j, masked, h) for (j, masked) in blocks for h in range(2)]
        scores = []
        for j, _, h in units:
            k_blk = ka_ref[0, pl.ds(pl.multiple_of(j * tk, tk), tk), h * HEAD_GROUP:(h + 1) * HEAD_GROUP]
            qT = qa_ref[0, h * HEAD_GROUP:(h + 1) * HEAD_GROUP, :]
            scores.append(jnp.dot(k_blk, qT, preferred_element_type=F32))
        for (j, masked, h), s in zip(units, scores):
            if masked:
                kv_pos = lax.broadcasted_iota(jnp.int32, (tk, tq), 0)
                q_pos = lax.broadcasted_iota(jnp.int32, (tk, tq), 1)
                s = jnp.where(kv_pos <= q_pos, s, NEG_BIG)
            m_old = m_refs[h][...]
            m_new = jnp.maximum(m_old, jnp.max(s, axis=0, keepdims=True))
            alpha = jnp.exp2(m_old - m_new)
            p = jnp.exp2(s - m_new).astype(BF16)
            m_refs[h][...] = m_new
            vT = vT_ref[0, j, h * V_GROUP:(h + 1) * V_GROUP, :]
            acc_refs[h][...] = alpha * acc_refs[h][...] + jnp.dot(vT, p, preferred_element_type=F32)

    n_full = i - j_lo

    def body(n, carry):
        j = j_lo + 2 * n
        process([(j, False), (j + 1, False)])
        return carry

    lax.fori_loop(0, n_full // 2, body, 0)

    @pl.when(n_full % 2 == 1)
    def _():
        process([(i - 1, False), (i, True)])

    @pl.when(n_full % 2 == 0)
    def _():
        process([(i, True)])

    oT = jnp.concatenate([acc[0:HEAD_DIM, :] / acc[HEAD_DIM:HEAD_DIM + 1, :] for acc in (acc0[...], acc1[...])],
                         axis=0)
    o_ref[0] = (oT.T * gate_ref[0].astype(F32)).astype(BF16)


def _skip_table(qn, kn, c2r, tq):
    bsz, nq = qn.shape[0], qn.shape[1]
    qn = qn[:, :, :N_HEADS, 0].transpose(0, 2, 1) * NORM_SLACK
    kn = kn[:, :, 0, :N_HEADS].transpose(0, 2, 1) * NORM_SLACK
    c2 = c2r[:, :N_HEADS, :]
    c_first = c2[:, :, ::tq]
    c_last = c2[:, :, tq - 1::tq]
    upper = qn[..., :, None] * kn[..., None, :] + c_first[..., :, None] - c_last[..., None, :]
    lower = -(qn * kn)[..., :, None]
    skip = (upper - lower) < -SKIP_GAP_LOG2
    nk = skip.shape[-1]
    jlo = jnp.min(jnp.where(skip, nk, jnp.arange(nk, dtype=jnp.int32)), axis=-1)
    jlo = jnp.minimum(jlo, jnp.arange(nq, dtype=jnp.int32))
    jlo = jnp.min(jlo.reshape(bsz, N_HEADS // 2, 2, nq), axis=2)
    return jlo.reshape(-1).astype(jnp.int32)


def _attention(jlo, qa, ka, vT, gate, *, tq):
    bsz, hg, s = qa.shape
    nk, tk = vT.shape[1], vT.shape[3]
    da = N_HEADS * HEAD_DIM
    assert tq == tk
    pair = 2 * HEAD_GROUP
    hw = 2 * HEAD_DIM
    grid_spec = pltpu.PrefetchScalarGridSpec(
        num_scalar_prefetch=1,
        grid=(bsz, N_HEADS // 2, s // tq),
        in_specs=[
            pl.BlockSpec((1, pair, tq), lambda b, hp, i, jlo: (b, hp, i)),
            pl.BlockSpec((1, s, pair), lambda b, hp, i, jlo: (b, 0, hp)),
            pl.BlockSpec((1, nk, 2 * V_GROUP, tk), lambda b, hp, i, jlo: (b, 0, hp, 0)),
            pl.BlockSpec((1, tq, hw), lambda b, hp, i, jlo: (b, i, hp)),
        ],
        out_specs=pl.BlockSpec((1, tq, hw), lambda b, hp, i, jlo: (b, i, hp)),
        scratch_shapes=[pltpu.VMEM((1, tq), F32)] * 2 + [pltpu.VMEM((V_GROUP, tq), F32)] * 2,
    )
    return pl.pallas_call(
        _attn_kernel,
        grid_spec=grid_spec,
        out_shape=jax.ShapeDtypeStruct((bsz, s, da), BF16),
        compiler_params=pltpu.CompilerParams(dimension_semantics=("parallel", "parallel", "arbitrary")),
        name="fox_attention",
    )(jlo, qa, ka, vT, gate)


def _oproj_kernel(x_ref, o_ref, w_ref, out_ref):
    out_ref[...] = x_ref[...] + jnp.dot(o_ref[...], w_ref[...], preferred_element_type=F32)


def _oproj(x2d, o2d, w_o, *, tm):
    t, d = x2d.shape
    da = o2d.shape[1]
    return pl.pallas_call(
        _oproj_kernel,
        grid=(t // tm,),
        in_specs=[
            pl.BlockSpec((tm, d), lambda i: (i, 0)),
            pl.BlockSpec((tm, da), lambda i: (i, 0)),
            pl.BlockSpec((da, d), lambda i: (0, 0)),
        ],
        out_specs=pl.BlockSpec((tm, d), lambda i: (i, 0)),
        out_shape=jax.ShapeDtypeStruct((t, d), F32),
        compiler_params=pltpu.CompilerParams(dimension_semantics=("parallel",)),
        name="attn_out_proj",
    )(x2d, o2d, w_o.astype(BF16))


def _pack_bf16_pairs(x):
    half = x.shape[1] // 2
    lo = lax.bitcast_convert_type(x[:, :half].astype(BF16).astype(F32), jnp.uint32)
    hi = lax.bitcast_convert_type(x[:, half:].astype(BF16).astype(F32), jnp.uint32)
    word = (lo >> 16) | (hi & jnp.uint32(0xFFFF0000))
    return lax.bitcast_convert_type(word, jnp.int32)


def _unpack_bf16_pairs(w):
    u = lax.bitcast_convert_type(w, jnp.uint32)
    lo = lax.bitcast_convert_type(u << 16, F32)
    hi = lax.bitcast_convert_type(u & jnp.uint32(0xFFFF0000), F32)
    return jnp.concatenate([lo, hi], axis=1)


def _router_kernel(x_ref, g_ref, rtT_ref, hnp_ref, route_ref):
    tm = x_ref.shape[0]
    hn = _rms(x_ref[...], g_ref[...])
    hnp_ref[...] = _pack_bf16_pairs(hn)
    h0, h1, _ = _split3(hn)
    r0, r1, _ = _split3(rtT_ref[...])
    lt = lax.dot_general(jnp.concatenate([r0, r1, r0], axis=1), jnp.concatenate([h0, h0, h1], axis=1), NT_DIMS,
                         preferred_element_type=F32)[:N_EXPERTS]
    row = lax.broadcasted_iota(jnp.int32, (N_EXPERTS, tm), 0)
    m1 = jnp.max(lt, axis=0, keepdims=True)
    i1 = jnp.min(jnp.where(lt == m1, row, N_EXPERTS), axis=0, keepdims=True)
    rest = jnp.where(row == i1, -jnp.inf, lt)
    m2 = jnp.max(rest, axis=0, keepdims=True)
    i2 = jnp.min(jnp.where(rest == m2, row, N_EXPERTS), axis=0, keepdims=True)
    e2 = jnp.exp(m2 - m1)
    w1 = 1.0 / (1.0 + e2)
    w2 = e2 / (1.0 + e2)
    route_ref[...] = jnp.where(row == 0, i1.astype(F32), jnp.where(row == 1, i2.astype(F32),
                               jnp.where(row == 2, w1, jnp.where(row == 3, w2, 0.0))))


def _router(x2d, g, router, *, tm):
    t, d = x2d.shape
    ne = router.shape[1]
    rtT = jnp.pad(router.T, ((0, 2 * SUBLANES - ne), (0, 0)))
    return pl.pallas_call(
        _router_kernel,
        grid=(t // tm,),
        in_specs=[
            pl.BlockSpec((tm, d), lambda i: (i, 0)),
            pl.BlockSpec((1, d), lambda i: (0, 0)),
            pl.BlockSpec((2 * SUBLANES, d), lambda i: (0, 0)),
        ],
        out_specs=[pl.BlockSpec((tm, d // 2), lambda i: (i, 0)), pl.BlockSpec((ne, tm), lambda i: (0, i))],
        out_shape=[jax.ShapeDtypeStruct((t, d // 2), jnp.int32), jax.ShapeDtypeStruct((ne, t), F32)],
        compiler_params=pltpu.CompilerParams(dimension_semantics=("parallel",)),
        name="moe_router",
    )(x2d, g.reshape(1, d), rtT)


def _routing_tables(route, ts):
    ne = N_EXPERTS
    t = route.shape[1]
    e_pair = route[0:2].astype(jnp.int32).T.reshape(-1)
    w_pair = route[2:4].T.reshape(-1)
    n_tiles = (2 * t) // ts + ne
    onehot = (e_pair[:, None] == jnp.arange(ne, dtype=jnp.int32)[None, :]).astype(jnp.int32)
    csum = jnp.cumsum(onehot, axis=0)
    rank = jnp.sum(onehot * csum, axis=1) - 1
    tiles = (csum[-1] + ts - 1) // ts
    tile_end = jnp.cumsum(tiles)
    tile_start = tile_end - tiles
    slot = jnp.sum(onehot * tile_start[None, :], axis=1) * ts + rank
    n_used = tile_end[-1]
    tile_ids = jnp.arange(n_tiles, dtype=jnp.int32)
    tile_expert = jnp.sum((tile_ids[:, None] >= tile_end[None, :]).astype(jnp.int32), axis=1)
    last_expert = jnp.sum(((n_used - 1) >= tile_end).astype(jnp.int32))
    tile_expert = jnp.minimum(tile_expert, last_expert).astype(jnp.int32)
    tok_of_slot = jnp.zeros((n_tiles * ts,), jnp.int32).at[slot].set(jnp.arange(2 * t, dtype=jnp.int32) // 2)
    w_slot = jnp.zeros((n_tiles * ts,), F32).at[slot].set(w_pair)
    return slot.astype(jnp.int32), tok_of_slot, w_slot, tile_expert, n_used.reshape(1).astype(jnp.int32)


def _gather_rows(table, idx):
    m = idx.shape[0]
    d = table.shape[1]
    assert m % SC_GATHER_WINDOW == 0
    mesh = plsc.VectorSubcoreMesh(core_axis_name="core", subcore_axis_name="subcore")

    @pl.kernel(out_type=jax.ShapeDtypeStruct((m, d), table.dtype), mesh=mesh)
    def gather_kernel(x_hbm, i_hbm, o_hbm):
        def body(i_vmem, o_vmem):
            pltpu.sync_copy(x_hbm.at[i_vmem.at[0]], o_vmem)

        pltpu.emit_pipeline(
            body,
            grid=(m // SC_GATHER_WINDOW,),
            in_specs=[pl.BlockSpec((1, SC_GATHER_WINDOW), index_map=lambda i: (0, i))],
            out_specs=[pl.BlockSpec((SC_GATHER_WINDOW, d), index_map=lambda i: (i, 0))],
            core_axis_name=("core", "subcore"),
            dimension_semantics=(pltpu.PARALLEL,),
        )(i_hbm, o_hbm)

    return gather_kernel(table, idx.reshape(1, m))


def _gather_packed_rows(packed, idx):
    n, w = packed.shape
    idx2 = jnp.stack([2 * idx, 2 * idx + 1], axis=1).reshape(-1)
    return _gather_rows(packed.reshape(2 * n, w // 2), idx2).reshape(idx.shape[0], w)


def _expert_kernel(te_ref, nu_ref, xs_ref, ws_ref, wg_ref, wu_ref, wd_ref, ys_ref, x_sc, acc_ref):
    i = pl.program_id(0)
    k = pl.program_id(1)
    last_k = pl.num_programs(1) - 1
    used = i < nu_ref[0]

    @pl.when(used & (k == 0))
    def _():
        x_sc[...] = _unpack_bf16_pairs(xs_ref[...]).astype(BF16)
        acc_ref[...] = jnp.zeros_like(acc_ref)

    @pl.when(used)
    def _():
        x = x_sc[...]
        gate = jnp.dot(x, wg_ref[0], preferred_element_type=F32)
        up = jnp.dot(x, wu_ref[0], preferred_element_type=F32)
        h = (_silu(gate) * up).astype(BF16)
        acc_ref[...] += jnp.dot(h, wd_ref[0], preferred_element_type=F32)

    @pl.when(used & (k == last_k))
    def _():
        ys_ref[...] = _pack_bf16_pairs(acc_ref[...] * ws_ref[...])

    @pl.when(jnp.logical_not(used) & (k == last_k))
    def _():
        ys_ref[...] = jnp.zeros_like(ys_ref)


def _experts(xs, w_slot, tile_expert, n_used, w_gu, w_down, *, ts, tf):
    n_slots, half = xs.shape
    d = 2 * half
    dexp = w_down.shape[1]
    nk = dexp // tf
    w_gu = w_gu.astype(BF16)
    grid_spec = pltpu.PrefetchScalarGridSpec(
        num_scalar_prefetch=2,
        grid=(n_slots // ts, nk),
        in_specs=[
            pl.BlockSpec((ts, half), lambda i, k, te, nu: (i, 0)),
            pl.BlockSpec((ts, 1), lambda i, k, te, nu: (i, 0)),
            pl.BlockSpec((1, d, tf), lambda i, k, te, nu: (te[i], 0, k)),
            pl.BlockSpec((1, d, tf), lambda i, k, te, nu: (te[i], 0, nk + k)),
            pl.BlockSpec((1, tf, d), lambda i, k, te, nu: (te[i], k, 0)),
        ],
        out_specs=pl.BlockSpec((ts, half), lambda i, k, te, nu: (i, 0)),
        scratch_shapes=[pltpu.VMEM((ts, d), BF16), pltpu.VMEM((ts, d), F32)],
    )
    return pl.pallas_call(
        _expert_kernel,
        grid_spec=grid_spec,
        out_shape=jax.ShapeDtypeStruct((n_slots, half), jnp.int32),
        compiler_params=pltpu.CompilerParams(dimension_semantics=("arbitrary", "arbitrary")),
        name="moe_experts",
    )(tile_expert, n_used, xs, w_slot.reshape(n_slots, 1), w_gu, w_gu, w_down.astype(BF16))


def _combine_kernel(x_ref, y_ref, gf_ref, o_ref):
    half = y_ref.shape[1] // 2
    y = y_ref[...]
    moe = _unpack_bf16_pairs(y[:, :half]) + _unpack_bf16_pairs(y[:, half:])
    o_ref[...] = _rms(x_ref[...] + moe, gf_ref[...])


def _combine(x2d, y_pairs, g_final, *, tm):
    t, d = x2d.shape
    return pl.pallas_call(
        _combine_kernel,
        grid=(t // tm,),
        in_specs=[
            pl.BlockSpec((tm, d), lambda i: (i, 0)),
            pl.BlockSpec((tm, d), lambda i: (i, 0)),
            pl.BlockSpec((1, d), lambda i: (0, 0)),
        ],
        out_specs=pl.BlockSpec((tm, d), lambda i: (i, 0)),
        out_shape=jax.ShapeDtypeStruct((t, d), F32),
        compiler_params=pltpu.CompilerParams(dimension_semantics=("parallel",)),
        name="moe_combine",
    )(x2d, y_pairs, g_final.reshape(1, d))


def _moe(x2d, g, router, w_gu, w_down, g_final, *, tm, ts, tf):
    t, d = x2d.shape
    hn_packed, route = _router(x2d, g, router, tm=tm)
    slot, tok_of_slot, w_slot, tile_expert, n_used = _routing_tables(route, ts)
    xs = _gather_packed_rows(hn_packed, tok_of_slot)
    ys = _experts(xs, w_slot, tile_expert, n_used, w_gu, w_down, ts=ts, tf=tf)
    y_pairs = _gather_packed_rows(ys, slot).reshape(t, d)
    return _combine(x2d, y_pairs, g_final, tm=tm)


def _tile(n, pref):
    t = min(n, pref)
    assert n % t == 0
    return t


def kernel(x, norm_g, lru_w_in, lru_conv_w, lru_conv_b, lru_gate_w, lru_gate_b, lru_lambda, lru_w_out,
           kv_norm_g, w_kvf, b_f, fox_w_qg, fox_w_o, mlp_w_gu, mlp_w_down, moe_router, moe_w_gu,
           moe_w_down, final_norm_g):
    bsz, s, d = x.shape
    t = bsz * s
    tm = _tile(t, 512)
    ts = _tile(s, 512)

    x = _lru_layer(x, norm_g[0, 0], lru_w_in[0], lru_conv_w[0], lru_conv_b[0], lru_gate_w[0],
                   lru_gate_b[0], lru_lambda[0], lru_w_out[0], tt=_tile(s, 256))
    dff = mlp_w_down.shape[1]
    x = _mlp(x.reshape(t, d), norm_g[0, 1], mlp_w_gu[0], mlp_w_down[0], tm=tm, tf=dff // 2).reshape(bsz, s, d)

    consts = _head_selectors()
    ka, vT, c2r, kn = _kv_proj(x, kv_norm_g, w_kvf, b_f, consts, tk=ts)
    qa, gate, qn = _q_proj(x, norm_g[1, 0], fox_w_qg[0], c2r, consts, tq=ts)
    jlo = _skip_table(qn, kn, c2r, ts)
    o = _attention(jlo, qa, ka, vT, gate, tq=ts)
    x2d = _oproj(x.reshape(t, d), o.reshape(t, -1), fox_w_o[0], tm=tm)

    dexp = moe_w_down.shape[2]
    out = _moe(x2d, norm_g[1, 1], moe_router[0], moe_w_gu[0], moe_w_down[0], final_norm_g, tm=tm, ts=tm,
               tf=dexp // 2)
    return out.reshape(bsz, s, d)
```

```python
import functools
import math

import jax
import jax.numpy as jnp
from jax import lax
from jax.experimental import pallas as pl
from jax.experimental.pallas import tpu as pltpu
from jax.experimental.pallas import tpu_sc as plsc

EPS = 1e-6
LRU_C = 8.0
N_LRU_BLOCKS = 8
CONV_W = 4
N_HEADS = 16
HEAD_DIM = 64
N_EXPERTS = 8
LANES = 128
SUBLANES = 8
HEAD_GROUP = 128
BIAS_Q = HEAD_DIM
BIAS_K = HEAD_DIM + 3
V_GROUP = 80
NEG_BIG = -1e30
LOG2E = math.log2(math.e)
SKIP_GAP_LOG2 = 160.0
NORM_SLACK = 1.01
SC_ROW_WINDOW = 128

F32 = jnp.float32
BF16 = jnp.bfloat16
NT_DIMS = (((1,), (1,)), ((), ()))


def _rms(x, g):
    ms = jnp.mean(x * x, axis=-1, keepdims=True)
    return x * lax.rsqrt(ms + EPS) * g


def _log_sigmoid(x):
    return jnp.minimum(x, 0.0) - jnp.log1p(jnp.exp(-jnp.abs(x)))


def _gelu_tanh(x):
    c = 0.7978845608028654
    return 0.5 * x * (1.0 + jnp.tanh(c * (x + 0.044715 * (x * x * x))))


def _silu(x):
    return x * jax.nn.sigmoid(x)


def _split3(x):
    p0 = x.astype(BF16)
    r1 = x - p0.astype(F32)
    p1 = r1.astype(BF16)
    p2 = (r1 - p1.astype(F32)).astype(BF16)
    return p0, p1, p2


def _const_spec(shape, n_grid):
    zeros = (0,) * len(shape)
    if n_grid == 1:
        return pl.BlockSpec(shape, lambda i: zeros)
    if n_grid == 2:
        return pl.BlockSpec(shape, lambda i, j: zeros)
    return pl.BlockSpec(shape, lambda i, j, k: zeros)


def _lru_kernel(x_ref, g_ref, win_ref, cw_ref, cb_ref, gw_ref, gb_ref, lam_ref, wout_ref,
                o_ref, xbuf, hcar):
    t = pl.program_id(1)
    tt = x_ref.shape[1]
    d = x_ref.shape[2]
    bw = d // N_LRU_BLOCKS

    @pl.when(t == 0)
    def _():
        xbuf[0:SUBLANES, :] = jnp.zeros((SUBLANES, d), F32)
        hcar[...] = jnp.zeros_like(hcar)

    x = x_ref[0]
    xn = _rms(x, g_ref[...]).astype(BF16)
    xg = jnp.dot(xn, win_ref[...], preferred_element_type=F32)
    xb = xg[:, :d]
    gbr = xg[:, d:]

    xbuf[SUBLANES:SUBLANES + tt, :] = xb
    cw = cw_ref[...]
    xc = cb_ref[...] + cw[CONV_W - 1:CONV_W, :] * xb
    for j in range(CONV_W - 1):
        xc = xc + cw[j:j + 1, :] * xbuf[pl.ds(SUBLANES - (CONV_W - 1) + j, tt), :]
    xbuf[0:SUBLANES, :] = xbuf[tt:tt + SUBLANES, :]

    xcb = xc.astype(BF16)
    rl, il = [], []
    for n in range(N_LRU_BLOCKS):
        gl = jnp.dot(xcb[:, n * bw:(n + 1) * bw], gw_ref[n], preferred_element_type=F32)
        rl.append(gl[:, :bw])
        il.append(gl[:, bw:])
    gb = gb_ref[...]
    r = jax.nn.sigmoid(jnp.concatenate(rl, axis=1) + gb[0:1, :])
    i = jax.nn.sigmoid(jnp.concatenate(il, axis=1) + gb[1:2, :])
    log_a = LRU_C * r * _log_sigmoid(lam_ref[...])
    a = jnp.exp(log_a)
    th = jnp.tanh(log_a)
    b = jnp.sqrt(-2.0 * th / (1.0 - th)) * (i * xc)

    row = lax.broadcasted_iota(jnp.int32, (tt, 1), 0)
    sh = 1
    while sh < tt:
        keep = row >= sh
        a_sh = jnp.where(keep, pltpu.roll(a, sh, 0), 1.0)
        b_sh = jnp.where(keep, pltpu.roll(b, sh, 0), 0.0)
        b = a * b_sh + b
        a = a * a_sh
        sh *= 2
    h = b + a * hcar[...]
    hcar[...] = h[tt - 1:tt, :]

    y = (_gelu_tanh(gbr) * h).astype(BF16)
    o_ref[0] = x + jnp.dot(y, wout_ref[...], preferred_element_type=F32)


def _lru_layer(x, g, w_in, conv_w, conv_b, gate_w, gate_b, lam, w_out, *, tt):
    bsz, s, d = x.shape
    bw = d // N_LRU_BLOCKS
    gw = jnp.concatenate([gate_w[0], gate_w[1]], axis=-1).astype(BF16)
    const = lambda *shape: _const_spec(shape, 2)
    return pl.pallas_call(
        _lru_kernel,
        grid=(bsz, s // tt),
        in_specs=[
            pl.BlockSpec((1, tt, d), lambda b, t: (b, t, 0)),
            const(1, d), const(d, 2 * d), const(CONV_W, d), const(1, d),
            const(N_LRU_BLOCKS, bw, 2 * bw), const(2, d), const(1, d), const(d, d),
        ],
        out_specs=pl.BlockSpec((1, tt, d), lambda b, t: (b, t, 0)),
        out_shape=jax.ShapeDtypeStruct((bsz, s, d), F32),
        scratch_shapes=[pltpu.VMEM((tt + 2 * SUBLANES, d), F32), pltpu.VMEM((1, d), F32)],
        compiler_params=pltpu.CompilerParams(dimension_semantics=("parallel", "arbitrary")),
        name="lru_layer",
    )(x, g.reshape(1, d), w_in.astype(BF16), conv_w, conv_b.reshape(1, d), gw, gate_b,
      lam.reshape(1, d), w_out.astype(BF16))


def _mlp_kernel(x_ref, g_ref, wg_ref, wu_ref, wd_ref, o_ref, hn_ref, acc_ref):
    k = pl.program_id(1)

    @pl.when(k == 0)
    def _():
        hn_ref[...] = _rms(x_ref[...], g_ref[...]).astype(BF16)
        acc_ref[...] = jnp.zeros_like(acc_ref)

    hn = hn_ref[...]
    gate = jnp.dot(hn, wg_ref[...], preferred_element_type=F32)
    up = jnp.dot(hn, wu_ref[...], preferred_element_type=F32)
    h = (_silu(gate) * up).astype(BF16)
    acc_ref[...] += jnp.dot(h, wd_ref[...], preferred_element_type=F32)

    @pl.when(k == pl.num_programs(1) - 1)
    def _():
        o_ref[...] = x_ref[...] + acc_ref[...]


def _mlp(x2d, g, w_gu, w_down, *, tm, tf):
    t, d = x2d.shape
    dff = w_down.shape[0]
    nk = dff // tf
    w_gu = w_gu.astype(BF16)
    return pl.pallas_call(
        _mlp_kernel,
        grid=(t // tm, nk),
        in_specs=[
            pl.BlockSpec((tm, d), lambda i, k: (i, 0)),
            pl.BlockSpec((1, d), lambda i, k: (0, 0)),
            pl.BlockSpec((d, tf), lambda i, k: (0, k)),
            pl.BlockSpec((d, tf), lambda i, k: (0, nk + k)),
            pl.BlockSpec((tf, d), lambda i, k: (k, 0)),
        ],
        out_specs=pl.BlockSpec((tm, d), lambda i, k: (i, 0)),
        out_shape=jax.ShapeDtypeStruct((t, d), F32),
        scratch_shapes=[pltpu.VMEM((tm, d), BF16), pltpu.VMEM((tm, d), F32)],
        compiler_params=pltpu.CompilerParams(dimension_semantics=("parallel", "arbitrary")),
        name="mlp",
    )(x2d, g.reshape(1, d), w_gu, w_gu, w_down.astype(BF16))


def _head_selectors():
    hg = N_HEADS * HEAD_GROUP
    head = jnp.arange(hg) // HEAD_GROUP
    lane = jnp.arange(hg) % HEAD_GROUP
    col = jnp.arange(LANES)
    def sel(base, p):
        return ((col[:, None] == head[None, :]) & (lane[None, :] == base + p)).astype(BF16)
    sel_k = jnp.concatenate([-sel(BIAS_K, p) for p in range(3)], axis=0)
    sel_q = jnp.concatenate([sel(BIAS_Q, p).T for p in range(3)], axis=1)
    ones_k = ((lane >= BIAS_Q) & (lane < BIAS_Q + 3)).astype(F32).reshape(1, hg)
    ones_q = ((lane >= BIAS_K) & (lane < BIAS_K + 3)).astype(F32).reshape(hg, 1)
    head_sum = ((col[None, :] == head[:, None]) & (lane[:, None] < HEAD_DIM)).astype(BF16)
    return sel_k, sel_q, ones_k, ones_q, head_sum


def _kv_kernel(x_ref, g_ref, wk_ref, wvT_ref, onesv_ref, wf_ref, wfT_ref, bfr_ref, bfc_ref, selk_ref, onesk_ref,
               hsum_ref, ka_ref, vT_ref, c2r_ref, kn_ref, car_r, car_c):
    t = pl.program_id(1)
    tk = x_ref.shape[1]

    @pl.when(t == 0)
    def _():
        car_r[...] = jnp.zeros_like(car_r)
        car_c[...] = jnp.zeros_like(car_c)

    xn = _rms(x_ref[0], g_ref[...]).astype(BF16)
    vT = lax.dot_general(wvT_ref[...], xn, NT_DIMS, preferred_element_type=F32) + onesv_ref[...]
    vT_ref[0, 0] = vT.astype(BF16)
    kf = jnp.dot(xn, wk_ref[...], preferred_element_type=F32)

    ksq = kf * kf
    ksq_hi = ksq.astype(BF16)
    ksq_lo = (ksq - ksq_hi.astype(F32)).astype(BF16)
    hs = hsum_ref[...]
    kss = jnp.dot(ksq_hi, hs, preferred_element_type=F32) + jnp.dot(ksq_lo, hs, preferred_element_type=F32)
    kn = jnp.sqrt(jnp.max(kss, axis=0, keepdims=True))
    kn_ref[0, 0] = jnp.broadcast_to(kn, (SUBLANES, LANES))

    ri = lax.broadcasted_iota(jnp.int32, (tk, tk), 0)
    ci = lax.broadcasted_iota(jnp.int32, (tk, tk), 1)
    upper = (ri <= ci).astype(BF16)
    lower = (ci <= ri).astype(BF16)

    lf_c = _log_sigmoid(jnp.dot(xn, wf_ref[...], preferred_element_type=F32) + bfr_ref[...])
    cs3 = jnp.dot(lower, jnp.concatenate(_split3(lf_c), axis=1), preferred_element_type=F32)
    cs_c = car_c[...] + (cs3[:, :LANES] + cs3[:, LANES:2 * LANES] + cs3[:, 2 * LANES:])
    car_c[...] = cs_c[tk - 1:tk, :]
    pieces = jnp.concatenate(_split3(cs_c * LOG2E), axis=1)
    bias = jnp.dot(pieces, selk_ref[...], preferred_element_type=F32)
    ka_ref[0] = (kf + bias + onesk_ref[...]).astype(BF16)

    lf_r = _log_sigmoid(lax.dot_general(wfT_ref[...], xn, NT_DIMS, preferred_element_type=F32) + bfc_ref[...])
    cs3 = jnp.dot(jnp.concatenate(_split3(lf_r), axis=0), upper, preferred_element_type=F32)
    cs_r = car_r[...] + (cs3[:LANES] + cs3[LANES:2 * LANES] + cs3[2 * LANES:])
    car_r[...] = cs_r[:, tk - 1:tk]
    c2r_ref[0] = cs_r * LOG2E


def _kv_proj(x, g, w_kvf, b_f, consts, *, tk):
    bsz, s, d = x.shape
    da = N_HEADS * HEAD_DIM
    hg = N_HEADS * HEAD_GROUP
    dv = N_HEADS * V_GROUP
    nk = s // tk
    sel_k, _, ones_k, _, head_sum = consts
    wk = w_kvf[:, :da].reshape(d, N_HEADS, HEAD_DIM)
    wk = jnp.pad(wk, ((0, 0), (0, 0), (0, HEAD_GROUP - HEAD_DIM))).reshape(d, hg).astype(BF16)
    wv = w_kvf[:, da:2 * da].reshape(d, N_HEADS, HEAD_DIM)
    wvT = jnp.pad(wv, ((0, 0), (0, 0), (0, V_GROUP - HEAD_DIM))).reshape(d, dv).T.astype(BF16)
    ones_v = (jnp.arange(dv) % V_GROUP == HEAD_DIM).astype(F32).reshape(dv, 1)
    wf = jnp.pad(w_kvf[:, 2 * da:], ((0, 0), (0, LANES - N_HEADS))).astype(BF16)
    bfr = jnp.pad(b_f, (0, LANES - N_HEADS)).reshape(1, LANES)
    const = lambda *shape: _const_spec(shape, 2)
    return pl.pallas_call(
        _kv_kernel,
        grid=(bsz, nk),
        in_specs=[
            pl.BlockSpec((1, tk, d), lambda b, t: (b, t, 0)),
            const(1, d), const(d, hg), const(dv, d), const(dv, 1), const(d, LANES), const(LANES, d),
            const(1, LANES), const(LANES, 1), const(3 * LANES, hg), const(1, hg), const(hg, LANES),
        ],
        out_specs=[
            pl.BlockSpec((1, tk, hg), lambda b, t: (b, t, 0)),
            pl.BlockSpec((1, 1, dv, tk), lambda b, t: (b, t, 0, 0)),
            pl.BlockSpec((1, LANES, tk), lambda b, t: (b, 0, t)),
            pl.BlockSpec((1, 1, SUBLANES, LANES), lambda b, t: (b, t, 0, 0)),
        ],
        out_shape=[
            jax.ShapeDtypeStruct((bsz, s, hg), BF16),
            jax.ShapeDtypeStruct((bsz, nk, dv, tk), BF16),
            jax.ShapeDtypeStruct((bsz, LANES, s), F32),
            jax.ShapeDtypeStruct((bsz, nk, SUBLANES, LANES), F32),
        ],
        scratch_shapes=[pltpu.VMEM((LANES, 1), F32), pltpu.VMEM((1, LANES), F32)],
        compiler_params=pltpu.CompilerParams(dimension_semantics=("parallel", "arbitrary")),
        name="kv_proj",
    )(x, g.reshape(1, d), wk, wvT, ones_v, wf, wf.T, bfr, bfr.reshape(LANES, 1), sel_k, ones_k, head_sum)


def _q_kernel(x_ref, g_ref, wqT_ref, wg_ref, c2r_ref, selq_ref, onesq_ref, hsumT_ref, qa_ref, gate_ref, qn_ref):
    xn = _rms(x_ref[0], g_ref[...]).astype(BF16)
    gate_ref[0] = jax.nn.sigmoid(jnp.dot(xn, wg_ref[...], preferred_element_type=F32)).astype(BF16)
    qf = lax.dot_general(wqT_ref[...], xn, NT_DIMS, preferred_element_type=F32) * (LOG2E * HEAD_DIM ** -0.5)

    qsq = qf * qf
    qsq_hi = qsq.astype(BF16)
    qsq_lo = (qsq - qsq_hi.astype(F32)).astype(BF16)
    hsT = hsumT_ref[...]
    qss = jnp.dot(hsT, qsq_hi, preferred_element_type=F32) + jnp.dot(hsT, qsq_lo, preferred_element_type=F32)
    qn = jnp.sqrt(jnp.max(qss, axis=1, keepdims=True))
    qn_ref[0, 0] = jnp.broadcast_to(qn, (LANES, LANES))

    pieces = jnp.concatenate(_split3(c2r_ref[0]), axis=0)
    bias = jnp.dot(selq_ref[...], pieces, preferred_element_type=F32)
    qa_ref[0] = (qf + bias + onesq_ref[...]).astype(BF16)


def _q_proj(x, g, w_qg, c2r, consts, *, tq):
    bsz, s, d = x.shape
    da = N_HEADS * HEAD_DIM
    hg = N_HEADS * HEAD_GROUP
    nq = s // tq
    _, sel_q, _, ones_q, head_sum = consts
    wq = w_qg[:, :da].reshape(d, N_HEADS, HEAD_DIM)
    wqT = jnp.pad(wq, ((0, 0), (0, 0), (0, HEAD_GROUP - HEAD_DIM))).reshape(d, hg).T.astype(BF16)
    const = lambda *shape: _const_spec(shape, 2)
    return pl.pallas_call(
        _q_kernel,
        grid=(bsz, nq),
        in_specs=[
            pl.BlockSpec((1, tq, d), lambda b, t: (b, t, 0)),
            const(1, d), const(hg, d), const(d, da),
            pl.BlockSpec((1, LANES, tq), lambda b, t: (b, 0, t)),
            const(hg, 3 * LANES), const(hg, 1), const(LANES, hg),
        ],
        out_specs=[
            pl.BlockSpec((1, hg, tq), lambda b, t: (b, 0, t)),
            pl.BlockSpec((1, tq, da), lambda b, t: (b, t, 0)),
            pl.BlockSpec((1, 1, LANES, LANES), lambda b, t: (b, t, 0, 0)),
        ],
        out_shape=[
            jax.ShapeDtypeStruct((bsz, hg, s), BF16),
            jax.ShapeDtypeStruct((bsz, s, da), BF16),
            jax.ShapeDtypeStruct((bsz, nq, LANES, LANES), F32),
        ],
        compiler_params=pltpu.CompilerParams(dimension_semantics=("parallel", "parallel")),
        name="q_proj",
    )(x, g.reshape(1, d), wqT, w_qg[:, da:].astype(BF16), c2r, sel_q, ones_q, head_sum.T)


def _attn_kernel(jlo_ref, qa_ref, ka_ref, vT_ref, gate_ref, o_ref, m0, m1, acc0, acc1):
    b = pl.program_id(0)
    hp = pl.program_id(1)
    i = pl.program_id(2)
    tq = qa_ref.shape[2]
    tk = vT_ref.shape[3]
    j_lo = jlo_ref[(b * pl.num_programs(1) + hp) * pl.num_programs(2) + i]
    m_refs, acc_refs = (m0, m1), (acc0, acc1)

    for h in range(2):
        m_refs[h][...] = jnp.full((1, tq), NEG_BIG, F32)
        acc_refs[h][...] = jnp.zeros((V_GROUP, tq), F32)

    def process(blocks):
        units = [(j, masked, h) for (j, masked) in blocks for h in range(2)]
        scores = []
        for j, _, h in units:
            k_blk = ka_ref[0, pl.ds(pl.multiple_of(j * tk, tk), tk), h * HEAD_GROUP:(h + 1) * HEAD_GROUP]
            qT = qa_ref[0, h * HEAD_GROUP:(h + 1) * HEAD_GROUP, :]
            scores.append(jnp.dot(k_blk, qT, preferred_element_type=F32))
        for (j, masked, h), s in zip(units, scores):
            if masked:
                kv_pos = lax.broadcasted_iota(jnp.int32, (tk, tq), 0)
                q_pos = lax.broadcasted_iota(jnp.int32, (tk, tq), 1)
                s = jnp.where(kv_pos <= q_pos, s, NEG_BIG)
            m_old = m_refs[h][...]
            m_new = jnp.maximum(m_old, jnp.max(s, axis=0, keepdims=True))
            alpha = jnp.exp2(m_old - m_new)
            p = jnp.exp2(s - m_new).astype(BF16)
            m_refs[h][...] = m_new
            vT = vT_ref[0, j, h * V_GROUP:(h + 1) * V_GROUP, :]
            acc_refs[h][...] = alpha * acc_refs[h][...] + jnp.dot(vT, p, preferred_element_type=F32)

    n_full = i - j_lo

    def body(n, carry):
        j = j_lo + 2 * n
        process([(j, False), (j + 1, False)])
        return carry

    lax.fori_loop(0, n_full // 2, body, 0)

    @pl.when(n_full % 2 == 1)
    def _():
        process([(i - 1, False), (i, True)])

    @pl.when(n_full % 2 == 0)
    def _():
        process([(i, True)])

    oT = jnp.concatenate([acc[0:HEAD_DIM, :] / acc[HEAD_DIM:HEAD_DIM + 1, :] for acc in (acc0[...], acc1[...])],
                         axis=0)
    o_ref[0] = (oT.T * gate_ref[0].astype(F32)).astype(BF16)


def _skip_table(qn, kn, c2r, tq):
    bsz, nq = qn.shape[0], qn.shape[1]
    qn = qn[:, :, :N_HEADS, 0].transpose(0, 2, 1) * NORM_SLACK
    kn = kn[:, :, 0, :N_HEADS].transpose(0, 2, 1) * NORM_SLACK
    c2 = c2r[:, :N_HEADS, :]
    c_first = c2[:, :, ::tq]
    c_last = c2[:, :, tq - 1::tq]
    upper = qn[..., :, None] * kn[..., None, :] + c_first[..., :, None] - c_last[..., None, :]
    lower = -(qn * kn)[..., :, None]
    skip = (upper - lower) < -SKIP_GAP_LOG2
    nk = skip.shape[-1]
    jlo = jnp.min(jnp.where(skip, nk, jnp.arange(nk, dtype=jnp.int32)), axis=-1)
    jlo = jnp.minimum(jlo, jnp.arange(nq, dtype=jnp.int32))
    jlo = jnp.min(jlo.reshape(bsz, N_HEADS // 2, 2, nq), axis=2)
    return jlo.reshape(-1).astype(jnp.int32)


def _attention(jlo, qa, ka, vT, gate, *, tq):
    bsz, hg, s = qa.shape
    nk, tk = vT.shape[1], vT.shape[3]
    da = N_HEADS * HEAD_DIM
    assert tq == tk
    pair = 2 * HEAD_GROUP
    hw = 2 * HEAD_DIM
    grid_spec = pltpu.PrefetchScalarGridSpec(
        num_scalar_prefetch=1,
        grid=(bsz, N_HEADS // 2, s // tq),
        in_specs=[
            pl.BlockSpec((1, pair, tq), lambda b, hp, i, jlo: (b, hp, i)),
            pl.BlockSpec((1, s, pair), lambda b, hp, i, jlo: (b, 0, hp)),
            pl.BlockSpec((1, nk, 2 * V_GROUP, tk), lambda b, hp, i, jlo: (b, 0, hp, 0)),
            pl.BlockSpec((1, tq, hw), lambda b, hp, i, jlo: (b, i, hp)),
        ],
        out_specs=pl.BlockSpec((1, tq, hw), lambda b, hp, i, jlo: (b, i, hp)),
        scratch_shapes=[pltpu.VMEM((1, tq), F32)] * 2 + [pltpu.VMEM((V_GROUP, tq), F32)] * 2,
    )
    return pl.pallas_call(
        _attn_kernel,
        grid_spec=grid_spec,
        out_shape=jax.ShapeDtypeStruct((bsz, s, da), BF16),
        compiler_params=pltpu.CompilerParams(dimension_semantics=("parallel", "parallel", "arbitrary")),
        name="fox_attention",
    )(jlo, qa, ka, vT, gate)


def _oproj_kernel(x_ref, o_ref, w_ref, out_ref):
    out_ref[...] = x_ref[...] + jnp.dot(o_ref[...], w_ref[...], preferred_element_type=F32)


def _oproj(x2d, o2d, w_o, *, tm):
    t, d = x2d.shape
    da = o2d.shape[1]
    return pl.pallas_call(
        _oproj_kernel,
        grid=(t // tm,),
        in_specs=[
            pl.BlockSpec((tm, d), lambda i: (i, 0)),
            pl.BlockSpec((tm, da), lambda i: (i, 0)),
            pl.BlockSpec((da, d), lambda i: (0, 0)),
        ],
        out_specs=pl.BlockSpec((tm, d), lambda i: (i, 0)),
        out_shape=jax.ShapeDtypeStruct((t, d), F32),
        compiler_params=pltpu.CompilerParams(dimension_semantics=("parallel",)),
        name="attn_out_proj",
    )(x2d, o2d, w_o.astype(BF16))


def _pack_bf16_pairs(x):
    half = x.shape[1] // 2
    lo = lax.bitcast_convert_type(x[:, :half].astype(BF16).astype(F32), jnp.uint32)
    hi = lax.bitcast_convert_type(x[:, half:].astype(BF16).astype(F32), jnp.uint32)
    word = (lo >> 16) | (hi & jnp.uint32(0xFFFF0000))
    return lax.bitcast_convert_type(word, jnp.int32)


def _unpack_bf16_pairs(w):
    u = lax.bitcast_convert_type(w, jnp.uint32)
    lo = lax.bitcast_convert_type(u << 16, F32)
    hi = lax.bitcast_convert_type(u & jnp.uint32(0xFFFF0000), F32)
    return jnp.concatenate([lo, hi], axis=1)


def _store_planes(ref, words):
    q = words.shape[1] // 2
    ref[0] = words[:, :q]
    ref[1] = words[:, q:]


def _load_planes(ref, base=0):
    return jnp.concatenate([ref[base], ref[base + 1]], axis=1)


def _router_kernel(x_ref, g_ref, rtT_ref, hnp_ref, route_ref):
    tm = x_ref.shape[0]
    hn = _rms(x_ref[...], g_ref[...])
    _store_planes(hnp_ref, _pack_bf16_pairs(hn))
    h0, h1, _ = _split3(hn)
    r0, r1, _ = _split3(rtT_ref[...])
    lt = lax.dot_general(jnp.concatenate([r0, r1, r0], axis=1), jnp.concatenate([h0, h0, h1], axis=1), NT_DIMS,
                         preferred_element_type=F32)[:N_EXPERTS]
    row = lax.broadcasted_iota(jnp.int32, (N_EXPERTS, tm), 0)
    m1 = jnp.max(lt, axis=0, keepdims=True)
    i1 = jnp.min(jnp.where(lt == m1, row, N_EXPERTS), axis=0, keepdims=True)
    rest = jnp.where(row == i1, -jnp.inf, lt)
    m2 = jnp.max(rest, axis=0, keepdims=True)
    i2 = jnp.min(jnp.where(rest == m2, row, N_EXPERTS), axis=0, keepdims=True)
    e2 = jnp.exp(m2 - m1)
    w1 = 1.0 / (1.0 + e2)
    w2 = e2 / (1.0 + e2)
    route_ref[...] = jnp.where(row == 0, i1.astype(F32), jnp.where(row == 1, i2.astype(F32),
                               jnp.where(row == 2, w1, jnp.where(row == 3, w2, 0.0))))


def _router(x2d, g, router, *, tm):
    t, d = x2d.shape
    ne = router.shape[1]
    rtT = jnp.pad(router.T, ((0, 2 * SUBLANES - ne), (0, 0)))
    return pl.pallas_call(
        _router_kernel,
        grid=(t // tm,),
        in_specs=[
            pl.BlockSpec((tm, d), lambda i: (i, 0)),
            pl.BlockSpec((1, d), lambda i: (0, 0)),
            pl.BlockSpec((2 * SUBLANES, d), lambda i: (0, 0)),
        ],
        out_specs=[pl.BlockSpec((2, tm, d // 4), lambda i: (0, i, 0)), pl.BlockSpec((ne, tm), lambda i: (0, i))],
        out_shape=[jax.ShapeDtypeStruct((2, t, d // 4), jnp.int32), jax.ShapeDtypeStruct((ne, t), F32)],
        compiler_params=pltpu.CompilerParams(dimension_semantics=("parallel",)),
        name="moe_router",
    )(x2d, g.reshape(1, d), rtT)


def _routing_tables(route, ts):
    ne = N_EXPERTS
    t = route.shape[1]
    e_pair = route[0:2].astype(jnp.int32).reshape(-1)
    n_tiles = (2 * t) // ts + ne
    n_slots = n_tiles * ts
    onehot = (e_pair[:, None] == jnp.arange(ne, dtype=jnp.int32)[None, :]).astype(jnp.int32)
    csum = jnp.cumsum(onehot, axis=0)
    rank = jnp.sum(onehot * csum, axis=1) - 1
    tiles = (csum[-1] + ts - 1) // ts
    tile_end = jnp.cumsum(tiles)
    tile_start = tile_end - tiles
    slot = (jnp.sum(onehot * tile_start[None, :], axis=1) * ts + rank).astype(jnp.int32)
    n_used = tile_end[-1]
    tile_ids = jnp.arange(n_tiles, dtype=jnp.int32)
    tile_expert = jnp.sum((tile_ids[:, None] >= tile_end[None, :]).astype(jnp.int32), axis=1)
    last_expert = jnp.sum(((n_used - 1) >= tile_end).astype(jnp.int32))
    tile_expert = jnp.minimum(tile_expert, last_expert).astype(jnp.int32)
    plane_rows = slot.reshape(2, 1, t) + (jnp.arange(2, dtype=jnp.int32) * n_slots).reshape(1, 2, 1)
    return plane_rows.reshape(-1), tile_expert, n_used.reshape(1).astype(jnp.int32), n_slots


def _sc_mesh():
    return plsc.VectorSubcoreMesh(core_axis_name="core", subcore_axis_name="subcore")


def _scatter_rows(table, idx, n_out):
    m = idx.shape[0]
    n, d = table.shape
    assert m % SC_ROW_WINDOW == 0 and n % SC_ROW_WINDOW == 0
    n_src_blocks = n // SC_ROW_WINDOW

    @pl.kernel(out_type=jax.ShapeDtypeStruct((n_out, d), table.dtype), mesh=_sc_mesh(), scratch_types=[])
    def scatter_kernel(x_hbm, i_hbm, o_hbm):
        def body(x_vmem, i_vmem):
            pltpu.sync_copy(x_vmem, o_hbm.at[i_vmem.at[0]])

        pltpu.emit_pipeline(
            body,
            grid=(m // SC_ROW_WINDOW,),
            in_specs=[pl.BlockSpec((SC_ROW_WINDOW, d), index_map=lambda i: (i % n_src_blocks, 0)),
                      pl.BlockSpec((1, SC_ROW_WINDOW), index_map=lambda i: (0, i))],
            out_specs=[],
            core_axis_name=("core", "subcore"),
            dimension_semantics=(pltpu.PARALLEL,),
        )(x_hbm, i_hbm)

    return scatter_kernel(table, idx.reshape(1, m))


def _gather_rows(table, idx):
    m = idx.shape[0]
    d = table.shape[1]
    assert m % SC_ROW_WINDOW == 0

    @pl.kernel(out_type=jax.ShapeDtypeStruct((m, d), table.dtype), mesh=_sc_mesh())
    def gather_kernel(x_hbm, i_hbm, o_hbm):
        def body(i_vmem, o_vmem):
            pltpu.sync_copy(x_hbm.at[i_vmem.at[0]], o_vmem)

        pltpu.emit_pipeline(
            body,
            grid=(m // SC_ROW_WINDOW,),
            in_specs=[pl.BlockSpec((1, SC_ROW_WINDOW), index_map=lambda i: (0, i))],
            out_specs=[pl.BlockSpec((SC_ROW_WINDOW, d), index_map=lambda i: (i, 0))],
            core_axis_name=("core", "subcore"),
            dimension_semantics=(pltpu.PARALLEL,),
        )(i_hbm, o_hbm)

    return gather_kernel(table, idx.reshape(1, m))


def _expert_kernel(te_ref, nu_ref, xs_ref, wg_ref, wu_ref, wd_ref, ys_ref, x_sc, acc_ref):
    i = pl.program_id(0)
    k = pl.program_id(1)
    last_k = pl.num_programs(1) - 1
    used = i < nu_ref[0]

    @pl.when(used & (k == 0))
    def _():
        x_sc[...] = _unpack_bf16_pairs(_load_planes(xs_ref)).astype(BF16)
        acc_ref[...] = jnp.zeros_like(acc_ref)

    @pl.when(used)
    def _():
        x = x_sc[...]
        gate = jnp.dot(x, wg_ref[0], preferred_element_type=F32)
        up = jnp.dot(x, wu_ref[0], preferred_element_type=F32)
        h = (_silu(gate) * up).astype(BF16)
        acc_ref[...] += jnp.dot(h, wd_ref[0], preferred_element_type=F32)

    @pl.when(used & (k == last_k))
    def _():
        _store_planes(ys_ref, _pack_bf16_pairs(acc_ref[...]))

    @pl.when(jnp.logical_not(used) & (k == last_k))
    def _():
        ys_ref[...] = jnp.zeros_like(ys_ref)


def _experts(xs, tile_expert, n_used, w_gu, w_down, *, ts, tf):
    _, n_slots, quarter = xs.shape
    d = 4 * quarter
    dexp = w_down.shape[1]
    nk = dexp // tf
    w_gu = w_gu.astype(BF16)
    grid_spec = pltpu.PrefetchScalarGridSpec(
        num_scalar_prefetch=2,
        grid=(n_slots // ts, nk),
        in_specs=[
            pl.BlockSpec((2, ts, quarter), lambda i, k, te, nu: (0, i, 0)),
            pl.BlockSpec((1, d, tf), lambda i, k, te, nu: (te[i], 0, k)),
            pl.BlockSpec((1, d, tf), lambda i, k, te, nu: (te[i], 0, nk + k)),
            pl.BlockSpec((1, tf, d), lambda i, k, te, nu: (te[i], k, 0)),
        ],
        out_specs=pl.BlockSpec((2, ts, quarter), lambda i, k, te, nu: (0, i, 0)),
        scratch_shapes=[pltpu.VMEM((ts, d), BF16), pltpu.VMEM((ts, d), F32)],
    )
    return pl.pallas_call(
        _expert_kernel,
        grid_spec=grid_spec,
        out_shape=jax.ShapeDtypeStruct((2, n_slots, quarter), jnp.int32),
        compiler_params=pltpu.CompilerParams(dimension_semantics=("arbitrary", "arbitrary")),
        name="moe_experts",
    )(tile_expert, n_used, xs, w_gu, w_gu, w_down.astype(BF16))


def _combine_kernel(x_ref, y_ref, rt_ref, gf_ref, o_ref):
    rt = rt_ref[...]
    moe = (rt[:, 2:3] * _unpack_bf16_pairs(_load_planes(y_ref, 0))
           + rt[:, 3:4] * _unpack_bf16_pairs(_load_planes(y_ref, 2)))
    o_ref[...] = _rms(x_ref[...] + moe, gf_ref[...])


def _combine(x2d, y4, route_t, g_final, *, tm):
    t, d = x2d.shape
    ne = route_t.shape[1]
    return pl.pallas_call(
        _combine_kernel,
        grid=(t // tm,),
        in_specs=[
            pl.BlockSpec((tm, d), lambda i: (i, 0)),
            pl.BlockSpec((4, tm, d // 4), lambda i: (0, i, 0)),
            pl.BlockSpec((tm, ne), lambda i: (i, 0)),
            pl.BlockSpec((1, d), lambda i: (0, 0)),
        ],
        out_specs=pl.BlockSpec((tm, d), lambda i: (i, 0)),
        out_shape=jax.ShapeDtypeStruct((t, d), F32),
        compiler_params=pltpu.CompilerParams(dimension_semantics=("parallel",)),
        name="moe_combine",
    )(x2d, y4, route_t, g_final.reshape(1, d))


def _moe(x2d, g, router, w_gu, w_down, g_final, *, tm, ts, tf):
    t, d = x2d.shape
    q = d // 4
    hn_planes, route = _router(x2d, g, router, tm=tm)
    plane_rows, tile_expert, n_used, n_slots = _routing_tables(route, ts)
    xs = _scatter_rows(hn_planes.reshape(2 * t, q), plane_rows, 2 * n_slots)
    ys = _experts(xs.reshape(2, n_slots, q), tile_expert, n_used, w_gu, w_down, ts=ts, tf=tf)
    y4 = _gather_rows(ys.reshape(2 * n_slots, q), plane_rows)
    return _combine(x2d, y4.reshape(4, t, q), route.T, g_final, tm=tm)


def _tile(n, pref):
    t = min(n, pref)
    assert n % t == 0
    return t


def kernel(x, norm_g, lru_w_in, lru_conv_w, lru_conv_b, lru_gate_w, lru_gate_b, lru_lambda, lru_w_out,
           kv_norm_g, w_kvf, b_f, fox_w_qg, fox_w_o, mlp_w_gu, mlp_w_down, moe_router, moe_w_gu,
           moe_w_down, final_norm_g):
    bsz, s, d = x.shape
    t = bsz * s
    tm = _tile(t, 512)
    ts = _tile(s, 512)

    x = _lru_layer(x, norm_g[0, 0], lru_w_in[0], lru_conv_w[0], lru_conv_b[0], lru_gate_w[0],
                   lru_gate_b[0], lru_lambda[0], lru_w_out[0], tt=_tile(s, 256))
    dff = mlp_w_down.shape[1]
    x = _mlp(x.reshape(t, d), norm_g[0, 1], mlp_w_gu[0], mlp_w_down[0], tm=tm, tf=dff // 2).reshape(bsz, s, d)

    consts = _head_selectors()
    ka, vT, c2r, kn = _kv_proj(x, kv_norm_g, w_kvf, b_f, consts, tk=ts)
    qa, gate, qn = _q_proj(x, norm_g[1, 0], fox_w_qg[0], c2r, consts, tq=ts)
    jlo = _skip_table(qn, kn, c2r, ts)
    o = _attention(jlo, qa, ka, vT, gate, tq=ts)
    x2d = _oproj(x.reshape(t, d), o.reshape(t, -1), fox_w_o[0], tm=tm)

    dexp = moe_w_down.shape[2]
    out = _moe(x2d, norm_g[1, 1], moe_router[0], moe_w_gu[0], moe_w_down[0], final_norm_g, tm=tm, ts=tm,
               tf=dexp // 2)
    return out.reshape(bsz, s, d)
```

```python
import functools
import math

import jax
import jax.numpy as jnp
from jax import lax
from jax.experimental import pallas as pl
from jax.experimental.pallas import tpu as pltpu
from jax.experimental.pallas import tpu_sc as plsc

EPS = 1e-6
LRU_C = 8.0
N_LRU_BLOCKS = 8
CONV_W = 4
N_HEADS = 16
HEAD_DIM = 64
N_EXPERTS = 8
LANES = 128
SUBLANES = 8
HEAD_GROUP = 128
BIAS_Q = HEAD_DIM
BIAS_K = HEAD_DIM + 3
V_GROUP = 80
NEG_BIG = -1e30
LOG2E = math.log2(math.e)
SKIP_GAP_LOG2 = 160.0
NORM_SLACK = 1.01
SC_ROW_WINDOW = 128

F32 = jnp.float32
BF16 = jnp.bfloat16
NT_DIMS = (((1,), (1,)), ((), ()))


def _rms(x, g):
    ms = jnp.mean(x * x, axis=-1, keepdims=True)
    return x * lax.rsqrt(ms + EPS) * g


def _log_sigmoid(x):
    return jnp.minimum(x, 0.0) - jnp.log1p(jnp.exp(-jnp.abs(x)))


def _gelu_tanh(x):
    c = 0.7978845608028654
    return 0.5 * x * (1.0 + jnp.tanh(c * (x + 0.044715 * (x * x * x))))


def _silu(x):
    return x * jax.nn.sigmoid(x)


def _split3(x):
    p0 = x.astype(BF16)
    r1 = x - p0.astype(F32)
    p1 = r1.astype(BF16)
    p2 = (r1 - p1.astype(F32)).astype(BF16)
    return p0, p1, p2


def _const_spec(shape, n_grid):
    zeros = (0,) * len(shape)
    if n_grid == 1:
        return pl.BlockSpec(shape, lambda i: zeros)
    if n_grid == 2:
        return pl.BlockSpec(shape, lambda i, j: zeros)
    return pl.BlockSpec(shape, lambda i, j, k: zeros)


def _lru_kernel(x_ref, g_ref, win_ref, cw_ref, cb_ref, gw_ref, gb_ref, lam_ref, wout_ref,
                o_ref, xbuf, hcar):
    t = pl.program_id(1)
    tt = x_ref.shape[1]
    d = x_ref.shape[2]
    bw = d // N_LRU_BLOCKS

    @pl.when(t == 0)
    def _():
        xbuf[0:SUBLANES, :] = jnp.zeros((SUBLANES, d), F32)
        hcar[...] = jnp.zeros_like(hcar)

    x = x_ref[0]
    xn = _rms(x, g_ref[...]).astype(BF16)
    xg = jnp.dot(xn, win_ref[...], preferred_element_type=F32)
    xb = xg[:, :d]
    gbr = xg[:, d:]

    xbuf[SUBLANES:SUBLANES + tt, :] = xb
    cw = cw_ref[...]
    xc = cb_ref[...] + cw[CONV_W - 1:CONV_W, :] * xb
    for j in range(CONV_W - 1):
        xc = xc + cw[j:j + 1, :] * xbuf[pl.ds(SUBLANES - (CONV_W - 1) + j, tt), :]
    xbuf[0:SUBLANES, :] = xbuf[tt:tt + SUBLANES, :]

    xcb = xc.astype(BF16)
    rl, il = [], []
    for n in range(N_LRU_BLOCKS):
        gl = jnp.dot(xcb[:, n * bw:(n + 1) * bw], gw_ref[n], preferred_element_type=F32)
        rl.append(gl[:, :bw])
        il.append(gl[:, bw:])
    gb = gb_ref[...]
    r = jax.nn.sigmoid(jnp.concatenate(rl, axis=1) + gb[0:1, :])
    i = jax.nn.sigmoid(jnp.concatenate(il, axis=1) + gb[1:2, :])
    log_a = LRU_C * r * _log_sigmoid(lam_ref[...])
    a = jnp.exp(log_a)
    th = jnp.tanh(log_a)
    b = jnp.sqrt(-2.0 * th / (1.0 - th)) * (i * xc)

    row = lax.broadcasted_iota(jnp.int32, (tt, 1), 0)
    sh = 1
    while sh < tt:
        keep = row >= sh
        a_sh = jnp.where(keep, pltpu.roll(a, sh, 0), 1.0)
        b_sh = jnp.where(keep, pltpu.roll(b, sh, 0), 0.0)
        b = a * b_sh + b
        a = a * a_sh
        sh *= 2
    h = b + a * hcar[...]
    hcar[...] = h[tt - 1:tt, :]

    y = (_gelu_tanh(gbr) * h).astype(BF16)
    o_ref[0] = x + jnp.dot(y, wout_ref[...], preferred_element_type=F32)


def _lru_layer(x, g, w_in, conv_w, conv_b, gate_w, gate_b, lam, w_out, *, tt):
    bsz, s, d = x.shape
    bw = d // N_LRU_BLOCKS
    gw = jnp.concatenate([gate_w[0], gate_w[1]], axis=-1).astype(BF16)
    const = lambda *shape: _const_spec(shape, 2)
    return pl.pallas_call(
        _lru_kernel,
        grid=(bsz, s // tt),
        in_specs=[
            pl.BlockSpec((1, tt, d), lambda b, t: (b, t, 0)),
            const(1, d), const(d, 2 * d), const(CONV_W, d), const(1, d),
            const(N_LRU_BLOCKS, bw, 2 * bw), const(2, d), const(1, d), const(d, d),
        ],
        out_specs=pl.BlockSpec((1, tt, d), lambda b, t: (b, t, 0)),
        out_shape=jax.ShapeDtypeStruct((bsz, s, d), F32),
        scratch_shapes=[pltpu.VMEM((tt + 2 * SUBLANES, d), F32), pltpu.VMEM((1, d), F32)],
        compiler_params=pltpu.CompilerParams(dimension_semantics=("parallel", "arbitrary")),
        name="lru_layer",
    )(x, g.reshape(1, d), w_in.astype(BF16), conv_w, conv_b.reshape(1, d), gw, gate_b,
      lam.reshape(1, d), w_out.astype(BF16))


def _mlp_kernel(x_ref, g_ref, wg_ref, wu_ref, wd_ref, o_ref, hn_ref, acc_ref):
    k = pl.program_id(1)

    @pl.when(k == 0)
    def _():
        hn_ref[...] = _rms(x_ref[...], g_ref[...]).astype(BF16)
        acc_ref[...] = jnp.zeros_like(acc_ref)

    hn = hn_ref[...]
    gate = jnp.dot(hn, wg_ref[...], preferred_element_type=F32)
    up = jnp.dot(hn, wu_ref[...], preferred_element_type=F32)
    h = (_silu(gate) * up).astype(BF16)
    acc_ref[...] += jnp.dot(h, wd_ref[...], preferred_element_type=F32)

    @pl.when(k == pl.num_programs(1) - 1)
    def _():
        o_ref[...] = x_ref[...] + acc_ref[...]


def _mlp(x2d, g, w_gu, w_down, *, tm, tf):
    t, d = x2d.shape
    dff = w_down.shape[0]
    nk = dff // tf
    w_gu = w_gu.astype(BF16)
    return pl.pallas_call(
        _mlp_kernel,
        grid=(t // tm, nk),
        in_specs=[
            pl.BlockSpec((tm, d), lambda i, k: (i, 0)),
            pl.BlockSpec((1, d), lambda i, k: (0, 0)),
            pl.BlockSpec((d, tf), lambda i, k: (0, k)),
            pl.BlockSpec((d, tf), lambda i, k: (0, nk + k)),
            pl.BlockSpec((tf, d), lambda i, k: (k, 0)),
        ],
        out_specs=pl.BlockSpec((tm, d), lambda i, k: (i, 0)),
        out_shape=jax.ShapeDtypeStruct((t, d), F32),
        scratch_shapes=[pltpu.VMEM((tm, d), BF16), pltpu.VMEM((tm, d), F32)],
        compiler_params=pltpu.CompilerParams(dimension_semantics=("parallel", "arbitrary")),
        name="mlp",
    )(x2d, g.reshape(1, d), w_gu, w_gu, w_down.astype(BF16))


def _head_selectors():
    hg = N_HEADS * HEAD_GROUP
    head = jnp.arange(hg) // HEAD_GROUP
    lane = jnp.arange(hg) % HEAD_GROUP
    col = jnp.arange(LANES)
    def sel(base, p):
        return ((col[:, None] == head[None, :]) & (lane[None, :] == base + p)).astype(BF16)
    sel_k = jnp.concatenate([-sel(BIAS_K, p) for p in range(3)], axis=0)
    sel_q = jnp.concatenate([sel(BIAS_Q, p).T for p in range(3)], axis=1)
    ones_k = ((lane >= BIAS_Q) & (lane < BIAS_Q + 3)).astype(F32).reshape(1, hg)
    ones_q = ((lane >= BIAS_K) & (lane < BIAS_K + 3)).astype(F32).reshape(hg, 1)
    head_sum = ((col[None, :] == head[:, None]) & (lane[:, None] < HEAD_DIM)).astype(BF16)
    return sel_k, sel_q, ones_k, ones_q, head_sum


def _kv_kernel(x_ref, g_ref, wk_ref, wvT_ref, onesv_ref, wf_ref, wfT_ref, bfr_ref, bfc_ref, selk_ref, onesk_ref,
               hsum_ref, ka_ref, vT_ref, c2r_ref, kn_ref, car_r, car_c):
    t = pl.program_id(1)
    tk = x_ref.shape[1]

    @pl.when(t == 0)
    def _():
        car_r[...] = jnp.zeros_like(car_r)
        car_c[...] = jnp.zeros_like(car_c)

    xn = _rms(x_ref[0], g_ref[...]).astype(BF16)
    vT = lax.dot_general(wvT_ref[...], xn, NT_DIMS, preferred_element_type=F32) + onesv_ref[...]
    vT_ref[0, 0] = vT.astype(BF16)
    kf = jnp.dot(xn, wk_ref[...], preferred_element_type=F32)

    ksq = kf * kf
    ksq_hi = ksq.astype(BF16)
    ksq_lo = (ksq - ksq_hi.astype(F32)).astype(BF16)
    hs = hsum_ref[...]
    kss = jnp.dot(ksq_hi, hs, preferred_element_type=F32) + jnp.dot(ksq_lo, hs, preferred_element_type=F32)
    kn = jnp.sqrt(jnp.max(kss, axis=0, keepdims=True))
    kn_ref[0, 0] = jnp.broadcast_to(kn, (SUBLANES, LANES))

    ri = lax.broadcasted_iota(jnp.int32, (tk, tk), 0)
    ci = lax.broadcasted_iota(jnp.int32, (tk, tk), 1)
    upper = (ri <= ci).astype(BF16)
    lower = (ci <= ri).astype(BF16)

    lf_c = _log_sigmoid(jnp.dot(xn, wf_ref[...], preferred_element_type=F32) + bfr_ref[...])
    cs3 = jnp.dot(lower, jnp.concatenate(_split3(lf_c), axis=1), preferred_element_type=F32)
    cs_c = car_c[...] + (cs3[:, :LANES] + cs3[:, LANES:2 * LANES] + cs3[:, 2 * LANES:])
    car_c[...] = cs_c[tk - 1:tk, :]
    pieces = jnp.concatenate(_split3(cs_c * LOG2E), axis=1)
    bias = jnp.dot(pieces, selk_ref[...], preferred_element_type=F32)
    ka_ref[0] = (kf + bias + onesk_ref[...]).astype(BF16)

    lf_r = _log_sigmoid(lax.dot_general(wfT_ref[...], xn, NT_DIMS, preferred_element_type=F32) + bfc_ref[...])
    cs3 = jnp.dot(jnp.concatenate(_split3(lf_r), axis=0), upper, preferred_element_type=F32)
    cs_r = car_r[...] + (cs3[:LANES] + cs3[LANES:2 * LANES] + cs3[2 * LANES:])
    car_r[...] = cs_r[:, tk - 1:tk]
    c2r_ref[0] = cs_r * LOG2E


def _kv_proj(x, g, w_kvf, b_f, consts, *, tk):
    bsz, s, d = x.shape
    da = N_HEADS * HEAD_DIM
    hg = N_HEADS * HEAD_GROUP
    dv = N_HEADS * V_GROUP
    nk = s // tk
    sel_k, _, ones_k, _, head_sum = consts
    wk = w_kvf[:, :da].reshape(d, N_HEADS, HEAD_DIM)
    wk = jnp.pad(wk, ((0, 0), (0, 0), (0, HEAD_GROUP - HEAD_DIM))).reshape(d, hg).astype(BF16)
    wv = w_kvf[:, da:2 * da].reshape(d, N_HEADS, HEAD_DIM)
    wvT = jnp.pad(wv, ((0, 0), (0, 0), (0, V_GROUP - HEAD_DIM))).reshape(d, dv).T.astype(BF16)
    ones_v = (jnp.arange(dv) % V_GROUP == HEAD_DIM).astype(F32).reshape(dv, 1)
    wf = jnp.pad(w_kvf[:, 2 * da:], ((0, 0), (0, LANES - N_HEADS))).astype(BF16)
    bfr = jnp.pad(b_f, (0, LANES - N_HEADS)).reshape(1, LANES)
    const = lambda *shape: _const_spec(shape, 2)
    return pl.pallas_call(
        _kv_kernel,
        grid=(bsz, nk),
        in_specs=[
            pl.BlockSpec((1, tk, d), lambda b, t: (b, t, 0)),
            const(1, d), const(d, hg), const(dv, d), const(dv, 1), const(d, LANES), const(LANES, d),
            const(1, LANES), const(LANES, 1), const(3 * LANES, hg), const(1, hg), const(hg, LANES),
        ],
        out_specs=[
            pl.BlockSpec((1, tk, hg), lambda b, t: (b, t, 0)),
            pl.BlockSpec((1, 1, dv, tk), lambda b, t: (b, t, 0, 0)),
            pl.BlockSpec((1, LANES, tk), lambda b, t: (b, 0, t)),
            pl.BlockSpec((1, 1, SUBLANES, LANES), lambda b, t: (b, t, 0, 0)),
        ],
        out_shape=[
            jax.ShapeDtypeStruct((bsz, s, hg), BF16),
            jax.ShapeDtypeStruct((bsz, nk, dv, tk), BF16),
            jax.ShapeDtypeStruct((bsz, LANES, s), F32),
            jax.ShapeDtypeStruct((bsz, nk, SUBLANES, LANES), F32),
        ],
        scratch_shapes=[pltpu.VMEM((LANES, 1), F32), pltpu.VMEM((1, LANES), F32)],
        compiler_params=pltpu.CompilerParams(dimension_semantics=("parallel", "arbitrary")),
        name="kv_proj",
    )(x, g.reshape(1, d), wk, wvT, ones_v, wf, wf.T, bfr, bfr.reshape(LANES, 1), sel_k, ones_k, head_sum)


def _q_kernel(x_ref, g_ref, wqT_ref, wg_ref, c2r_ref, selq_ref, onesq_ref, hsumT_ref, qa_ref, gate_ref, qn_ref):
    xn = _rms(x_ref[0], g_ref[...]).astype(BF16)
    gate_ref[0] = jax.nn.sigmoid(jnp.dot(xn, wg_ref[...], preferred_element_type=F32)).astype(BF16)
    qf = lax.dot_general(wqT_ref[...], xn, NT_DIMS, preferred_element_type=F32) * (LOG2E * HEAD_DIM ** -0.5)

    qsq = qf * qf
    qsq_hi = qsq.astype(BF16)
    qsq_lo = (qsq - qsq_hi.astype(F32)).astype(BF16)
    hsT = hsumT_ref[...]
    qss = jnp.dot(hsT, qsq_hi, preferred_element_type=F32) + jnp.dot(hsT, qsq_lo, preferred_element_type=F32)
    qn = jnp.sqrt(jnp.max(qss, axis=1, keepdims=True))
    qn_ref[0, 0] = jnp.broadcast_to(qn, (LANES, LANES))

    pieces = jnp.concatenate(_split3(c2r_ref[0]), axis=0)
    bias = jnp.dot(selq_ref[...], pieces, preferred_element_type=F32)
    qa_ref[0] = (qf + bias + onesq_ref[...]).astype(BF16)


def _q_proj(x, g, w_qg, c2r, consts, *, tq):
    bsz, s, d = x.shape
    da = N_HEADS * HEAD_DIM
    hg = N_HEADS * HEAD_GROUP
    nq = s // tq
    _, sel_q, _, ones_q, head_sum = consts
    wq = w_qg[:, :da].reshape(d, N_HEADS, HEAD_DIM)
    wqT = jnp.pad(wq, ((0, 0), (0, 0), (0, HEAD_GROUP - HEAD_DIM))).reshape(d, hg).T.astype(BF16)
    const = lambda *shape: _const_spec(shape, 2)
    return pl.pallas_call(
        _q_kernel,
        grid=(bsz, nq),
        in_specs=[
            pl.BlockSpec((1, tq, d), lambda b, t: (b, t, 0)),
            const(1, d), const(hg, d), const(d, da),
            pl.BlockSpec((1, LANES, tq), lambda b, t: (b, 0, t)),
            const(hg, 3 * LANES), const(hg, 1), const(LANES, hg),
        ],
        out_specs=[
            pl.BlockSpec((1, hg, tq), lambda b, t: (b, 0, t)),
            pl.BlockSpec((1, tq, da), lambda b, t: (b, t, 0)),
            pl.BlockSpec((1, 1, LANES, LANES), lambda b, t: (b, t, 0, 0)),
        ],
        out_shape=[
            jax.ShapeDtypeStruct((bsz, hg, s), BF16),
            jax.ShapeDtypeStruct((bsz, s, da), BF16),
            jax.ShapeDtypeStruct((bsz, nq, LANES, LANES), F32),
        ],
        compiler_params=pltpu.CompilerParams(dimension_semantics=("parallel", "parallel")),
        name="q_proj",
    )(x, g.reshape(1, d), wqT, w_qg[:, da:].astype(BF16), c2r, sel_q, ones_q, head_sum.T)


def _attn_kernel(jlo_ref, qa_ref, ka_ref, vT_ref, gate_ref, o_ref, m0, m1, acc0, acc1,
                 sa0, sa1, sb0, sb1, pa0, pa1, pb0, pb1):
    b = pl.program_id(0)
    hp = pl.program_id(1)
    i = pl.program_id(2)
    tq = qa_ref.shape[2]
    tk = vT_ref.shape[3]
    j_lo = jlo_ref[(b * pl.num_programs(1) + hp) * pl.num_programs(2) + i]
    m_refs, acc_refs = (m0, m1), (acc0, acc1)
    s_slots = ((sa0, sa1), (sb0, sb1))
    p_slots = ((pa0, pa1), (pb0, pb1))

    def scores(j, h):
        k_blk = ka_ref[0, pl.ds(pl.multiple_of(j * tk, tk), tk), h * HEAD_GROUP:(h + 1) * HEAD_GROUP]
        qT = qa_ref[0, h * HEAD_GROUP:(h + 1) * HEAD_GROUP, :]
        return jnp.dot(k_blk, qT, preferred_element_type=F32)

    def values(j, h, p):
        vT = vT_ref[0, j, h * V_GROUP:(h + 1) * V_GROUP, :]
        return jnp.dot(vT, p, preferred_element_type=F32)

    def softmax_tile(h, s, masked):
        if masked:
            kv_pos = lax.broadcasted_iota(jnp.int32, (tk, tq), 0)
            q_pos = lax.broadcasted_iota(jnp.int32, (tk, tq), 1)
            s = jnp.where(kv_pos <= q_pos, s, NEG_BIG)
        m_old = m_refs[h][...]
        m_new = jnp.maximum(m_old, jnp.max(s, axis=0, keepdims=True))
        m_refs[h][...] = m_new
        return jnp.exp2(m_old - m_new), jnp.exp2(s - m_new).astype(BF16)

    def step(cur, j):
        nxt = 1 - cur
        for h in range(2):
            s_slots[nxt][h][...] = scores(j + 1, h)
        pv = [values(jnp.maximum(j - 1, 0), h, p_slots[nxt][h][...]) for h in range(2)]
        for h in range(2):
            alpha, p = softmax_tile(h, s_slots[cur][h][...], masked=False)
            p_slots[cur][h][...] = p
            acc_refs[h][...] = alpha * (acc_refs[h][...] + pv[h])

    def finish(cur):
        nxt = 1 - cur
        pv = [values(jnp.maximum(i - 1, 0), h, p_slots[nxt][h][...]) for h in range(2)]
        outs = []
        for h in range(2):
            alpha, p = softmax_tile(h, s_slots[cur][h][...], masked=True)
            acc = alpha * (acc_refs[h][...] + pv[h]) + values(i, h, p)
            outs.append(acc[0:HEAD_DIM, :] / acc[HEAD_DIM:HEAD_DIM + 1, :])
        oT = jnp.concatenate(outs, axis=0)
        o_ref[0] = (oT.T * gate_ref[0].astype(F32)).astype(BF16)

    for h in range(2):
        m_refs[h][...] = jnp.full((1, tq), NEG_BIG, F32)
        acc_refs[h][...] = jnp.zeros((V_GROUP, tq), F32)
        p_slots[1][h][...] = jnp.zeros((tk, tq), BF16)
        s_slots[0][h][...] = scores(j_lo, h)

    n_full = i - j_lo

    def body(n, carry):
        j = j_lo + 2 * n
        step(0, j)
        step(1, j + 1)
        return carry

    lax.fori_loop(0, n_full // 2, body, 0)

    @pl.when(n_full % 2 == 1)
    def _():
        step(0, i - 1)
        finish(1)

    @pl.when(n_full % 2 == 0)
    def _():
        finish(0)


def _permute_heads(w_kvf, b_f, w_qg, w_o):
    da = N_HEADS * HEAD_DIM
    perm = jnp.argsort(b_f)
    cols = (perm[:, None] * HEAD_DIM + jnp.arange(HEAD_DIM)[None, :]).reshape(-1)
    w_kvf_p = jnp.concatenate([w_kvf[:, :da][:, cols], w_kvf[:, da:2 * da][:, cols], w_kvf[:, 2 * da:][:, perm]],
                              axis=1)
    w_qg_p = jnp.concatenate([w_qg[:, :da][:, cols], w_qg[:, da:][:, cols]], axis=1)
    return w_kvf_p, b_f[perm], w_qg_p, w_o[cols, :]


def _skip_table(qn, kn, c2r, tq):
    bsz, nq = qn.shape[0], qn.shape[1]
    qn = qn[:, :, :N_HEADS, 0].transpose(0, 2, 1) * NORM_SLACK
    kn = kn[:, :, 0, :N_HEADS].transpose(0, 2, 1) * NORM_SLACK
    c2 = c2r[:, :N_HEADS, :]
    c_first = c2[:, :, ::tq]
    c_last = c2[:, :, tq - 1::tq]
    upper = qn[..., :, None] * kn[..., None, :] + c_first[..., :, None] - c_last[..., None, :]
    lower = -(qn * kn)[..., :, None]
    skip = (upper - lower) < -SKIP_GAP_LOG2
    nk = skip.shape[-1]
    jlo = jnp.min(jnp.where(skip, nk, jnp.arange(nk, dtype=jnp.int32)), axis=-1)
    jlo = jnp.minimum(jlo, jnp.arange(nq, dtype=jnp.int32))
    jlo = jnp.min(jlo.reshape(bsz, N_HEADS // 2, 2, nq), axis=2)
    return jlo.reshape(-1).astype(jnp.int32)


def _attention(jlo, qa, ka, vT, gate, *, tq):
    bsz, hg, s = qa.shape
    nk, tk = vT.shape[1], vT.shape[3]
    da = N_HEADS * HEAD_DIM
    assert tq == tk
    pair = 2 * HEAD_GROUP
    hw = 2 * HEAD_DIM
    grid_spec = pltpu.PrefetchScalarGridSpec(
        num_scalar_prefetch=1,
        grid=(bsz, N_HEADS // 2, s // tq),
        in_specs=[
            pl.BlockSpec((1, pair, tq), lambda b, hp, i, jlo: (b, hp, i)),
            pl.BlockSpec((1, s, pair), lambda b, hp, i, jlo: (b, 0, hp)),
            pl.BlockSpec((1, nk, 2 * V_GROUP, tk), lambda b, hp, i, jlo: (b, 0, hp, 0)),
            pl.BlockSpec((1, tq, hw), lambda b, hp, i, jlo: (b, i, hp)),
        ],
        out_specs=pl.BlockSpec((1, tq, hw), lambda b, hp, i, jlo: (b, i, hp)),
        scratch_shapes=([pltpu.VMEM((1, tq), F32)] * 2 + [pltpu.VMEM((V_GROUP, tq), F32)] * 2
                        + [pltpu.VMEM((tk, tq), F32)] * 4 + [pltpu.VMEM((tk, tq), BF16)] * 4),
    )
    return pl.pallas_call(
        _attn_kernel,
        grid_spec=grid_spec,
        out_shape=jax.ShapeDtypeStruct((bsz, s, da), BF16),
        compiler_params=pltpu.CompilerParams(dimension_semantics=("parallel", "parallel", "arbitrary")),
        name="fox_attention",
    )(jlo, qa, ka, vT, gate)


def _oproj_kernel(x_ref, o_ref, w_ref, out_ref):
    out_ref[...] = x_ref[...] + jnp.dot(o_ref[...], w_ref[...], preferred_element_type=F32)


def _oproj(x2d, o2d, w_o, *, tm):
    t, d = x2d.shape
    da = o2d.shape[1]
    return pl.pallas_call(
        _oproj_kernel,
        grid=(t // tm,),
        in_specs=[
            pl.BlockSpec((tm, d), lambda i: (i, 0)),
            pl.BlockSpec((tm, da), lambda i: (i, 0)),
            pl.BlockSpec((da, d), lambda i: (0, 0)),
        ],
        out_specs=pl.BlockSpec((tm, d), lambda i: (i, 0)),
        out_shape=jax.ShapeDtypeStruct((t, d), F32),
        compiler_params=pltpu.CompilerParams(dimension_semantics=("parallel",)),
        name="attn_out_proj",
    )(x2d, o2d, w_o.astype(BF16))


def _pack_bf16_pairs(x):
    half = x.shape[1] // 2
    lo = lax.bitcast_convert_type(x[:, :half].astype(BF16).astype(F32), jnp.uint32)
    hi = lax.bitcast_convert_type(x[:, half:].astype(BF16).astype(F32), jnp.uint32)
    word = (lo >> 16) | (hi & jnp.uint32(0xFFFF0000))
    return lax.bitcast_convert_type(word, jnp.int32)


def _unpack_bf16_pairs(w):
    u = lax.bitcast_convert_type(w, jnp.uint32)
    lo = lax.bitcast_convert_type(u << 16, F32)
    hi = lax.bitcast_convert_type(u & jnp.uint32(0xFFFF0000), F32)
    return jnp.concatenate([lo, hi], axis=1)


def _store_planes(ref, words):
    q = words.shape[1] // 2
    ref[0] = words[:, :q]
    ref[1] = words[:, q:]


def _load_planes(ref, base=0):
    return jnp.concatenate([ref[base], ref[base + 1]], axis=1)


def _router_kernel(x_ref, g_ref, rtT_ref, hnp_ref, route_ref):
    tm = x_ref.shape[0]
    hn = _rms(x_ref[...], g_ref[...])
    _store_planes(hnp_ref, _pack_bf16_pairs(hn))
    h0, h1, _ = _split3(hn)
    r0, r1, _ = _split3(rtT_ref[...])
    lt = lax.dot_general(jnp.concatenate([r0, r1, r0], axis=1), jnp.concatenate([h0, h0, h1], axis=1), NT_DIMS,
                         preferred_element_type=F32)[:N_EXPERTS]
    row = lax.broadcasted_iota(jnp.int32, (N_EXPERTS, tm), 0)
    m1 = jnp.max(lt, axis=0, keepdims=True)
    i1 = jnp.min(jnp.where(lt == m1, row, N_EXPERTS), axis=0, keepdims=True)
    rest = jnp.where(row == i1, -jnp.inf, lt)
    m2 = jnp.max(rest, axis=0, keepdims=True)
    i2 = jnp.min(jnp.where(rest == m2, row, N_EXPERTS), axis=0, keepdims=True)
    e2 = jnp.exp(m2 - m1)
    w1 = 1.0 / (1.0 + e2)
    w2 = e2 / (1.0 + e2)
    route_ref[...] = jnp.where(row == 0, i1.astype(F32), jnp.where(row == 1, i2.astype(F32),
                               jnp.where(row == 2, w1, jnp.where(row == 3, w2, 0.0))))


def _router(x2d, g, router, *, tm):
    t, d = x2d.shape
    ne = router.shape[1]
    rtT = jnp.pad(router.T, ((0, 2 * SUBLANES - ne), (0, 0)))
    return pl.pallas_call(
        _router_kernel,
        grid=(t // tm,),
        in_specs=[
            pl.BlockSpec((tm, d), lambda i: (i, 0)),
            pl.BlockSpec((1, d), lambda i: (0, 0)),
            pl.BlockSpec((2 * SUBLANES, d), lambda i: (0, 0)),
        ],
        out_specs=[pl.BlockSpec((2, tm, d // 4), lambda i: (0, i, 0)), pl.BlockSpec((ne, tm), lambda i: (0, i))],
        out_shape=[jax.ShapeDtypeStruct((2, t, d // 4), jnp.int32), jax.ShapeDtypeStruct((ne, t), F32)],
        compiler_params=pltpu.CompilerParams(dimension_semantics=("parallel",)),
        name="moe_router",
    )(x2d, g.reshape(1, d), rtT)


def _routing_tables(route, ts):
    ne = N_EXPERTS
    t = route.shape[1]
    e_pair = route[0:2].astype(jnp.int32).reshape(-1)
    n_tiles = (2 * t) // ts + ne
    n_slots = n_tiles * ts
    onehot = (e_pair[:, None] == jnp.arange(ne, dtype=jnp.int32)[None, :]).astype(jnp.int32)
    csum = jnp.cumsum(onehot, axis=0)
    rank = jnp.sum(onehot * csum, axis=1) - 1
    tiles = (csum[-1] + ts - 1) // ts
    tile_end = jnp.cumsum(tiles)
    tile_start = tile_end - tiles
    slot = (jnp.sum(onehot * tile_start[None, :], axis=1) * ts + rank).astype(jnp.int32)
    n_used = tile_end[-1]
    tile_ids = jnp.arange(n_tiles, dtype=jnp.int32)
    tile_expert = jnp.sum((tile_ids[:, None] >= tile_end[None, :]).astype(jnp.int32), axis=1)
    last_expert = jnp.sum(((n_used - 1) >= tile_end).astype(jnp.int32))
    tile_expert = jnp.minimum(tile_expert, last_expert).astype(jnp.int32)
    plane_rows = slot.reshape(2, 1, t) + (jnp.arange(2, dtype=jnp.int32) * n_slots).reshape(1, 2, 1)
    return plane_rows.reshape(-1), tile_expert, n_used.reshape(1).astype(jnp.int32), n_slots


def _sc_mesh():
    return plsc.VectorSubcoreMesh(core_axis_name="core", subcore_axis_name="subcore")


def _scatter_rows(table, idx, n_out):
    m = idx.shape[0]
    n, d = table.shape
    assert m % SC_ROW_WINDOW == 0 and n % SC_ROW_WINDOW == 0
    n_src_blocks = n // SC_ROW_WINDOW

    @pl.kernel(out_type=jax.ShapeDtypeStruct((n_out, d), table.dtype), mesh=_sc_mesh(), scratch_types=[])
    def scatter_kernel(x_hbm, i_hbm, o_hbm):
        def body(x_vmem, i_vmem):
            pltpu.sync_copy(x_vmem, o_hbm.at[i_vmem.at[0]])

        pltpu.emit_pipeline(
            body,
            grid=(m // SC_ROW_WINDOW,),
            in_specs=[pl.BlockSpec((SC_ROW_WINDOW, d), index_map=lambda i: (i % n_src_blocks, 0)),
                      pl.BlockSpec((1, SC_ROW_WINDOW), index_map=lambda i: (0, i))],
            out_specs=[],
            core_axis_name=("core", "subcore"),
            dimension_semantics=(pltpu.PARALLEL,),
        )(x_hbm, i_hbm)

    return scatter_kernel(table, idx.reshape(1, m))


def _gather_rows(table, idx):
    m = idx.shape[0]
    d = table.shape[1]
    assert m % SC_ROW_WINDOW == 0

    @pl.kernel(out_type=jax.ShapeDtypeStruct((m, d), table.dtype), mesh=_sc_mesh())
    def gather_kernel(x_hbm, i_hbm, o_hbm):
        def body(i_vmem, o_vmem):
            pltpu.sync_copy(x_hbm.at[i_vmem.at[0]], o_vmem)

        pltpu.emit_pipeline(
            body,
            grid=(m // SC_ROW_WINDOW,),
            in_specs=[pl.BlockSpec((1, SC_ROW_WINDOW), index_map=lambda i: (0, i))],
            out_specs=[pl.BlockSpec((SC_ROW_WINDOW, d), index_map=lambda i: (i, 0))],
            core_axis_name=("core", "subcore"),
            dimension_semantics=(pltpu.PARALLEL,),
        )(i_hbm, o_hbm)

    return gather_kernel(table, idx.reshape(1, m))


def _expert_kernel(te_ref, nu_ref, xs_ref, wg_ref, wu_ref, wd_ref, ys_ref, x_sc, acc_ref):
    i = pl.program_id(0)
    k = pl.program_id(1)
    last_k = pl.num_programs(1) - 1
    used = i < nu_ref[0]

    @pl.when(used & (k == 0))
    def _():
        x_sc[...] = _unpack_bf16_pairs(_load_planes(xs_ref)).astype(BF16)
        acc_ref[...] = jnp.zeros_like(acc_ref)

    @pl.when(used)
    def _():
        x = x_sc[...]
        gate = jnp.dot(x, wg_ref[0], preferred_element_type=F32)
        up = jnp.dot(x, wu_ref[0], preferred_element_type=F32)
        h = (_silu(gate) * up).astype(BF16)
        acc_ref[...] += jnp.dot(h, wd_ref[0], preferred_element_type=F32)

    @pl.when(used & (k == last_k))
    def _():
        _store_planes(ys_ref, _pack_bf16_pairs(acc_ref[...]))

    @pl.when(jnp.logical_not(used) & (k == last_k))
    def _():
        ys_ref[...] = jnp.zeros_like(ys_ref)


def _experts(xs, tile_expert, n_used, w_gu, w_down, *, ts, tf):
    _, n_slots, quarter = xs.shape
    d = 4 * quarter
    dexp = w_down.shape[1]
    nk = dexp // tf
    w_gu = w_gu.astype(BF16)
    grid_spec = pltpu.PrefetchScalarGridSpec(
        num_scalar_prefetch=2,
        grid=(n_slots // ts, nk),
        in_specs=[
            pl.BlockSpec((2, ts, quarter), lambda i, k, te, nu: (0, i, 0)),
            pl.BlockSpec((1, d, tf), lambda i, k, te, nu: (te[i], 0, k)),
            pl.BlockSpec((1, d, tf), lambda i, k, te, nu: (te[i], 0, nk + k)),
            pl.BlockSpec((1, tf, d), lambda i, k, te, nu: (te[i], k, 0)),
        ],
        out_specs=pl.BlockSpec((2, ts, quarter), lambda i, k, te, nu: (0, i, 0)),
        scratch_shapes=[pltpu.VMEM((ts, d), BF16), pltpu.VMEM((ts, d), F32)],
    )
    return pl.pallas_call(
        _expert_kernel,
        grid_spec=grid_spec,
        out_shape=jax.ShapeDtypeStruct((2, n_slots, quarter), jnp.int32),
        compiler_params=pltpu.CompilerParams(dimension_semantics=("arbitrary", "arbitrary")),
        name="moe_experts",
    )(tile_expert, n_used, xs, w_gu, w_gu, w_down.astype(BF16))


def _combine_kernel(x_ref, y_ref, rt_ref, gf_ref, o_ref):
    rt = rt_ref[...]
    moe = (rt[:, 2:3] * _unpack_bf16_pairs(_load_planes(y_ref, 0))
           + rt[:, 3:4] * _unpack_bf16_pairs(_load_planes(y_ref, 2)))
    o_ref[...] = _rms(x_ref[...] + moe, gf_ref[...])


def _combine(x2d, y4, route_t, g_final, *, tm):
    t, d = x2d.shape
    ne = route_t.shape[1]
    return pl.pallas_call(
        _combine_kernel,
        grid=(t // tm,),
        in_specs=[
            pl.BlockSpec((tm, d), lambda i: (i, 0)),
            pl.BlockSpec((4, tm, d // 4), lambda i: (0, i, 0)),
            pl.BlockSpec((tm, ne), lambda i: (i, 0)),
            pl.BlockSpec((1, d), lambda i: (0, 0)),
        ],
        out_specs=pl.BlockSpec((tm, d), lambda i: (i, 0)),
        out_shape=jax.ShapeDtypeStruct((t, d), F32),
        compiler_params=pltpu.CompilerParams(dimension_semantics=("parallel",)),
        name="moe_combine",
    )(x2d, y4, route_t, g_final.reshape(1, d))


def _moe(x2d, g, router, w_gu, w_down, g_final, *, tm, ts, tf):
    t, d = x2d.shape
    q = d // 4
    hn_planes, route = _router(x2d, g, router, tm=tm)
    plane_rows, tile_expert, n_used, n_slots = _routing_tables(route, ts)
    xs = _scatter_rows(hn_planes.reshape(2 * t, q), plane_rows, 2 * n_slots)
    ys = _experts(xs.reshape(2, n_slots, q), tile_expert, n_used, w_gu, w_down, ts=ts, tf=tf)
    y4 = _gather_rows(ys.reshape(2 * n_slots, q), plane_rows)
    return _combine(x2d, y4.reshape(4, t, q), route.T, g_final, tm=tm)


def _tile(n, pref):
    t = min(n, pref)
    assert n % t == 0
    return t


def kernel(x, norm_g, lru_w_in, lru_conv_w, lru_conv_b, lru_gate_w, lru_gate_b, lru_lambda, lru_w_out,
           kv_norm_g, w_kvf, b_f, fox_w_qg, fox_w_o, mlp_w_gu, mlp_w_down, moe_router, moe_w_gu,
           moe_w_down, final_norm_g):
    bsz, s, d = x.shape
    t = bsz * s
    tm = _tile(t, 512)
    ts = _tile(s, 512)

    x = _lru_layer(x, norm_g[0, 0], lru_w_in[0], lru_conv_w[0], lru_conv_b[0], lru_gate_w[0],
                   lru_gate_b[0], lru_lambda[0], lru_w_out[0], tt=_tile(s, 256))
    dff = mlp_w_down.shape[1]
    x = _mlp(x.reshape(t, d), norm_g[0, 1], mlp_w_gu[0], mlp_w_down[0], tm=tm, tf=dff // 2).reshape(bsz, s, d)

    consts = _head_selectors()
    w_kvf_p, b_f_p, w_qg_p, w_o_p = _permute_heads(w_kvf, b_f, fox_w_qg[0], fox_w_o[0])
    ka, vT, c2r, kn = _kv_proj(x, kv_norm_g, w_kvf_p, b_f_p, consts, tk=ts)
    qa, gate, qn = _q_proj(x, norm_g[1, 0], w_qg_p, c2r, consts, tq=ts)
    jlo = _skip_table(qn, kn, c2r, ts)
    o = _attention(jlo, qa, ka, vT, gate, tq=ts)
    x2d = _oproj(x.reshape(t, d), o.reshape(t, -1), w_o_p, tm=tm)

    dexp = moe_w_down.shape[2]
    out = _moe(x2d, norm_g[1, 1], moe_router[0], moe_w_gu[0], moe_w_down[0], final_norm_g, tm=tm, ts=tm,
               tf=dexp // 2)
    return out.reshape(bsz, s, d)
```

```python
import functools
import math

import jax
import jax.numpy as jnp
from jax import lax
from jax.experimental import pallas as pl
from jax.experimental.pallas import tpu as pltpu
from jax.experimental.pallas import tpu_sc as plsc

EPS = 1e-6
LRU_C = 8.0
N_LRU_BLOCKS = 8
CONV_W = 4
N_HEADS = 16
HEAD_DIM = 64
N_EXPERTS = 8
LANES = 128
SUBLANES = 8
HEAD_GROUP = 128
BIAS_Q = HEAD_DIM
BIAS_K = HEAD_DIM + 3
V_GROUP = 80
NEG_BIG = -1e30
LOG2E = math.log2(math.e)
SKIP_GAP_LOG2 = 152.0
NORM_SLACK = 1.01
SC_ROW_WINDOW = 128

F32 = jnp.float32
BF16 = jnp.bfloat16
NT_DIMS = (((1,), (1,)), ((), ()))


def _rms(x, g):
    ms = jnp.mean(x * x, axis=-1, keepdims=True)
    return x * lax.rsqrt(ms + EPS) * g


def _log_sigmoid(x):
    return jnp.minimum(x, 0.0) - jnp.log1p(jnp.exp(-jnp.abs(x)))


def _gelu_tanh(x):
    c = 0.7978845608028654
    return 0.5 * x * (1.0 + jnp.tanh(c * (x + 0.044715 * (x * x * x))))


def _silu(x):
    return x * jax.nn.sigmoid(x)


def _split3(x):
    p0 = x.astype(BF16)
    r1 = x - p0.astype(F32)
    p1 = r1.astype(BF16)
    p2 = (r1 - p1.astype(F32)).astype(BF16)
    return p0, p1, p2


def _const_spec(shape, n_grid):
    zeros = (0,) * len(shape)
    if n_grid == 1:
        return pl.BlockSpec(shape, lambda i: zeros)
    if n_grid == 2:
        return pl.BlockSpec(shape, lambda i, j: zeros)
    return pl.BlockSpec(shape, lambda i, j, k: zeros)


def _lru_kernel(x_ref, g_ref, win_ref, cw_ref, cb_ref, gw_ref, gb_ref, lam_ref, wout_ref,
                o_ref, xbuf, hcar):
    t = pl.program_id(1)
    tt = x_ref.shape[1]
    d = x_ref.shape[2]
    bw = d // N_LRU_BLOCKS

    @pl.when(t == 0)
    def _():
        xbuf[0:SUBLANES, :] = jnp.zeros((SUBLANES, d), F32)
        hcar[...] = jnp.zeros_like(hcar)

    x = x_ref[0]
    xn = _rms(x, g_ref[...]).astype(BF16)
    xg = jnp.dot(xn, win_ref[...], preferred_element_type=F32)
    xb = xg[:, :d]
    gbr = xg[:, d:]

    xbuf[SUBLANES:SUBLANES + tt, :] = xb
    cw = cw_ref[...]
    xc = cb_ref[...] + cw[CONV_W - 1:CONV_W, :] * xb
    for j in range(CONV_W - 1):
        xc = xc + cw[j:j + 1, :] * xbuf[pl.ds(SUBLANES - (CONV_W - 1) + j, tt), :]
    xbuf[0:SUBLANES, :] = xbuf[tt:tt + SUBLANES, :]

    xcb = xc.astype(BF16)
    rl, il = [], []
    for n in range(N_LRU_BLOCKS):
        gl = jnp.dot(xcb[:, n * bw:(n + 1) * bw], gw_ref[n], preferred_element_type=F32)
        rl.append(gl[:, :bw])
        il.append(gl[:, bw:])
    gb = gb_ref[...]
    r = jax.nn.sigmoid(jnp.concatenate(rl, axis=1) + gb[0:1, :])
    i = jax.nn.sigmoid(jnp.concatenate(il, axis=1) + gb[1:2, :])
    log_a = LRU_C * r * _log_sigmoid(lam_ref[...])
    a = jnp.exp(log_a)
    b = jnp.sqrt((1.0 - a) * (1.0 + a)) * (i * xc)

    row = lax.broadcasted_iota(jnp.int32, (tt, 1), 0)
    sh = 1
    while sh < tt:
        keep = row >= sh
        a_sh = jnp.where(keep, pltpu.roll(a, sh, 0), 1.0)
        b_sh = jnp.where(keep, pltpu.roll(b, sh, 0), 0.0)
        b = a * b_sh + b
        a = a * a_sh
        sh *= 2
    h = b + a * hcar[...]
    hcar[...] = h[tt - 1:tt, :]

    y = (_gelu_tanh(gbr) * h).astype(BF16)
    o_ref[0] = x + jnp.dot(y, wout_ref[...], preferred_element_type=F32)


def _lru_layer(x, g, w_in, conv_w, conv_b, gate_w, gate_b, lam, w_out, *, tt):
    bsz, s, d = x.shape
    bw = d // N_LRU_BLOCKS
    gw = jnp.concatenate([gate_w[0], gate_w[1]], axis=-1).astype(BF16)
    const = lambda *shape: _const_spec(shape, 2)
    return pl.pallas_call(
        _lru_kernel,
        grid=(bsz, s // tt),
        in_specs=[
            pl.BlockSpec((1, tt, d), lambda b, t: (b, t, 0)),
            const(1, d), const(d, 2 * d), const(CONV_W, d), const(1, d),
            const(N_LRU_BLOCKS, bw, 2 * bw), const(2, d), const(1, d), const(d, d),
        ],
        out_specs=pl.BlockSpec((1, tt, d), lambda b, t: (b, t, 0)),
        out_shape=jax.ShapeDtypeStruct((bsz, s, d), F32),
        scratch_shapes=[pltpu.VMEM((tt + 2 * SUBLANES, d), F32), pltpu.VMEM((1, d), F32)],
        compiler_params=pltpu.CompilerParams(dimension_semantics=("parallel", "arbitrary")),
        name="lru_layer",
    )(x, g.reshape(1, d), w_in.astype(BF16), conv_w, conv_b.reshape(1, d), gw, gate_b,
      lam.reshape(1, d), w_out.astype(BF16))


def _mlp_kernel(x_ref, g_ref, wg_ref, wu_ref, wd_ref, o_ref, hn_ref, acc_ref):
    k = pl.program_id(1)

    @pl.when(k == 0)
    def _():
        hn_ref[...] = _rms(x_ref[...], g_ref[...]).astype(BF16)
        acc_ref[...] = jnp.zeros_like(acc_ref)

    hn = hn_ref[...]
    gate = jnp.dot(hn, wg_ref[...], preferred_element_type=F32)
    up = jnp.dot(hn, wu_ref[...], preferred_element_type=F32)
    h = (_silu(gate) * up).astype(BF16)
    acc_ref[...] += jnp.dot(h, wd_ref[...], preferred_element_type=F32)

    @pl.when(k == pl.num_programs(1) - 1)
    def _():
        o_ref[...] = x_ref[...] + acc_ref[...]


def _mlp(x2d, g, w_gu, w_down, *, tm, tf):
    t, d = x2d.shape
    dff = w_down.shape[0]
    nk = dff // tf
    w_gu = w_gu.astype(BF16)
    return pl.pallas_call(
        _mlp_kernel,
        grid=(t // tm, nk),
        in_specs=[
            pl.BlockSpec((tm, d), lambda i, k: (i, 0)),
            pl.BlockSpec((1, d), lambda i, k: (0, 0)),
            pl.BlockSpec((d, tf), lambda i, k: (0, k)),
            pl.BlockSpec((d, tf), lambda i, k: (0, nk + k)),
            pl.BlockSpec((tf, d), lambda i, k: (k, 0)),
        ],
        out_specs=pl.BlockSpec((tm, d), lambda i, k: (i, 0)),
        out_shape=jax.ShapeDtypeStruct((t, d), F32),
        scratch_shapes=[pltpu.VMEM((tm, d), BF16), pltpu.VMEM((tm, d), F32)],
        compiler_params=pltpu.CompilerParams(dimension_semantics=("parallel", "arbitrary")),
        name="mlp",
    )(x2d, g.reshape(1, d), w_gu, w_gu, w_down.astype(BF16))


def _head_selectors():
    hg = N_HEADS * HEAD_GROUP
    head = jnp.arange(hg) // HEAD_GROUP
    lane = jnp.arange(hg) % HEAD_GROUP
    col = jnp.arange(LANES)
    def sel(base):
        return ((col[:, None] % N_HEADS == head[None, :]) & (col[:, None] < 3 * N_HEADS)
                & (lane[None, :] == base + col[:, None] // N_HEADS)).astype(BF16)
    sel_k = -sel(BIAS_K)
    sel_q = sel(BIAS_Q).T
    ones_k = ((lane >= BIAS_Q) & (lane < BIAS_Q + 3)).astype(F32).reshape(1, hg)
    ones_q = ((lane >= BIAS_K) & (lane < BIAS_K + 3)).astype(F32).reshape(hg, 1)
    head_sum = ((col[None, :] == head[:, None]) & (lane[:, None] < HEAD_DIM)).astype(BF16)
    return sel_k, sel_q, ones_k, ones_q, head_sum


def _kv_kernel(x_ref, g_ref, wk_ref, wvT_ref, onesv_ref, wf_ref, wfT_ref, bfr_ref, bfc_ref, selk_ref, onesk_ref,
               hsum_ref, ka_ref, vT_ref, c2r_ref, kn_ref, car_r, car_c):
    t = pl.program_id(1)
    tk = x_ref.shape[1]

    @pl.when(t == 0)
    def _():
        car_r[...] = jnp.zeros_like(car_r)
        car_c[...] = jnp.zeros_like(car_c)

    xn = _rms(x_ref[0], g_ref[...]).astype(BF16)
    vT = lax.dot_general(wvT_ref[...], xn, NT_DIMS, preferred_element_type=F32) + onesv_ref[...]
    vT_ref[0, 0] = vT.astype(BF16)
    kf = jnp.dot(xn, wk_ref[...], preferred_element_type=F32)

    kss = jnp.dot((kf * kf).astype(BF16), hsum_ref[...], preferred_element_type=F32)
    kn = jnp.sqrt(jnp.max(kss, axis=0, keepdims=True))
    kn_ref[0, 0] = jnp.broadcast_to(kn, (SUBLANES, LANES))

    ri = lax.broadcasted_iota(jnp.int32, (tk, tk), 0)
    ci = lax.broadcasted_iota(jnp.int32, (tk, tk), 1)
    upper = (ri <= ci).astype(BF16)
    lower = (ci <= ri).astype(BF16)

    lf_c = _log_sigmoid(jnp.dot(xn, wf_ref[...], preferred_element_type=F32) + bfr_ref[...])
    cs3 = jnp.dot(lower, jnp.concatenate(_split3(lf_c), axis=1), preferred_element_type=F32)
    cs_c = car_c[...] + (cs3[:, :LANES] + cs3[:, LANES:2 * LANES] + cs3[:, 2 * LANES:])
    car_c[...] = cs_c[tk - 1:tk, :]
    p0, p1, p2 = (p.astype(F32) for p in _split3(cs_c * LOG2E))
    lane = lax.broadcasted_iota(jnp.int32, (tk, LANES), 1)
    packed = jnp.where(lane < N_HEADS, p0,
                       jnp.where(lane < 2 * N_HEADS, pltpu.roll(p1, N_HEADS, 1),
                                 jnp.where(lane < 3 * N_HEADS, pltpu.roll(p2, 2 * N_HEADS, 1), 0.0)))
    bias = jnp.dot(packed.astype(BF16), selk_ref[...], preferred_element_type=F32)
    ka_ref[0] = (kf + bias + onesk_ref[...]).astype(BF16)

    lf_r = _log_sigmoid(lax.dot_general(wfT_ref[...], xn, NT_DIMS, preferred_element_type=F32) + bfc_ref[...])
    cs3 = jnp.dot(jnp.concatenate(_split3(lf_r), axis=0), upper, preferred_element_type=F32)
    cs_r = car_r[...] + (cs3[:LANES] + cs3[LANES:2 * LANES] + cs3[2 * LANES:])
    car_r[...] = cs_r[:, tk - 1:tk]
    c2r_ref[0] = cs_r * LOG2E


def _kv_proj(x, g, w_kvf, b_f, consts, *, tk):
    bsz, s, d = x.shape
    da = N_HEADS * HEAD_DIM
    hg = N_HEADS * HEAD_GROUP
    dv = N_HEADS * V_GROUP
    nk = s // tk
    sel_k, _, ones_k, _, head_sum = consts
    wk = w_kvf[:, :da].reshape(d, N_HEADS, HEAD_DIM)
    wk = jnp.pad(wk, ((0, 0), (0, 0), (0, HEAD_GROUP - HEAD_DIM))).reshape(d, hg).astype(BF16)
    wv = w_kvf[:, da:2 * da].reshape(d, N_HEADS, HEAD_DIM)
    wvT = jnp.pad(wv, ((0, 0), (0, 0), (0, V_GROUP - HEAD_DIM))).reshape(d, dv).T.astype(BF16)
    ones_v = (jnp.arange(dv) % V_GROUP == HEAD_DIM).astype(F32).reshape(dv, 1)
    wf = jnp.pad(w_kvf[:, 2 * da:], ((0, 0), (0, LANES - N_HEADS))).astype(BF16)
    bfr = jnp.pad(b_f, (0, LANES - N_HEADS)).reshape(1, LANES)
    const = lambda *shape: _const_spec(shape, 2)
    return pl.pallas_call(
        _kv_kernel,
        grid=(bsz, nk),
        in_specs=[
            pl.BlockSpec((1, tk, d), lambda b, t: (b, t, 0)),
            const(1, d), const(d, hg), const(dv, d), const(dv, 1), const(d, LANES), const(LANES, d),
            const(1, LANES), const(LANES, 1), const(LANES, hg), const(1, hg), const(hg, LANES),
        ],
        out_specs=[
            pl.BlockSpec((1, tk, hg), lambda b, t: (b, t, 0)),
            pl.BlockSpec((1, 1, dv, tk), lambda b, t: (b, t, 0, 0)),
            pl.BlockSpec((1, LANES, tk), lambda b, t: (b, 0, t)),
            pl.BlockSpec((1, 1, SUBLANES, LANES), lambda b, t: (b, t, 0, 0)),
        ],
        out_shape=[
            jax.ShapeDtypeStruct((bsz, s, hg), BF16),
            jax.ShapeDtypeStruct((bsz, nk, dv, tk), BF16),
            jax.ShapeDtypeStruct((bsz, LANES, s), F32),
            jax.ShapeDtypeStruct((bsz, nk, SUBLANES, LANES), F32),
        ],
        scratch_shapes=[pltpu.VMEM((LANES, 1), F32), pltpu.VMEM((1, LANES), F32)],
        compiler_params=pltpu.CompilerParams(dimension_semantics=("parallel", "arbitrary")),
        name="kv_proj",
    )(x, g.reshape(1, d), wk, wvT, ones_v, wf, wf.T, bfr, bfr.reshape(LANES, 1), sel_k, ones_k, head_sum)


def _q_kernel(x_ref, g_ref, wqT_ref, wg_ref, c2r_ref, selq_ref, onesq_ref, hsumT_ref, qa_ref, gate_ref, qn_ref):
    xn = _rms(x_ref[0], g_ref[...]).astype(BF16)
    gate_ref[0] = jax.nn.sigmoid(jnp.dot(xn, wg_ref[...], preferred_element_type=F32)).astype(BF16)
    qf = lax.dot_general(wqT_ref[...], xn, NT_DIMS, preferred_element_type=F32) * (LOG2E * HEAD_DIM ** -0.5)

    qss = jnp.dot(hsumT_ref[...], (qf * qf).astype(BF16), preferred_element_type=F32)
    qn = jnp.sqrt(jnp.max(qss, axis=1, keepdims=True))
    qn_ref[0, 0] = jnp.broadcast_to(qn, (LANES, LANES))

    tq = c2r_ref.shape[2]
    packed = jnp.concatenate([p[:N_HEADS] for p in _split3(c2r_ref[0])]
                             + [jnp.zeros((LANES - 3 * N_HEADS, tq), BF16)], axis=0)
    bias = jnp.dot(selq_ref[...], packed, preferred_element_type=F32)
    qa_ref[0] = (qf + bias + onesq_ref[...]).astype(BF16)


def _q_proj(x, g, w_qg, c2r, consts, *, tq):
    bsz, s, d = x.shape
    da = N_HEADS * HEAD_DIM
    hg = N_HEADS * HEAD_GROUP
    nq = s // tq
    _, sel_q, _, ones_q, head_sum = consts
    wq = w_qg[:, :da].reshape(d, N_HEADS, HEAD_DIM)
    wqT = jnp.pad(wq, ((0, 0), (0, 0), (0, HEAD_GROUP - HEAD_DIM))).reshape(d, hg).T.astype(BF16)
    const = lambda *shape: _const_spec(shape, 2)
    return pl.pallas_call(
        _q_kernel,
        grid=(bsz, nq),
        in_specs=[
            pl.BlockSpec((1, tq, d), lambda b, t: (b, t, 0)),
            const(1, d), const(hg, d), const(d, da),
            pl.BlockSpec((1, LANES, tq), lambda b, t: (b, 0, t)),
            const(hg, LANES), const(hg, 1), const(LANES, hg),
        ],
        out_specs=[
            pl.BlockSpec((1, hg, tq), lambda b, t: (b, 0, t)),
            pl.BlockSpec((1, tq, da), lambda b, t: (b, t, 0)),
            pl.BlockSpec((1, 1, LANES, LANES), lambda b, t: (b, t, 0, 0)),
        ],
        out_shape=[
            jax.ShapeDtypeStruct((bsz, hg, s), BF16),
            jax.ShapeDtypeStruct((bsz, s, da), BF16),
            jax.ShapeDtypeStruct((bsz, nq, LANES, LANES), F32),
        ],
        compiler_params=pltpu.CompilerParams(dimension_semantics=("parallel", "parallel")),
        name="q_proj",
    )(x, g.reshape(1, d), wqT, w_qg[:, da:].astype(BF16), c2r, sel_q, ones_q, head_sum.T)


def _attn_kernel(jlo_ref, qa_ref, ka_ref, vT_ref, gate_ref, o_ref, m0, m1, acc0, acc1,
                 sa0, sa1, sb0, sb1, pa0, pa1, pb0, pb1):
    b = pl.program_id(0)
    hp = pl.program_id(1)
    i = pl.program_id(2)
    tq = qa_ref.shape[2]
    tk = vT_ref.shape[3]
    j_lo = jlo_ref[(b * pl.num_programs(1) + hp) * pl.num_programs(2) + i]
    m_refs, acc_refs = (m0, m1), (acc0, acc1)
    s_slots = ((sa0, sa1), (sb0, sb1))
    p_slots = ((pa0, pa1), (pb0, pb1))

    def scores(j, h):
        k_blk = ka_ref[0, pl.ds(pl.multiple_of(j * tk, tk), tk), h * HEAD_GROUP:(h + 1) * HEAD_GROUP]
        qT = qa_ref[0, h * HEAD_GROUP:(h + 1) * HEAD_GROUP, :]
        return jnp.dot(k_blk, qT, preferred_element_type=F32)

    def values(j, h, p):
        vT = vT_ref[0, j, h * V_GROUP:(h + 1) * V_GROUP, :]
        return jnp.dot(vT, p, preferred_element_type=F32)

    def softmax_tile(h, s, masked):
        if masked:
            kv_pos = lax.broadcasted_iota(jnp.int32, (tk, tq), 0)
            q_pos = lax.broadcasted_iota(jnp.int32, (tk, tq), 1)
            s = jnp.where(kv_pos <= q_pos, s, NEG_BIG)
        m_old = m_refs[h][...]
        m_new = jnp.maximum(m_old, jnp.max(s, axis=0, keepdims=True))
        m_refs[h][...] = m_new
        return jnp.exp2(m_old - m_new), jnp.exp2(s - m_new).astype(BF16)

    def step(cur, j):
        nxt = 1 - cur
        for h in range(2):
            s_slots[nxt][h][...] = scores(j + 1, h)
        pv = [values(jnp.maximum(j - 1, 0), h, p_slots[nxt][h][...]) for h in range(2)]
        for h in range(2):
            alpha, p = softmax_tile(h, s_slots[cur][h][...], masked=False)
            p_slots[cur][h][...] = p
            acc_refs[h][...] = alpha * (acc_refs[h][...] + pv[h])

    def finish(cur):
        nxt = 1 - cur
        pv = [values(jnp.maximum(i - 1, 0), h, p_slots[nxt][h][...]) for h in range(2)]
        outs = []
        for h in range(2):
            alpha, p = softmax_tile(h, s_slots[cur][h][...], masked=True)
            acc = alpha * (acc_refs[h][...] + pv[h]) + values(i, h, p)
            outs.append(acc[0:HEAD_DIM, :] / acc[HEAD_DIM:HEAD_DIM + 1, :])
        oT = jnp.concatenate(outs, axis=0)
        o_ref[0] = (oT.T * gate_ref[0].astype(F32)).astype(BF16)

    for h in range(2):
        m_refs[h][...] = jnp.full((1, tq), NEG_BIG, F32)
        acc_refs[h][...] = jnp.zeros((V_GROUP, tq), F32)
        p_slots[1][h][...] = jnp.zeros((tk, tq), BF16)
        s_slots[0][h][...] = scores(j_lo, h)

    n_full = i - j_lo

    def body(n, carry):
        j = j_lo + 2 * n
        step(0, j)
        step(1, j + 1)
        return carry

    lax.fori_loop(0, n_full // 2, body, 0)

    @pl.when(n_full % 2 == 1)
    def _():
        step(0, i - 1)
        finish(1)

    @pl.when(n_full % 2 == 0)
    def _():
        finish(0)


def _permute_heads(w_kvf, b_f, w_qg, w_o):
    da = N_HEADS * HEAD_DIM
    perm = jnp.argsort(b_f)
    cols = (perm[:, None] * HEAD_DIM + jnp.arange(HEAD_DIM)[None, :]).reshape(-1)
    w_kvf_p = jnp.concatenate([w_kvf[:, :da][:, cols], w_kvf[:, da:2 * da][:, cols], w_kvf[:, 2 * da:][:, perm]],
                              axis=1)
    w_qg_p = jnp.concatenate([w_qg[:, :da][:, cols], w_qg[:, da:][:, cols]], axis=1)
    return w_kvf_p, b_f[perm], w_qg_p, w_o[cols, :]


def _skip_table(qn, kn, c2r, tq):
    bsz, nq = qn.shape[0], qn.shape[1]
    qn = qn[:, :, :N_HEADS, 0].transpose(0, 2, 1) * NORM_SLACK
    kn = kn[:, :, 0, :N_HEADS].transpose(0, 2, 1) * NORM_SLACK
    c2 = c2r[:, :N_HEADS, :]
    c_first = c2[:, :, ::tq]
    c_last = c2[:, :, tq - 1::tq]
    upper = qn[..., :, None] * kn[..., None, :] + c_first[..., :, None] - c_last[..., None, :]
    lower = -(qn * kn)[..., :, None]
    skip = (upper - lower) < -SKIP_GAP_LOG2
    nk = skip.shape[-1]
    jlo = jnp.min(jnp.where(skip, nk, jnp.arange(nk, dtype=jnp.int32)), axis=-1)
    jlo = jnp.minimum(jlo, jnp.arange(nq, dtype=jnp.int32))
    jlo = jnp.min(jlo.reshape(bsz, N_HEADS // 2, 2, nq), axis=2)
    return jlo.reshape(-1).astype(jnp.int32)


def _attention(jlo, qa, ka, vT, gate, *, tq):
    bsz, hg, s = qa.shape
    nk, tk = vT.shape[1], vT.shape[3]
    da = N_HEADS * HEAD_DIM
    assert tq == tk
    pair = 2 * HEAD_GROUP
    hw = 2 * HEAD_DIM
    grid_spec = pltpu.PrefetchScalarGridSpec(
        num_scalar_prefetch=1,
        grid=(bsz, N_HEADS // 2, s // tq),
        in_specs=[
            pl.BlockSpec((1, pair, tq), lambda b, hp, i, jlo: (b, hp, i)),
            pl.BlockSpec((1, s, pair), lambda b, hp, i, jlo: (b, 0, hp)),
            pl.BlockSpec((1, nk, 2 * V_GROUP, tk), lambda b, hp, i, jlo: (b, 0, hp, 0)),
            pl.BlockSpec((1, tq, hw), lambda b, hp, i, jlo: (b, i, hp)),
        ],
        out_specs=pl.BlockSpec((1, tq, hw), lambda b, hp, i, jlo: (b, i, hp)),
        scratch_shapes=([pltpu.VMEM((1, tq), F32)] * 2 + [pltpu.VMEM((V_GROUP, tq), F32)] * 2
                        + [pltpu.VMEM((tk, tq), F32)] * 4 + [pltpu.VMEM((tk, tq), BF16)] * 4),
    )
    return pl.pallas_call(
        _attn_kernel,
        grid_spec=grid_spec,
        out_shape=jax.ShapeDtypeStruct((bsz, s, da), BF16),
        compiler_params=pltpu.CompilerParams(dimension_semantics=("parallel", "parallel", "arbitrary")),
        name="fox_attention",
    )(jlo, qa, ka, vT, gate)


def _pack_bf16_pairs(x):
    half = x.shape[1] // 2
    lo = lax.bitcast_convert_type(x[:, :half].astype(BF16).astype(F32), jnp.uint32)
    hi = lax.bitcast_convert_type(x[:, half:].astype(BF16).astype(F32), jnp.uint32)
    word = (lo >> 16) | (hi & jnp.uint32(0xFFFF0000))
    return lax.bitcast_convert_type(word, jnp.int32)


def _unpack_bf16_pairs(w):
    u = lax.bitcast_convert_type(w, jnp.uint32)
    lo = lax.bitcast_convert_type(u << 16, F32)
    hi = lax.bitcast_convert_type(u & jnp.uint32(0xFFFF0000), F32)
    return jnp.concatenate([lo, hi], axis=1)


def _store_planes(ref, words):
    q = words.shape[1] // 2
    ref[0] = words[:, :q]
    ref[1] = words[:, q:]


def _load_planes(ref, base=0):
    return jnp.concatenate([ref[base], ref[base + 1]], axis=1)


def _router_kernel(x_ref, o_ref, wo_ref, g_ref, rtT_ref, x2_ref, hnp_ref, route_ref):
    tm = x_ref.shape[0]
    x2 = x_ref[...] + jnp.dot(o_ref[...], wo_ref[...], preferred_element_type=F32)
    x2_ref[...] = x2
    hn = _rms(x2, g_ref[...])
    _store_planes(hnp_ref, _pack_bf16_pairs(hn))
    h0, h1, _ = _split3(hn)
    r0, r1, _ = _split3(rtT_ref[...])
    lt = lax.dot_general(jnp.concatenate([r0, r1, r0], axis=1), jnp.concatenate([h0, h0, h1], axis=1), NT_DIMS,
                         preferred_element_type=F32)[:N_EXPERTS]
    row = lax.broadcasted_iota(jnp.int32, (N_EXPERTS, tm), 0)
    m1 = jnp.max(lt, axis=0, keepdims=True)
    i1 = jnp.min(jnp.where(lt == m1, row, N_EXPERTS), axis=0, keepdims=True)
    rest = jnp.where(row == i1, -jnp.inf, lt)
    m2 = jnp.max(rest, axis=0, keepdims=True)
    i2 = jnp.min(jnp.where(rest == m2, row, N_EXPERTS), axis=0, keepdims=True)
    e2 = jnp.exp(m2 - m1)
    w1 = 1.0 / (1.0 + e2)
    w2 = e2 / (1.0 + e2)
    route_ref[...] = jnp.where(row == 0, i1.astype(F32), jnp.where(row == 1, i2.astype(F32),
                               jnp.where(row == 2, w1, jnp.where(row == 3, w2, 0.0))))


def _router(x2d, o2d, w_o, g, router, *, tm):
    t, d = x2d.shape
    da = o2d.shape[1]
    ne = router.shape[1]
    rtT = jnp.pad(router.T, ((0, 2 * SUBLANES - ne), (0, 0)))
    return pl.pallas_call(
        _router_kernel,
        grid=(t // tm,),
        in_specs=[
            pl.BlockSpec((tm, d), lambda i: (i, 0)),
            pl.BlockSpec((tm, da), lambda i: (i, 0)),
            pl.BlockSpec((da, d), lambda i: (0, 0)),
            pl.BlockSpec((1, d), lambda i: (0, 0)),
            pl.BlockSpec((2 * SUBLANES, d), lambda i: (0, 0)),
        ],
        out_specs=[pl.BlockSpec((tm, d), lambda i: (i, 0)), pl.BlockSpec((2, tm, d // 4), lambda i: (0, i, 0)),
                   pl.BlockSpec((ne, tm), lambda i: (0, i))],
        out_shape=[jax.ShapeDtypeStruct((t, d), F32), jax.ShapeDtypeStruct((2, t, d // 4), jnp.int32),
                   jax.ShapeDtypeStruct((ne, t), F32)],
        compiler_params=pltpu.CompilerParams(dimension_semantics=("parallel",)),
        name="attn_out_moe_router",
    )(x2d, o2d, w_o.astype(BF16), g.reshape(1, d), rtT)


def _routing_tables(route, ts):
    ne = N_EXPERTS
    t = route.shape[1]
    e_pair = route[0:2].astype(jnp.int32).reshape(-1)
    n_tiles = (2 * t) // ts + ne
    n_slots = n_tiles * ts
    onehot = (e_pair[:, None] == jnp.arange(ne, dtype=jnp.int32)[None, :]).astype(jnp.int32)
    csum = jnp.cumsum(onehot, axis=0)
    rank = jnp.sum(onehot * csum, axis=1) - 1
    tiles = (csum[-1] + ts - 1) // ts
    tile_end = jnp.cumsum(tiles)
    tile_start = tile_end - tiles
    slot = (jnp.sum(onehot * tile_start[None, :], axis=1) * ts + rank).astype(jnp.int32)
    n_used = tile_end[-1]
    tile_ids = jnp.arange(n_tiles, dtype=jnp.int32)
    tile_expert = jnp.sum((tile_ids[:, None] >= tile_end[None, :]).astype(jnp.int32), axis=1)
    last_expert = jnp.sum(((n_used - 1) >= tile_end).astype(jnp.int32))
    tile_expert = jnp.minimum(tile_expert, last_expert).astype(jnp.int32)
    plane_rows = slot.reshape(2, 1, t) + (jnp.arange(2, dtype=jnp.int32) * n_slots).reshape(1, 2, 1)
    return plane_rows.reshape(-1), tile_expert, n_used.reshape(1).astype(jnp.int32), n_slots


def _sc_mesh():
    return plsc.VectorSubcoreMesh(core_axis_name="core", subcore_axis_name="subcore")


def _scatter_rows(table, idx, n_out):
    m = idx.shape[0]
    n, d = table.shape
    assert m % SC_ROW_WINDOW == 0 and n % SC_ROW_WINDOW == 0
    n_src_blocks = n // SC_ROW_WINDOW

    @pl.kernel(out_type=jax.ShapeDtypeStruct((n_out, d), table.dtype), mesh=_sc_mesh(), scratch_types=[])
    def scatter_kernel(x_hbm, i_hbm, o_hbm):
        def body(x_vmem, i_vmem):
            pltpu.sync_copy(x_vmem, o_hbm.at[i_vmem.at[0]])

        pltpu.emit_pipeline(
            body,
            grid=(m // SC_ROW_WINDOW,),
            in_specs=[pl.BlockSpec((SC_ROW_WINDOW, d), index_map=lambda i: (i % n_src_blocks, 0)),
                      pl.BlockSpec((1, SC_ROW_WINDOW), index_map=lambda i: (0, i))],
            out_specs=[],
            core_axis_name=("core", "subcore"),
            dimension_semantics=(pltpu.PARALLEL,),
        )(x_hbm, i_hbm)

    return scatter_kernel(table, idx.reshape(1, m))


def _gather_rows(table, idx):
    m = idx.shape[0]
    d = table.shape[1]
    assert m % SC_ROW_WINDOW == 0

    @pl.kernel(out_type=jax.ShapeDtypeStruct((m, d), table.dtype), mesh=_sc_mesh())
    def gather_kernel(x_hbm, i_hbm, o_hbm):
        def body(i_vmem, o_vmem):
            pltpu.sync_copy(x_hbm.at[i_vmem.at[0]], o_vmem)

        pltpu.emit_pipeline(
            body,
            grid=(m // SC_ROW_WINDOW,),
            in_specs=[pl.BlockSpec((1, SC_ROW_WINDOW), index_map=lambda i: (0, i))],
            out_specs=[pl.BlockSpec((SC_ROW_WINDOW, d), index_map=lambda i: (i, 0))],
            core_axis_name=("core", "subcore"),
            dimension_semantics=(pltpu.PARALLEL,),
        )(i_hbm, o_hbm)

    return gather_kernel(table, idx.reshape(1, m))


def _expert_kernel(te_ref, nu_ref, xs_ref, wg_ref, wu_ref, wd_ref, ys_ref, x_sc, acc_ref):
    i = pl.program_id(0)
    k = pl.program_id(1)
    last_k = pl.num_programs(1) - 1
    used = i < nu_ref[0]

    @pl.when(used & (k == 0))
    def _():
        x_sc[...] = _unpack_bf16_pairs(_load_planes(xs_ref)).astype(BF16)
        acc_ref[...] = jnp.zeros_like(acc_ref)

    @pl.when(used)
    def _():
        x = x_sc[...]
        gate = jnp.dot(x, wg_ref[0], preferred_element_type=F32)
        up = jnp.dot(x, wu_ref[0], preferred_element_type=F32)
        h = (_silu(gate) * up).astype(BF16)
        acc_ref[...] += jnp.dot(h, wd_ref[0], preferred_element_type=F32)

    @pl.when(used & (k == last_k))
    def _():
        _store_planes(ys_ref, _pack_bf16_pairs(acc_ref[...]))

    @pl.when(jnp.logical_not(used) & (k == last_k))
    def _():
        ys_ref[...] = jnp.zeros_like(ys_ref)


def _experts(xs, tile_expert, n_used, w_gu, w_down, *, ts, tf):
    _, n_slots, quarter = xs.shape
    d = 4 * quarter
    dexp = w_down.shape[1]
    nk = dexp // tf
    w_gu = w_gu.astype(BF16)
    grid_spec = pltpu.PrefetchScalarGridSpec(
        num_scalar_prefetch=2,
        grid=(n_slots // ts, nk),
        in_specs=[
            pl.BlockSpec((2, ts, quarter), lambda i, k, te, nu: (0, i, 0)),
            pl.BlockSpec((1, d, tf), lambda i, k, te, nu: (te[i], 0, k)),
            pl.BlockSpec((1, d, tf), lambda i, k, te, nu: (te[i], 0, nk + k)),
            pl.BlockSpec((1, tf, d), lambda i, k, te, nu: (te[i], k, 0)),
        ],
        out_specs=pl.BlockSpec((2, ts, quarter), lambda i, k, te, nu: (0, i, 0)),
        scratch_shapes=[pltpu.VMEM((ts, d), BF16), pltpu.VMEM((ts, d), F32)],
    )
    return pl.pallas_call(
        _expert_kernel,
        grid_spec=grid_spec,
        out_shape=jax.ShapeDtypeStruct((2, n_slots, quarter), jnp.int32),
        compiler_params=pltpu.CompilerParams(dimension_semantics=("arbitrary", "arbitrary")),
        name="moe_experts",
    )(tile_expert, n_used, xs, w_gu, w_gu, w_down.astype(BF16))


def _combine_kernel(x_ref, y_ref, rt_ref, gf_ref, o_ref):
    rt = rt_ref[...]
    moe = (rt[:, 2:3] * _unpack_bf16_pairs(_load_planes(y_ref, 0))
           + rt[:, 3:4] * _unpack_bf16_pairs(_load_planes(y_ref, 2)))
    o_ref[...] = _rms(x_ref[...] + moe, gf_ref[...])


def _combine(x2d, y4, route_t, g_final, *, tm):
    t, d = x2d.shape
    ne = route_t.shape[1]
    return pl.pallas_call(
        _combine_kernel,
        grid=(t // tm,),
        in_specs=[
            pl.BlockSpec((tm, d), lambda i: (i, 0)),
            pl.BlockSpec((4, tm, d // 4), lambda i: (0, i, 0)),
            pl.BlockSpec((tm, ne), lambda i: (i, 0)),
            pl.BlockSpec((1, d), lambda i: (0, 0)),
        ],
        out_specs=pl.BlockSpec((tm, d), lambda i: (i, 0)),
        out_shape=jax.ShapeDtypeStruct((t, d), F32),
        compiler_params=pltpu.CompilerParams(dimension_semantics=("parallel",)),
        name="moe_combine",
    )(x2d, y4, route_t, g_final.reshape(1, d))


def _attn_out_moe(x1, o2d, w_o, g, router, w_gu, w_down, g_final, *, tm, ts, tf):
    t, d = x1.shape
    q = d // 4
    x2d, hn_planes, route = _router(x1, o2d, w_o, g, router, tm=tm)
    plane_rows, tile_expert, n_used, n_slots = _routing_tables(route, ts)
    xs = _scatter_rows(hn_planes.reshape(2 * t, q), plane_rows, 2 * n_slots)
    ys = _experts(xs.reshape(2, n_slots, q), tile_expert, n_used, w_gu, w_down, ts=ts, tf=tf)
    y4 = _gather_rows(ys.reshape(2 * n_slots, q), plane_rows)
    return _combine(x2d, y4.reshape(4, t, q), route.T, g_final, tm=tm)


def _tile(n, pref):
    t = min(n, pref)
    assert n % t == 0
    return t


def kernel(x, norm_g, lru_w_in, lru_conv_w, lru_conv_b, lru_gate_w, lru_gate_b, lru_lambda, lru_w_out,
           kv_norm_g, w_kvf, b_f, fox_w_qg, fox_w_o, mlp_w_gu, mlp_w_down, moe_router, moe_w_gu,
           moe_w_down, final_norm_g):
    bsz, s, d = x.shape
    t = bsz * s
    tm = _tile(t, 512)
    ts = _tile(s, 512)

    x = _lru_layer(x, norm_g[0, 0], lru_w_in[0], lru_conv_w[0], lru_conv_b[0], lru_gate_w[0],
                   lru_gate_b[0], lru_lambda[0], lru_w_out[0], tt=_tile(s, 256))
    dff = mlp_w_down.shape[1]
    x = _mlp(x.reshape(t, d), norm_g[0, 1], mlp_w_gu[0], mlp_w_down[0], tm=tm, tf=dff // 2).reshape(bsz, s, d)

    consts = _head_selectors()
    w_kvf_p, b_f_p, w_qg_p, w_o_p = _permute_heads(w_kvf, b_f, fox_w_qg[0], fox_w_o[0])
    ka, vT, c2r, kn = _kv_proj(x, kv_norm_g, w_kvf_p, b_f_p, consts, tk=ts)
    qa, gate, qn = _q_proj(x, norm_g[1, 0], w_qg_p, c2r, consts, tq=ts)
    jlo = _skip_table(qn, kn, c2r, ts)
    o = _attention(jlo, qa, ka, vT, gate, tq=ts)
    dexp = moe_w_down.shape[2]
    out = _attn_out_moe(x.reshape(t, d), o.reshape(t, -1), w_o_p, norm_g[1, 1], moe_router[0], moe_w_gu[0],
                        moe_w_down[0], final_norm_g, tm=tm, ts=tm, tf=dexp // 2)
    return out.reshape(bsz, s, d)
```

```python
import functools
import math

import jax
import jax.numpy as jnp
from jax import lax
from jax.experimental import pallas as pl
from jax.experimental.pallas import tpu as pltpu
from jax.experimental.pallas import tpu_sc as plsc

EPS = 1e-6
LRU_C = 8.0
N_LRU_BLOCKS = 8
CONV_W = 4
N_HEADS = 16
HEAD_DIM = 64
N_EXPERTS = 8
LANES = 128
SUBLANES = 8
HEAD_GROUP = 128
BIAS_Q = HEAD_DIM
BIAS_K = HEAD_DIM + 3
V_GROUP = 80
NEG_BIG = -1e30
LOG2E = math.log2(math.e)
SKIP_GAP_LOG2 = 152.0
NORM_SLACK = 1.01
SC_ROW_WINDOW = 128

F32 = jnp.float32
BF16 = jnp.bfloat16
NT_DIMS = (((1,), (1,)), ((), ()))


def _rms(x, g):
    ms = jnp.mean(x * x, axis=-1, keepdims=True)
    return x * lax.rsqrt(ms + EPS) * g


def _log_sigmoid(x):
    return jnp.minimum(x, 0.0) - jnp.log1p(jnp.exp(-jnp.abs(x)))


def _gelu_tanh(x):
    c = 0.7978845608028654
    return 0.5 * x * (1.0 + jnp.tanh(c * (x + 0.044715 * (x * x * x))))


def _silu(x):
    return x * jax.nn.sigmoid(x)


def _split3(x):
    p0 = x.astype(BF16)
    r1 = x - p0.astype(F32)
    p1 = r1.astype(BF16)
    p2 = (r1 - p1.astype(F32)).astype(BF16)
    return p0, p1, p2


def _const_spec(shape, n_grid):
    zeros = (0,) * len(shape)
    if n_grid == 1:
        return pl.BlockSpec(shape, lambda i: zeros)
    if n_grid == 2:
        return pl.BlockSpec(shape, lambda i, j: zeros)
    return pl.BlockSpec(shape, lambda i, j, k: zeros)


def _lru_kernel(x_ref, g_ref, win_ref, cw_ref, cb_ref, gw_ref, gb_ref, lam_ref, wout_ref,
                o_ref, xbuf, hcar):
    t = pl.program_id(1)
    tt = x_ref.shape[1]
    d = x_ref.shape[2]
    bw = d // N_LRU_BLOCKS

    @pl.when(t == 0)
    def _():
        xbuf[0:SUBLANES, :] = jnp.zeros((SUBLANES, d), F32)
        hcar[...] = jnp.zeros_like(hcar)

    x = x_ref[0]
    xn = _rms(x, g_ref[...]).astype(BF16)
    xg = jnp.dot(xn, win_ref[...], preferred_element_type=F32)
    xb = xg[:, :d]
    gbr = xg[:, d:]

    xbuf[SUBLANES:SUBLANES + tt, :] = xb
    cw = cw_ref[...]
    xc = cb_ref[...] + cw[CONV_W - 1:CONV_W, :] * xb
    for j in range(CONV_W - 1):
        xc = xc + cw[j:j + 1, :] * xbuf[pl.ds(SUBLANES - (CONV_W - 1) + j, tt), :]
    xbuf[0:SUBLANES, :] = xbuf[tt:tt + SUBLANES, :]

    xcb = xc.astype(BF16)
    rl, il = [], []
    for n in range(N_LRU_BLOCKS):
        gl = jnp.dot(xcb[:, n * bw:(n + 1) * bw], gw_ref[n], preferred_element_type=F32)
        rl.append(gl[:, :bw])
        il.append(gl[:, bw:])
    gb = gb_ref[...]
    r = jax.nn.sigmoid(jnp.concatenate(rl, axis=1) + gb[0:1, :])
    i = jax.nn.sigmoid(jnp.concatenate(il, axis=1) + gb[1:2, :])
    log_a = LRU_C * r * _log_sigmoid(lam_ref[...])
    a = jnp.exp(log_a)
    b = jnp.sqrt((1.0 - a) * (1.0 + a)) * (i * xc)

    row = lax.broadcasted_iota(jnp.int32, (tt, 1), 0)
    sh = 1
    while sh < tt:
        if sh < SUBLANES:
            keep = row >= sh
            a_sh = jnp.where(keep, pltpu.roll(a, sh, 0), 1.0)
            b_sh = jnp.where(keep, pltpu.roll(b, sh, 0), 0.0)
            b = a * b_sh + b
            a = a * a_sh
        else:
            b = jnp.concatenate([b[:sh], a[sh:] * b[:tt - sh] + b[sh:]], axis=0)
            a = jnp.concatenate([a[:sh], a[sh:] * a[:tt - sh]], axis=0)
        sh *= 2
    h = b + a * hcar[...]
    hcar[...] = h[tt - 1:tt, :]

    y = (_gelu_tanh(gbr) * h).astype(BF16)
    o_ref[0] = x + jnp.dot(y, wout_ref[...], preferred_element_type=F32)


def _lru_layer(x, g, w_in, conv_w, conv_b, gate_w, gate_b, lam, w_out, *, tt):
    bsz, s, d = x.shape
    bw = d // N_LRU_BLOCKS
    gw = jnp.concatenate([gate_w[0], gate_w[1]], axis=-1).astype(BF16)
    const = lambda *shape: _const_spec(shape, 2)
    return pl.pallas_call(
        _lru_kernel,
        grid=(bsz, s // tt),
        in_specs=[
            pl.BlockSpec((1, tt, d), lambda b, t: (b, t, 0)),
            const(1, d), const(d, 2 * d), const(CONV_W, d), const(1, d),
            const(N_LRU_BLOCKS, bw, 2 * bw), const(2, d), const(1, d), const(d, d),
        ],
        out_specs=pl.BlockSpec((1, tt, d), lambda b, t: (b, t, 0)),
        out_shape=jax.ShapeDtypeStruct((bsz, s, d), F32),
        scratch_shapes=[pltpu.VMEM((tt + 2 * SUBLANES, d), F32), pltpu.VMEM((1, d), F32)],
        compiler_params=pltpu.CompilerParams(dimension_semantics=("parallel", "arbitrary")),
        name="lru_layer",
    )(x, g.reshape(1, d), w_in.astype(BF16), conv_w, conv_b.reshape(1, d), gw, gate_b,
      lam.reshape(1, d), w_out.astype(BF16))


def _swiglu_accumulate(x, wg, wu, wd, acc_ref):
    gate = jnp.dot(x, wg[...], preferred_element_type=F32)
    up = jnp.dot(x, wu[...], preferred_element_type=F32)
    h = (_silu(gate) * up).astype(BF16)
    acc_ref[...] += jnp.dot(h, wd[...], preferred_element_type=F32)


def _mlp_kernel(x_ref, g_ref, wg_ref, wu_ref, wd_ref, o_ref, hn_ref, acc_ref):
    k = pl.program_id(1)

    @pl.when(k == 0)
    def _():
        hn_ref[...] = _rms(x_ref[...], g_ref[...]).astype(BF16)
        acc_ref[...] = jnp.zeros_like(acc_ref)

    _swiglu_accumulate(hn_ref[...], wg_ref, wu_ref, wd_ref, acc_ref)

    @pl.when(k == pl.num_programs(1) - 1)
    def _():
        o_ref[...] = x_ref[...] + acc_ref[...]


def _mlp(x2d, g, w_gu, w_down, *, tm, tf):
    t, d = x2d.shape
    dff = w_down.shape[0]
    nk = dff // tf
    w_gu = w_gu.astype(BF16)
    return pl.pallas_call(
        _mlp_kernel,
        grid=(t // tm, nk),
        in_specs=[
            pl.BlockSpec((tm, d), lambda i, k: (i, 0)),
            pl.BlockSpec((1, d), lambda i, k: (0, 0)),
            pl.BlockSpec((d, tf), lambda i, k: (0, k)),
            pl.BlockSpec((d, tf), lambda i, k: (0, nk + k)),
            pl.BlockSpec((tf, d), lambda i, k: (k, 0)),
        ],
        out_specs=pl.BlockSpec((tm, d), lambda i, k: (i, 0)),
        out_shape=jax.ShapeDtypeStruct((t, d), F32),
        scratch_shapes=[pltpu.VMEM((tm, d), BF16), pltpu.VMEM((tm, d), F32)],
        compiler_params=pltpu.CompilerParams(dimension_semantics=("parallel", "arbitrary")),
        name="mlp",
    )(x2d, g.reshape(1, d), w_gu, w_gu, w_down.astype(BF16))


def _head_selectors():
    hg = N_HEADS * HEAD_GROUP
    head = jnp.arange(hg) // HEAD_GROUP
    lane = jnp.arange(hg) % HEAD_GROUP
    col = jnp.arange(LANES)
    def sel(base):
        return ((col[:, None] % N_HEADS == head[None, :]) & (col[:, None] < 3 * N_HEADS)
                & (lane[None, :] == base + col[:, None] // N_HEADS)).astype(BF16)
    sel_k = -sel(BIAS_K)
    sel_q = sel(BIAS_Q).T
    ones_k = ((lane >= BIAS_Q) & (lane < BIAS_Q + 3)).astype(F32).reshape(1, hg)
    ones_q = ((lane >= BIAS_K) & (lane < BIAS_K + 3)).astype(F32).reshape(hg, 1)
    head_sum = ((col[None, :] == head[:, None]) & (lane[:, None] < HEAD_DIM)).astype(BF16)
    return sel_k, sel_q, ones_k, ones_q, head_sum


def _kv_kernel(x_ref, g_ref, wk_ref, wvT_ref, onesv_ref, wf_ref, wfT_ref, bfr_ref, bfc_ref, selk_ref, onesk_ref,
               hsum_ref, ka_ref, vT_ref, c2r_ref, kn_ref, car_r, car_c):
    t = pl.program_id(1)
    tk = x_ref.shape[1]

    @pl.when(t == 0)
    def _():
        car_r[...] = jnp.zeros_like(car_r)
        car_c[...] = jnp.zeros_like(car_c)

    xn = _rms(x_ref[0], g_ref[...]).astype(BF16)
    vT = lax.dot_general(wvT_ref[...], xn, NT_DIMS, preferred_element_type=F32) + onesv_ref[...]
    vT_ref[0, 0] = vT.astype(BF16)
    kf = jnp.dot(xn, wk_ref[...], preferred_element_type=F32)

    kss = jnp.dot((kf * kf).astype(BF16), hsum_ref[...], preferred_element_type=F32)
    kn = jnp.sqrt(jnp.max(kss, axis=0, keepdims=True))
    kn_ref[0, 0] = jnp.broadcast_to(kn, (SUBLANES, LANES))

    ri = lax.broadcasted_iota(jnp.int32, (tk, tk), 0)
    ci = lax.broadcasted_iota(jnp.int32, (tk, tk), 1)
    upper = (ri <= ci).astype(BF16)
    lower = (ci <= ri).astype(BF16)

    lf_c = _log_sigmoid(jnp.dot(xn, wf_ref[...], preferred_element_type=F32) + bfr_ref[...])
    cs3 = jnp.dot(lower, jnp.concatenate(_split3(lf_c), axis=1), preferred_element_type=F32)
    cs_c = car_c[...] + (cs3[:, :LANES] + cs3[:, LANES:2 * LANES] + cs3[:, 2 * LANES:])
    car_c[...] = cs_c[tk - 1:tk, :]
    p0, p1, p2 = (p.astype(F32) for p in _split3(cs_c * LOG2E))
    lane = lax.broadcasted_iota(jnp.int32, (tk, LANES), 1)
    packed = jnp.where(lane < N_HEADS, p0,
                       jnp.where(lane < 2 * N_HEADS, pltpu.roll(p1, N_HEADS, 1),
                                 jnp.where(lane < 3 * N_HEADS, pltpu.roll(p2, 2 * N_HEADS, 1), 0.0)))
    bias = jnp.dot(packed.astype(BF16), selk_ref[...], preferred_element_type=F32)
    ka_ref[0] = (kf + bias + onesk_ref[...]).astype(BF16)

    lf_r = _log_sigmoid(lax.dot_general(wfT_ref[...], xn, NT_DIMS, preferred_element_type=F32) + bfc_ref[...])
    cs3 = jnp.dot(jnp.concatenate(_split3(lf_r), axis=0), upper, preferred_element_type=F32)
    cs_r = car_r[...] + (cs3[:LANES] + cs3[LANES:2 * LANES] + cs3[2 * LANES:])
    car_r[...] = cs_r[:, tk - 1:tk]
    c2r_ref[0] = cs_r * LOG2E


def _kv_proj(x, g, w_kvf, b_f, consts, *, tk):
    bsz, s, d = x.shape
    da = N_HEADS * HEAD_DIM
    hg = N_HEADS * HEAD_GROUP
    dv = N_HEADS * V_GROUP
    nk = s // tk
    sel_k, _, ones_k, _, head_sum = consts
    wk = w_kvf[:, :da].reshape(d, N_HEADS, HEAD_DIM)
    wk = jnp.pad(wk, ((0, 0), (0, 0), (0, HEAD_GROUP - HEAD_DIM))).reshape(d, hg).astype(BF16)
    wv = w_kvf[:, da:2 * da].reshape(d, N_HEADS, HEAD_DIM)
    wvT = jnp.pad(wv, ((0, 0), (0, 0), (0, V_GROUP - HEAD_DIM))).reshape(d, dv).T.astype(BF16)
    ones_v = (jnp.arange(dv) % V_GROUP == HEAD_DIM).astype(F32).reshape(dv, 1)
    wf = jnp.pad(w_kvf[:, 2 * da:], ((0, 0), (0, LANES - N_HEADS))).astype(BF16)
    bfr = jnp.pad(b_f, (0, LANES - N_HEADS)).reshape(1, LANES)
    const = lambda *shape: _const_spec(shape, 2)
    return pl.pallas_call(
        _kv_kernel,
        grid=(bsz, nk),
        in_specs=[
            pl.BlockSpec((1, tk, d), lambda b, t: (b, t, 0)),
            const(1, d), const(d, hg), const(dv, d), const(dv, 1), const(d, LANES), const(LANES, d),
            const(1, LANES), const(LANES, 1), const(LANES, hg), const(1, hg), const(hg, LANES),
        ],
        out_specs=[
            pl.BlockSpec((1, tk, hg), lambda b, t: (b, t, 0)),
            pl.BlockSpec((1, 1, dv, tk), lambda b, t: (b, t, 0, 0)),
            pl.BlockSpec((1, LANES, tk), lambda b, t: (b, 0, t)),
            pl.BlockSpec((1, 1, SUBLANES, LANES), lambda b, t: (b, t, 0, 0)),
        ],
        out_shape=[
            jax.ShapeDtypeStruct((bsz, s, hg), BF16),
            jax.ShapeDtypeStruct((bsz, nk, dv, tk), BF16),
            jax.ShapeDtypeStruct((bsz, LANES, s), F32),
            jax.ShapeDtypeStruct((bsz, nk, SUBLANES, LANES), F32),
        ],
        scratch_shapes=[pltpu.VMEM((LANES, 1), F32), pltpu.VMEM((1, LANES), F32)],
        compiler_params=pltpu.CompilerParams(dimension_semantics=("parallel", "arbitrary")),
        name="kv_proj",
    )(x, g.reshape(1, d), wk, wvT, ones_v, wf, wf.T, bfr, bfr.reshape(LANES, 1), sel_k, ones_k, head_sum)


def _q_kernel(x_ref, g_ref, wqT_ref, wg_ref, c2r_ref, selq_ref, onesq_ref, hsumT_ref, qa_ref, gate_ref, qn_ref):
    xn = _rms(x_ref[0], g_ref[...]).astype(BF16)
    gate_ref[0] = jax.nn.sigmoid(jnp.dot(xn, wg_ref[...], preferred_element_type=F32)).astype(BF16)
    qf = lax.dot_general(wqT_ref[...], xn, NT_DIMS, preferred_element_type=F32) * (LOG2E * HEAD_DIM ** -0.5)

    qss = jnp.dot(hsumT_ref[...], (qf * qf).astype(BF16), preferred_element_type=F32)
    qn = jnp.sqrt(jnp.max(qss, axis=1, keepdims=True))
    qn_ref[0, 0] = jnp.broadcast_to(qn, (LANES, LANES))

    tq = c2r_ref.shape[2]
    packed = jnp.concatenate([p[:N_HEADS] for p in _split3(c2r_ref[0])]
                             + [jnp.zeros((LANES - 3 * N_HEADS, tq), BF16)], axis=0)
    bias = jnp.dot(selq_ref[...], packed, preferred_element_type=F32)
    qa_ref[0] = (qf + bias + onesq_ref[...]).astype(BF16)


def _q_proj(x, g, w_qg, c2r, consts, *, tq):
    bsz, s, d = x.shape
    da = N_HEADS * HEAD_DIM
    hg = N_HEADS * HEAD_GROUP
    nq = s // tq
    _, sel_q, _, ones_q, head_sum = consts
    wq = w_qg[:, :da].reshape(d, N_HEADS, HEAD_DIM)
    wqT = jnp.pad(wq, ((0, 0), (0, 0), (0, HEAD_GROUP - HEAD_DIM))).reshape(d, hg).T.astype(BF16)
    const = lambda *shape: _const_spec(shape, 2)
    return pl.pallas_call(
        _q_kernel,
        grid=(bsz, nq),
        in_specs=[
            pl.BlockSpec((1, tq, d), lambda b, t: (b, t, 0)),
            const(1, d), const(hg, d), const(d, da),
            pl.BlockSpec((1, LANES, tq), lambda b, t: (b, 0, t)),
            const(hg, LANES), const(hg, 1), const(LANES, hg),
        ],
        out_specs=[
            pl.BlockSpec((1, hg, tq), lambda b, t: (b, 0, t)),
            pl.BlockSpec((1, tq, da), lambda b, t: (b, t, 0)),
            pl.BlockSpec((1, 1, LANES, LANES), lambda b, t: (b, t, 0, 0)),
        ],
        out_shape=[
            jax.ShapeDtypeStruct((bsz, hg, s), BF16),
            jax.ShapeDtypeStruct((bsz, s, da), BF16),
            jax.ShapeDtypeStruct((bsz, nq, LANES, LANES), F32),
        ],
        compiler_params=pltpu.CompilerParams(dimension_semantics=("parallel", "parallel")),
        name="q_proj",
    )(x, g.reshape(1, d), wqT, w_qg[:, da:].astype(BF16), c2r, sel_q, ones_q, head_sum.T)


def _attn_kernel(jlo_ref, qa_ref, qan_ref, ka_ref, vT_ref, gate_ref, o_ref, ready, m0, m1, acc0, acc1,
                 sa0, sa1, sb0, sb1, sc0, sc1, pa0, pa1, pb0, pb1):
    b = pl.program_id(0)
    hp = pl.program_id(1)
    i = pl.program_id(2)
    nq = pl.num_programs(2)
    tq = qa_ref.shape[2]
    tk = vT_ref.shape[3]
    base = (b * pl.num_programs(1) + hp) * nq
    j_lo = jlo_ref[base + i]
    j_lo_next = jlo_ref[base + jnp.minimum(i + 1, nq - 1)]
    m_refs, acc_refs = (m0, m1), (acc0, acc1)
    s_a, s_b, s_c = (sa0, sa1), (sb0, sb1), (sc0, sc1)
    p_a, p_b = (pa0, pa1), (pb0, pb1)

    def scores(q_ref, j, h):
        k_blk = ka_ref[0, pl.ds(pl.multiple_of(j * tk, tk), tk), h * HEAD_GROUP:(h + 1) * HEAD_GROUP]
        qT = q_ref[0, h * HEAD_GROUP:(h + 1) * HEAD_GROUP, :]
        return jnp.dot(k_blk, qT, preferred_element_type=F32)

    def values(j, h, p):
        vT = vT_ref[0, j, h * V_GROUP:(h + 1) * V_GROUP, :]
        return jnp.dot(vT, p, preferred_element_type=F32)

    def softmax_tile(h, s, masked):
        if masked:
            kv_pos = lax.broadcasted_iota(jnp.int32, (tk, tq), 0)
            q_pos = lax.broadcasted_iota(jnp.int32, (tk, tq), 1)
            s = jnp.where(kv_pos <= q_pos, s, NEG_BIG)
        m_old = m_refs[h][...]
        m_new = jnp.maximum(m_old, jnp.max(s, axis=0, keepdims=True))
        m_refs[h][...] = m_new
        return jnp.exp2(m_old - m_new), jnp.exp2(s - m_new).astype(BF16)

    def step(s_src, s_dst, p_prev, p_dst, j):
        for h in range(2):
            s_dst[h][...] = scores(qa_ref, j + 1, h)
        pv = None if p_prev is None else [values(j - 1, h, p_prev[h][...]) for h in range(2)]
        for h in range(2):
            alpha, p = softmax_tile(h, s_src[h][...], masked=False)
            p_dst[h][...] = p
            if pv is not None:
                acc_refs[h][...] = alpha * (acc_refs[h][...] + pv[h])

    def finish(s_src, p_prev, prefetch):
        if prefetch:
            for h in range(2):
                s_c[h][...] = scores(qan_ref, j_lo_next, h)
            ready[0] = (i + 1 < nq).astype(jnp.int32)
        else:
            ready[0] = jnp.int32(0)
        pv = None if p_prev is None else [values(i - 1, h, p_prev[h][...]) for h in range(2)]
        outs = []
        for h in range(2):
            alpha, p = softmax_tile(h, s_src[h][...], masked=True)
            acc = values(i, h, p)
            if pv is not None:
                acc = acc + alpha * (acc_refs[h][...] + pv[h])
            outs.append(acc[0:HEAD_DIM, :] / acc[HEAD_DIM:HEAD_DIM + 1, :])
        oT = jnp.concatenate(outs, axis=0)
        o_ref[0] = (oT.T * gate_ref[0].astype(F32)).astype(BF16)

    @pl.when(i == 0)
    def _():
        ready[0] = jnp.int32(0)

    @pl.when(ready[0] == 0)
    def _():
        for h in range(2):
            s_c[h][...] = scores(qa_ref, j_lo, h)

    for h in range(2):
        m_refs[h][...] = jnp.full((1, tq), NEG_BIG, F32)
        acc_refs[h][...] = jnp.zeros((V_GROUP, tq), F32)

    n_full = i - j_lo

    @pl.when(n_full == 0)
    def _():
        finish(s_c, None, prefetch=False)

    @pl.when(n_full > 0)
    def _():
        step(s_c, s_a, None, p_a, j_lo)
        rest = n_full - 1

        def body(n, carry):
            j = j_lo + 1 + 2 * n
            step(s_a, s_b, p_a, p_b, j)
            step(s_b, s_a, p_b, p_a, j + 1)
            return carry

        lax.fori_loop(0, rest // 2, body, 0)

        @pl.when(rest % 2 == 1)
        def _():
            step(s_a, s_b, p_a, p_b, i - 1)
            finish(s_b, p_b, prefetch=True)

        @pl.when(rest % 2 == 0)
        def _():
            finish(s_a, p_a, prefetch=True)


def _permute_heads(w_kvf, b_f, w_qg, w_o):
    da = N_HEADS * HEAD_DIM
    perm = jnp.argsort(b_f)
    cols = (perm[:, None] * HEAD_DIM + jnp.arange(HEAD_DIM)[None, :]).reshape(-1)
    w_kvf_p = jnp.concatenate([w_kvf[:, :da][:, cols], w_kvf[:, da:2 * da][:, cols], w_kvf[:, 2 * da:][:, perm]],
                              axis=1)
    w_qg_p = jnp.concatenate([w_qg[:, :da][:, cols], w_qg[:, da:][:, cols]], axis=1)
    return w_kvf_p, b_f[perm], w_qg_p, w_o[cols, :]


def _skip_table(qn, kn, c2r, tq):
    bsz, nq = qn.shape[0], qn.shape[1]
    qn = qn[:, :, :N_HEADS, 0].transpose(0, 2, 1) * NORM_SLACK
    kn = kn[:, :, 0, :N_HEADS].transpose(0, 2, 1) * NORM_SLACK
    c2 = c2r[:, :N_HEADS, :]
    c_first = c2[:, :, ::tq]
    c_last = c2[:, :, tq - 1::tq]
    upper = qn[..., :, None] * kn[..., None, :] + c_first[..., :, None] - c_last[..., None, :]
    lower = -(qn * kn)[..., :, None]
    skip = (upper - lower) < -SKIP_GAP_LOG2
    nk = skip.shape[-1]
    jlo = jnp.min(jnp.where(skip, nk, jnp.arange(nk, dtype=jnp.int32)), axis=-1)
    jlo = jnp.minimum(jlo, jnp.arange(nq, dtype=jnp.int32))
    jlo = jnp.min(jlo.reshape(bsz, N_HEADS // 2, 2, nq), axis=2)
    return jlo.reshape(-1).astype(jnp.int32)


def _attention(jlo, qa, ka, vT, gate, *, tq):
    bsz, hg, s = qa.shape
    nk, tk = vT.shape[1], vT.shape[3]
    da = N_HEADS * HEAD_DIM
    assert tq == tk
    pair = 2 * HEAD_GROUP
    hw = 2 * HEAD_DIM
    nq = s // tq
    grid_spec = pltpu.PrefetchScalarGridSpec(
        num_scalar_prefetch=1,
        grid=(bsz, N_HEADS // 2, nq),
        in_specs=[
            pl.BlockSpec((1, pair, tq), lambda b, hp, i, jlo: (b, hp, i)),
            pl.BlockSpec((1, pair, tq), lambda b, hp, i, jlo: (b, hp, jnp.minimum(i + 1, nq - 1))),
            pl.BlockSpec((1, s, pair), lambda b, hp, i, jlo: (b, 0, hp)),
            pl.BlockSpec((1, nk, 2 * V_GROUP, tk), lambda b, hp, i, jlo: (b, 0, hp, 0)),
            pl.BlockSpec((1, tq, hw), lambda b, hp, i, jlo: (b, i, hp)),
        ],
        out_specs=pl.BlockSpec((1, tq, hw), lambda b, hp, i, jlo: (b, i, hp)),
        scratch_shapes=([pltpu.SMEM((1,), jnp.int32)]
                        + [pltpu.VMEM((1, tq), F32)] * 2 + [pltpu.VMEM((V_GROUP, tq), F32)] * 2
                        + [pltpu.VMEM((tk, tq), F32)] * 6 + [pltpu.VMEM((tk, tq), BF16)] * 4),
    )
    return pl.pallas_call(
        _attn_kernel,
        grid_spec=grid_spec,
        out_shape=jax.ShapeDtypeStruct((bsz, s, da), BF16),
        compiler_params=pltpu.CompilerParams(dimension_semantics=("parallel", "parallel", "arbitrary")),
        name="fox_attention",
    )(jlo, qa, qa, ka, vT, gate)


def _pack_bf16_pairs(x):
    half = x.shape[1] // 2
    lo = lax.bitcast_convert_type(x[:, :half].astype(BF16).astype(F32), jnp.uint32)
    hi = lax.bitcast_convert_type(x[:, half:].astype(BF16).astype(F32), jnp.uint32)
    word = (lo >> 16) | (hi & jnp.uint32(0xFFFF0000))
    return lax.bitcast_convert_type(word, jnp.int32)


def _unpack_bf16_pairs(w):
    u = lax.bitcast_convert_type(w, jnp.uint32)
    lo = lax.bitcast_convert_type(u << 16, F32)
    hi = lax.bitcast_convert_type(u & jnp.uint32(0xFFFF0000), F32)
    return jnp.concatenate([lo, hi], axis=1)


def _store_planes(ref, words):
    q = words.shape[1] // 2
    ref[0] = words[:, :q]
    ref[1] = words[:, q:]


def _load_planes(ref, base=0):
    return jnp.concatenate([ref[base], ref[base + 1]], axis=1)


def _router_kernel(x_ref, o_ref, wo_ref, g_ref, rtT_ref, x2_ref, hnp_ref, route_ref):
    tm = x_ref.shape[0]
    x2 = x_ref[...] + jnp.dot(o_ref[...], wo_ref[...], preferred_element_type=F32)
    x2_ref[...] = x2
    hn = _rms(x2, g_ref[...])
    _store_planes(hnp_ref, _pack_bf16_pairs(hn))
    h0, h1, _ = _split3(hn)
    r0, r1, _ = _split3(rtT_ref[...])
    lt = lax.dot_general(jnp.concatenate([r0, r1, r0], axis=1), jnp.concatenate([h0, h0, h1], axis=1), NT_DIMS,
                         preferred_element_type=F32)[:N_EXPERTS]
    row = lax.broadcasted_iota(jnp.int32, (N_EXPERTS, tm), 0)
    m1 = jnp.max(lt, axis=0, keepdims=True)
    i1 = jnp.min(jnp.where(lt == m1, row, N_EXPERTS), axis=0, keepdims=True)
    rest = jnp.where(row == i1, -jnp.inf, lt)
    m2 = jnp.max(rest, axis=0, keepdims=True)
    i2 = jnp.min(jnp.where(rest == m2, row, N_EXPERTS), axis=0, keepdims=True)
    e2 = jnp.exp(m2 - m1)
    w1 = 1.0 / (1.0 + e2)
    w2 = e2 / (1.0 + e2)
    route_ref[...] = jnp.where(row == 0, i1.astype(F32), jnp.where(row == 1, i2.astype(F32),
                               jnp.where(row == 2, w1, jnp.where(row == 3, w2, 0.0))))


def _router(x2d, o2d, w_o, g, router, *, tm):
    t, d = x2d.shape
    da = o2d.shape[1]
    ne = router.shape[1]
    rtT = jnp.pad(router.T, ((0, 2 * SUBLANES - ne), (0, 0)))
    return pl.pallas_call(
        _router_kernel,
        grid=(t // tm,),
        in_specs=[
            pl.BlockSpec((tm, d), lambda i: (i, 0)),
            pl.BlockSpec((tm, da), lambda i: (i, 0)),
            pl.BlockSpec((da, d), lambda i: (0, 0)),
            pl.BlockSpec((1, d), lambda i: (0, 0)),
            pl.BlockSpec((2 * SUBLANES, d), lambda i: (0, 0)),
        ],
        out_specs=[pl.BlockSpec((tm, d), lambda i: (i, 0)), pl.BlockSpec((2, tm, d // 4), lambda i: (0, i, 0)),
                   pl.BlockSpec((ne, tm), lambda i: (0, i))],
        out_shape=[jax.ShapeDtypeStruct((t, d), F32), jax.ShapeDtypeStruct((2, t, d // 4), jnp.int32),
                   jax.ShapeDtypeStruct((ne, t), F32)],
        compiler_params=pltpu.CompilerParams(dimension_semantics=("parallel",)),
        name="attn_out_moe_router",
    )(x2d, o2d, w_o.astype(BF16), g.reshape(1, d), rtT)


def _routing_tables(route, ts):
    ne = N_EXPERTS
    t = route.shape[1]
    e_pair = route[0:2].astype(jnp.int32).reshape(-1)
    n_tiles = (2 * t) // ts + ne
    n_slots = n_tiles * ts
    onehot = (e_pair[:, None] == jnp.arange(ne, dtype=jnp.int32)[None, :]).astype(jnp.int32)
    csum = jnp.cumsum(onehot, axis=0)
    rank = jnp.sum(onehot * csum, axis=1) - 1
    tiles = (csum[-1] + ts - 1) // ts
    tile_end = jnp.cumsum(tiles)
    tile_start = tile_end - tiles
    slot = (jnp.sum(onehot * tile_start[None, :], axis=1) * ts + rank).astype(jnp.int32)
    n_used = tile_end[-1]
    tile_ids = jnp.arange(n_tiles, dtype=jnp.int32)
    tile_expert = jnp.sum((tile_ids[:, None] >= tile_end[None, :]).astype(jnp.int32), axis=1)
    last_expert = jnp.sum(((n_used - 1) >= tile_end).astype(jnp.int32))
    tile_expert = jnp.minimum(tile_expert, last_expert).astype(jnp.int32)
    plane_rows = slot.reshape(2, 1, t) + (jnp.arange(2, dtype=jnp.int32) * n_slots).reshape(1, 2, 1)
    return plane_rows.reshape(-1), tile_expert, n_used.reshape(1).astype(jnp.int32), n_slots


def _sc_mesh():
    return plsc.VectorSubcoreMesh(core_axis_name="core", subcore_axis_name="subcore")


def _scatter_rows(table, idx, n_out):
    m = idx.shape[0]
    n, d = table.shape
    assert m % SC_ROW_WINDOW == 0 and n % SC_ROW_WINDOW == 0
    n_src_blocks = n // SC_ROW_WINDOW

    @pl.kernel(out_type=jax.ShapeDtypeStruct((n_out, d), table.dtype), mesh=_sc_mesh(), scratch_types=[])
    def scatter_kernel(x_hbm, i_hbm, o_hbm):
        def body(x_vmem, i_vmem):
            pltpu.sync_copy(x_vmem, o_hbm.at[i_vmem.at[0]])

        pltpu.emit_pipeline(
            body,
            grid=(m // SC_ROW_WINDOW,),
            in_specs=[pl.BlockSpec((SC_ROW_WINDOW, d), index_map=lambda i: (i % n_src_blocks, 0)),
                      pl.BlockSpec((1, SC_ROW_WINDOW), index_map=lambda i: (0, i))],
            out_specs=[],
            core_axis_name=("core", "subcore"),
            dimension_semantics=(pltpu.PARALLEL,),
        )(x_hbm, i_hbm)

    return scatter_kernel(table, idx.reshape(1, m))


def _gather_rows(table, idx):
    m = idx.shape[0]
    d = table.shape[1]
    assert m % SC_ROW_WINDOW == 0

    @pl.kernel(out_type=jax.ShapeDtypeStruct((m, d), table.dtype), mesh=_sc_mesh())
    def gather_kernel(x_hbm, i_hbm, o_hbm):
        def body(i_vmem, o_vmem):
            pltpu.sync_copy(x_hbm.at[i_vmem.at[0]], o_vmem)

        pltpu.emit_pipeline(
            body,
            grid=(m // SC_ROW_WINDOW,),
            in_specs=[pl.BlockSpec((1, SC_ROW_WINDOW), index_map=lambda i: (0, i))],
            out_specs=[pl.BlockSpec((SC_ROW_WINDOW, d), index_map=lambda i: (i, 0))],
            core_axis_name=("core", "subcore"),
            dimension_semantics=(pltpu.PARALLEL,),
        )(i_hbm, o_hbm)

    return gather_kernel(table, idx.reshape(1, m))


def _expert_kernel(te_ref, nu_ref, xs_ref, wg_ref, wu_ref, wd_ref, ys_ref, x_sc, acc_ref):
    i = pl.program_id(0)
    k = pl.program_id(1)
    last_k = pl.num_programs(1) - 1
    used = i < nu_ref[0]

    @pl.when(used & (k == 0))
    def _():
        x_sc[...] = _unpack_bf16_pairs(_load_planes(xs_ref)).astype(BF16)
        acc_ref[...] = jnp.zeros_like(acc_ref)

    @pl.when(used)
    def _():
        _swiglu_accumulate(x_sc[...], wg_ref.at[0], wu_ref.at[0], wd_ref.at[0], acc_ref)

    @pl.when(used & (k == last_k))
    def _():
        _store_planes(ys_ref, _pack_bf16_pairs(acc_ref[...]))

    @pl.when(jnp.logical_not(used) & (k == last_k))
    def _():
        ys_ref[...] = jnp.zeros_like(ys_ref)


def _experts(xs, tile_expert, n_used, w_gu, w_down, *, ts, tf):
    _, n_slots, quarter = xs.shape
    d = 4 * quarter
    dexp = w_down.shape[1]
    nk = dexp // tf
    w_gu = w_gu.astype(BF16)
    grid_spec = pltpu.PrefetchScalarGridSpec(
        num_scalar_prefetch=2,
        grid=(n_slots // ts, nk),
        in_specs=[
            pl.BlockSpec((2, ts, quarter), lambda i, k, te, nu: (0, i, 0)),
            pl.BlockSpec((1, d, tf), lambda i, k, te, nu: (te[i], 0, k)),
            pl.BlockSpec((1, d, tf), lambda i, k, te, nu: (te[i], 0, nk + k)),
            pl.BlockSpec((1, tf, d), lambda i, k, te, nu: (te[i], k, 0)),
        ],
        out_specs=pl.BlockSpec((2, ts, quarter), lambda i, k, te, nu: (0, i, 0)),
        scratch_shapes=[pltpu.VMEM((ts, d), BF16), pltpu.VMEM((ts, d), F32)],
    )
    return pl.pallas_call(
        _expert_kernel,
        grid_spec=grid_spec,
        out_shape=jax.ShapeDtypeStruct((2, n_slots, quarter), jnp.int32),
        compiler_params=pltpu.CompilerParams(dimension_semantics=("arbitrary", "arbitrary")),
        name="moe_experts",
    )(tile_expert, n_used, xs, w_gu, w_gu, w_down.astype(BF16))


def _combine_kernel(x_ref, y_ref, rt_ref, gf_ref, o_ref):
    rt = rt_ref[...]
    moe = (rt[:, 2:3] * _unpack_bf16_pairs(_load_planes(y_ref, 0))
           + rt[:, 3:4] * _unpack_bf16_pairs(_load_planes(y_ref, 2)))
    o_ref[...] = _rms(x_ref[...] + moe, gf_ref[...])


def _combine(x2d, y4, route_t, g_final, *, tm):
    t, d = x2d.shape
    ne = route_t.shape[1]
    return pl.pallas_call(
        _combine_kernel,
        grid=(t // tm,),
        in_specs=[
            pl.BlockSpec((tm, d), lambda i: (i, 0)),
            pl.BlockSpec((4, tm, d // 4), lambda i: (0, i, 0)),
            pl.BlockSpec((tm, ne), lambda i: (i, 0)),
            pl.BlockSpec((1, d), lambda i: (0, 0)),
        ],
        out_specs=pl.BlockSpec((tm, d), lambda i: (i, 0)),
        out_shape=jax.ShapeDtypeStruct((t, d), F32),
        compiler_params=pltpu.CompilerParams(dimension_semantics=("parallel",)),
        name="moe_combine",
    )(x2d, y4, route_t, g_final.reshape(1, d))


def _attn_out_moe(x1, o2d, w_o, g, router, w_gu, w_down, g_final, *, tm, ts, tf):
    t, d = x1.shape
    q = d // 4
    x2d, hn_planes, route = _router(x1, o2d, w_o, g, router, tm=tm)
    plane_rows, tile_expert, n_used, n_slots = _routing_tables(route, ts)
    xs = _scatter_rows(hn_planes.reshape(2 * t, q), plane_rows, 2 * n_slots)
    ys = _experts(xs.reshape(2, n_slots, q), tile_expert, n_used, w_gu, w_down, ts=ts, tf=tf)
    y4 = _gather_rows(ys.reshape(2 * n_slots, q), plane_rows)
    return _combine(x2d, y4.reshape(4, t, q), route.T, g_final, tm=tm)


def _tile(n, pref):
    t = min(n, pref)
    assert n % t == 0
    return t


def kernel(x, norm_g, lru_w_in, lru_conv_w, lru_conv_b, lru_gate_w, lru_gate_b, lru_lambda, lru_w_out,
           kv_norm_g, w_kvf, b_f, fox_w_qg, fox_w_o, mlp_w_gu, mlp_w_down, moe_router, moe_w_gu,
           moe_w_down, final_norm_g):
    bsz, s, d = x.shape
    t = bsz * s
    tm = _tile(t, 512)
    ts = _tile(s, 512)

    x = _lru_layer(x, norm_g[0, 0], lru_w_in[0], lru_conv_w[0], lru_conv_b[0], lru_gate_w[0],
                   lru_gate_b[0], lru_lambda[0], lru_w_out[0], tt=_tile(s, 256))
    dff = mlp_w_down.shape[1]
    x = _mlp(x.reshape(t, d), norm_g[0, 1], mlp_w_gu[0], mlp_w_down[0], tm=tm, tf=dff // 2).reshape(bsz, s, d)

    consts = _head_selectors()
    w_kvf_p, b_f_p, w_qg_p, w_o_p = _permute_heads(w_kvf, b_f, fox_w_qg[0], fox_w_o[0])
    ka, vT, c2r, kn = _kv_proj(x, kv_norm_g, w_kvf_p, b_f_p, consts, tk=ts)
    qa, gate, qn = _q_proj(x, norm_g[1, 0], w_qg_p, c2r, consts, tq=ts)
    jlo = _skip_table(qn, kn, c2r, ts)
    o = _attention(jlo, qa, ka, vT, gate, tq=ts)
    dexp = moe_w_down.shape[2]
    out = _attn_out_moe(x.reshape(t, d), o.reshape(t, -1), w_o_p, norm_g[1, 1], moe_router[0], moe_w_gu[0],
                        moe_w_down[0], final_norm_g, tm=tm, ts=tm, tf=dexp // 2)
    return out.reshape(bsz, s, d)
```

```python
import functools
import math

import jax
import jax.numpy as jnp
from jax import lax
from jax.experimental import pallas as pl
from jax.experimental.pallas import tpu as pltpu
from jax.experimental.pallas import tpu_sc as plsc

EPS = 1e-6
LRU_C = 8.0
N_LRU_BLOCKS = 8
CONV_W = 4
N_HEADS = 16
HEAD_DIM = 64
N_EXPERTS = 8
LANES = 128
SUBLANES = 8
HEAD_GROUP = 128
BIAS_Q = HEAD_DIM
BIAS_K = HEAD_DIM + 3
V_GROUP = 80
NEG_BIG = -1e30
LOG2E = math.log2(math.e)
SKIP_GAP_LOG2 = 152.0
NORM_SLACK = 1.01
SC_ROW_WINDOW = 128

F32 = jnp.float32
BF16 = jnp.bfloat16
NT_DIMS = (((1,), (1,)), ((), ()))


def _rms(x, g):
    ms = jnp.mean(x * x, axis=-1, keepdims=True)
    return x * lax.rsqrt(ms + EPS) * g


def _log_sigmoid(x):
    return jnp.minimum(x, 0.0) - jnp.log1p(jnp.exp(-jnp.abs(x)))


def _gelu_tanh(x):
    c = 0.7978845608028654
    return 0.5 * x * (1.0 + jnp.tanh(c * (x + 0.044715 * (x * x * x))))


def _silu(x):
    return x * jax.nn.sigmoid(x)


def _split3(x):
    p0 = x.astype(BF16)
    r1 = x - p0.astype(F32)
    p1 = r1.astype(BF16)
    p2 = (r1 - p1.astype(F32)).astype(BF16)
    return p0, p1, p2


def _const_spec(shape, n_grid):
    zeros = (0,) * len(shape)
    if n_grid == 1:
        return pl.BlockSpec(shape, lambda i: zeros)
    if n_grid == 2:
        return pl.BlockSpec(shape, lambda i, j: zeros)
    return pl.BlockSpec(shape, lambda i, j, k: zeros)


def _lru_kernel(x_ref, g_ref, win_ref, cw_ref, cb_ref, gw_ref, gb_ref, lam_ref, wout_ref,
                o_ref, xbuf, hcar):
    t = pl.program_id(1)
    tt = x_ref.shape[1]
    d = x_ref.shape[2]
    bw = d // N_LRU_BLOCKS

    @pl.when(t == 0)
    def _():
        xbuf[0:SUBLANES, :] = jnp.zeros((SUBLANES, d), F32)
        hcar[...] = jnp.zeros_like(hcar)

    x = x_ref[0]
    xn = _rms(x, g_ref[...]).astype(BF16)
    xg = jnp.dot(xn, win_ref[...], preferred_element_type=F32)
    xb = xg[:, :d]
    gbr = xg[:, d:]

    xbuf[SUBLANES:SUBLANES + tt, :] = xb
    cw = cw_ref[...]
    xc = cb_ref[...] + cw[CONV_W - 1:CONV_W, :] * xb
    for j in range(CONV_W - 1):
        xc = xc + cw[j:j + 1, :] * xbuf[pl.ds(SUBLANES - (CONV_W - 1) + j, tt), :]
    xbuf[0:SUBLANES, :] = xbuf[tt:tt + SUBLANES, :]

    xcb = xc.astype(BF16)
    rl, il = [], []
    for n in range(N_LRU_BLOCKS):
        gl = jnp.dot(xcb[:, n * bw:(n + 1) * bw], gw_ref[n], preferred_element_type=F32)
        rl.append(gl[:, :bw])
        il.append(gl[:, bw:])
    gb = gb_ref[...]
    r = jax.nn.sigmoid(jnp.concatenate(rl, axis=1) + gb[0:1, :])
    i = jax.nn.sigmoid(jnp.concatenate(il, axis=1) + gb[1:2, :])
    log_a = LRU_C * r * _log_sigmoid(lam_ref[...])
    a = jnp.exp(log_a)
    b = jnp.sqrt((1.0 - a) * (1.0 + a)) * (i * xc)

    row = lax.broadcasted_iota(jnp.int32, (tt, 1), 0)
    sh = 1
    while sh < tt:
        if sh < SUBLANES:
            keep = row >= sh
            a_sh = jnp.where(keep, pltpu.roll(a, sh, 0), 1.0)
            b_sh = jnp.where(keep, pltpu.roll(b, sh, 0), 0.0)
            b = a * b_sh + b
            a = a * a_sh
        else:
            b = jnp.concatenate([b[:sh], a[sh:] * b[:tt - sh] + b[sh:]], axis=0)
            a = jnp.concatenate([a[:sh], a[sh:] * a[:tt - sh]], axis=0)
        sh *= 2
    h = b + a * hcar[...]
    hcar[...] = h[tt - 1:tt, :]

    y = (_gelu_tanh(gbr) * h).astype(BF16)
    o_ref[0] = x + jnp.dot(y, wout_ref[...], preferred_element_type=F32)


def _lru_layer(x, g, w_in, conv_w, conv_b, gate_w, gate_b, lam, w_out, *, tt):
    bsz, s, d = x.shape
    bw = d // N_LRU_BLOCKS
    gw = jnp.concatenate([gate_w[0], gate_w[1]], axis=-1).astype(BF16)
    const = lambda *shape: _const_spec(shape, 2)
    return pl.pallas_call(
        _lru_kernel,
        grid=(bsz, s // tt),
        in_specs=[
            pl.BlockSpec((1, tt, d), lambda b, t: (b, t, 0)),
            const(1, d), const(d, 2 * d), const(CONV_W, d), const(1, d),
            const(N_LRU_BLOCKS, bw, 2 * bw), const(2, d), const(1, d), const(d, d),
        ],
        out_specs=pl.BlockSpec((1, tt, d), lambda b, t: (b, t, 0)),
        out_shape=jax.ShapeDtypeStruct((bsz, s, d), F32),
        scratch_shapes=[pltpu.VMEM((tt + 2 * SUBLANES, d), F32), pltpu.VMEM((1, d), F32)],
        compiler_params=pltpu.CompilerParams(dimension_semantics=("parallel", "arbitrary")),
        name="lru_layer",
    )(x, g.reshape(1, d), w_in.astype(BF16), conv_w, conv_b.reshape(1, d), gw, gate_b,
      lam.reshape(1, d), w_out.astype(BF16))


def _swiglu_accumulate(x, wg, wu, wd, acc_ref):
    gate = jnp.dot(x, wg[...], preferred_element_type=F32)
    up = jnp.dot(x, wu[...], preferred_element_type=F32)
    h = (_silu(gate) * up).astype(BF16)
    acc_ref[...] += jnp.dot(h, wd[...], preferred_element_type=F32)


def _mlp_kernel(x_ref, g_ref, wg_ref, wu_ref, wd_ref, o_ref, hn_ref, acc_ref):
    k = pl.program_id(1)

    @pl.when(k == 0)
    def _():
        hn_ref[...] = _rms(x_ref[...], g_ref[...]).astype(BF16)
        acc_ref[...] = jnp.zeros_like(acc_ref)

    _swiglu_accumulate(hn_ref[...], wg_ref, wu_ref, wd_ref, acc_ref)

    @pl.when(k == pl.num_programs(1) - 1)
    def _():
        o_ref[...] = x_ref[...] + acc_ref[...]


def _mlp(x2d, g, w_gu, w_down, *, tm, tf):
    t, d = x2d.shape
    dff = w_down.shape[0]
    nk = dff // tf
    w_gu = w_gu.astype(BF16)
    return pl.pallas_call(
        _mlp_kernel,
        grid=(t // tm, nk),
        in_specs=[
            pl.BlockSpec((tm, d), lambda i, k: (i, 0)),
            pl.BlockSpec((1, d), lambda i, k: (0, 0)),
            pl.BlockSpec((d, tf), lambda i, k: (0, k)),
            pl.BlockSpec((d, tf), lambda i, k: (0, nk + k)),
            pl.BlockSpec((tf, d), lambda i, k: (k, 0)),
        ],
        out_specs=pl.BlockSpec((tm, d), lambda i, k: (i, 0)),
        out_shape=jax.ShapeDtypeStruct((t, d), F32),
        scratch_shapes=[pltpu.VMEM((tm, d), BF16), pltpu.VMEM((tm, d), F32)],
        compiler_params=pltpu.CompilerParams(dimension_semantics=("parallel", "arbitrary")),
        name="mlp",
    )(x2d, g.reshape(1, d), w_gu, w_gu, w_down.astype(BF16))


def _head_selectors():
    hg = N_HEADS * HEAD_GROUP
    head = jnp.arange(hg) // HEAD_GROUP
    lane = jnp.arange(hg) % HEAD_GROUP
    col = jnp.arange(LANES)
    def sel(base):
        return ((col[:, None] % N_HEADS == head[None, :]) & (col[:, None] < 3 * N_HEADS)
                & (lane[None, :] == base + col[:, None] // N_HEADS)).astype(BF16)
    sel_k = -sel(BIAS_K)
    sel_q = sel(BIAS_Q).T
    ones_k = ((lane >= BIAS_Q) & (lane < BIAS_Q + 3)).astype(F32).reshape(1, hg)
    ones_q = ((lane >= BIAS_K) & (lane < BIAS_K + 3)).astype(F32).reshape(hg, 1)
    feat_head = jnp.arange(N_HEADS * HEAD_DIM) // HEAD_DIM
    head_sum = (col[None, :] == feat_head[:, None]).astype(BF16)
    return sel_k, sel_q, ones_k, ones_q, head_sum


def _kv_kernel(x_ref, g_ref, wk_ref, wvT_ref, onesv_ref, wf_ref, wfT_ref, bfr_ref, bfc_ref, selk_ref, onesk_ref,
               hsum_ref, ka_ref, vT_ref, c2r_ref, kn_ref, car_r, car_c):
    t = pl.program_id(1)
    tk = x_ref.shape[1]

    @pl.when(t == 0)
    def _():
        car_r[...] = jnp.zeros_like(car_r)
        car_c[...] = jnp.zeros_like(car_c)

    xn = _rms(x_ref[0], g_ref[...]).astype(BF16)
    vT = lax.dot_general(wvT_ref[...], xn, NT_DIMS, preferred_element_type=F32) + onesv_ref[...]
    vT_ref[0, 0] = vT.astype(BF16)
    kr = jnp.dot(xn, wk_ref[...], preferred_element_type=F32)

    kss = jnp.dot((kr * kr).astype(BF16), hsum_ref[...], preferred_element_type=F32)
    kn = jnp.sqrt(jnp.max(kss, axis=0, keepdims=True))
    kn_ref[0, 0] = jnp.broadcast_to(kn, (SUBLANES, LANES))

    ri = lax.broadcasted_iota(jnp.int32, (tk, tk), 0)
    ci = lax.broadcasted_iota(jnp.int32, (tk, tk), 1)
    upper = (ri <= ci).astype(BF16)
    lower = (ci <= ri).astype(BF16)

    lf_c = _log_sigmoid(jnp.dot(xn, wf_ref[...], preferred_element_type=F32) + bfr_ref[...])
    cs3 = jnp.dot(lower, jnp.concatenate(_split3(lf_c), axis=1), preferred_element_type=F32)
    cs_c = car_c[...] + (cs3[:, :LANES] + cs3[:, LANES:2 * LANES] + cs3[:, 2 * LANES:])
    car_c[...] = cs_c[tk - 1:tk, :]
    p0, p1, p2 = (p.astype(F32) for p in _split3(cs_c * LOG2E))
    lane = lax.broadcasted_iota(jnp.int32, (tk, LANES), 1)
    groups = []
    for pair in range(N_HEADS // 2):
        two = kr[:, pair * LANES:(pair + 1) * LANES]
        groups.append(jnp.where(lane < HEAD_DIM, two, 0.0))
        groups.append(jnp.where(lane < HEAD_DIM, pltpu.roll(two, HEAD_DIM, 1), 0.0))
    kf = jnp.concatenate(groups, axis=1)
    packed = jnp.where(lane < N_HEADS, p0,
                       jnp.where(lane < 2 * N_HEADS, pltpu.roll(p1, N_HEADS, 1),
                                 jnp.where(lane < 3 * N_HEADS, pltpu.roll(p2, 2 * N_HEADS, 1), 0.0)))
    bias = jnp.dot(packed.astype(BF16), selk_ref[...], preferred_element_type=F32)
    ka_ref[0] = (kf + bias + onesk_ref[...]).astype(BF16)

    lf_r = _log_sigmoid(lax.dot_general(wfT_ref[...], xn, NT_DIMS, preferred_element_type=F32) + bfc_ref[...])
    cs3 = jnp.dot(jnp.concatenate(_split3(lf_r), axis=0), upper, preferred_element_type=F32)
    cs_r = car_r[...] + (cs3[:LANES] + cs3[LANES:2 * LANES] + cs3[2 * LANES:])
    car_r[...] = cs_r[:, tk - 1:tk]
    c2r_ref[0] = cs_r * LOG2E


def _kv_proj(x, g, w_kvf, b_f, consts, *, tk):
    bsz, s, d = x.shape
    da = N_HEADS * HEAD_DIM
    hg = N_HEADS * HEAD_GROUP
    dv = N_HEADS * V_GROUP
    nk = s // tk
    sel_k, _, ones_k, _, head_sum = consts
    wk = w_kvf[:, :da].astype(BF16)
    wv = w_kvf[:, da:2 * da].reshape(d, N_HEADS, HEAD_DIM)
    wvT = jnp.pad(wv, ((0, 0), (0, 0), (0, V_GROUP - HEAD_DIM))).reshape(d, dv).T.astype(BF16)
    ones_v = (jnp.arange(dv) % V_GROUP == HEAD_DIM).astype(F32).reshape(dv, 1)
    wf = jnp.pad(w_kvf[:, 2 * da:], ((0, 0), (0, LANES - N_HEADS))).astype(BF16)
    bfr = jnp.pad(b_f, (0, LANES - N_HEADS)).reshape(1, LANES)
    const = lambda *shape: _const_spec(shape, 2)
    return pl.pallas_call(
        _kv_kernel,
        grid=(bsz, nk),
        in_specs=[
            pl.BlockSpec((1, tk, d), lambda b, t: (b, t, 0)),
            const(1, d), const(d, da), const(dv, d), const(dv, 1), const(d, LANES), const(LANES, d),
            const(1, LANES), const(LANES, 1), const(LANES, hg), const(1, hg), const(da, LANES),
        ],
        out_specs=[
            pl.BlockSpec((1, tk, hg), lambda b, t: (b, t, 0)),
            pl.BlockSpec((1, 1, dv, tk), lambda b, t: (b, t, 0, 0)),
            pl.BlockSpec((1, LANES, tk), lambda b, t: (b, 0, t)),
            pl.BlockSpec((1, 1, SUBLANES, LANES), lambda b, t: (b, t, 0, 0)),
        ],
        out_shape=[
            jax.ShapeDtypeStruct((bsz, s, hg), BF16),
            jax.ShapeDtypeStruct((bsz, nk, dv, tk), BF16),
            jax.ShapeDtypeStruct((bsz, LANES, s), F32),
            jax.ShapeDtypeStruct((bsz, nk, SUBLANES, LANES), F32),
        ],
        scratch_shapes=[pltpu.VMEM((LANES, 1), F32), pltpu.VMEM((1, LANES), F32)],
        compiler_params=pltpu.CompilerParams(dimension_semantics=("parallel", "arbitrary")),
        name="kv_proj",
    )(x, g.reshape(1, d), wk, wvT, ones_v, wf, wf.T, bfr, bfr.reshape(LANES, 1), sel_k, ones_k, head_sum)


def _q_kernel(x_ref, g_ref, wqT_ref, wg_ref, c2r_ref, selq_ref, onesq_ref, hsumT_ref, qa_ref, gate_ref, qn_ref):
    xn = _rms(x_ref[0], g_ref[...]).astype(BF16)
    gate_ref[0] = jax.nn.sigmoid(jnp.dot(xn, wg_ref[...], preferred_element_type=F32)).astype(BF16)
    qr = lax.dot_general(wqT_ref[...], xn, NT_DIMS, preferred_element_type=F32) * (LOG2E * HEAD_DIM ** -0.5)

    qss = jnp.dot(hsumT_ref[...], (qr * qr).astype(BF16), preferred_element_type=F32)
    qn = jnp.sqrt(jnp.max(qss, axis=1, keepdims=True))
    qn_ref[0, 0] = jnp.broadcast_to(qn, (LANES, LANES))

    tq = c2r_ref.shape[2]
    gap = jnp.zeros((HEAD_GROUP - HEAD_DIM, tq), F32)
    qf = jnp.concatenate([blk for h in range(N_HEADS) for blk in (qr[h * HEAD_DIM:(h + 1) * HEAD_DIM], gap)], axis=0)
    packed = jnp.concatenate([p[:N_HEADS] for p in _split3(c2r_ref[0])]
                             + [jnp.zeros((LANES - 3 * N_HEADS, tq), BF16)], axis=0)
    bias = jnp.dot(selq_ref[...], packed, preferred_element_type=F32)
    qa_ref[0] = (qf + bias + onesq_ref[...]).astype(BF16)


def _q_proj(x, g, w_qg, c2r, consts, *, tq):
    bsz, s, d = x.shape
    da = N_HEADS * HEAD_DIM
    hg = N_HEADS * HEAD_GROUP
    nq = s // tq
    _, sel_q, _, ones_q, head_sum = consts
    wqT = w_qg[:, :da].T.astype(BF16)
    const = lambda *shape: _const_spec(shape, 2)
    return pl.pallas_call(
        _q_kernel,
        grid=(bsz, nq),
        in_specs=[
            pl.BlockSpec((1, tq, d), lambda b, t: (b, t, 0)),
            const(1, d), const(da, d), const(d, da),
            pl.BlockSpec((1, LANES, tq), lambda b, t: (b, 0, t)),
            const(hg, LANES), const(hg, 1), const(LANES, da),
        ],
        out_specs=[
            pl.BlockSpec((1, hg, tq), lambda b, t: (b, 0, t)),
            pl.BlockSpec((1, tq, da), lambda b, t: (b, t, 0)),
            pl.BlockSpec((1, 1, LANES, LANES), lambda b, t: (b, t, 0, 0)),
        ],
        out_shape=[
            jax.ShapeDtypeStruct((bsz, hg, s), BF16),
            jax.ShapeDtypeStruct((bsz, s, da), BF16),
            jax.ShapeDtypeStruct((bsz, nq, LANES, LANES), F32),
        ],
        compiler_params=pltpu.CompilerParams(dimension_semantics=("parallel", "parallel")),
        name="q_proj",
    )(x, g.reshape(1, d), wqT, w_qg[:, da:].astype(BF16), c2r, sel_q, ones_q, head_sum.T)


def _attn_kernel(jlo_ref, qa_ref, qan_ref, ka_ref, vT_ref, gate_ref, o_ref, ready, m0, m1, acc0, acc1,
                 sa0, sa1, sb0, sb1, sc0, sc1, pa0, pa1, pb0, pb1):
    b = pl.program_id(0)
    hp = pl.program_id(1)
    i = pl.program_id(2)
    nq = pl.num_programs(2)
    tq = qa_ref.shape[2]
    tk = vT_ref.shape[3]
    base = (b * pl.num_programs(1) + hp) * nq
    j_lo = jlo_ref[base + i]
    j_lo_next = jlo_ref[base + jnp.minimum(i + 1, nq - 1)]
    m_refs, acc_refs = (m0, m1), (acc0, acc1)
    s_a, s_b, s_c = (sa0, sa1), (sb0, sb1), (sc0, sc1)
    p_a, p_b = (pa0, pa1), (pb0, pb1)

    def scores(q_ref, j, h):
        k_blk = ka_ref[0, pl.ds(pl.multiple_of(j * tk, tk), tk), h * HEAD_GROUP:(h + 1) * HEAD_GROUP]
        qT = q_ref[0, h * HEAD_GROUP:(h + 1) * HEAD_GROUP, :]
        return jnp.dot(k_blk, qT, preferred_element_type=F32)

    def values(j, h, p):
        vT = vT_ref[0, j, h * V_GROUP:(h + 1) * V_GROUP, :]
        return jnp.dot(vT, p, preferred_element_type=F32)

    def softmax_tile(h, s, masked):
        if masked:
            kv_pos = lax.broadcasted_iota(jnp.int32, (tk, tq), 0)
            q_pos = lax.broadcasted_iota(jnp.int32, (tk, tq), 1)
            s = jnp.where(kv_pos <= q_pos, s, NEG_BIG)
        m_old = m_refs[h][...]
        m_new = jnp.maximum(m_old, jnp.max(s, axis=0, keepdims=True))
        m_refs[h][...] = m_new
        return jnp.exp2(m_old - m_new), jnp.exp2(s - m_new).astype(BF16)

    def step(s_src, s_dst, p_prev, p_dst, j):
        for h in range(2):
            s_dst[h][...] = scores(qa_ref, j + 1, h)
        pv = None if p_prev is None else [values(j - 1, h, p_prev[h][...]) for h in range(2)]
        for h in range(2):
            alpha, p = softmax_tile(h, s_src[h][...], masked=False)
            p_dst[h][...] = p
            if pv is not None:
                acc_refs[h][...] = alpha * (acc_refs[h][...] + pv[h])

    def finish(s_src, p_prev, prefetch):
        if prefetch:
            for h in range(2):
                s_c[h][...] = scores(qan_ref, j_lo_next, h)
            ready[0] = (i + 1 < nq).astype(jnp.int32)
        else:
            ready[0] = jnp.int32(0)
        pv = None if p_prev is None else [values(i - 1, h, p_prev[h][...]) for h in range(2)]
        outs = []
        for h in range(2):
            alpha, p = softmax_tile(h, s_src[h][...], masked=True)
            acc = values(i, h, p)
            if pv is not None:
                acc = acc + alpha * (acc_refs[h][...] + pv[h])
            outs.append(acc[0:HEAD_DIM, :] / acc[HEAD_DIM:HEAD_DIM + 1, :])
        oT = jnp.concatenate(outs, axis=0)
        o_ref[0] = (oT.T * gate_ref[0].astype(F32)).astype(BF16)

    @pl.when(i == 0)
    def _():
        ready[0] = jnp.int32(0)

    @pl.when(ready[0] == 0)
    def _():
        for h in range(2):
            s_c[h][...] = scores(qa_ref, j_lo, h)

    for h in range(2):
        m_refs[h][...] = jnp.full((1, tq), NEG_BIG, F32)
        acc_refs[h][...] = jnp.zeros((V_GROUP, tq), F32)

    n_full = i - j_lo

    @pl.when(n_full == 0)
    def _():
        finish(s_c, None, prefetch=False)

    @pl.when(n_full > 0)
    def _():
        step(s_c, s_a, None, p_a, j_lo)
        rest = n_full - 1

        def body(n, carry):
            j = j_lo + 1 + 2 * n
            step(s_a, s_b, p_a, p_b, j)
            step(s_b, s_a, p_b, p_a, j + 1)
            return carry

        lax.fori_loop(0, rest // 2, body, 0)

        @pl.when(rest % 2 == 1)
        def _():
            step(s_a, s_b, p_a, p_b, i - 1)
            finish(s_b, p_b, prefetch=True)

        @pl.when(rest % 2 == 0)
        def _():
            finish(s_a, p_a, prefetch=True)


def _permute_heads(w_kvf, b_f, w_qg, w_o):
    da = N_HEADS * HEAD_DIM
    perm = jnp.argsort(b_f)
    cols = (perm[:, None] * HEAD_DIM + jnp.arange(HEAD_DIM)[None, :]).reshape(-1)
    w_kvf_p = jnp.concatenate([w_kvf[:, :da][:, cols], w_kvf[:, da:2 * da][:, cols], w_kvf[:, 2 * da:][:, perm]],
                              axis=1)
    w_qg_p = jnp.concatenate([w_qg[:, :da][:, cols], w_qg[:, da:][:, cols]], axis=1)
    return w_kvf_p, b_f[perm], w_qg_p, w_o[cols, :]


def _skip_table(qn, kn, c2r, tq):
    bsz, nq = qn.shape[0], qn.shape[1]
    qn = qn[:, :, :N_HEADS, 0].transpose(0, 2, 1) * NORM_SLACK
    kn = kn[:, :, 0, :N_HEADS].transpose(0, 2, 1) * NORM_SLACK
    c2 = c2r[:, :N_HEADS, :]
    c_first = c2[:, :, ::tq]
    c_last = c2[:, :, tq - 1::tq]
    upper = qn[..., :, None] * kn[..., None, :] + c_first[..., :, None] - c_last[..., None, :]
    lower = -(qn * kn)[..., :, None]
    skip = (upper - lower) < -SKIP_GAP_LOG2
    nk = skip.shape[-1]
    jlo = jnp.min(jnp.where(skip, nk, jnp.arange(nk, dtype=jnp.int32)), axis=-1)
    jlo = jnp.minimum(jlo, jnp.arange(nq, dtype=jnp.int32))
    jlo = jnp.min(jlo.reshape(bsz, N_HEADS // 2, 2, nq), axis=2)
    return jlo.reshape(-1).astype(jnp.int32)


def _attention(jlo, qa, ka, vT, gate, *, tq):
    bsz, hg, s = qa.shape
    nk, tk = vT.shape[1], vT.shape[3]
    da = N_HEADS * HEAD_DIM
    assert tq == tk
    pair = 2 * HEAD_GROUP
    hw = 2 * HEAD_DIM
    nq = s // tq
    grid_spec = pltpu.PrefetchScalarGridSpec(
        num_scalar_prefetch=1,
        grid=(bsz, N_HEADS // 2, nq),
        in_specs=[
            pl.BlockSpec((1, pair, tq), lambda b, hp, i, jlo: (b, hp, i)),
            pl.BlockSpec((1, pair, tq), lambda b, hp, i, jlo: (b, hp, jnp.minimum(i + 1, nq - 1))),
            pl.BlockSpec((1, s, pair), lambda b, hp, i, jlo: (b, 0, hp)),
            pl.BlockSpec((1, nk, 2 * V_GROUP, tk), lambda b, hp, i, jlo: (b, 0, hp, 0)),
            pl.BlockSpec((1, tq, hw), lambda b, hp, i, jlo: (b, i, hp)),
        ],
        out_specs=pl.BlockSpec((1, tq, hw), lambda b, hp, i, jlo: (b, i, hp)),
        scratch_shapes=([pltpu.SMEM((1,), jnp.int32)]
                        + [pltpu.VMEM((1, tq), F32)] * 2 + [pltpu.VMEM((V_GROUP, tq), F32)] * 2
                        + [pltpu.VMEM((tk, tq), F32)] * 6 + [pltpu.VMEM((tk, tq), BF16)] * 4),
    )
    return pl.pallas_call(
        _attn_kernel,
        grid_spec=grid_spec,
        out_shape=jax.ShapeDtypeStruct((bsz, s, da), BF16),
        compiler_params=pltpu.CompilerParams(dimension_semantics=("parallel", "parallel", "arbitrary")),
        name="fox_attention",
    )(jlo, qa, qa, ka, vT, gate)


def _pack_bf16_pairs(x):
    half = x.shape[1] // 2
    lo = lax.bitcast_convert_type(x[:, :half].astype(BF16).astype(F32), jnp.uint32)
    hi = lax.bitcast_convert_type(x[:, half:].astype(BF16).astype(F32), jnp.uint32)
    word = (lo >> 16) | (hi & jnp.uint32(0xFFFF0000))
    return lax.bitcast_convert_type(word, jnp.int32)


def _unpack_bf16_pairs(w):
    u = lax.bitcast_convert_type(w, jnp.uint32)
    lo = lax.bitcast_convert_type(u << 16, F32)
    hi = lax.bitcast_convert_type(u & jnp.uint32(0xFFFF0000), F32)
    return jnp.concatenate([lo, hi], axis=1)


def _store_planes(ref, words):
    q = words.shape[1] // 2
    ref[0] = words[:, :q]
    ref[1] = words[:, q:]


def _load_planes(ref, base=0):
    return jnp.concatenate([ref[base], ref[base + 1]], axis=1)


def _router_kernel(x_ref, o_ref, wo_ref, g_ref, rtT_ref, x2_ref, hnp_ref, route_ref):
    tm = x_ref.shape[0]
    x2 = x_ref[...] + jnp.dot(o_ref[...], wo_ref[...], preferred_element_type=F32)
    x2_ref[...] = x2
    hn = _rms(x2, g_ref[...])
    _store_planes(hnp_ref, _pack_bf16_pairs(hn))
    h0, h1, _ = _split3(hn)
    r0, r1, _ = _split3(rtT_ref[...])
    lt = lax.dot_general(jnp.concatenate([r0, r1, r0], axis=1), jnp.concatenate([h0, h0, h1], axis=1), NT_DIMS,
                         preferred_element_type=F32)[:N_EXPERTS]
    row = lax.broadcasted_iota(jnp.int32, (N_EXPERTS, tm), 0)
    m1 = jnp.max(lt, axis=0, keepdims=True)
    i1 = jnp.min(jnp.where(lt == m1, row, N_EXPERTS), axis=0, keepdims=True)
    rest = jnp.where(row == i1, -jnp.inf, lt)
    m2 = jnp.max(rest, axis=0, keepdims=True)
    i2 = jnp.min(jnp.where(rest == m2, row, N_EXPERTS), axis=0, keepdims=True)
    e2 = jnp.exp(m2 - m1)
    w1 = 1.0 / (1.0 + e2)
    w2 = e2 / (1.0 + e2)
    route_ref[...] = jnp.where(row == 0, i1.astype(F32), jnp.where(row == 1, i2.astype(F32),
                               jnp.where(row == 2, w1, jnp.where(row == 3, w2, 0.0))))


def _router(x2d, o2d, w_o, g, router, *, tm):
    t, d = x2d.shape
    da = o2d.shape[1]
    ne = router.shape[1]
    rtT = jnp.pad(router.T, ((0, 2 * SUBLANES - ne), (0, 0)))
    return pl.pallas_call(
        _router_kernel,
        grid=(t // tm,),
        in_specs=[
            pl.BlockSpec((tm, d), lambda i: (i, 0)),
            pl.BlockSpec((tm, da), lambda i: (i, 0)),
            pl.BlockSpec((da, d), lambda i: (0, 0)),
            pl.BlockSpec((1, d), lambda i: (0, 0)),
            pl.BlockSpec((2 * SUBLANES, d), lambda i: (0, 0)),
        ],
        out_specs=[pl.BlockSpec((tm, d), lambda i: (i, 0)), pl.BlockSpec((2, tm, d // 4), lambda i: (0, i, 0)),
                   pl.BlockSpec((ne, tm), lambda i: (0, i))],
        out_shape=[jax.ShapeDtypeStruct((t, d), F32), jax.ShapeDtypeStruct((2, t, d // 4), jnp.int32),
                   jax.ShapeDtypeStruct((ne, t), F32)],
        compiler_params=pltpu.CompilerParams(dimension_semantics=("parallel",)),
        name="attn_out_moe_router",
    )(x2d, o2d, w_o.astype(BF16), g.reshape(1, d), rtT)


def _routing_tables(route, ts):
    ne = N_EXPERTS
    t = route.shape[1]
    e_pair = route[0:2].astype(jnp.int32).reshape(-1)
    n_tiles = (2 * t) // ts + ne
    n_slots = n_tiles * ts
    onehot = (e_pair[:, None] == jnp.arange(ne, dtype=jnp.int32)[None, :]).astype(jnp.int32)
    csum = jnp.cumsum(onehot, axis=0)
    rank = jnp.sum(onehot * csum, axis=1) - 1
    tiles = (csum[-1] + ts - 1) // ts
    tile_end = jnp.cumsum(tiles)
    tile_start = tile_end - tiles
    slot = (jnp.sum(onehot * tile_start[None, :], axis=1) * ts + rank).astype(jnp.int32)
    n_used = tile_end[-1]
    tile_ids = jnp.arange(n_tiles, dtype=jnp.int32)
    tile_expert = jnp.sum((tile_ids[:, None] >= tile_end[None, :]).astype(jnp.int32), axis=1)
    last_expert = jnp.sum(((n_used - 1) >= tile_end).astype(jnp.int32))
    tile_expert = jnp.minimum(tile_expert, last_expert).astype(jnp.int32)
    plane_rows = slot.reshape(2, 1, t) + (jnp.arange(2, dtype=jnp.int32) * n_slots).reshape(1, 2, 1)
    return plane_rows.reshape(-1), tile_expert, n_used.reshape(1).astype(jnp.int32), n_slots


def _sc_mesh():
    return plsc.VectorSubcoreMesh(core_axis_name="core", subcore_axis_name="subcore")


def _scatter_rows(table, idx, n_out):
    m = idx.shape[0]
    n, d = table.shape
    assert m % SC_ROW_WINDOW == 0 and n % SC_ROW_WINDOW == 0
    n_src_blocks = n // SC_ROW_WINDOW

    @pl.kernel(out_type=jax.ShapeDtypeStruct((n_out, d), table.dtype), mesh=_sc_mesh(), scratch_types=[])
    def scatter_kernel(x_hbm, i_hbm, o_hbm):
        def body(x_vmem, i_vmem):
            pltpu.sync_copy(x_vmem, o_hbm.at[i_vmem.at[0]])

        pltpu.emit_pipeline(
            body,
            grid=(m // SC_ROW_WINDOW,),
            in_specs=[pl.BlockSpec((SC_ROW_WINDOW, d), index_map=lambda i: (i % n_src_blocks, 0)),
                      pl.BlockSpec((1, SC_ROW_WINDOW), index_map=lambda i: (0, i))],
            out_specs=[],
            core_axis_name=("core", "subcore"),
            dimension_semantics=(pltpu.PARALLEL,),
        )(x_hbm, i_hbm)

    return scatter_kernel(table, idx.reshape(1, m))


def _gather_rows(table, idx):
    m = idx.shape[0]
    d = table.shape[1]
    assert m % SC_ROW_WINDOW == 0

    @pl.kernel(out_type=jax.ShapeDtypeStruct((m, d), table.dtype), mesh=_sc_mesh())
    def gather_kernel(x_hbm, i_hbm, o_hbm):
        def body(i_vmem, o_vmem):
            pltpu.sync_copy(x_hbm.at[i_vmem.at[0]], o_vmem)

        pltpu.emit_pipeline(
            body,
            grid=(m // SC_ROW_WINDOW,),
            in_specs=[pl.BlockSpec((1, SC_ROW_WINDOW), index_map=lambda i: (0, i))],
            out_specs=[pl.BlockSpec((SC_ROW_WINDOW, d), index_map=lambda i: (i, 0))],
            core_axis_name=("core", "subcore"),
            dimension_semantics=(pltpu.PARALLEL,),
        )(i_hbm, o_hbm)

    return gather_kernel(table, idx.reshape(1, m))


def _expert_kernel(te_ref, nu_ref, xs_ref, wg_ref, wu_ref, wd_ref, ys_ref, x_sc, acc_ref):
    i = pl.program_id(0)
    k = pl.program_id(1)
    last_k = pl.num_programs(1) - 1
    used = i < nu_ref[0]

    @pl.when(used & (k == 0))
    def _():
        x_sc[...] = _unpack_bf16_pairs(_load_planes(xs_ref)).astype(BF16)
        acc_ref[...] = jnp.zeros_like(acc_ref)

    @pl.when(used)
    def _():
        _swiglu_accumulate(x_sc[...], wg_ref.at[0], wu_ref.at[0], wd_ref.at[0], acc_ref)

    @pl.when(used & (k == last_k))
    def _():
        _store_planes(ys_ref, _pack_bf16_pairs(acc_ref[...]))

    @pl.when(jnp.logical_not(used) & (k == last_k))
    def _():
        ys_ref[...] = jnp.zeros_like(ys_ref)


def _experts(xs, tile_expert, n_used, w_gu, w_down, *, ts, tf):
    _, n_slots, quarter = xs.shape
    d = 4 * quarter
    dexp = w_down.shape[1]
    nk = dexp // tf
    w_gu = w_gu.astype(BF16)
    grid_spec = pltpu.PrefetchScalarGridSpec(
        num_scalar_prefetch=2,
        grid=(n_slots // ts, nk),
        in_specs=[
            pl.BlockSpec((2, ts, quarter), lambda i, k, te, nu: (0, i, 0)),
            pl.BlockSpec((1, d, tf), lambda i, k, te, nu: (te[i], 0, k)),
            pl.BlockSpec((1, d, tf), lambda i, k, te, nu: (te[i], 0, nk + k)),
            pl.BlockSpec((1, tf, d), lambda i, k, te, nu: (te[i], k, 0)),
        ],
        out_specs=pl.BlockSpec((2, ts, quarter), lambda i, k, te, nu: (0, i, 0)),
        scratch_shapes=[pltpu.VMEM((ts, d), BF16), pltpu.VMEM((ts, d), F32)],
    )
    return pl.pallas_call(
        _expert_kernel,
        grid_spec=grid_spec,
        out_shape=jax.ShapeDtypeStruct((2, n_slots, quarter), jnp.int32),
        compiler_params=pltpu.CompilerParams(dimension_semantics=("arbitrary", "arbitrary")),
        name="moe_experts",
    )(tile_expert, n_used, xs, w_gu, w_gu, w_down.astype(BF16))


def _combine_kernel(x_ref, y_ref, rt_ref, gf_ref, o_ref):
    rt = rt_ref[...]
    moe = (rt[:, 2:3] * _unpack_bf16_pairs(_load_planes(y_ref, 0))
           + rt[:, 3:4] * _unpack_bf16_pairs(_load_planes(y_ref, 2)))
    o_ref[...] = _rms(x_ref[...] + moe, gf_ref[...])


def _combine(x2d, y4, route_t, g_final, *, tm):
    t, d = x2d.shape
    ne = route_t.shape[1]
    return pl.pallas_call(
        _combine_kernel,
        grid=(t // tm,),
        in_specs=[
            pl.BlockSpec((tm, d), lambda i: (i, 0)),
            pl.BlockSpec((4, tm, d // 4), lambda i: (0, i, 0)),
            pl.BlockSpec((tm, ne), lambda i: (i, 0)),
            pl.BlockSpec((1, d), lambda i: (0, 0)),
        ],
        out_specs=pl.BlockSpec((tm, d), lambda i: (i, 0)),
        out_shape=jax.ShapeDtypeStruct((t, d), F32),
        compiler_params=pltpu.CompilerParams(dimension_semantics=("parallel",)),
        name="moe_combine",
    )(x2d, y4, route_t, g_final.reshape(1, d))


def _attn_out_moe(x1, o2d, w_o, g, router, w_gu, w_down, g_final, *, tm, ts, tf):
    t, d = x1.shape
    q = d // 4
    x2d, hn_planes, route = _router(x1, o2d, w_o, g, router, tm=tm)
    plane_rows, tile_expert, n_used, n_slots = _routing_tables(route, ts)
    xs = _scatter_rows(hn_planes.reshape(2 * t, q), plane_rows, 2 * n_slots)
    ys = _experts(xs.reshape(2, n_slots, q), tile_expert, n_used, w_gu, w_down, ts=ts, tf=tf)
    y4 = _gather_rows(ys.reshape(2 * n_slots, q), plane_rows)
    return _combine(x2d, y4.reshape(4, t, q), route.T, g_final, tm=tm)


def _tile(n, pref):
    t = min(n, pref)
    assert n % t == 0
    return t


def kernel(x, norm_g, lru_w_in, lru_conv_w, lru_conv_b, lru_gate_w, lru_gate_b, lru_lambda, lru_w_out,
           kv_norm_g, w_kvf, b_f, fox_w_qg, fox_w_o, mlp_w_gu, mlp_w_down, moe_router, moe_w_gu,
           moe_w_down, final_norm_g):
    bsz, s, d = x.shape
    t = bsz * s
    tm = _tile(t, 512)
    ts = _tile(s, 512)

    x = _lru_layer(x, norm_g[0, 0], lru_w_in[0], lru_conv_w[0], lru_conv_b[0], lru_gate_w[0],
                   lru_gate_b[0], lru_lambda[0], lru_w_out[0], tt=_tile(s, 256))
    dff = mlp_w_down.shape[1]
    x = _mlp(x.reshape(t, d), norm_g[0, 1], mlp_w_gu[0], mlp_w_down[0], tm=tm, tf=dff // 2).reshape(bsz, s, d)

    consts = _head_selectors()
    w_kvf_p, b_f_p, w_qg_p, w_o_p = _permute_heads(w_kvf, b_f, fox_w_qg[0], fox_w_o[0])
    ka, vT, c2r, kn = _kv_proj(x, kv_norm_g, w_kvf_p, b_f_p, consts, tk=ts)
    qa, gate, qn = _q_proj(x, norm_g[1, 0], w_qg_p, c2r, consts, tq=ts)
    jlo = _skip_table(qn, kn, c2r, ts)
    o = _attention(jlo, qa, ka, vT, gate, tq=ts)
    dexp = moe_w_down.shape[2]
    out = _attn_out_moe(x.reshape(t, d), o.reshape(t, -1), w_o_p, norm_g[1, 1], moe_router[0], moe_w_gu[0],
                        moe_w_down[0], final_norm_g, tm=tm, ts=tm, tf=dexp // 2)
    return out.reshape(bsz, s, d)
```

```python
import functools
import math

import jax
import jax.numpy as jnp
from jax import lax
from jax.experimental import pallas as pl
from jax.experimental.pallas import tpu as pltpu
from jax.experimental.pallas import tpu_sc as plsc

EPS = 1e-6
LRU_C = 8.0
N_LRU_BLOCKS = 8
CONV_W = 4
N_HEADS = 16
HEAD_DIM = 64
N_EXPERTS = 8
LANES = 128
SUBLANES = 8
HEAD_GROUP = 128
BIAS_Q = HEAD_DIM
BIAS_K = HEAD_DIM + 3
V_GROUP = 80
NEG_BIG = -1e30
LOG2E = math.log2(math.e)
SKIP_GAP_LOG2 = 152.0
NORM_SLACK = 1.01
SC_ROW_WINDOW = 128

F32 = jnp.float32
BF16 = jnp.bfloat16
NT_DIMS = (((1,), (1,)), ((), ()))


def _rms(x, g):
    ms = jnp.mean(x * x, axis=-1, keepdims=True)
    return x * lax.rsqrt(ms + EPS) * g


def _log_sigmoid(x):
    return jnp.minimum(x, 0.0) - jnp.log1p(jnp.exp(-jnp.abs(x)))


def _gelu_tanh(x):
    c = 0.7978845608028654
    return 0.5 * x * (1.0 + jnp.tanh(c * (x + 0.044715 * (x * x * x))))


def _silu(x):
    return x * jax.nn.sigmoid(x)


def _split3(x):
    p0 = x.astype(BF16)
    r1 = x - p0.astype(F32)
    p1 = r1.astype(BF16)
    p2 = (r1 - p1.astype(F32)).astype(BF16)
    return p0, p1, p2


def _const_spec(shape, n_grid):
    zeros = (0,) * len(shape)
    if n_grid == 1:
        return pl.BlockSpec(shape, lambda i: zeros)
    if n_grid == 2:
        return pl.BlockSpec(shape, lambda i, j: zeros)
    return pl.BlockSpec(shape, lambda i, j, k: zeros)


def _lru_kernel(x_ref, g_ref, win_ref, cw_ref, cb_ref, gw_ref, gb_ref, lam_ref, wout_ref,
                o_ref, xbuf, hcar):
    t = pl.program_id(1)
    tt = x_ref.shape[1]
    d = x_ref.shape[2]
    bw = d // N_LRU_BLOCKS

    @pl.when(t == 0)
    def _():
        xbuf[0:SUBLANES, :] = jnp.zeros((SUBLANES, d), F32)
        hcar[...] = jnp.zeros_like(hcar)

    x = x_ref[0]
    xn = _rms(x, g_ref[...]).astype(BF16)
    xg = jnp.dot(xn, win_ref[...], preferred_element_type=F32)
    xb = xg[:, :d]
    gbr = xg[:, d:]

    xbuf[SUBLANES:SUBLANES + tt, :] = xb
    cw = cw_ref[...]
    xc = cb_ref[...] + cw[CONV_W - 1:CONV_W, :] * xb
    for j in range(CONV_W - 1):
        xc = xc + cw[j:j + 1, :] * xbuf[pl.ds(SUBLANES - (CONV_W - 1) + j, tt), :]
    xbuf[0:SUBLANES, :] = xbuf[tt:tt + SUBLANES, :]

    xcb = xc.astype(BF16)
    rl, il = [], []
    for n in range(N_LRU_BLOCKS):
        gl = jnp.dot(xcb[:, n * bw:(n + 1) * bw], gw_ref[n], preferred_element_type=F32)
        rl.append(gl[:, :bw])
        il.append(gl[:, bw:])
    gb = gb_ref[...]
    r = jax.nn.sigmoid(jnp.concatenate(rl, axis=1) + gb[0:1, :])
    i = jax.nn.sigmoid(jnp.concatenate(il, axis=1) + gb[1:2, :])
    log_a = LRU_C * r * _log_sigmoid(lam_ref[...])
    a = jnp.exp(log_a)
    b = jnp.sqrt((1.0 - a) * (1.0 + a)) * (i * xc)

    row = lax.broadcasted_iota(jnp.int32, (tt, 1), 0)
    sh = 1
    while sh < tt:
        if sh < SUBLANES:
            keep = row >= sh
            a_sh = jnp.where(keep, pltpu.roll(a, sh, 0), 1.0)
            b_sh = jnp.where(keep, pltpu.roll(b, sh, 0), 0.0)
            b = a * b_sh + b
            a = a * a_sh
        else:
            b = jnp.concatenate([b[:sh], a[sh:] * b[:tt - sh] + b[sh:]], axis=0)
            a = jnp.concatenate([a[:sh], a[sh:] * a[:tt - sh]], axis=0)
        sh *= 2
    h = b + a * hcar[...]
    hcar[...] = h[tt - 1:tt, :]

    y = (_gelu_tanh(gbr) * h).astype(BF16)
    o_ref[0] = x + jnp.dot(y, wout_ref[...], preferred_element_type=F32)


def _lru_layer(x, g, w_in, conv_w, conv_b, gate_w, gate_b, lam, w_out, *, tt):
    bsz, s, d = x.shape
    bw = d // N_LRU_BLOCKS
    gw = jnp.concatenate([gate_w[0], gate_w[1]], axis=-1).astype(BF16)
    const = lambda *shape: _const_spec(shape, 2)
    return pl.pallas_call(
        _lru_kernel,
        grid=(bsz, s // tt),
        in_specs=[
            pl.BlockSpec((1, tt, d), lambda b, t: (b, t, 0)),
            const(1, d), const(d, 2 * d), const(CONV_W, d), const(1, d),
            const(N_LRU_BLOCKS, bw, 2 * bw), const(2, d), const(1, d), const(d, d),
        ],
        out_specs=pl.BlockSpec((1, tt, d), lambda b, t: (b, t, 0)),
        out_shape=jax.ShapeDtypeStruct((bsz, s, d), F32),
        scratch_shapes=[pltpu.VMEM((tt + 2 * SUBLANES, d), F32), pltpu.VMEM((1, d), F32)],
        compiler_params=pltpu.CompilerParams(dimension_semantics=("parallel", "arbitrary")),
        name="lru_layer",
    )(x, g.reshape(1, d), w_in.astype(BF16), conv_w, conv_b.reshape(1, d), gw, gate_b,
      lam.reshape(1, d), w_out.astype(BF16))


def _swiglu_chunk(x, wg, wu, wd):
    gate = jnp.dot(x, wg[...], preferred_element_type=F32)
    up = jnp.dot(x, wu[...], preferred_element_type=F32)
    h = (_silu(gate) * up).astype(BF16)
    return jnp.dot(h, wd[...], preferred_element_type=F32)


def _chunked_swiglu(k, nk, chunk, acc_ref, first, last):
    if nk == 1:
        first()
        last(chunk())
        return

    @pl.when(k == 0)
    def _():
        first()
        acc_ref[...] = chunk()

    if nk > 2:
        @pl.when((k > 0) & (k < nk - 1))
        def _():
            acc_ref[...] += chunk()

    @pl.when(k == nk - 1)
    def _():
        last(acc_ref[...] + chunk())


def _mlp_kernel(nk, x_ref, g_ref, wg_ref, wu_ref, wd_ref, o_ref, hn_ref, acc_ref):
    def first():
        hn_ref[...] = _rms(x_ref[...], g_ref[...]).astype(BF16)

    def last(total):
        o_ref[...] = x_ref[...] + total

    _chunked_swiglu(pl.program_id(1), nk, lambda: _swiglu_chunk(hn_ref[...], wg_ref, wu_ref, wd_ref), acc_ref,
                    first, last)


def _mlp(x2d, g, w_gu, w_down, *, tm, tf):
    t, d = x2d.shape
    dff = w_down.shape[0]
    nk = dff // tf
    w_gu = w_gu.astype(BF16)
    return pl.pallas_call(
        functools.partial(_mlp_kernel, nk),
        grid=(t // tm, nk),
        in_specs=[
            pl.BlockSpec((tm, d), lambda i, k: (i, 0)),
            pl.BlockSpec((1, d), lambda i, k: (0, 0)),
            pl.BlockSpec((d, tf), lambda i, k: (0, k)),
            pl.BlockSpec((d, tf), lambda i, k: (0, nk + k)),
            pl.BlockSpec((tf, d), lambda i, k: (k, 0)),
        ],
        out_specs=pl.BlockSpec((tm, d), lambda i, k: (i, 0)),
        out_shape=jax.ShapeDtypeStruct((t, d), F32),
        scratch_shapes=[pltpu.VMEM((tm, d), BF16), pltpu.VMEM((tm, d), F32)],
        compiler_params=pltpu.CompilerParams(dimension_semantics=("parallel", "arbitrary")),
        name="mlp",
    )(x2d, g.reshape(1, d), w_gu, w_gu, w_down.astype(BF16))


def _head_selectors():
    hg = N_HEADS * HEAD_GROUP
    head = jnp.arange(hg) // HEAD_GROUP
    lane = jnp.arange(hg) % HEAD_GROUP
    col = jnp.arange(LANES)
    def sel(base):
        return ((col[:, None] % N_HEADS == head[None, :]) & (col[:, None] < 3 * N_HEADS)
                & (lane[None, :] == base + col[:, None] // N_HEADS)).astype(BF16)
    sel_k = -sel(BIAS_K)
    sel_q = sel(BIAS_Q).T
    ones_k = ((lane >= BIAS_Q) & (lane < BIAS_Q + 3)).astype(F32).reshape(1, hg)
    ones_q = ((lane >= BIAS_K) & (lane < BIAS_K + 3)).astype(F32).reshape(hg, 1)
    feat_head = jnp.arange(N_HEADS * HEAD_DIM) // HEAD_DIM
    head_sum = (col[None, :] == feat_head[:, None]).astype(BF16)
    return sel_k, sel_q, ones_k, ones_q, head_sum


def _kv_kernel(x_ref, g_ref, wk_ref, wvT_ref, onesv_ref, wf_ref, wfT_ref, bfr_ref, bfc_ref, selk_ref, onesk_ref,
               hsum_ref, ka_ref, vT_ref, c2r_ref, kn_ref, car_r, car_c):
    t = pl.program_id(1)
    tk = x_ref.shape[1]

    @pl.when(t == 0)
    def _():
        car_r[...] = jnp.zeros_like(car_r)
        car_c[...] = jnp.zeros_like(car_c)

    xn = _rms(x_ref[0], g_ref[...]).astype(BF16)
    vT = lax.dot_general(wvT_ref[...], xn, NT_DIMS, preferred_element_type=F32) + onesv_ref[...]
    vT_ref[0, 0] = vT.astype(BF16)
    kr = jnp.dot(xn, wk_ref[...], preferred_element_type=F32)

    kss = jnp.dot((kr * kr).astype(BF16), hsum_ref[...], preferred_element_type=F32)
    kn = jnp.sqrt(jnp.max(kss, axis=0, keepdims=True))
    kn_ref[0, 0] = jnp.broadcast_to(kn, (SUBLANES, LANES))

    ri = lax.broadcasted_iota(jnp.int32, (tk, tk), 0)
    ci = lax.broadcasted_iota(jnp.int32, (tk, tk), 1)
    upper = (ri <= ci).astype(BF16)
    lower = (ci <= ri).astype(BF16)

    lf_c = _log_sigmoid(jnp.dot(xn, wf_ref[...], preferred_element_type=F32) + bfr_ref[...])
    cs3 = jnp.dot(lower, jnp.concatenate(_split3(lf_c), axis=1), preferred_element_type=F32)
    cs_c = car_c[...] + (cs3[:, :LANES] + cs3[:, LANES:2 * LANES] + cs3[:, 2 * LANES:])
    car_c[...] = cs_c[tk - 1:tk, :]
    p0, p1, p2 = (p.astype(F32) for p in _split3(cs_c * LOG2E))
    lane = lax.broadcasted_iota(jnp.int32, (tk, LANES), 1)
    groups = []
    for pair in range(N_HEADS // 2):
        two = kr[:, pair * LANES:(pair + 1) * LANES]
        groups.append(jnp.where(lane < HEAD_DIM, two, 0.0))
        groups.append(jnp.where(lane < HEAD_DIM, pltpu.roll(two, HEAD_DIM, 1), 0.0))
    kf = jnp.concatenate(groups, axis=1)
    packed = jnp.where(lane < N_HEADS, p0,
                       jnp.where(lane < 2 * N_HEADS, pltpu.roll(p1, N_HEADS, 1),
                                 jnp.where(lane < 3 * N_HEADS, pltpu.roll(p2, 2 * N_HEADS, 1), 0.0)))
    bias = jnp.dot(packed.astype(BF16), selk_ref[...], preferred_element_type=F32)
    ka_ref[0] = (kf + bias + onesk_ref[...]).astype(BF16)

    lf_r = _log_sigmoid(lax.dot_general(wfT_ref[...], xn, NT_DIMS, preferred_element_type=F32) + bfc_ref[...])
    cs3 = jnp.dot(jnp.concatenate(_split3(lf_r), axis=0), upper, preferred_element_type=F32)
    cs_r = car_r[...] + (cs3[:LANES] + cs3[LANES:2 * LANES] + cs3[2 * LANES:])
    car_r[...] = cs_r[:, tk - 1:tk]
    c2r_ref[0] = cs_r * LOG2E


def _kv_proj(x, g, w_kvf, b_f, consts, *, tk):
    bsz, s, d = x.shape
    da = N_HEADS * HEAD_DIM
    hg = N_HEADS * HEAD_GROUP
    dv = N_HEADS * V_GROUP
    nk = s // tk
    sel_k, _, ones_k, _, head_sum = consts
    wk = w_kvf[:, :da].astype(BF16)
    wv = w_kvf[:, da:2 * da].reshape(d, N_HEADS, HEAD_DIM)
    wvT = jnp.pad(wv, ((0, 0), (0, 0), (0, V_GROUP - HEAD_DIM))).reshape(d, dv).T.astype(BF16)
    ones_v = (jnp.arange(dv) % V_GROUP == HEAD_DIM).astype(F32).reshape(dv, 1)
    wf = jnp.pad(w_kvf[:, 2 * da:], ((0, 0), (0, LANES - N_HEADS))).astype(BF16)
    bfr = jnp.pad(b_f, (0, LANES - N_HEADS)).reshape(1, LANES)
    const = lambda *shape: _const_spec(shape, 2)
    return pl.pallas_call(
        _kv_kernel,
        grid=(bsz, nk),
        in_specs=[
            pl.BlockSpec((1, tk, d), lambda b, t: (b, t, 0)),
            const(1, d), const(d, da), const(dv, d), const(dv, 1), const(d, LANES), const(LANES, d),
            const(1, LANES), const(LANES, 1), const(LANES, hg), const(1, hg), const(da, LANES),
        ],
        out_specs=[
            pl.BlockSpec((1, tk, hg), lambda b, t: (b, t, 0)),
            pl.BlockSpec((1, 1, dv, tk), lambda b, t: (b, t, 0, 0)),
            pl.BlockSpec((1, LANES, tk), lambda b, t: (b, 0, t)),
            pl.BlockSpec((1, 1, SUBLANES, LANES), lambda b, t: (b, t, 0, 0)),
        ],
        out_shape=[
            jax.ShapeDtypeStruct((bsz, s, hg), BF16),
            jax.ShapeDtypeStruct((bsz, nk, dv, tk), BF16),
            jax.ShapeDtypeStruct((bsz, LANES, s), F32),
            jax.ShapeDtypeStruct((bsz, nk, SUBLANES, LANES), F32),
        ],
        scratch_shapes=[pltpu.VMEM((LANES, 1), F32), pltpu.VMEM((1, LANES), F32)],
        compiler_params=pltpu.CompilerParams(dimension_semantics=("parallel", "arbitrary")),
        name="kv_proj",
    )(x, g.reshape(1, d), wk, wvT, ones_v, wf, wf.T, bfr, bfr.reshape(LANES, 1), sel_k, ones_k, head_sum)


def _q_kernel(x_ref, g_ref, wqT_ref, wg_ref, c2r_ref, selq_ref, onesq_ref, hsumT_ref, qa_ref, gate_ref, qn_ref):
    xn = _rms(x_ref[0], g_ref[...]).astype(BF16)
    gate_ref[0] = jax.nn.sigmoid(jnp.dot(xn, wg_ref[...], preferred_element_type=F32)).astype(BF16)
    qr = lax.dot_general(wqT_ref[...], xn, NT_DIMS, preferred_element_type=F32) * (LOG2E * HEAD_DIM ** -0.5)

    qss = jnp.dot(hsumT_ref[...], (qr * qr).astype(BF16), preferred_element_type=F32)
    qn = jnp.sqrt(jnp.max(qss, axis=1, keepdims=True))
    qn_ref[0, 0] = jnp.broadcast_to(qn, (LANES, LANES))

    tq = c2r_ref.shape[2]
    gap = jnp.zeros((HEAD_GROUP - HEAD_DIM, tq), F32)
    qf = jnp.concatenate([blk for h in range(N_HEADS) for blk in (qr[h * HEAD_DIM:(h + 1) * HEAD_DIM], gap)], axis=0)
    packed = jnp.concatenate([p[:N_HEADS] for p in _split3(c2r_ref[0])]
                             + [jnp.zeros((LANES - 3 * N_HEADS, tq), BF16)], axis=0)
    bias = jnp.dot(selq_ref[...], packed, preferred_element_type=F32)
    qa_ref[0] = (qf + bias + onesq_ref[...]).astype(BF16)


def _q_proj(x, g, w_qg, c2r, consts, *, tq):
    bsz, s, d = x.shape
    da = N_HEADS * HEAD_DIM
    hg = N_HEADS * HEAD_GROUP
    nq = s // tq
    _, sel_q, _, ones_q, head_sum = consts
    wqT = w_qg[:, :da].T.astype(BF16)
    const = lambda *shape: _const_spec(shape, 2)
    return pl.pallas_call(
        _q_kernel,
        grid=(bsz, nq),
        in_specs=[
            pl.BlockSpec((1, tq, d), lambda b, t: (b, t, 0)),
            const(1, d), const(da, d), const(d, da),
            pl.BlockSpec((1, LANES, tq), lambda b, t: (b, 0, t)),
            const(hg, LANES), const(hg, 1), const(LANES, da),
        ],
        out_specs=[
            pl.BlockSpec((1, hg, tq), lambda b, t: (b, 0, t)),
            pl.BlockSpec((1, tq, da), lambda b, t: (b, t, 0)),
            pl.BlockSpec((1, 1, LANES, LANES), lambda b, t: (b, t, 0, 0)),
        ],
        out_shape=[
            jax.ShapeDtypeStruct((bsz, hg, s), BF16),
            jax.ShapeDtypeStruct((bsz, s, da), BF16),
            jax.ShapeDtypeStruct((bsz, nq, LANES, LANES), F32),
        ],
        compiler_params=pltpu.CompilerParams(dimension_semantics=("parallel", "parallel")),
        name="q_proj",
    )(x, g.reshape(1, d), wqT, w_qg[:, da:].astype(BF16), c2r, sel_q, ones_q, head_sum.T)


def _attn_kernel(jlo_ref, qa_ref, qan_ref, ka_ref, vT_ref, gate_ref, o_ref, ready, m0, m1, acc0, acc1,
                 sa0, sa1, sb0, sb1, sc0, sc1, pa0, pa1, pb0, pb1):
    b = pl.program_id(0)
    hp = pl.program_id(1)
    i = pl.program_id(2)
    nq = pl.num_programs(2)
    tq = qa_ref.shape[2]
    tk = vT_ref.shape[3]
    base = (b * pl.num_programs(1) + hp) * nq
    j_lo = jlo_ref[base + i]
    j_lo_next = jlo_ref[base + jnp.minimum(i + 1, nq - 1)]
    m_refs, acc_refs = (m0, m1), (acc0, acc1)
    s_a, s_b, s_c = (sa0, sa1), (sb0, sb1), (sc0, sc1)
    p_a, p_b = (pa0, pa1), (pb0, pb1)

    def scores(q_ref, j, h):
        k_blk = ka_ref[0, pl.ds(pl.multiple_of(j * tk, tk), tk), h * HEAD_GROUP:(h + 1) * HEAD_GROUP]
        qT = q_ref[0, h * HEAD_GROUP:(h + 1) * HEAD_GROUP, :]
        return jnp.dot(k_blk, qT, preferred_element_type=F32)

    def values(j, h, p):
        vT = vT_ref[0, j, h * V_GROUP:(h + 1) * V_GROUP, :]
        return jnp.dot(vT, p, preferred_element_type=F32)

    def softmax_tile(h, s, masked):
        if masked:
            kv_pos = lax.broadcasted_iota(jnp.int32, (tk, tq), 0)
            q_pos = lax.broadcasted_iota(jnp.int32, (tk, tq), 1)
            s = jnp.where(kv_pos <= q_pos, s, NEG_BIG)
        m_old = m_refs[h][...]
        m_new = jnp.maximum(m_old, jnp.max(s, axis=0, keepdims=True))
        m_refs[h][...] = m_new
        return jnp.exp2(m_old - m_new), jnp.exp2(s - m_new).astype(BF16)

    def step(s_src, s_dst, p_prev, p_dst, j):
        for h in range(2):
            s_dst[h][...] = scores(qa_ref, j + 1, h)
        pv = None if p_prev is None else [values(j - 1, h, p_prev[h][...]) for h in range(2)]
        for h in range(2):
            alpha, p = softmax_tile(h, s_src[h][...], masked=False)
            p_dst[h][...] = p
            if pv is not None:
                acc_refs[h][...] = alpha * (acc_refs[h][...] + pv[h])

    def finish(s_src, p_prev, prefetch):
        if prefetch:
            for h in range(2):
                s_c[h][...] = scores(qan_ref, j_lo_next, h)
            ready[0] = (i + 1 < nq).astype(jnp.int32)
        else:
            ready[0] = jnp.int32(0)
        pv = None if p_prev is None else [values(i - 1, h, p_prev[h][...]) for h in range(2)]
        outs = []
        for h in range(2):
            alpha, p = softmax_tile(h, s_src[h][...], masked=True)
            acc = values(i, h, p)
            if pv is not None:
                acc = acc + alpha * (acc_refs[h][...] + pv[h])
            outs.append(acc[0:HEAD_DIM, :] / acc[HEAD_DIM:HEAD_DIM + 1, :])
        oT = jnp.concatenate(outs, axis=0)
        o_ref[0] = (oT.T * gate_ref[0].astype(F32)).astype(BF16)

    @pl.when(i == 0)
    def _():
        ready[0] = jnp.int32(0)

    @pl.when(ready[0] == 0)
    def _():
        for h in range(2):
            s_c[h][...] = scores(qa_ref, j_lo, h)

    for h in range(2):
        m_refs[h][...] = jnp.full((1, tq), NEG_BIG, F32)
        acc_refs[h][...] = jnp.zeros((V_GROUP, tq), F32)

    n_full = i - j_lo

    @pl.when(n_full == 0)
    def _():
        finish(s_c, None, prefetch=False)

    @pl.when(n_full > 0)
    def _():
        step(s_c, s_a, None, p_a, j_lo)
        rest = n_full - 1

        def body(n, carry):
            j = j_lo + 1 + 2 * n
            step(s_a, s_b, p_a, p_b, j)
            step(s_b, s_a, p_b, p_a, j + 1)
            return carry

        lax.fori_loop(0, rest // 2, body, 0)

        @pl.when(rest % 2 == 1)
        def _():
            step(s_a, s_b, p_a, p_b, i - 1)
            finish(s_b, p_b, prefetch=True)

        @pl.when(rest % 2 == 0)
        def _():
            finish(s_a, p_a, prefetch=True)


def _permute_heads(w_kvf, b_f, w_qg, w_o):
    da = N_HEADS * HEAD_DIM
    perm = jnp.argsort(b_f)
    cols = (perm[:, None] * HEAD_DIM + jnp.arange(HEAD_DIM)[None, :]).reshape(-1)
    w_kvf_p = jnp.concatenate([w_kvf[:, :da][:, cols], w_kvf[:, da:2 * da][:, cols], w_kvf[:, 2 * da:][:, perm]],
                              axis=1)
    w_qg_p = jnp.concatenate([w_qg[:, :da][:, cols], w_qg[:, da:][:, cols]], axis=1)
    return w_kvf_p, b_f[perm], w_qg_p, w_o[cols, :]


def _skip_table(qn, kn, c2r, tq):
    bsz, nq = qn.shape[0], qn.shape[1]
    qn = qn[:, :, :N_HEADS, 0].transpose(0, 2, 1) * NORM_SLACK
    kn = kn[:, :, 0, :N_HEADS].transpose(0, 2, 1) * NORM_SLACK
    c2 = c2r[:, :N_HEADS, :]
    c_first = c2[:, :, ::tq]
    c_last = c2[:, :, tq - 1::tq]
    upper = qn[..., :, None] * kn[..., None, :] + c_first[..., :, None] - c_last[..., None, :]
    lower = -(qn * kn)[..., :, None]
    skip = (upper - lower) < -SKIP_GAP_LOG2
    nk = skip.shape[-1]
    jlo = jnp.min(jnp.where(skip, nk, jnp.arange(nk, dtype=jnp.int32)), axis=-1)
    jlo = jnp.minimum(jlo, jnp.arange(nq, dtype=jnp.int32))
    jlo = jnp.min(jlo.reshape(bsz, N_HEADS // 2, 2, nq), axis=2)
    return jlo.reshape(-1).astype(jnp.int32)


def _attention(jlo, qa, ka, vT, gate, *, tq):
    bsz, hg, s = qa.shape
    nk, tk = vT.shape[1], vT.shape[3]
    da = N_HEADS * HEAD_DIM
    assert tq == tk
    pair = 2 * HEAD_GROUP
    hw = 2 * HEAD_DIM
    nq = s // tq
    grid_spec = pltpu.PrefetchScalarGridSpec(
        num_scalar_prefetch=1,
        grid=(bsz, N_HEADS // 2, nq),
        in_specs=[
            pl.BlockSpec((1, pair, tq), lambda b, hp, i, jlo: (b, hp, i)),
            pl.BlockSpec((1, pair, tq), lambda b, hp, i, jlo: (b, hp, jnp.minimum(i + 1, nq - 1))),
            pl.BlockSpec((1, s, pair), lambda b, hp, i, jlo: (b, 0, hp)),
            pl.BlockSpec((1, nk, 2 * V_GROUP, tk), lambda b, hp, i, jlo: (b, 0, hp, 0)),
            pl.BlockSpec((1, tq, hw), lambda b, hp, i, jlo: (b, i, hp)),
        ],
        out_specs=pl.BlockSpec((1, tq, hw), lambda b, hp, i, jlo: (b, i, hp)),
        scratch_shapes=([pltpu.SMEM((1,), jnp.int32)]
                        + [pltpu.VMEM((1, tq), F32)] * 2 + [pltpu.VMEM((V_GROUP, tq), F32)] * 2
                        + [pltpu.VMEM((tk, tq), F32)] * 6 + [pltpu.VMEM((tk, tq), BF16)] * 4),
    )
    return pl.pallas_call(
        _attn_kernel,
        grid_spec=grid_spec,
        out_shape=jax.ShapeDtypeStruct((bsz, s, da), BF16),
        compiler_params=pltpu.CompilerParams(dimension_semantics=("parallel", "parallel", "arbitrary")),
        name="fox_attention",
    )(jlo, qa, qa, ka, vT, gate)


def _pack_bf16_pairs(x):
    half = x.shape[1] // 2
    lo = lax.bitcast_convert_type(x[:, :half].astype(BF16).astype(F32), jnp.uint32)
    hi = lax.bitcast_convert_type(x[:, half:].astype(BF16).astype(F32), jnp.uint32)
    word = (lo >> 16) | (hi & jnp.uint32(0xFFFF0000))
    return lax.bitcast_convert_type(word, jnp.int32)


def _unpack_bf16_pairs(w):
    u = lax.bitcast_convert_type(w, jnp.uint32)
    lo = lax.bitcast_convert_type(u << 16, F32)
    hi = lax.bitcast_convert_type(u & jnp.uint32(0xFFFF0000), F32)
    return jnp.concatenate([lo, hi], axis=1)


def _store_planes(ref, words):
    q = words.shape[1] // 2
    ref[0] = words[:, :q]
    ref[1] = words[:, q:]


def _load_planes(ref, base=0):
    return jnp.concatenate([ref[base], ref[base + 1]], axis=1)


def _router_kernel(x_ref, o_ref, wo_ref, g_ref, rtT_ref, x2_ref, hnp_ref, route_ref):
    tm = x_ref.shape[0]
    x2 = x_ref[...] + jnp.dot(o_ref[...], wo_ref[...], preferred_element_type=F32)
    x2_ref[...] = x2
    hn = _rms(x2, g_ref[...])
    _store_planes(hnp_ref, _pack_bf16_pairs(hn))
    h0, h1, _ = _split3(hn)
    r0, r1, _ = _split3(rtT_ref[...])
    lt = lax.dot_general(jnp.concatenate([r0, r1, r0], axis=1), jnp.concatenate([h0, h0, h1], axis=1), NT_DIMS,
                         preferred_element_type=F32)[:N_EXPERTS]
    row = lax.broadcasted_iota(jnp.int32, (N_EXPERTS, tm), 0)
    m1 = jnp.max(lt, axis=0, keepdims=True)
    i1 = jnp.min(jnp.where(lt == m1, row, N_EXPERTS), axis=0, keepdims=True)
    rest = jnp.where(row == i1, -jnp.inf, lt)
    m2 = jnp.max(rest, axis=0, keepdims=True)
    i2 = jnp.min(jnp.where(rest == m2, row, N_EXPERTS), axis=0, keepdims=True)
    e2 = jnp.exp(m2 - m1)
    w1 = 1.0 / (1.0 + e2)
    w2 = e2 / (1.0 + e2)
    route_ref[...] = jnp.where(row == 0, i1.astype(F32), jnp.where(row == 1, i2.astype(F32),
                               jnp.where(row == 2, w1, jnp.where(row == 3, w2, 0.0))))


def _router(x2d, o2d, w_o, g, router, *, tm):
    t, d = x2d.shape
    da = o2d.shape[1]
    ne = router.shape[1]
    rtT = jnp.pad(router.T, ((0, 2 * SUBLANES - ne), (0, 0)))
    return pl.pallas_call(
        _router_kernel,
        grid=(t // tm,),
        in_specs=[
            pl.BlockSpec((tm, d), lambda i: (i, 0)),
            pl.BlockSpec((tm, da), lambda i: (i, 0)),
            pl.BlockSpec((da, d), lambda i: (0, 0)),
            pl.BlockSpec((1, d), lambda i: (0, 0)),
            pl.BlockSpec((2 * SUBLANES, d), lambda i: (0, 0)),
        ],
        out_specs=[pl.BlockSpec((tm, d), lambda i: (i, 0)), pl.BlockSpec((2, tm, d // 4), lambda i: (0, i, 0)),
                   pl.BlockSpec((ne, tm), lambda i: (0, i))],
        out_shape=[jax.ShapeDtypeStruct((t, d), F32), jax.ShapeDtypeStruct((2, t, d // 4), jnp.int32),
                   jax.ShapeDtypeStruct((ne, t), F32)],
        compiler_params=pltpu.CompilerParams(dimension_semantics=("parallel",)),
        name="attn_out_moe_router",
    )(x2d, o2d, w_o.astype(BF16), g.reshape(1, d), rtT)


def _routing_tables(route, ts):
    ne = N_EXPERTS
    t = route.shape[1]
    e_pair = route[0:2].astype(jnp.int32).reshape(-1)
    n_tiles = (2 * t) // ts + ne
    n_slots = n_tiles * ts
    onehot = (e_pair[:, None] == jnp.arange(ne, dtype=jnp.int32)[None, :]).astype(jnp.int32)
    csum = jnp.cumsum(onehot, axis=0)
    rank = jnp.sum(onehot * csum, axis=1) - 1
    tiles = (csum[-1] + ts - 1) // ts
    tile_end = jnp.cumsum(tiles)
    tile_start = tile_end - tiles
    slot = (jnp.sum(onehot * tile_start[None, :], axis=1) * ts + rank).astype(jnp.int32)
    n_used = tile_end[-1]
    tile_ids = jnp.arange(n_tiles, dtype=jnp.int32)
    tile_expert = jnp.sum((tile_ids[:, None] >= tile_end[None, :]).astype(jnp.int32), axis=1)
    last_expert = jnp.sum(((n_used - 1) >= tile_end).astype(jnp.int32))
    tile_expert = jnp.minimum(tile_expert, last_expert).astype(jnp.int32)
    plane_rows = slot.reshape(2, 1, t) + (jnp.arange(2, dtype=jnp.int32) * n_slots).reshape(1, 2, 1)
    return plane_rows.reshape(-1), tile_expert, n_used.reshape(1).astype(jnp.int32), n_slots


def _sc_mesh():
    return plsc.VectorSubcoreMesh(core_axis_name="core", subcore_axis_name="subcore")


def _scatter_rows(table, idx, n_out):
    m = idx.shape[0]
    n, d = table.shape
    assert m % SC_ROW_WINDOW == 0 and n % SC_ROW_WINDOW == 0
    n_src_blocks = n // SC_ROW_WINDOW

    @pl.kernel(out_type=jax.ShapeDtypeStruct((n_out, d), table.dtype), mesh=_sc_mesh(), scratch_types=[])
    def scatter_kernel(x_hbm, i_hbm, o_hbm):
        def body(x_vmem, i_vmem):
            pltpu.sync_copy(x_vmem, o_hbm.at[i_vmem.at[0]])

        pltpu.emit_pipeline(
            body,
            grid=(m // SC_ROW_WINDOW,),
            in_specs=[pl.BlockSpec((SC_ROW_WINDOW, d), index_map=lambda i: (i % n_src_blocks, 0)),
                      pl.BlockSpec((1, SC_ROW_WINDOW), index_map=lambda i: (0, i))],
            out_specs=[],
            core_axis_name=("core", "subcore"),
            dimension_semantics=(pltpu.PARALLEL,),
        )(x_hbm, i_hbm)

    return scatter_kernel(table, idx.reshape(1, m))


def _gather_rows(table, idx):
    m = idx.shape[0]
    d = table.shape[1]
    assert m % SC_ROW_WINDOW == 0

    @pl.kernel(out_type=jax.ShapeDtypeStruct((m, d), table.dtype), mesh=_sc_mesh())
    def gather_kernel(x_hbm, i_hbm, o_hbm):
        def body(i_vmem, o_vmem):
            pltpu.sync_copy(x_hbm.at[i_vmem.at[0]], o_vmem)

        pltpu.emit_pipeline(
            body,
            grid=(m // SC_ROW_WINDOW,),
            in_specs=[pl.BlockSpec((1, SC_ROW_WINDOW), index_map=lambda i: (0, i))],
            out_specs=[pl.BlockSpec((SC_ROW_WINDOW, d), index_map=lambda i: (i, 0))],
            core_axis_name=("core", "subcore"),
            dimension_semantics=(pltpu.PARALLEL,),
        )(i_hbm, o_hbm)

    return gather_kernel(table, idx.reshape(1, m))


def _expert_kernel(nk, te_ref, nu_ref, xs_ref, wg_ref, wu_ref, wd_ref, ys_ref, x_sc, acc_ref):
    i = pl.program_id(0)
    k = pl.program_id(1)
    used = i < nu_ref[0]

    def first():
        x_sc[...] = _unpack_bf16_pairs(_load_planes(xs_ref)).astype(BF16)

    def last(total):
        _store_planes(ys_ref, _pack_bf16_pairs(total))

    @pl.when(used)
    def _():
        _chunked_swiglu(k, nk, lambda: _swiglu_chunk(x_sc[...], wg_ref.at[0], wu_ref.at[0], wd_ref.at[0]), acc_ref,
                        first, last)

    @pl.when(jnp.logical_not(used) & (k == nk - 1))
    def _():
        ys_ref[...] = jnp.zeros_like(ys_ref)


def _experts(xs, tile_expert, n_used, w_gu, w_down, *, ts, tf):
    _, n_slots, quarter = xs.shape
    d = 4 * quarter
    dexp = w_down.shape[1]
    nk = dexp // tf
    w_gu = w_gu.astype(BF16)
    grid_spec = pltpu.PrefetchScalarGridSpec(
        num_scalar_prefetch=2,
        grid=(n_slots // ts, nk),
        in_specs=[
            pl.BlockSpec((2, ts, quarter), lambda i, k, te, nu: (0, i, 0)),
            pl.BlockSpec((1, d, tf), lambda i, k, te, nu: (te[i], 0, k)),
            pl.BlockSpec((1, d, tf), lambda i, k, te, nu: (te[i], 0, nk + k)),
            pl.BlockSpec((1, tf, d), lambda i, k, te, nu: (te[i], k, 0)),
        ],
        out_specs=pl.BlockSpec((2, ts, quarter), lambda i, k, te, nu: (0, i, 0)),
        scratch_shapes=[pltpu.VMEM((ts, d), BF16), pltpu.VMEM((ts, d), F32)],
    )
    return pl.pallas_call(
        functools.partial(_expert_kernel, nk),
        grid_spec=grid_spec,
        out_shape=jax.ShapeDtypeStruct((2, n_slots, quarter), jnp.int32),
        compiler_params=pltpu.CompilerParams(dimension_semantics=("arbitrary", "arbitrary")),
        name="moe_experts",
    )(tile_expert, n_used, xs, w_gu, w_gu, w_down.astype(BF16))


def _combine_kernel(x_ref, y_ref, rt_ref, gf_ref, o_ref):
    rt = rt_ref[...]
    moe = (rt[:, 2:3] * _unpack_bf16_pairs(_load_planes(y_ref, 0))
           + rt[:, 3:4] * _unpack_bf16_pairs(_load_planes(y_ref, 2)))
    o_ref[...] = _rms(x_ref[...] + moe, gf_ref[...])


def _combine(x2d, y4, route_t, g_final, *, tm):
    t, d = x2d.shape
    ne = route_t.shape[1]
    return pl.pallas_call(
        _combine_kernel,
        grid=(t // tm,),
        in_specs=[
            pl.BlockSpec((tm, d), lambda i: (i, 0)),
            pl.BlockSpec((4, tm, d // 4), lambda i: (0, i, 0)),
            pl.BlockSpec((tm, ne), lambda i: (i, 0)),
            pl.BlockSpec((1, d), lambda i: (0, 0)),
        ],
        out_specs=pl.BlockSpec((tm, d), lambda i: (i, 0)),
        out_shape=jax.ShapeDtypeStruct((t, d), F32),
        compiler_params=pltpu.CompilerParams(dimension_semantics=("parallel",)),
        name="moe_combine",
    )(x2d, y4, route_t, g_final.reshape(1, d))


def _attn_out_moe(x1, o2d, w_o, g, router, w_gu, w_down, g_final, *, tm, ts, tf):
    t, d = x1.shape
    q = d // 4
    x2d, hn_planes, route = _router(x1, o2d, w_o, g, router, tm=tm)
    plane_rows, tile_expert, n_used, n_slots = _routing_tables(route, ts)
    xs = _scatter_rows(hn_planes.reshape(2 * t, q), plane_rows, 2 * n_slots)
    ys = _experts(xs.reshape(2, n_slots, q), tile_expert, n_used, w_gu, w_down, ts=ts, tf=tf)
    y4 = _gather_rows(ys.reshape(2 * n_slots, q), plane_rows)
    return _combine(x2d, y4.reshape(4, t, q), route.T, g_final, tm=tm)


def _tile(n, pref):
    t = min(n, pref)
    assert n % t == 0
    return t


def kernel(x, norm_g, lru_w_in, lru_conv_w, lru_conv_b, lru_gate_w, lru_gate_b, lru_lambda, lru_w_out,
           kv_norm_g, w_kvf, b_f, fox_w_qg, fox_w_o, mlp_w_gu, mlp_w_down, moe_router, moe_w_gu,
           moe_w_down, final_norm_g):
    bsz, s, d = x.shape
    t = bsz * s
    tm = _tile(t, 512)
    ts = _tile(s, 512)

    x = _lru_layer(x, norm_g[0, 0], lru_w_in[0], lru_conv_w[0], lru_conv_b[0], lru_gate_w[0],
                   lru_gate_b[0], lru_lambda[0], lru_w_out[0], tt=_tile(s, 256))
    dff = mlp_w_down.shape[1]
    x = _mlp(x.reshape(t, d), norm_g[0, 1], mlp_w_gu[0], mlp_w_down[0], tm=tm, tf=dff // 2).reshape(bsz, s, d)

    consts = _head_selectors()
    w_kvf_p, b_f_p, w_qg_p, w_o_p = _permute_heads(w_kvf, b_f, fox_w_qg[0], fox_w_o[0])
    ka, vT, c2r, kn = _kv_proj(x, kv_norm_g, w_kvf_p, b_f_p, consts, tk=ts)
    qa, gate, qn = _q_proj(x, norm_g[1, 0], w_qg_p, c2r, consts, tq=ts)
    jlo = _skip_table(qn, kn, c2r, ts)
    o = _attention(jlo, qa, ka, vT, gate, tq=ts)
    dexp = moe_w_down.shape[2]
    out = _attn_out_moe(x.reshape(t, d), o.reshape(t, -1), w_o_p, norm_g[1, 1], moe_router[0], moe_w_gu[0],
                        moe_w_down[0], final_norm_g, tm=tm, ts=tm, tf=dexp // 2)
    return out.reshape(bsz, s, d)
```

```python
import functools
import math

import jax
import jax.numpy as jnp
from jax import lax
from jax.experimental import pallas as pl
from jax.experimental.pallas import tpu as pltpu
from jax.experimental.pallas import tpu_sc as plsc

EPS = 1e-6
LRU_C = 8.0
N_LRU_BLOCKS = 8
CONV_W = 4
N_HEADS = 16
HEAD_DIM = 64
N_EXPERTS = 8
LANES = 128
SUBLANES = 8
HEAD_GROUP = 128
BIAS_Q = HEAD_DIM
BIAS_K = HEAD_DIM + 3
V_GROUP = 80
NEG_BIG = -1e30
LOG2E = math.log2(math.e)
SKIP_GAP_LOG2 = 152.0
NORM_SLACK = 1.01
SC_ROW_WINDOW = 128

F32 = jnp.float32
BF16 = jnp.bfloat16
NT_DIMS = (((1,), (1,)), ((), ()))


def _rms(x, g):
    ms = jnp.mean(x * x, axis=-1, keepdims=True)
    return x * lax.rsqrt(ms + EPS) * g


def _log_sigmoid(x):
    return jnp.minimum(x, 0.0) - jnp.log1p(jnp.exp(-jnp.abs(x)))


def _gelu_tanh(x):
    c = 0.7978845608028654
    return 0.5 * x * (1.0 + jnp.tanh(c * (x + 0.044715 * (x * x * x))))


def _silu(x):
    return x * jax.nn.sigmoid(x)


def _split3(x):
    p0 = x.astype(BF16)
    r1 = x - p0.astype(F32)
    p1 = r1.astype(BF16)
    p2 = (r1 - p1.astype(F32)).astype(BF16)
    return p0, p1, p2


def _const_spec(shape, n_grid):
    zeros = (0,) * len(shape)
    if n_grid == 1:
        return pl.BlockSpec(shape, lambda i: zeros)
    if n_grid == 2:
        return pl.BlockSpec(shape, lambda i, j: zeros)
    return pl.BlockSpec(shape, lambda i, j, k: zeros)


def _lru_in_proj(x, g_ref, win_ref):
    d = x.shape[1]
    xn = _rms(x, g_ref[...]).astype(BF16)
    xg = jnp.dot(xn, win_ref[...], preferred_element_type=F32)
    return xg[:, :d], xg[:, d:]


def _lru_conv_gates(xb, cw_ref, cb_ref, gw_ref, xbuf):
    tt, d = xb.shape
    bw = d // N_LRU_BLOCKS
    xbuf[SUBLANES:SUBLANES + tt, :] = xb
    cw = cw_ref[...]
    xc = cb_ref[...] + cw[CONV_W - 1:CONV_W, :] * xb
    for j in range(CONV_W - 1):
        xc = xc + cw[j:j + 1, :] * xbuf[pl.ds(SUBLANES - (CONV_W - 1) + j, tt), :]
    xbuf[0:SUBLANES, :] = xbuf[tt:tt + SUBLANES, :]

    xcb = xc.astype(BF16)
    rl, il = [], []
    for n in range(N_LRU_BLOCKS):
        gl = jnp.dot(xcb[:, n * bw:(n + 1) * bw], gw_ref[n], preferred_element_type=F32)
        rl.append(gl[:, :bw])
        il.append(gl[:, bw:])
    return xc, jnp.concatenate(rl, axis=1), jnp.concatenate(il, axis=1)


def _lru_recurrence(xc, r_logit, i_logit, gbr, gb_ref, lam_ref, hcar):
    tt = xc.shape[0]
    gb = gb_ref[...]
    r = jax.nn.sigmoid(r_logit + gb[0:1, :])
    i = jax.nn.sigmoid(i_logit + gb[1:2, :])
    log_a = LRU_C * r * _log_sigmoid(lam_ref[...])
    a = jnp.exp(log_a)
    b = jnp.sqrt((1.0 - a) * (1.0 + a)) * (i * xc)

    row = lax.broadcasted_iota(jnp.int32, (tt, 1), 0)
    sh = 1
    while sh < tt:
        if sh < SUBLANES:
            keep = row >= sh
            a_sh = jnp.where(keep, pltpu.roll(a, sh, 0), 1.0)
            b_sh = jnp.where(keep, pltpu.roll(b, sh, 0), 0.0)
            b = a * b_sh + b
            a = a * a_sh
        else:
            b = jnp.concatenate([b[:sh], a[sh:] * b[:tt - sh] + b[sh:]], axis=0)
            a = jnp.concatenate([a[:sh], a[sh:] * a[:tt - sh]], axis=0)
        sh *= 2
    h = b + a * hcar[...]
    hcar[...] = h[tt - 1:tt, :]
    return (_gelu_tanh(gbr) * h).astype(BF16)


def _lru_mlp_kernel(x_ref, g0_ref, win_ref, cw_ref, cb_ref, gw_ref, gb_ref, lam_ref, wout_ref,
                    g1_ref, wg_ref, wu_ref, wd_ref, o_ref, xbuf, hcar, x1_sc, hn_sc, acc_ref):
    t = pl.program_id(1)
    k = pl.program_id(2)
    half = x_ref.shape[1]
    tf = wg_ref.shape[1]
    mid = (tf // (2 * LANES)) * LANES
    slot = t % 2
    prev = 1 - slot

    @pl.when((t == 0) & (k == 0))
    def _():
        xbuf[0:SUBLANES, :] = jnp.zeros((SUBLANES, xbuf.shape[1]), F32)
        hcar[...] = jnp.zeros_like(hcar)
        x1_sc[...] = jnp.zeros_like(x1_sc)
        hn_sc[...] = jnp.zeros_like(hn_sc)

    def gate_up(hn, lo, hi):
        return (jnp.dot(hn, wg_ref[:, lo:hi], preferred_element_type=F32),
                jnp.dot(hn, wu_ref[:, lo:hi], preferred_element_type=F32))

    def both():
        x = x_ref[0]
        hn = hn_sc[prev]
        xb, gbr = _lru_in_proj(x, g0_ref, win_ref)
        g_a, u_a = gate_up(hn, 0, mid)
        xc, r_logit, i_logit = _lru_conv_gates(xb, cw_ref, cb_ref, gw_ref, xbuf)
        g_b, u_b = gate_up(hn, mid, tf)
        h = jnp.concatenate([(_silu(g_a) * u_a).astype(BF16), (_silu(g_b) * u_b).astype(BF16)], axis=1)
        y_mlp = jnp.dot(h, wd_ref[...], preferred_element_type=F32)
        y_lru = _lru_recurrence(xc, r_logit, i_logit, gbr, gb_ref, lam_ref, hcar)
        x1 = x + jnp.dot(y_lru, wout_ref[...], preferred_element_type=F32)
        rows = pl.ds(pl.multiple_of(k * half, half), half)
        x1_sc[slot, rows, :] = x1
        hn_sc[slot, rows, :] = _rms(x1, g1_ref[...]).astype(BF16)
        return y_mlp

    @pl.when(k == 0)
    def _():
        acc_ref[...] = both()

    @pl.when(k == 1)
    def _():
        y = both()
        o_ref[0] = x1_sc[prev] + acc_ref[...] + y


def _lru_mlp(x, g0, w_in, conv_w, conv_b, gate_w, gate_b, lam, w_out, g1, w_gu, w_down, *, tm):
    bsz, s, d = x.shape
    bw = d // N_LRU_BLOCKS
    dff = w_down.shape[0]
    tf = dff // 2
    half = tm // 2
    nt = s // tm
    n_half = s // half
    gw = jnp.concatenate([gate_w[0], gate_w[1]], axis=-1).astype(BF16)
    w_gu = w_gu.astype(BF16)

    def const(*shape):
        return pl.BlockSpec(shape, lambda b, t, k: (0,) * len(shape), pipeline_mode=pl.Buffered(1))

    return pl.pallas_call(
        _lru_mlp_kernel,
        grid=(bsz, nt + 1, 2),
        in_specs=[
            pl.BlockSpec((1, half, d), lambda b, t, k: (b, jnp.minimum(2 * t + k, n_half - 1), 0)),
            const(1, d), const(d, 2 * d), const(CONV_W, d), const(1, d),
            const(N_LRU_BLOCKS, bw, 2 * bw), const(2, d), const(1, d), const(d, d),
            const(1, d),
            pl.BlockSpec((d, tf), lambda b, t, k: (0, k)),
            pl.BlockSpec((d, tf), lambda b, t, k: (0, 2 + k)),
            pl.BlockSpec((tf, d), lambda b, t, k: (k, 0)),
        ],
        out_specs=pl.BlockSpec((1, tm, d), lambda b, t, k: (b, jnp.maximum(t - 1, 0), 0)),
        out_shape=jax.ShapeDtypeStruct((bsz, s, d), F32),
        scratch_shapes=[pltpu.VMEM((half + 2 * SUBLANES, d), F32), pltpu.VMEM((1, d), F32),
                        pltpu.VMEM((2, tm, d), F32), pltpu.VMEM((2, tm, d), BF16), pltpu.VMEM((tm, d), F32)],
        compiler_params=pltpu.CompilerParams(dimension_semantics=("parallel", "arbitrary", "arbitrary")),
        name="lru_mlp",
    )(x, g0.reshape(1, d), w_in.astype(BF16), conv_w, conv_b.reshape(1, d), gw, gate_b, lam.reshape(1, d),
      w_out.astype(BF16), g1.reshape(1, d), w_gu, w_gu, w_down.astype(BF16))


def _swiglu_chunk(x, wg, wu, wd):
    gate = jnp.dot(x, wg[...], preferred_element_type=F32)
    up = jnp.dot(x, wu[...], preferred_element_type=F32)
    h = (_silu(gate) * up).astype(BF16)
    return jnp.dot(h, wd[...], preferred_element_type=F32)


def _chunked_swiglu(k, nk, chunk, acc_ref, first, last):
    if nk == 1:
        first()
        last(chunk())
        return

    @pl.when(k == 0)
    def _():
        first()
        acc_ref[...] = chunk()

    if nk > 2:
        @pl.when((k > 0) & (k < nk - 1))
        def _():
            acc_ref[...] += chunk()

    @pl.when(k == nk - 1)
    def _():
        last(acc_ref[...] + chunk())


def _mlp_kernel(nk, x_ref, g_ref, wg_ref, wu_ref, wd_ref, o_ref, hn_ref, acc_ref):
    def first():
        hn_ref[...] = _rms(x_ref[...], g_ref[...]).astype(BF16)

    def last(total):
        o_ref[...] = x_ref[...] + total

    _chunked_swiglu(pl.program_id(1), nk, lambda: _swiglu_chunk(hn_ref[...], wg_ref, wu_ref, wd_ref), acc_ref,
                    first, last)


def _mlp(x2d, g, w_gu, w_down, *, tm, tf):
    t, d = x2d.shape
    dff = w_down.shape[0]
    nk = dff // tf
    w_gu = w_gu.astype(BF16)
    return pl.pallas_call(
        functools.partial(_mlp_kernel, nk),
        grid=(t // tm, nk),
        in_specs=[
            pl.BlockSpec((tm, d), lambda i, k: (i, 0)),
            pl.BlockSpec((1, d), lambda i, k: (0, 0)),
            pl.BlockSpec((d, tf), lambda i, k: (0, k)),
            pl.BlockSpec((d, tf), lambda i, k: (0, nk + k)),
            pl.BlockSpec((tf, d), lambda i, k: (k, 0)),
        ],
        out_specs=pl.BlockSpec((tm, d), lambda i, k: (i, 0)),
        out_shape=jax.ShapeDtypeStruct((t, d), F32),
        scratch_shapes=[pltpu.VMEM((tm, d), BF16), pltpu.VMEM((tm, d), F32)],
        compiler_params=pltpu.CompilerParams(dimension_semantics=("parallel", "arbitrary")),
        name="mlp",
    )(x2d, g.reshape(1, d), w_gu, w_gu, w_down.astype(BF16))


def _head_selectors():
    hg = N_HEADS * HEAD_GROUP
    head = jnp.arange(hg) // HEAD_GROUP
    lane = jnp.arange(hg) % HEAD_GROUP
    col = jnp.arange(LANES)
    def sel(base):
        return ((col[:, None] % N_HEADS == head[None, :]) & (col[:, None] < 3 * N_HEADS)
                & (lane[None, :] == base + col[:, None] // N_HEADS)).astype(BF16)
    sel_k = -sel(BIAS_K)
    sel_q = sel(BIAS_Q).T
    ones_k = ((lane >= BIAS_Q) & (lane < BIAS_Q + 3)).astype(F32).reshape(1, hg)
    ones_q = ((lane >= BIAS_K) & (lane < BIAS_K + 3)).astype(F32).reshape(hg, 1)
    feat_head = jnp.arange(N_HEADS * HEAD_DIM) // HEAD_DIM
    head_sum = (col[None, :] == feat_head[:, None]).astype(BF16)
    return sel_k, sel_q, ones_k, ones_q, head_sum


def _kv_kernel(x_ref, g_ref, wk_ref, wvT_ref, onesv_ref, wf_ref, wfT_ref, bfr_ref, bfc_ref, selk_ref, onesk_ref,
               hsum_ref, ka_ref, vT_ref, c2r_ref, kn_ref, car_r, car_c):
    t = pl.program_id(1)
    tk = x_ref.shape[1]

    @pl.when(t == 0)
    def _():
        car_r[...] = jnp.zeros_like(car_r)
        car_c[...] = jnp.zeros_like(car_c)

    xn = _rms(x_ref[0], g_ref[...]).astype(BF16)
    vT = lax.dot_general(wvT_ref[...], xn, NT_DIMS, preferred_element_type=F32) + onesv_ref[...]
    vT_ref[0, 0] = vT.astype(BF16)
    kr = jnp.dot(xn, wk_ref[...], preferred_element_type=F32)

    kss = jnp.dot((kr * kr).astype(BF16), hsum_ref[...], preferred_element_type=F32)
    kn = jnp.sqrt(jnp.max(kss, axis=0, keepdims=True))
    kn_ref[0, 0] = jnp.broadcast_to(kn, (SUBLANES, LANES))

    ri = lax.broadcasted_iota(jnp.int32, (tk, tk), 0)
    ci = lax.broadcasted_iota(jnp.int32, (tk, tk), 1)
    upper = (ri <= ci).astype(BF16)
    lower = (ci <= ri).astype(BF16)

    lf_c = _log_sigmoid(jnp.dot(xn, wf_ref[...], preferred_element_type=F32) + bfr_ref[...])
    cs3 = jnp.dot(lower, jnp.concatenate(_split3(lf_c), axis=1), preferred_element_type=F32)
    cs_c = car_c[...] + (cs3[:, :LANES] + cs3[:, LANES:2 * LANES] + cs3[:, 2 * LANES:])
    car_c[...] = cs_c[tk - 1:tk, :]
    p0, p1, p2 = (p.astype(F32) for p in _split3(cs_c * LOG2E))
    lane = lax.broadcasted_iota(jnp.int32, (tk, LANES), 1)
    groups = []
    for pair in range(N_HEADS // 2):
        two = kr[:, pair * LANES:(pair + 1) * LANES]
        groups.append(jnp.where(lane < HEAD_DIM, two, 0.0))
        groups.append(jnp.where(lane < HEAD_DIM, pltpu.roll(two, HEAD_DIM, 1), 0.0))
    kf = jnp.concatenate(groups, axis=1)
    packed = jnp.where(lane < N_HEADS, p0,
                       jnp.where(lane < 2 * N_HEADS, pltpu.roll(p1, N_HEADS, 1),
                                 jnp.where(lane < 3 * N_HEADS, pltpu.roll(p2, 2 * N_HEADS, 1), 0.0)))
    bias = jnp.dot(packed.astype(BF16), selk_ref[...], preferred_element_type=F32)
    ka_ref[0] = (kf + bias + onesk_ref[...]).astype(BF16)

    lf_r = _log_sigmoid(lax.dot_general(wfT_ref[...], xn, NT_DIMS, preferred_element_type=F32) + bfc_ref[...])
    cs3 = jnp.dot(jnp.concatenate(_split3(lf_r), axis=0), upper, preferred_element_type=F32)
    cs_r = car_r[...] + (cs3[:LANES] + cs3[LANES:2 * LANES] + cs3[2 * LANES:])
    car_r[...] = cs_r[:, tk - 1:tk]
    c2r_ref[0] = cs_r * LOG2E


def _kv_proj(x, g, w_kvf, b_f, consts, *, tk):
    bsz, s, d = x.shape
    da = N_HEADS * HEAD_DIM
    hg = N_HEADS * HEAD_GROUP
    dv = N_HEADS * V_GROUP
    nk = s // tk
    sel_k, _, ones_k, _, head_sum = consts
    wk = w_kvf[:, :da].astype(BF16)
    wv = w_kvf[:, da:2 * da].reshape(d, N_HEADS, HEAD_DIM)
    wvT = jnp.pad(wv, ((0, 0), (0, 0), (0, V_GROUP - HEAD_DIM))).reshape(d, dv).T.astype(BF16)
    ones_v = (jnp.arange(dv) % V_GROUP == HEAD_DIM).astype(F32).reshape(dv, 1)
    wf = jnp.pad(w_kvf[:, 2 * da:], ((0, 0), (0, LANES - N_HEADS))).astype(BF16)
    bfr = jnp.pad(b_f, (0, LANES - N_HEADS)).reshape(1, LANES)
    const = lambda *shape: _const_spec(shape, 2)
    return pl.pallas_call(
        _kv_kernel,
        grid=(bsz, nk),
        in_specs=[
            pl.BlockSpec((1, tk, d), lambda b, t: (b, t, 0)),
            const(1, d), const(d, da), const(dv, d), const(dv, 1), const(d, LANES), const(LANES, d),
            const(1, LANES), const(LANES, 1), const(LANES, hg), const(1, hg), const(da, LANES),
        ],
        out_specs=[
            pl.BlockSpec((1, tk, hg), lambda b, t: (b, t, 0)),
            pl.BlockSpec((1, 1, dv, tk), lambda b, t: (b, t, 0, 0)),
            pl.BlockSpec((1, LANES, tk), lambda b, t: (b, 0, t)),
            pl.BlockSpec((1, 1, SUBLANES, LANES), lambda b, t: (b, t, 0, 0)),
        ],
        out_shape=[
            jax.ShapeDtypeStruct((bsz, s, hg), BF16),
            jax.ShapeDtypeStruct((bsz, nk, dv, tk), BF16),
            jax.ShapeDtypeStruct((bsz, LANES, s), F32),
            jax.ShapeDtypeStruct((bsz, nk, SUBLANES, LANES), F32),
        ],
        scratch_shapes=[pltpu.VMEM((LANES, 1), F32), pltpu.VMEM((1, LANES), F32)],
        compiler_params=pltpu.CompilerParams(dimension_semantics=("parallel", "arbitrary")),
        name="kv_proj",
    )(x, g.reshape(1, d), wk, wvT, ones_v, wf, wf.T, bfr, bfr.reshape(LANES, 1), sel_k, ones_k, head_sum)


def _q_kernel(x_ref, g_ref, wqT_ref, wg_ref, c2r_ref, selq_ref, onesq_ref, hsumT_ref, qa_ref, gate_ref, qn_ref):
    xn = _rms(x_ref[0], g_ref[...]).astype(BF16)
    gate_ref[0] = jax.nn.sigmoid(jnp.dot(xn, wg_ref[...], preferred_element_type=F32)).astype(BF16)
    qr = lax.dot_general(wqT_ref[...], xn, NT_DIMS, preferred_element_type=F32) * (LOG2E * HEAD_DIM ** -0.5)

    qss = jnp.dot(hsumT_ref[...], (qr * qr).astype(BF16), preferred_element_type=F32)
    qn = jnp.sqrt(jnp.max(qss, axis=1, keepdims=True))
    qn_ref[0, 0] = jnp.broadcast_to(qn, (LANES, LANES))

    tq = c2r_ref.shape[2]
    gap = jnp.zeros((HEAD_GROUP - HEAD_DIM, tq), F32)
    qf = jnp.concatenate([blk for h in range(N_HEADS) for blk in (qr[h * HEAD_DIM:(h + 1) * HEAD_DIM], gap)], axis=0)
    packed = jnp.concatenate([p[:N_HEADS] for p in _split3(c2r_ref[0])]
                             + [jnp.zeros((LANES - 3 * N_HEADS, tq), BF16)], axis=0)
    bias = jnp.dot(selq_ref[...], packed, preferred_element_type=F32)
    qa_ref[0] = (qf + bias + onesq_ref[...]).astype(BF16)


def _q_proj(x, g, w_qg, c2r, consts, *, tq):
    bsz, s, d = x.shape
    da = N_HEADS * HEAD_DIM
    hg = N_HEADS * HEAD_GROUP
    nq = s // tq
    _, sel_q, _, ones_q, head_sum = consts
    wqT = w_qg[:, :da].T.astype(BF16)
    const = lambda *shape: _const_spec(shape, 2)
    return pl.pallas_call(
        _q_kernel,
        grid=(bsz, nq),
        in_specs=[
            pl.BlockSpec((1, tq, d), lambda b, t: (b, t, 0)),
            const(1, d), const(da, d), const(d, da),
            pl.BlockSpec((1, LANES, tq), lambda b, t: (b, 0, t)),
            const(hg, LANES), const(hg, 1), const(LANES, da),
        ],
        out_specs=[
            pl.BlockSpec((1, hg, tq), lambda b, t: (b, 0, t)),
            pl.BlockSpec((1, tq, da), lambda b, t: (b, t, 0)),
            pl.BlockSpec((1, 1, LANES, LANES), lambda b, t: (b, t, 0, 0)),
        ],
        out_shape=[
            jax.ShapeDtypeStruct((bsz, hg, s), BF16),
            jax.ShapeDtypeStruct((bsz, s, da), BF16),
            jax.ShapeDtypeStruct((bsz, nq, LANES, LANES), F32),
        ],
        compiler_params=pltpu.CompilerParams(dimension_semantics=("parallel", "parallel")),
        name="q_proj",
    )(x, g.reshape(1, d), wqT, w_qg[:, da:].astype(BF16), c2r, sel_q, ones_q, head_sum.T)


def _attn_kernel(jlo_ref, qa_ref, qan_ref, ka_ref, vT_ref, gate_ref, o_ref, ready, m0, m1, acc0, acc1,
                 sa0, sa1, sb0, sb1, sc0, sc1, pa0, pa1, pb0, pb1):
    b = pl.program_id(0)
    hp = pl.program_id(1)
    i = pl.program_id(2)
    nq = pl.num_programs(2)
    tq = qa_ref.shape[2]
    tk = vT_ref.shape[3]
    base = (b * pl.num_programs(1) + hp) * nq
    j_lo = jlo_ref[base + i]
    j_lo_next = jlo_ref[base + jnp.minimum(i + 1, nq - 1)]
    m_refs, acc_refs = (m0, m1), (acc0, acc1)
    s_a, s_b, s_c = (sa0, sa1), (sb0, sb1), (sc0, sc1)
    p_a, p_b = (pa0, pa1), (pb0, pb1)

    def scores(q_ref, j, h):
        k_blk = ka_ref[0, pl.ds(pl.multiple_of(j * tk, tk), tk), h * HEAD_GROUP:(h + 1) * HEAD_GROUP]
        qT = q_ref[0, h * HEAD_GROUP:(h + 1) * HEAD_GROUP, :]
        return jnp.dot(k_blk, qT, preferred_element_type=F32)

    def values(j, h, p):
        vT = vT_ref[0, j, h * V_GROUP:(h + 1) * V_GROUP, :]
        return jnp.dot(vT, p, preferred_element_type=F32)

    def softmax_tile(h, s, masked):
        if masked:
            kv_pos = lax.broadcasted_iota(jnp.int32, (tk, tq), 0)
            q_pos = lax.broadcasted_iota(jnp.int32, (tk, tq), 1)
            s = jnp.where(kv_pos <= q_pos, s, NEG_BIG)
        m_old = m_refs[h][...]
        m_new = jnp.maximum(m_old, jnp.max(s, axis=0, keepdims=True))
        m_refs[h][...] = m_new
        return jnp.exp2(m_old - m_new), jnp.exp2(s - m_new).astype(BF16)

    def step(s_src, s_dst, p_prev, p_dst, j):
        for h in range(2):
            s_dst[h][...] = scores(qa_ref, j + 1, h)
        pv = None if p_prev is None else [values(j - 1, h, p_prev[h][...]) for h in range(2)]
        for h in range(2):
            alpha, p = softmax_tile(h, s_src[h][...], masked=False)
            p_dst[h][...] = p
            if pv is not None:
                acc_refs[h][...] = alpha * (acc_refs[h][...] + pv[h])

    def finish(s_src, p_prev, prefetch):
        if prefetch:
            for h in range(2):
                s_c[h][...] = scores(qan_ref, j_lo_next, h)
            ready[0] = (i + 1 < nq).astype(jnp.int32)
        else:
            ready[0] = jnp.int32(0)
        pv = None if p_prev is None else [values(i - 1, h, p_prev[h][...]) for h in range(2)]
        outs = []
        for h in range(2):
            alpha, p = softmax_tile(h, s_src[h][...], masked=True)
            acc = values(i, h, p)
            if pv is not None:
                acc = acc + alpha * (acc_refs[h][...] + pv[h])
            outs.append(acc[0:HEAD_DIM, :] / acc[HEAD_DIM:HEAD_DIM + 1, :])
        oT = jnp.concatenate(outs, axis=0)
        o_ref[0] = (oT.T * gate_ref[0].astype(F32)).astype(BF16)

    @pl.when(i == 0)
    def _():
        ready[0] = jnp.int32(0)

    @pl.when(ready[0] == 0)
    def _():
        for h in range(2):
            s_c[h][...] = scores(qa_ref, j_lo, h)

    for h in range(2):
        m_refs[h][...] = jnp.full((1, tq), NEG_BIG, F32)
        acc_refs[h][...] = jnp.zeros((V_GROUP, tq), F32)

    n_full = i - j_lo

    @pl.when(n_full == 0)
    def _():
        finish(s_c, None, prefetch=False)

    @pl.when(n_full > 0)
    def _():
        step(s_c, s_a, None, p_a, j_lo)
        rest = n_full - 1

        def body(n, carry):
            j = j_lo + 1 + 2 * n
            step(s_a, s_b, p_a, p_b, j)
            step(s_b, s_a, p_b, p_a, j + 1)
            return carry

        lax.fori_loop(0, rest // 2, body, 0)

        @pl.when(rest % 2 == 1)
        def _():
            step(s_a, s_b, p_a, p_b, i - 1)
            finish(s_b, p_b, prefetch=True)

        @pl.when(rest % 2 == 0)
        def _():
            finish(s_a, p_a, prefetch=True)


def _permute_heads(w_kvf, b_f, w_qg, w_o):
    da = N_HEADS * HEAD_DIM
    perm = jnp.argsort(b_f)
    cols = (perm[:, None] * HEAD_DIM + jnp.arange(HEAD_DIM)[None, :]).reshape(-1)
    w_kvf_p = jnp.concatenate([w_kvf[:, :da][:, cols], w_kvf[:, da:2 * da][:, cols], w_kvf[:, 2 * da:][:, perm]],
                              axis=1)
    w_qg_p = jnp.concatenate([w_qg[:, :da][:, cols], w_qg[:, da:][:, cols]], axis=1)
    return w_kvf_p, b_f[perm], w_qg_p, w_o[cols, :]


def _skip_table(qn, kn, c2r, tq):
    bsz, nq = qn.shape[0], qn.shape[1]
    qn = qn[:, :, :N_HEADS, 0].transpose(0, 2, 1) * NORM_SLACK
    kn = kn[:, :, 0, :N_HEADS].transpose(0, 2, 1) * NORM_SLACK
    c2 = c2r[:, :N_HEADS, :]
    c_first = c2[:, :, ::tq]
    c_last = c2[:, :, tq - 1::tq]
    upper = qn[..., :, None] * kn[..., None, :] + c_first[..., :, None] - c_last[..., None, :]
    lower = -(qn * kn)[..., :, None]
    skip = (upper - lower) < -SKIP_GAP_LOG2
    nk = skip.shape[-1]
    jlo = jnp.min(jnp.where(skip, nk, jnp.arange(nk, dtype=jnp.int32)), axis=-1)
    jlo = jnp.minimum(jlo, jnp.arange(nq, dtype=jnp.int32))
    jlo = jnp.min(jlo.reshape(bsz, N_HEADS // 2, 2, nq), axis=2)
    return jlo.reshape(-1).astype(jnp.int32)


def _attention(jlo, qa, ka, vT, gate, *, tq):
    bsz, hg, s = qa.shape
    nk, tk = vT.shape[1], vT.shape[3]
    da = N_HEADS * HEAD_DIM
    assert tq == tk
    pair = 2 * HEAD_GROUP
    hw = 2 * HEAD_DIM
    nq = s // tq
    grid_spec = pltpu.PrefetchScalarGridSpec(
        num_scalar_prefetch=1,
        grid=(bsz, N_HEADS // 2, nq),
        in_specs=[
            pl.BlockSpec((1, pair, tq), lambda b, hp, i, jlo: (b, hp, i)),
            pl.BlockSpec((1, pair, tq), lambda b, hp, i, jlo: (b, hp, jnp.minimum(i + 1, nq - 1))),
            pl.BlockSpec((1, s, pair), lambda b, hp, i, jlo: (b, 0, hp)),
            pl.BlockSpec((1, nk, 2 * V_GROUP, tk), lambda b, hp, i, jlo: (b, 0, hp, 0)),
            pl.BlockSpec((1, tq, hw), lambda b, hp, i, jlo: (b, i, hp)),
        ],
        out_specs=pl.BlockSpec((1, tq, hw), lambda b, hp, i, jlo: (b, i, hp)),
        scratch_shapes=([pltpu.SMEM((1,), jnp.int32)]
                        + [pltpu.VMEM((1, tq), F32)] * 2 + [pltpu.VMEM((V_GROUP, tq), F32)] * 2
                        + [pltpu.VMEM((tk, tq), F32)] * 6 + [pltpu.VMEM((tk, tq), BF16)] * 4),
    )
    return pl.pallas_call(
        _attn_kernel,
        grid_spec=grid_spec,
        out_shape=jax.ShapeDtypeStruct((bsz, s, da), BF16),
        compiler_params=pltpu.CompilerParams(dimension_semantics=("parallel", "parallel", "arbitrary")),
        name="fox_attention",
    )(jlo, qa, qa, ka, vT, gate)


def _pack_bf16_pairs(x):
    half = x.shape[1] // 2
    lo = lax.bitcast_convert_type(x[:, :half].astype(BF16).astype(F32), jnp.uint32)
    hi = lax.bitcast_convert_type(x[:, half:].astype(BF16).astype(F32), jnp.uint32)
    word = (lo >> 16) | (hi & jnp.uint32(0xFFFF0000))
    return lax.bitcast_convert_type(word, jnp.int32)


def _unpack_bf16_pairs(w):
    u = lax.bitcast_convert_type(w, jnp.uint32)
    lo = lax.bitcast_convert_type(u << 16, F32)
    hi = lax.bitcast_convert_type(u & jnp.uint32(0xFFFF0000), F32)
    return jnp.concatenate([lo, hi], axis=1)


def _store_planes(ref, words):
    q = words.shape[1] // 2
    ref[0] = words[:, :q]
    ref[1] = words[:, q:]


def _load_planes(ref, base=0):
    return jnp.concatenate([ref[base], ref[base + 1]], axis=1)


def _router_kernel(x_ref, o_ref, wo_ref, g_ref, rtT_ref, x2_ref, hnp_ref, route_ref):
    tm = x_ref.shape[0]
    x2 = x_ref[...] + jnp.dot(o_ref[...], wo_ref[...], preferred_element_type=F32)
    x2_ref[...] = x2
    hn = _rms(x2, g_ref[...])
    _store_planes(hnp_ref, _pack_bf16_pairs(hn))
    h0, h1, _ = _split3(hn)
    r0, r1, _ = _split3(rtT_ref[...])
    lt = lax.dot_general(jnp.concatenate([r0, r1, r0], axis=1), jnp.concatenate([h0, h0, h1], axis=1), NT_DIMS,
                         preferred_element_type=F32)[:N_EXPERTS]
    row = lax.broadcasted_iota(jnp.int32, (N_EXPERTS, tm), 0)
    m1 = jnp.max(lt, axis=0, keepdims=True)
    i1 = jnp.min(jnp.where(lt == m1, row, N_EXPERTS), axis=0, keepdims=True)
    rest = jnp.where(row == i1, -jnp.inf, lt)
    m2 = jnp.max(rest, axis=0, keepdims=True)
    i2 = jnp.min(jnp.where(rest == m2, row, N_EXPERTS), axis=0, keepdims=True)
    e2 = jnp.exp(m2 - m1)
    w1 = 1.0 / (1.0 + e2)
    w2 = e2 / (1.0 + e2)
    route_ref[...] = jnp.where(row == 0, i1.astype(F32), jnp.where(row == 1, i2.astype(F32),
                               jnp.where(row == 2, w1, jnp.where(row == 3, w2, 0.0))))


def _router(x2d, o2d, w_o, g, router, *, tm):
    t, d = x2d.shape
    da = o2d.shape[1]
    ne = router.shape[1]
    rtT = jnp.pad(router.T, ((0, 2 * SUBLANES - ne), (0, 0)))
    return pl.pallas_call(
        _router_kernel,
        grid=(t // tm,),
        in_specs=[
            pl.BlockSpec((tm, d), lambda i: (i, 0)),
            pl.BlockSpec((tm, da), lambda i: (i, 0)),
            pl.BlockSpec((da, d), lambda i: (0, 0)),
            pl.BlockSpec((1, d), lambda i: (0, 0)),
            pl.BlockSpec((2 * SUBLANES, d), lambda i: (0, 0)),
        ],
        out_specs=[pl.BlockSpec((tm, d), lambda i: (i, 0)), pl.BlockSpec((2, tm, d // 4), lambda i: (0, i, 0)),
                   pl.BlockSpec((ne, tm), lambda i: (0, i))],
        out_shape=[jax.ShapeDtypeStruct((t, d), F32), jax.ShapeDtypeStruct((2, t, d // 4), jnp.int32),
                   jax.ShapeDtypeStruct((ne, t), F32)],
        compiler_params=pltpu.CompilerParams(dimension_semantics=("parallel",)),
        name="attn_out_moe_router",
    )(x2d, o2d, w_o.astype(BF16), g.reshape(1, d), rtT)


def _routing_tables(route, ts):
    ne = N_EXPERTS
    t = route.shape[1]
    e_pair = route[0:2].astype(jnp.int32).reshape(-1)
    n_tiles = (2 * t) // ts + ne
    n_slots = n_tiles * ts
    onehot = (e_pair[:, None] == jnp.arange(ne, dtype=jnp.int32)[None, :]).astype(jnp.int32)
    csum = jnp.cumsum(onehot, axis=0)
    rank = jnp.sum(onehot * csum, axis=1) - 1
    tiles = (csum[-1] + ts - 1) // ts
    tile_end = jnp.cumsum(tiles)
    tile_start = tile_end - tiles
    slot = (jnp.sum(onehot * tile_start[None, :], axis=1) * ts + rank).astype(jnp.int32)
    n_used = tile_end[-1]
    tile_ids = jnp.arange(n_tiles, dtype=jnp.int32)
    tile_expert = jnp.sum((tile_ids[:, None] >= tile_end[None, :]).astype(jnp.int32), axis=1)
    last_expert = jnp.sum(((n_used - 1) >= tile_end).astype(jnp.int32))
    tile_expert = jnp.minimum(tile_expert, last_expert).astype(jnp.int32)
    plane_rows = slot.reshape(2, 1, t) + (jnp.arange(2, dtype=jnp.int32) * n_slots).reshape(1, 2, 1)
    return plane_rows.reshape(-1), tile_expert, n_used.reshape(1).astype(jnp.int32), n_slots


def _sc_mesh():
    return plsc.VectorSubcoreMesh(core_axis_name="core", subcore_axis_name="subcore")


def _scatter_rows(table, idx, n_out):
    m = idx.shape[0]
    n, d = table.shape
    assert m % SC_ROW_WINDOW == 0 and n % SC_ROW_WINDOW == 0
    n_src_blocks = n // SC_ROW_WINDOW

    @pl.kernel(out_type=jax.ShapeDtypeStruct((n_out, d), table.dtype), mesh=_sc_mesh(), scratch_types=[])
    def scatter_kernel(x_hbm, i_hbm, o_hbm):
        def body(x_vmem, i_vmem):
            pltpu.sync_copy(x_vmem, o_hbm.at[i_vmem.at[0]])

        pltpu.emit_pipeline(
            body,
            grid=(m // SC_ROW_WINDOW,),
            in_specs=[pl.BlockSpec((SC_ROW_WINDOW, d), index_map=lambda i: (i % n_src_blocks, 0)),
                      pl.BlockSpec((1, SC_ROW_WINDOW), index_map=lambda i: (0, i))],
            out_specs=[],
            core_axis_name=("core", "subcore"),
            dimension_semantics=(pltpu.PARALLEL,),
        )(x_hbm, i_hbm)

    return scatter_kernel(table, idx.reshape(1, m))


def _gather_rows(table, idx):
    m = idx.shape[0]
    d = table.shape[1]
    assert m % SC_ROW_WINDOW == 0

    @pl.kernel(out_type=jax.ShapeDtypeStruct((m, d), table.dtype), mesh=_sc_mesh())
    def gather_kernel(x_hbm, i_hbm, o_hbm):
        def body(i_vmem, o_vmem):
            pltpu.sync_copy(x_hbm.at[i_vmem.at[0]], o_vmem)

        pltpu.emit_pipeline(
            body,
            grid=(m // SC_ROW_WINDOW,),
            in_specs=[pl.BlockSpec((1, SC_ROW_WINDOW), index_map=lambda i: (0, i))],
            out_specs=[pl.BlockSpec((SC_ROW_WINDOW, d), index_map=lambda i: (i, 0))],
            core_axis_name=("core", "subcore"),
            dimension_semantics=(pltpu.PARALLEL,),
        )(i_hbm, o_hbm)

    return gather_kernel(table, idx.reshape(1, m))


def _expert_kernel(nk, te_ref, nu_ref, xs_ref, wg_ref, wu_ref, wd_ref, ys_ref, x_sc, acc_ref):
    i = pl.program_id(0)
    k = pl.program_id(1)
    used = i < nu_ref[0]

    def first():
        x_sc[...] = _unpack_bf16_pairs(_load_planes(xs_ref)).astype(BF16)

    def last(total):
        _store_planes(ys_ref, _pack_bf16_pairs(total))

    @pl.when(used)
    def _():
        _chunked_swiglu(k, nk, lambda: _swiglu_chunk(x_sc[...], wg_ref.at[0], wu_ref.at[0], wd_ref.at[0]), acc_ref,
                        first, last)

    @pl.when(jnp.logical_not(used) & (k == nk - 1))
    def _():
        ys_ref[...] = jnp.zeros_like(ys_ref)


def _experts(xs, tile_expert, n_used, w_gu, w_down, *, ts, tf):
    _, n_slots, quarter = xs.shape
    d = 4 * quarter
    dexp = w_down.shape[1]
    nk = dexp // tf
    w_gu = w_gu.astype(BF16)
    grid_spec = pltpu.PrefetchScalarGridSpec(
        num_scalar_prefetch=2,
        grid=(n_slots // ts, nk),
        in_specs=[
            pl.BlockSpec((2, ts, quarter), lambda i, k, te, nu: (0, i, 0)),
            pl.BlockSpec((1, d, tf), lambda i, k, te, nu: (te[i], 0, k)),
            pl.BlockSpec((1, d, tf), lambda i, k, te, nu: (te[i], 0, nk + k)),
            pl.BlockSpec((1, tf, d), lambda i, k, te, nu: (te[i], k, 0)),
        ],
        out_specs=pl.BlockSpec((2, ts, quarter), lambda i, k, te, nu: (0, i, 0)),
        scratch_shapes=[pltpu.VMEM((ts, d), BF16), pltpu.VMEM((ts, d), F32)],
    )
    return pl.pallas_call(
        functools.partial(_expert_kernel, nk),
        grid_spec=grid_spec,
        out_shape=jax.ShapeDtypeStruct((2, n_slots, quarter), jnp.int32),
        compiler_params=pltpu.CompilerParams(dimension_semantics=("arbitrary", "arbitrary")),
        name="moe_experts",
    )(tile_expert, n_used, xs, w_gu, w_gu, w_down.astype(BF16))


def _combine_kernel(x_ref, y_ref, rt_ref, gf_ref, o_ref):
    rt = rt_ref[...]
    moe = (rt[:, 2:3] * _unpack_bf16_pairs(_load_planes(y_ref, 0))
           + rt[:, 3:4] * _unpack_bf16_pairs(_load_planes(y_ref, 2)))
    o_ref[...] = _rms(x_ref[...] + moe, gf_ref[...])


def _combine(x2d, y4, route_t, g_final, *, tm):
    t, d = x2d.shape
    ne = route_t.shape[1]
    return pl.pallas_call(
        _combine_kernel,
        grid=(t // tm,),
        in_specs=[
            pl.BlockSpec((tm, d), lambda i: (i, 0)),
            pl.BlockSpec((4, tm, d // 4), lambda i: (0, i, 0)),
            pl.BlockSpec((tm, ne), lambda i: (i, 0)),
            pl.BlockSpec((1, d), lambda i: (0, 0)),
        ],
        out_specs=pl.BlockSpec((tm, d), lambda i: (i, 0)),
        out_shape=jax.ShapeDtypeStruct((t, d), F32),
        compiler_params=pltpu.CompilerParams(dimension_semantics=("parallel",)),
        name="moe_combine",
    )(x2d, y4, route_t, g_final.reshape(1, d))


def _attn_out_moe(x1, o2d, w_o, g, router, w_gu, w_down, g_final, *, tm, ts, tf):
    t, d = x1.shape
    q = d // 4
    x2d, hn_planes, route = _router(x1, o2d, w_o, g, router, tm=tm)
    plane_rows, tile_expert, n_used, n_slots = _routing_tables(route, ts)
    xs = _scatter_rows(hn_planes.reshape(2 * t, q), plane_rows, 2 * n_slots)
    ys = _experts(xs.reshape(2, n_slots, q), tile_expert, n_used, w_gu, w_down, ts=ts, tf=tf)
    y4 = _gather_rows(ys.reshape(2 * n_slots, q), plane_rows)
    return _combine(x2d, y4.reshape(4, t, q), route.T, g_final, tm=tm)


def _tile(n, pref):
    t = min(n, pref)
    assert n % t == 0
    return t


def kernel(x, norm_g, lru_w_in, lru_conv_w, lru_conv_b, lru_gate_w, lru_gate_b, lru_lambda, lru_w_out,
           kv_norm_g, w_kvf, b_f, fox_w_qg, fox_w_o, mlp_w_gu, mlp_w_down, moe_router, moe_w_gu,
           moe_w_down, final_norm_g):
    bsz, s, d = x.shape
    t = bsz * s
    tm = _tile(t, 512)
    ts = _tile(s, 512)

    x = _lru_mlp(x, norm_g[0, 0], lru_w_in[0], lru_conv_w[0], lru_conv_b[0], lru_gate_w[0], lru_gate_b[0],
                 lru_lambda[0], lru_w_out[0], norm_g[0, 1], mlp_w_gu[0], mlp_w_down[0], tm=ts)

    consts = _head_selectors()
    w_kvf_p, b_f_p, w_qg_p, w_o_p = _permute_heads(w_kvf, b_f, fox_w_qg[0], fox_w_o[0])
    ka, vT, c2r, kn = _kv_proj(x, kv_norm_g, w_kvf_p, b_f_p, consts, tk=ts)
    qa, gate, qn = _q_proj(x, norm_g[1, 0], w_qg_p, c2r, consts, tq=ts)
    jlo = _skip_table(qn, kn, c2r, ts)
    o = _attention(jlo, qa, ka, vT, gate, tq=ts)
    dexp = moe_w_down.shape[2]
    out = _attn_out_moe(x.reshape(t, d), o.reshape(t, -1), w_o_p, norm_g[1, 1], moe_router[0], moe_w_gu[0],
                        moe_w_down[0], final_norm_g, tm=tm, ts=tm, tf=dexp // 2)
    return out.reshape(bsz, s, d)
```

```python
import functools
import math

import jax
import jax.numpy as jnp
from jax import lax
from jax.experimental import pallas as pl
from jax.experimental.pallas import tpu as pltpu
from jax.experimental.pallas import tpu_sc as plsc

EPS = 1e-6
LRU_C = 8.0
N_LRU_BLOCKS = 8
CONV_W = 4
N_HEADS = 16
HEAD_DIM = 64
N_EXPERTS = 8
LANES = 128
SUBLANES = 8
HEAD_GROUP = 128
BIAS_Q = HEAD_DIM
BIAS_K = HEAD_DIM + 3
V_GROUP = 80
NEG_BIG = -1e30
LOG2E = math.log2(math.e)
SKIP_GAP_LOG2 = 152.0
NORM_SLACK = 1.01
SC_ROW_WINDOW = 128

F32 = jnp.float32
BF16 = jnp.bfloat16
NT_DIMS = (((1,), (1,)), ((), ()))


def _rms(x, g):
    ms = jnp.mean(x * x, axis=-1, keepdims=True)
    return x * lax.rsqrt(ms + EPS) * g


def _log_sigmoid(x):
    return jnp.minimum(x, 0.0) - jnp.log1p(jnp.exp(-jnp.abs(x)))


def _gelu_tanh(x):
    c = 0.7978845608028654
    return 0.5 * x * (1.0 + jnp.tanh(c * (x + 0.044715 * (x * x * x))))


def _silu(x):
    return x * jax.nn.sigmoid(x)


def _split3(x):
    p0 = x.astype(BF16)
    r1 = x - p0.astype(F32)
    p1 = r1.astype(BF16)
    p2 = (r1 - p1.astype(F32)).astype(BF16)
    return p0, p1, p2


def _const_spec(shape, n_grid):
    zeros = (0,) * len(shape)
    if n_grid == 1:
        return pl.BlockSpec(shape, lambda i: zeros)
    if n_grid == 2:
        return pl.BlockSpec(shape, lambda i, j: zeros)
    return pl.BlockSpec(shape, lambda i, j, k: zeros)


def _lru_in_proj(x, g_ref, win_ref):
    d = x.shape[1]
    xn = _rms(x, g_ref[...]).astype(BF16)
    xg = jnp.dot(xn, win_ref[...], preferred_element_type=F32)
    return xg[:, :d], xg[:, d:]


def _lru_conv_gates(xb, cw_ref, cb_ref, gw_ref, xbuf):
    tt, d = xb.shape
    bw = d // N_LRU_BLOCKS
    xbuf[SUBLANES:SUBLANES + tt, :] = xb
    cw = cw_ref[...]
    xc = cb_ref[...] + cw[CONV_W - 1:CONV_W, :] * xb
    for j in range(CONV_W - 1):
        xc = xc + cw[j:j + 1, :] * xbuf[pl.ds(SUBLANES - (CONV_W - 1) + j, tt), :]
    xbuf[0:SUBLANES, :] = xbuf[tt:tt + SUBLANES, :]

    xcb = xc.astype(BF16)
    rl, il = [], []
    for n in range(N_LRU_BLOCKS):
        gl = jnp.dot(xcb[:, n * bw:(n + 1) * bw], gw_ref[n], preferred_element_type=F32)
        rl.append(gl[:, :bw])
        il.append(gl[:, bw:])
    return xc, jnp.concatenate(rl, axis=1), jnp.concatenate(il, axis=1)


def _lru_recurrence(xc, r_logit, i_logit, gbr, gb_ref, lam_ref, hcar):
    tt = xc.shape[0]
    gb = gb_ref[...]
    r = jax.nn.sigmoid(r_logit + gb[0:1, :])
    i = jax.nn.sigmoid(i_logit + gb[1:2, :])
    log_a = LRU_C * r * _log_sigmoid(lam_ref[...])
    a = jnp.exp(log_a)
    b = jnp.sqrt((1.0 - a) * (1.0 + a)) * (i * xc)

    row = lax.broadcasted_iota(jnp.int32, (tt, 1), 0)
    sh = 1
    while sh < tt:
        if sh < SUBLANES:
            keep = row >= sh
            a_sh = jnp.where(keep, pltpu.roll(a, sh, 0), 1.0)
            b_sh = jnp.where(keep, pltpu.roll(b, sh, 0), 0.0)
            b = a * b_sh + b
            a = a * a_sh
        else:
            b = jnp.concatenate([b[:sh], a[sh:] * b[:tt - sh] + b[sh:]], axis=0)
            a = jnp.concatenate([a[:sh], a[sh:] * a[:tt - sh]], axis=0)
        sh *= 2
    h = b + a * hcar[...]
    hcar[...] = h[tt - 1:tt, :]
    return (_gelu_tanh(gbr) * h).astype(BF16)


def _lru_mlp_kernel(x_ref, g0_ref, win_ref, cw_ref, cb_ref, gw_ref, gb_ref, lam_ref, wout_ref,
                    g1_ref, wg_ref, wu_ref, wd_ref, o_ref, xbuf, hcar, x1_sc, hn_sc, acc_ref):
    t = pl.program_id(1)
    k = pl.program_id(2)
    half = x_ref.shape[1]
    tf = wg_ref.shape[1]
    mid = (tf // (2 * LANES)) * LANES
    slot = t % 2
    prev = 1 - slot

    @pl.when((t == 0) & (k == 0))
    def _():
        xbuf[0:SUBLANES, :] = jnp.zeros((SUBLANES, xbuf.shape[1]), F32)
        hcar[...] = jnp.zeros_like(hcar)
        x1_sc[...] = jnp.zeros_like(x1_sc)
        hn_sc[...] = jnp.zeros_like(hn_sc)

    def gate_up(hn, lo, hi):
        return (jnp.dot(hn, wg_ref[:, lo:hi], preferred_element_type=F32),
                jnp.dot(hn, wu_ref[:, lo:hi], preferred_element_type=F32))

    def both():
        x = x_ref[0]
        hn = hn_sc[prev]
        xb, gbr = _lru_in_proj(x, g0_ref, win_ref)
        g_a, u_a = gate_up(hn, 0, mid)
        xc, r_logit, i_logit = _lru_conv_gates(xb, cw_ref, cb_ref, gw_ref, xbuf)
        g_b, u_b = gate_up(hn, mid, tf)
        h = jnp.concatenate([(_silu(g_a) * u_a).astype(BF16), (_silu(g_b) * u_b).astype(BF16)], axis=1)
        y_mlp = jnp.dot(h, wd_ref[...], preferred_element_type=F32)
        y_lru = _lru_recurrence(xc, r_logit, i_logit, gbr, gb_ref, lam_ref, hcar)
        x1 = x + jnp.dot(y_lru, wout_ref[...], preferred_element_type=F32)
        rows = pl.ds(pl.multiple_of(k * half, half), half)
        x1_sc[slot, rows, :] = x1
        hn_sc[slot, rows, :] = _rms(x1, g1_ref[...]).astype(BF16)
        return y_mlp

    @pl.when(k == 0)
    def _():
        acc_ref[...] = both()

    @pl.when(k == 1)
    def _():
        y = both()
        o_ref[0] = x1_sc[prev] + acc_ref[...] + y


def _lru_mlp(x, g0, w_in, conv_w, conv_b, gate_w, gate_b, lam, w_out, g1, w_gu, w_down, *, tm):
    bsz, s, d = x.shape
    bw = d // N_LRU_BLOCKS
    dff = w_down.shape[0]
    tf = dff // 2
    half = tm // 2
    nt = s // tm
    n_half = s // half
    gw = jnp.concatenate([gate_w[0], gate_w[1]], axis=-1).astype(BF16)
    w_gu = w_gu.astype(BF16)

    def const(*shape):
        return pl.BlockSpec(shape, lambda b, t, k: (0,) * len(shape), pipeline_mode=pl.Buffered(1))

    return pl.pallas_call(
        _lru_mlp_kernel,
        grid=(bsz, nt + 1, 2),
        in_specs=[
            pl.BlockSpec((1, half, d), lambda b, t, k: (b, jnp.minimum(2 * t + k, n_half - 1), 0)),
            const(1, d), const(d, 2 * d), const(CONV_W, d), const(1, d),
            const(N_LRU_BLOCKS, bw, 2 * bw), const(2, d), const(1, d), const(d, d),
            const(1, d),
            pl.BlockSpec((d, tf), lambda b, t, k: (0, k)),
            pl.BlockSpec((d, tf), lambda b, t, k: (0, 2 + k)),
            pl.BlockSpec((tf, d), lambda b, t, k: (k, 0)),
        ],
        out_specs=pl.BlockSpec((1, tm, d), lambda b, t, k: (b, jnp.maximum(t - 1, 0), 0)),
        out_shape=jax.ShapeDtypeStruct((bsz, s, d), F32),
        scratch_shapes=[pltpu.VMEM((half + 2 * SUBLANES, d), F32), pltpu.VMEM((1, d), F32),
                        pltpu.VMEM((2, tm, d), F32), pltpu.VMEM((2, tm, d), BF16), pltpu.VMEM((tm, d), F32)],
        compiler_params=pltpu.CompilerParams(dimension_semantics=("parallel", "arbitrary", "arbitrary")),
        name="lru_mlp",
    )(x, g0.reshape(1, d), w_in.astype(BF16), conv_w, conv_b.reshape(1, d), gw, gate_b, lam.reshape(1, d),
      w_out.astype(BF16), g1.reshape(1, d), w_gu, w_gu, w_down.astype(BF16))


def _swiglu_chunk(x, wg, wu, wd):
    gate = jnp.dot(x, wg[...], preferred_element_type=F32)
    up = jnp.dot(x, wu[...], preferred_element_type=F32)
    h = (_silu(gate) * up).astype(BF16)
    return jnp.dot(h, wd[...], preferred_element_type=F32)


def _chunked_swiglu(k, nk, chunk, acc_ref, first, last):
    if nk == 1:
        first()
        last(chunk())
        return

    @pl.when(k == 0)
    def _():
        first()
        acc_ref[...] = chunk()

    if nk > 2:
        @pl.when((k > 0) & (k < nk - 1))
        def _():
            acc_ref[...] += chunk()

    @pl.when(k == nk - 1)
    def _():
        last(acc_ref[...] + chunk())


def _head_selectors():
    hg = N_HEADS * HEAD_GROUP
    head = jnp.arange(hg) // HEAD_GROUP
    lane = jnp.arange(hg) % HEAD_GROUP
    col = jnp.arange(LANES)
    def sel(base):
        return ((col[:, None] % N_HEADS == head[None, :]) & (col[:, None] < 3 * N_HEADS)
                & (lane[None, :] == base + col[:, None] // N_HEADS)).astype(BF16)
    sel_k = -sel(BIAS_K)
    sel_q = sel(BIAS_Q).T
    ones_k = ((lane >= BIAS_Q) & (lane < BIAS_Q + 3)).astype(F32).reshape(1, hg)
    ones_q = ((lane >= BIAS_K) & (lane < BIAS_K + 3)).astype(F32).reshape(hg, 1)
    feat_head = jnp.arange(N_HEADS * HEAD_DIM) // HEAD_DIM
    head_sum = (col[None, :] == feat_head[:, None]).astype(BF16)
    return sel_k, sel_q, ones_k, ones_q, head_sum


def _kv_kernel(x_ref, g_ref, wk_ref, wvT_ref, onesv_ref, wf_ref, wfT_ref, bfr_ref, bfc_ref, selk_ref, onesk_ref,
               hsum_ref, ka_ref, vT_ref, c2r_ref, kn_ref, car_r, car_c):
    t = pl.program_id(1)
    tk = x_ref.shape[1]

    @pl.when(t == 0)
    def _():
        car_r[...] = jnp.zeros_like(car_r)
        car_c[...] = jnp.zeros_like(car_c)

    xn = _rms(x_ref[0], g_ref[...]).astype(BF16)
    vT = lax.dot_general(wvT_ref[...], xn, NT_DIMS, preferred_element_type=F32) + onesv_ref[...]
    vT_ref[0, 0] = vT.astype(BF16)
    kr = jnp.dot(xn, wk_ref[...], preferred_element_type=F32)

    kss = jnp.dot((kr * kr).astype(BF16), hsum_ref[...], preferred_element_type=F32)
    kn = jnp.sqrt(jnp.max(kss, axis=0, keepdims=True))
    kn_ref[0, 0] = jnp.broadcast_to(kn, (SUBLANES, LANES))

    ri = lax.broadcasted_iota(jnp.int32, (tk, tk), 0)
    ci = lax.broadcasted_iota(jnp.int32, (tk, tk), 1)
    upper = (ri <= ci).astype(BF16)
    lower = (ci <= ri).astype(BF16)

    lf_c = _log_sigmoid(jnp.dot(xn, wf_ref[...], preferred_element_type=F32) + bfr_ref[...])
    cs3 = jnp.dot(lower, jnp.concatenate(_split3(lf_c), axis=1), preferred_element_type=F32)
    cs_c = car_c[...] + (cs3[:, :LANES] + cs3[:, LANES:2 * LANES] + cs3[:, 2 * LANES:])
    car_c[...] = cs_c[tk - 1:tk, :]
    p0, p1, p2 = (p.astype(F32) for p in _split3(cs_c * LOG2E))
    lane = lax.broadcasted_iota(jnp.int32, (tk, LANES), 1)
    groups = []
    for pair in range(N_HEADS // 2):
        two = kr[:, pair * LANES:(pair + 1) * LANES]
        groups.append(jnp.where(lane < HEAD_DIM, two, 0.0))
        groups.append(jnp.where(lane < HEAD_DIM, pltpu.roll(two, HEAD_DIM, 1), 0.0))
    kf = jnp.concatenate(groups, axis=1)
    packed = jnp.where(lane < N_HEADS, p0,
                       jnp.where(lane < 2 * N_HEADS, pltpu.roll(p1, N_HEADS, 1),
                                 jnp.where(lane < 3 * N_HEADS, pltpu.roll(p2, 2 * N_HEADS, 1), 0.0)))
    bias = jnp.dot(packed.astype(BF16), selk_ref[...], preferred_element_type=F32)
    ka_ref[0] = (kf + bias + onesk_ref[...]).astype(BF16)

    lf_r = _log_sigmoid(lax.dot_general(wfT_ref[...], xn, NT_DIMS, preferred_element_type=F32) + bfc_ref[...])
    cs3 = jnp.dot(jnp.concatenate(_split3(lf_r), axis=0), upper, preferred_element_type=F32)
    cs_r = car_r[...] + (cs3[:LANES] + cs3[LANES:2 * LANES] + cs3[2 * LANES:])
    car_r[...] = cs_r[:, tk - 1:tk]
    c2r_ref[0] = cs_r * LOG2E


def _kv_proj(x, g, w_kvf, b_f, consts, *, tk):
    bsz, s, d = x.shape
    da = N_HEADS * HEAD_DIM
    hg = N_HEADS * HEAD_GROUP
    dv = N_HEADS * V_GROUP
    nk = s // tk
    sel_k, _, ones_k, _, head_sum = consts
    wk = w_kvf[:, :da].astype(BF16)
    wv = w_kvf[:, da:2 * da].reshape(d, N_HEADS, HEAD_DIM)
    wvT = jnp.pad(wv, ((0, 0), (0, 0), (0, V_GROUP - HEAD_DIM))).reshape(d, dv).T.astype(BF16)
    ones_v = (jnp.arange(dv) % V_GROUP == HEAD_DIM).astype(F32).reshape(dv, 1)
    wf = jnp.pad(w_kvf[:, 2 * da:], ((0, 0), (0, LANES - N_HEADS))).astype(BF16)
    bfr = jnp.pad(b_f, (0, LANES - N_HEADS)).reshape(1, LANES)
    const = lambda *shape: _const_spec(shape, 2)
    return pl.pallas_call(
        _kv_kernel,
        grid=(bsz, nk),
        in_specs=[
            pl.BlockSpec((1, tk, d), lambda b, t: (b, t, 0)),
            const(1, d), const(d, da), const(dv, d), const(dv, 1), const(d, LANES), const(LANES, d),
            const(1, LANES), const(LANES, 1), const(LANES, hg), const(1, hg), const(da, LANES),
        ],
        out_specs=[
            pl.BlockSpec((1, tk, hg), lambda b, t: (b, t, 0)),
            pl.BlockSpec((1, 1, dv, tk), lambda b, t: (b, t, 0, 0)),
            pl.BlockSpec((1, LANES, tk), lambda b, t: (b, 0, t)),
            pl.BlockSpec((1, 1, SUBLANES, LANES), lambda b, t: (b, t, 0, 0)),
        ],
        out_shape=[
            jax.ShapeDtypeStruct((bsz, s, hg), BF16),
            jax.ShapeDtypeStruct((bsz, nk, dv, tk), BF16),
            jax.ShapeDtypeStruct((bsz, LANES, s), F32),
            jax.ShapeDtypeStruct((bsz, nk, SUBLANES, LANES), F32),
        ],
        scratch_shapes=[pltpu.VMEM((LANES, 1), F32), pltpu.VMEM((1, LANES), F32)],
        compiler_params=pltpu.CompilerParams(dimension_semantics=("parallel", "arbitrary")),
        name="kv_proj",
    )(x, g.reshape(1, d), wk, wvT, ones_v, wf, wf.T, bfr, bfr.reshape(LANES, 1), sel_k, ones_k, head_sum)


def _q_kernel(x_ref, g_ref, wqT_ref, wg_ref, c2r_ref, selq_ref, onesq_ref, hsumT_ref, qa_ref, gate_ref, qn_ref):
    xn = _rms(x_ref[0], g_ref[...]).astype(BF16)
    gate_ref[0] = jax.nn.sigmoid(jnp.dot(xn, wg_ref[...], preferred_element_type=F32)).astype(BF16)
    qr = lax.dot_general(wqT_ref[...], xn, NT_DIMS, preferred_element_type=F32) * (LOG2E * HEAD_DIM ** -0.5)

    qss = jnp.dot(hsumT_ref[...], (qr * qr).astype(BF16), preferred_element_type=F32)
    qn = jnp.sqrt(jnp.max(qss, axis=1, keepdims=True))
    qn_ref[0, 0] = jnp.broadcast_to(qn, (LANES, LANES))

    tq = c2r_ref.shape[2]
    gap = jnp.zeros((HEAD_GROUP - HEAD_DIM, tq), F32)
    qf = jnp.concatenate([blk for h in range(N_HEADS) for blk in (qr[h * HEAD_DIM:(h + 1) * HEAD_DIM], gap)], axis=0)
    packed = jnp.concatenate([p[:N_HEADS] for p in _split3(c2r_ref[0])]
                             + [jnp.zeros((LANES - 3 * N_HEADS, tq), BF16)], axis=0)
    bias = jnp.dot(selq_ref[...], packed, preferred_element_type=F32)
    qa_ref[0] = (qf + bias + onesq_ref[...]).astype(BF16)


def _q_proj(x, g, w_qg, c2r, consts, *, tq):
    bsz, s, d = x.shape
    da = N_HEADS * HEAD_DIM
    hg = N_HEADS * HEAD_GROUP
    nq = s // tq
    _, sel_q, _, ones_q, head_sum = consts
    wqT = w_qg[:, :da].T.astype(BF16)
    const = lambda *shape: _const_spec(shape, 2)
    return pl.pallas_call(
        _q_kernel,
        grid=(bsz, nq),
        in_specs=[
            pl.BlockSpec((1, tq, d), lambda b, t: (b, t, 0)),
            const(1, d), const(da, d), const(d, da),
            pl.BlockSpec((1, LANES, tq), lambda b, t: (b, 0, t)),
            const(hg, LANES), const(hg, 1), const(LANES, da),
        ],
        out_specs=[
            pl.BlockSpec((1, hg, tq), lambda b, t: (b, 0, t)),
            pl.BlockSpec((1, tq, da), lambda b, t: (b, t, 0)),
            pl.BlockSpec((1, 1, LANES, LANES), lambda b, t: (b, t, 0, 0)),
        ],
        out_shape=[
            jax.ShapeDtypeStruct((bsz, hg, s), BF16),
            jax.ShapeDtypeStruct((bsz, s, da), BF16),
            jax.ShapeDtypeStruct((bsz, nq, LANES, LANES), F32),
        ],
        compiler_params=pltpu.CompilerParams(dimension_semantics=("parallel", "parallel")),
        name="q_proj",
    )(x, g.reshape(1, d), wqT, w_qg[:, da:].astype(BF16), c2r, sel_q, ones_q, head_sum.T)


def _attn_kernel(jlo_ref, qa_ref, qan_ref, ka_ref, vT_ref, gate_ref, o_ref, ready, m0, m1, acc0, acc1,
                 sa0, sa1, sb0, sb1, sc0, sc1, pa0, pa1, pb0, pb1):
    b = pl.program_id(0)
    hp = pl.program_id(1)
    i = pl.program_id(2)
    nq = pl.num_programs(2)
    tq = qa_ref.shape[2]
    tk = vT_ref.shape[3]
    base = (b * pl.num_programs(1) + hp) * nq
    j_lo = jlo_ref[base + i]
    j_lo_next = jlo_ref[base + jnp.minimum(i + 1, nq - 1)]
    m_refs, acc_refs = (m0, m1), (acc0, acc1)
    s_a, s_b, s_c = (sa0, sa1), (sb0, sb1), (sc0, sc1)
    p_a, p_b = (pa0, pa1), (pb0, pb1)

    def scores(q_ref, j, h):
        k_blk = ka_ref[0, pl.ds(pl.multiple_of(j * tk, tk), tk), h * HEAD_GROUP:(h + 1) * HEAD_GROUP]
        qT = q_ref[0, h * HEAD_GROUP:(h + 1) * HEAD_GROUP, :]
        return jnp.dot(k_blk, qT, preferred_element_type=F32)

    def values(j, h, p):
        vT = vT_ref[0, j, h * V_GROUP:(h + 1) * V_GROUP, :]
        return jnp.dot(vT, p, preferred_element_type=F32)

    def softmax_tile(h, s, masked):
        if masked:
            kv_pos = lax.broadcasted_iota(jnp.int32, (tk, tq), 0)
            q_pos = lax.broadcasted_iota(jnp.int32, (tk, tq), 1)
            s = jnp.where(kv_pos <= q_pos, s, NEG_BIG)
        m_old = m_refs[h][...]
        m_new = jnp.maximum(m_old, jnp.max(s, axis=0, keepdims=True))
        m_refs[h][...] = m_new
        return jnp.exp2(m_old - m_new), jnp.exp2(s - m_new).astype(BF16)

    def step(s_src, s_dst, p_prev, p_dst, j):
        for h in range(2):
            s_dst[h][...] = scores(qa_ref, j + 1, h)
        pv = None if p_prev is None else [values(j - 1, h, p_prev[h][...]) for h in range(2)]
        for h in range(2):
            alpha, p = softmax_tile(h, s_src[h][...], masked=False)
            p_dst[h][...] = p
            if pv is not None:
                acc_refs[h][...] = alpha * (acc_refs[h][...] + pv[h])

    def finish(s_src, p_prev, prefetch):
        if prefetch:
            for h in range(2):
                s_c[h][...] = scores(qan_ref, j_lo_next, h)
            ready[0] = (i + 1 < nq).astype(jnp.int32)
        else:
            ready[0] = jnp.int32(0)
        pv = None if p_prev is None else [values(i - 1, h, p_prev[h][...]) for h in range(2)]
        outs = []
        for h in range(2):
            alpha, p = softmax_tile(h, s_src[h][...], masked=True)
            acc = values(i, h, p)
            if pv is not None:
                acc = acc + alpha * (acc_refs[h][...] + pv[h])
            outs.append(acc[0:HEAD_DIM, :] / acc[HEAD_DIM:HEAD_DIM + 1, :])
        oT = jnp.concatenate(outs, axis=0)
        o_ref[0] = (oT.T * gate_ref[0].astype(F32)).astype(BF16)

    @pl.when(i == 0)
    def _():
        ready[0] = jnp.int32(0)

    @pl.when(ready[0] == 0)
    def _():
        for h in range(2):
            s_c[h][...] = scores(qa_ref, j_lo, h)

    for h in range(2):
        m_refs[h][...] = jnp.full((1, tq), NEG_BIG, F32)
        acc_refs[h][...] = jnp.zeros((V_GROUP, tq), F32)

    n_full = i - j_lo

    @pl.when(n_full == 0)
    def _():
        finish(s_c, None, prefetch=False)

    @pl.when(n_full > 0)
    def _():
        step(s_c, s_a, None, p_a, j_lo)
        rest = n_full - 1

        def body(n, carry):
            j = j_lo + 1 + 2 * n
            step(s_a, s_b, p_a, p_b, j)
            step(s_b, s_a, p_b, p_a, j + 1)
            return carry

        lax.fori_loop(0, rest // 2, body, 0)

        @pl.when(rest % 2 == 1)
        def _():
            step(s_a, s_b, p_a, p_b, i - 1)
            finish(s_b, p_b, prefetch=True)

        @pl.when(rest % 2 == 0)
        def _():
            finish(s_a, p_a, prefetch=True)


def _permute_heads(w_kvf, b_f, w_qg, w_o):
    da = N_HEADS * HEAD_DIM
    perm = jnp.argsort(b_f)
    cols = (perm[:, None] * HEAD_DIM + jnp.arange(HEAD_DIM)[None, :]).reshape(-1)
    w_kvf_p = jnp.concatenate([w_kvf[:, :da][:, cols], w_kvf[:, da:2 * da][:, cols], w_kvf[:, 2 * da:][:, perm]],
                              axis=1)
    w_qg_p = jnp.concatenate([w_qg[:, :da][:, cols], w_qg[:, da:][:, cols]], axis=1)
    return w_kvf_p, b_f[perm], w_qg_p, w_o[cols, :]


def _skip_table(qn, kn, c2r, tq):
    bsz, nq = qn.shape[0], qn.shape[1]
    qn = qn[:, :, :N_HEADS, 0].transpose(0, 2, 1) * NORM_SLACK
    kn = kn[:, :, 0, :N_HEADS].transpose(0, 2, 1) * NORM_SLACK
    c2 = c2r[:, :N_HEADS, :]
    c_first = c2[:, :, ::tq]
    c_last = c2[:, :, tq - 1::tq]
    upper = qn[..., :, None] * kn[..., None, :] + c_first[..., :, None] - c_last[..., None, :]
    lower = -(qn * kn)[..., :, None]
    skip = (upper - lower) < -SKIP_GAP_LOG2
    nk = skip.shape[-1]
    jlo = jnp.min(jnp.where(skip, nk, jnp.arange(nk, dtype=jnp.int32)), axis=-1)
    jlo = jnp.minimum(jlo, jnp.arange(nq, dtype=jnp.int32))
    jlo = jnp.min(jlo.reshape(bsz, N_HEADS // 2, 2, nq), axis=2)
    return jlo.reshape(-1).astype(jnp.int32)


def _attention(jlo, qa, ka, vT, gate, *, tq):
    bsz, hg, s = qa.shape
    nk, tk = vT.shape[1], vT.shape[3]
    da = N_HEADS * HEAD_DIM
    assert tq == tk
    pair = 2 * HEAD_GROUP
    hw = 2 * HEAD_DIM
    nq = s // tq
    grid_spec = pltpu.PrefetchScalarGridSpec(
        num_scalar_prefetch=1,
        grid=(bsz, N_HEADS // 2, nq),
        in_specs=[
            pl.BlockSpec((1, pair, tq), lambda b, hp, i, jlo: (b, hp, i)),
            pl.BlockSpec((1, pair, tq), lambda b, hp, i, jlo: (b, hp, jnp.minimum(i + 1, nq - 1))),
            pl.BlockSpec((1, s, pair), lambda b, hp, i, jlo: (b, 0, hp)),
            pl.BlockSpec((1, nk, 2 * V_GROUP, tk), lambda b, hp, i, jlo: (b, 0, hp, 0)),
            pl.BlockSpec((1, tq, hw), lambda b, hp, i, jlo: (b, i, hp)),
        ],
        out_specs=pl.BlockSpec((1, tq, hw), lambda b, hp, i, jlo: (b, i, hp)),
        scratch_shapes=([pltpu.SMEM((1,), jnp.int32)]
                        + [pltpu.VMEM((1, tq), F32)] * 2 + [pltpu.VMEM((V_GROUP, tq), F32)] * 2
                        + [pltpu.VMEM((tk, tq), F32)] * 6 + [pltpu.VMEM((tk, tq), BF16)] * 4),
    )
    return pl.pallas_call(
        _attn_kernel,
        grid_spec=grid_spec,
        out_shape=jax.ShapeDtypeStruct((bsz, s, da), BF16),
        compiler_params=pltpu.CompilerParams(dimension_semantics=("parallel", "parallel", "arbitrary")),
        name="fox_attention",
    )(jlo, qa, qa, ka, vT, gate)


def _pack_bf16_pairs(x):
    half = x.shape[1] // 2
    lo = lax.bitcast_convert_type(x[:, :half].astype(BF16).astype(F32), jnp.uint32)
    hi = lax.bitcast_convert_type(x[:, half:].astype(BF16).astype(F32), jnp.uint32)
    word = (lo >> 16) | (hi & jnp.uint32(0xFFFF0000))
    return lax.bitcast_convert_type(word, jnp.int32)


def _unpack_bf16_pairs(w):
    u = lax.bitcast_convert_type(w, jnp.uint32)
    lo = lax.bitcast_convert_type(u << 16, F32)
    hi = lax.bitcast_convert_type(u & jnp.uint32(0xFFFF0000), F32)
    return jnp.concatenate([lo, hi], axis=1)


def _store_planes(ref, words):
    q = words.shape[1] // 2
    ref[0] = words[:, :q]
    ref[1] = words[:, q:]


def _load_planes(ref, base=0):
    return jnp.concatenate([ref[base], ref[base + 1]], axis=1)


def _router_kernel(x_ref, o_ref, wo_ref, g_ref, rtT_ref, x2_ref, hnp_ref, route_ref):
    tm = x_ref.shape[0]
    x2 = x_ref[...] + jnp.dot(o_ref[...], wo_ref[...], preferred_element_type=F32)
    x2_ref[...] = x2
    hn = _rms(x2, g_ref[...])
    _store_planes(hnp_ref, _pack_bf16_pairs(hn))
    h0, h1, _ = _split3(hn)
    r0, r1, _ = _split3(rtT_ref[...])
    lt = lax.dot_general(jnp.concatenate([r0, r1, r0], axis=1), jnp.concatenate([h0, h0, h1], axis=1), NT_DIMS,
                         preferred_element_type=F32)[:N_EXPERTS]
    row = lax.broadcasted_iota(jnp.int32, (N_EXPERTS, tm), 0)
    m1 = jnp.max(lt, axis=0, keepdims=True)
    i1 = jnp.min(jnp.where(lt == m1, row, N_EXPERTS), axis=0, keepdims=True)
    rest = jnp.where(row == i1, -jnp.inf, lt)
    m2 = jnp.max(rest, axis=0, keepdims=True)
    i2 = jnp.min(jnp.where(rest == m2, row, N_EXPERTS), axis=0, keepdims=True)
    e2 = jnp.exp(m2 - m1)
    w1 = 1.0 / (1.0 + e2)
    w2 = e2 / (1.0 + e2)
    route_ref[...] = jnp.where(row == 0, i1.astype(F32), jnp.where(row == 1, i2.astype(F32),
                               jnp.where(row == 2, w1, jnp.where(row == 3, w2, 0.0))))


def _router(x2d, o2d, w_o, g, router, *, tm):
    t, d = x2d.shape
    da = o2d.shape[1]
    ne = router.shape[1]
    rtT = jnp.pad(router.T, ((0, 2 * SUBLANES - ne), (0, 0)))
    return pl.pallas_call(
        _router_kernel,
        grid=(t // tm,),
        in_specs=[
            pl.BlockSpec((tm, d), lambda i: (i, 0)),
            pl.BlockSpec((tm, da), lambda i: (i, 0)),
            pl.BlockSpec((da, d), lambda i: (0, 0)),
            pl.BlockSpec((1, d), lambda i: (0, 0)),
            pl.BlockSpec((2 * SUBLANES, d), lambda i: (0, 0)),
        ],
        out_specs=[pl.BlockSpec((tm, d), lambda i: (i, 0)), pl.BlockSpec((2, tm, d // 4), lambda i: (0, i, 0)),
                   pl.BlockSpec((ne, tm), lambda i: (0, i))],
        out_shape=[jax.ShapeDtypeStruct((t, d), F32), jax.ShapeDtypeStruct((2, t, d // 4), jnp.int32),
                   jax.ShapeDtypeStruct((ne, t), F32)],
        compiler_params=pltpu.CompilerParams(dimension_semantics=("parallel",)),
        name="attn_out_moe_router",
    )(x2d, o2d, w_o.astype(BF16), g.reshape(1, d), rtT)


def _routing_tables(route, ts):
    ne = N_EXPERTS
    t = route.shape[1]
    e_pair = route[0:2].astype(jnp.int32).reshape(-1)
    n_tiles = (2 * t) // ts + ne
    n_slots = n_tiles * ts
    onehot = (e_pair[:, None] == jnp.arange(ne, dtype=jnp.int32)[None, :]).astype(jnp.int32)
    csum = jnp.cumsum(onehot, axis=0)
    rank = jnp.sum(onehot * csum, axis=1) - 1
    tiles = (csum[-1] + ts - 1) // ts
    tile_end = jnp.cumsum(tiles)
    tile_start = tile_end - tiles
    slot = (jnp.sum(onehot * tile_start[None, :], axis=1) * ts + rank).astype(jnp.int32)
    n_used = tile_end[-1]
    tile_ids = jnp.arange(n_tiles, dtype=jnp.int32)
    tile_expert = jnp.sum((tile_ids[:, None] >= tile_end[None, :]).astype(jnp.int32), axis=1)
    last_expert = jnp.sum(((n_used - 1) >= tile_end).astype(jnp.int32))
    tile_expert = jnp.minimum(tile_expert, last_expert).astype(jnp.int32)
    plane_rows = slot.reshape(2, 1, t) + (jnp.arange(2, dtype=jnp.int32) * n_slots).reshape(1, 2, 1)
    return plane_rows.reshape(-1), tile_expert, n_used.reshape(1).astype(jnp.int32), n_slots


def _sc_mesh():
    return plsc.VectorSubcoreMesh(core_axis_name="core", subcore_axis_name="subcore")


def _scatter_rows(table, idx, n_out):
    m = idx.shape[0]
    n, d = table.shape
    assert m % SC_ROW_WINDOW == 0 and n % SC_ROW_WINDOW == 0
    n_src_blocks = n // SC_ROW_WINDOW

    @pl.kernel(out_type=jax.ShapeDtypeStruct((n_out, d), table.dtype), mesh=_sc_mesh(), scratch_types=[])
    def scatter_kernel(x_hbm, i_hbm, o_hbm):
        def body(x_vmem, i_vmem):
            pltpu.sync_copy(x_vmem, o_hbm.at[i_vmem.at[0]])

        pltpu.emit_pipeline(
            body,
            grid=(m // SC_ROW_WINDOW,),
            in_specs=[pl.BlockSpec((SC_ROW_WINDOW, d), index_map=lambda i: (i % n_src_blocks, 0)),
                      pl.BlockSpec((1, SC_ROW_WINDOW), index_map=lambda i: (0, i))],
            out_specs=[],
            core_axis_name=("core", "subcore"),
            dimension_semantics=(pltpu.PARALLEL,),
        )(x_hbm, i_hbm)

    return scatter_kernel(table, idx.reshape(1, m))


def _gather_rows(table, idx):
    m = idx.shape[0]
    d = table.shape[1]
    assert m % SC_ROW_WINDOW == 0

    @pl.kernel(out_type=jax.ShapeDtypeStruct((m, d), table.dtype), mesh=_sc_mesh())
    def gather_kernel(x_hbm, i_hbm, o_hbm):
        def body(i_vmem, o_vmem):
            pltpu.sync_copy(x_hbm.at[i_vmem.at[0]], o_vmem)

        pltpu.emit_pipeline(
            body,
            grid=(m // SC_ROW_WINDOW,),
            in_specs=[pl.BlockSpec((1, SC_ROW_WINDOW), index_map=lambda i: (0, i))],
            out_specs=[pl.BlockSpec((SC_ROW_WINDOW, d), index_map=lambda i: (i, 0))],
            core_axis_name=("core", "subcore"),
            dimension_semantics=(pltpu.PARALLEL,),
        )(i_hbm, o_hbm)

    return gather_kernel(table, idx.reshape(1, m))


def _expert_kernel(nk, te_ref, nu_ref, xs_ref, wg_ref, wu_ref, wd_ref, ys_ref, x_sc, acc_ref):
    i = pl.program_id(0)
    k = pl.program_id(1)
    used = i < nu_ref[0]

    def first():
        x_sc[...] = _unpack_bf16_pairs(_load_planes(xs_ref)).astype(BF16)

    def last(total):
        _store_planes(ys_ref, _pack_bf16_pairs(total))

    @pl.when(used)
    def _():
        _chunked_swiglu(k, nk, lambda: _swiglu_chunk(x_sc[...], wg_ref.at[0], wu_ref.at[0], wd_ref.at[0]), acc_ref,
                        first, last)

    @pl.when(jnp.logical_not(used) & (k == nk - 1))
    def _():
        ys_ref[...] = jnp.zeros_like(ys_ref)


def _experts(xs, tile_expert, n_used, w_gu, w_down, *, ts, tf):
    _, n_slots, quarter = xs.shape
    d = 4 * quarter
    dexp = w_down.shape[1]
    nk = dexp // tf
    w_gu = w_gu.astype(BF16)
    grid_spec = pltpu.PrefetchScalarGridSpec(
        num_scalar_prefetch=2,
        grid=(n_slots // ts, nk),
        in_specs=[
            pl.BlockSpec((2, ts, quarter), lambda i, k, te, nu: (0, i, 0)),
            pl.BlockSpec((1, d, tf), lambda i, k, te, nu: (te[i], 0, jnp.where(i < nu[0], k, 0))),
            pl.BlockSpec((1, d, tf), lambda i, k, te, nu: (te[i], 0, nk + jnp.where(i < nu[0], k, 0))),
            pl.BlockSpec((1, tf, d), lambda i, k, te, nu: (te[i], jnp.where(i < nu[0], k, 0), 0)),
        ],
        out_specs=pl.BlockSpec((2, ts, quarter), lambda i, k, te, nu: (0, i, 0)),
        scratch_shapes=[pltpu.VMEM((ts, d), BF16), pltpu.VMEM((ts, d), F32)],
    )
    return pl.pallas_call(
        functools.partial(_expert_kernel, nk),
        grid_spec=grid_spec,
        out_shape=jax.ShapeDtypeStruct((2, n_slots, quarter), jnp.int32),
        compiler_params=pltpu.CompilerParams(dimension_semantics=("arbitrary", "arbitrary")),
        name="moe_experts",
    )(tile_expert, n_used, xs, w_gu, w_gu, w_down.astype(BF16))


def _combine_kernel(x_ref, y_ref, rt_ref, gf_ref, o_ref):
    rt = rt_ref[...]
    moe = (rt[:, 2:3] * _unpack_bf16_pairs(_load_planes(y_ref, 0))
           + rt[:, 3:4] * _unpack_bf16_pairs(_load_planes(y_ref, 2)))
    o_ref[...] = _rms(x_ref[...] + moe, gf_ref[...])


def _combine(x2d, y4, route_t, g_final, *, tm):
    t, d = x2d.shape
    ne = route_t.shape[1]
    return pl.pallas_call(
        _combine_kernel,
        grid=(t // tm,),
        in_specs=[
            pl.BlockSpec((tm, d), lambda i: (i, 0)),
            pl.BlockSpec((4, tm, d // 4), lambda i: (0, i, 0)),
            pl.BlockSpec((tm, ne), lambda i: (i, 0)),
            pl.BlockSpec((1, d), lambda i: (0, 0)),
        ],
        out_specs=pl.BlockSpec((tm, d), lambda i: (i, 0)),
        out_shape=jax.ShapeDtypeStruct((t, d), F32),
        compiler_params=pltpu.CompilerParams(dimension_semantics=("parallel",)),
        name="moe_combine",
    )(x2d, y4, route_t, g_final.reshape(1, d))


def _attn_out_moe(x1, o2d, w_o, g, router, w_gu, w_down, g_final, *, tm, ts, tf):
    t, d = x1.shape
    q = d // 4
    x2d, hn_planes, route = _router(x1, o2d, w_o, g, router, tm=tm)
    plane_rows, tile_expert, n_used, n_slots = _routing_tables(route, ts)
    xs = _scatter_rows(hn_planes.reshape(2 * t, q), plane_rows, 2 * n_slots)
    ys = _experts(xs.reshape(2, n_slots, q), tile_expert, n_used, w_gu, w_down, ts=ts, tf=tf)
    y4 = _gather_rows(ys.reshape(2 * n_slots, q), plane_rows)
    return _combine(x2d, y4.reshape(4, t, q), route.T, g_final, tm=tm)


ROW_TILE = 512


def _tile(n):
    t = min(n, ROW_TILE)
    assert n % t == 0 and t % (2 * LANES) == 0
    return t


def kernel(x, norm_g, lru_w_in, lru_conv_w, lru_conv_b, lru_gate_w, lru_gate_b, lru_lambda, lru_w_out,
           kv_norm_g, w_kvf, b_f, fox_w_qg, fox_w_o, mlp_w_gu, mlp_w_down, moe_router, moe_w_gu,
           moe_w_down, final_norm_g):
    bsz, s, d = x.shape
    t = bsz * s
    tm = _tile(t)
    ts = _tile(s)

    x = _lru_mlp(x, norm_g[0, 0], lru_w_in[0], lru_conv_w[0], lru_conv_b[0], lru_gate_w[0], lru_gate_b[0],
                 lru_lambda[0], lru_w_out[0], norm_g[0, 1], mlp_w_gu[0], mlp_w_down[0], tm=ts)

    consts = _head_selectors()
    w_kvf_p, b_f_p, w_qg_p, w_o_p = _permute_heads(w_kvf, b_f, fox_w_qg[0], fox_w_o[0])
    ka, vT, c2r, kn = _kv_proj(x, kv_norm_g, w_kvf_p, b_f_p, consts, tk=ts)
    qa, gate, qn = _q_proj(x, norm_g[1, 0], w_qg_p, c2r, consts, tq=ts)
    jlo = _skip_table(qn, kn, c2r, ts)
    o = _attention(jlo, qa, ka, vT, gate, tq=ts)
    dexp = moe_w_down.shape[2]
    out = _attn_out_moe(x.reshape(t, d), o.reshape(t, -1), w_o_p, norm_g[1, 1], moe_router[0], moe_w_gu[0],
                        moe_w_down[0], final_norm_g, tm=tm, ts=tm, tf=dexp // 2)
    return out.reshape(bsz, s, d)
```

```python
import functools
import math

import jax
import jax.numpy as jnp
from jax import lax
from jax.experimental import pallas as pl
from jax.experimental.pallas import tpu as pltpu
from jax.experimental.pallas import tpu_sc as plsc

EPS = 1e-6
LRU_C = 8.0
N_LRU_BLOCKS = 8
CONV_W = 4
N_HEADS = 16
HEAD_DIM = 64
N_EXPERTS = 8
LANES = 128
SUBLANES = 8
HEAD_GROUP = 128
BIAS_Q = HEAD_DIM
BIAS_K = HEAD_DIM + 3
V_GROUP = 80
NEG_BIG = -1e30
LOG2E = math.log2(math.e)
SKIP_GAP_LOG2 = 152.0
NORM_SLACK = 1.01
SC_ROW_WINDOW = 128

F32 = jnp.float32
BF16 = jnp.bfloat16
NT_DIMS = (((1,), (1,)), ((), ()))


def _rms(x, g):
    ms = jnp.mean(x * x, axis=-1, keepdims=True)
    return x * lax.rsqrt(ms + EPS) * g


def _log_sigmoid(x):
    return jnp.minimum(x, 0.0) - jnp.log1p(jnp.exp(-jnp.abs(x)))


def _gelu_tanh(x):
    c = 0.7978845608028654
    return 0.5 * x * (1.0 + jnp.tanh(c * (x + 0.044715 * (x * x * x))))


def _silu(x):
    return x * jax.nn.sigmoid(x)


def _split3(x):
    p0 = x.astype(BF16)
    r1 = x - p0.astype(F32)
    p1 = r1.astype(BF16)
    p2 = (r1 - p1.astype(F32)).astype(BF16)
    return p0, p1, p2


def _const_spec(shape, n_grid):
    zeros = (0,) * len(shape)
    if n_grid == 1:
        return pl.BlockSpec(shape, lambda i: zeros)
    if n_grid == 2:
        return pl.BlockSpec(shape, lambda i, j: zeros)
    return pl.BlockSpec(shape, lambda i, j, k: zeros)


def _lru_in_proj(x, g_ref, win_ref):
    d = x.shape[1]
    xn = _rms(x, g_ref[...]).astype(BF16)
    xg = jnp.dot(xn, win_ref[...], preferred_element_type=F32)
    return xg[:, :d], xg[:, d:]


def _lru_conv_gates(xb, cw_ref, cb_ref, gw_ref, xbuf):
    tt, d = xb.shape
    bw = d // N_LRU_BLOCKS
    xbuf[SUBLANES:SUBLANES + tt, :] = xb
    cw = cw_ref[...]
    xc = cb_ref[...] + cw[CONV_W - 1:CONV_W, :] * xb
    for j in range(CONV_W - 1):
        xc = xc + cw[j:j + 1, :] * xbuf[pl.ds(SUBLANES - (CONV_W - 1) + j, tt), :]
    xbuf[0:SUBLANES, :] = xbuf[tt:tt + SUBLANES, :]

    xcb = xc.astype(BF16)
    rl, il = [], []
    for n in range(N_LRU_BLOCKS):
        gl = jnp.dot(xcb[:, n * bw:(n + 1) * bw], gw_ref[n], preferred_element_type=F32)
        rl.append(gl[:, :bw])
        il.append(gl[:, bw:])
    return xc, jnp.concatenate(rl, axis=1), jnp.concatenate(il, axis=1)


def _lru_recurrence(xc, r_logit, i_logit, gbr, gb_ref, lam_ref, hcar):
    tt = xc.shape[0]
    gb = gb_ref[...]
    r = jax.nn.sigmoid(r_logit + gb[0:1, :])
    i = jax.nn.sigmoid(i_logit + gb[1:2, :])
    log_a = LRU_C * r * _log_sigmoid(lam_ref[...])
    a = jnp.exp(log_a)
    b = jnp.sqrt((1.0 - a) * (1.0 + a)) * (i * xc)

    row = lax.broadcasted_iota(jnp.int32, (tt, 1), 0)
    sh = 1
    while sh < tt:
        if sh < SUBLANES:
            keep = row >= sh
            a_sh = jnp.where(keep, pltpu.roll(a, sh, 0), 1.0)
            b_sh = jnp.where(keep, pltpu.roll(b, sh, 0), 0.0)
            b = a * b_sh + b
            a = a * a_sh
        else:
            b = jnp.concatenate([b[:sh], a[sh:] * b[:tt - sh] + b[sh:]], axis=0)
            a = jnp.concatenate([a[:sh], a[sh:] * a[:tt - sh]], axis=0)
        sh *= 2
    h = b + a * hcar[...]
    hcar[...] = h[tt - 1:tt, :]
    return (_gelu_tanh(gbr) * h).astype(BF16)


def _lru_mlp_kernel(x_ref, g0_ref, win_ref, cw_ref, cb_ref, gw_ref, gb_ref, lam_ref, wout_ref,
                    g1_ref, wg_ref, wu_ref, wd_ref, o_ref, xbuf, hcar, x1_sc, hn_sc, acc_ref):
    t = pl.program_id(1)
    k = pl.program_id(2)
    half = x_ref.shape[1]
    tf = wg_ref.shape[1]
    mid = (tf // (2 * LANES)) * LANES
    slot = t % 2
    prev = 1 - slot

    @pl.when((t == 0) & (k == 0))
    def _():
        xbuf[0:SUBLANES, :] = jnp.zeros((SUBLANES, xbuf.shape[1]), F32)
        hcar[...] = jnp.zeros_like(hcar)
        x1_sc[...] = jnp.zeros_like(x1_sc)
        hn_sc[...] = jnp.zeros_like(hn_sc)

    def gate_up(hn, lo, hi):
        return (jnp.dot(hn, wg_ref[:, lo:hi], preferred_element_type=F32),
                jnp.dot(hn, wu_ref[:, lo:hi], preferred_element_type=F32))

    def both():
        x = x_ref[0]
        hn = hn_sc[prev]
        xb, gbr = _lru_in_proj(x, g0_ref, win_ref)
        g_a, u_a = gate_up(hn, 0, mid)
        xc, r_logit, i_logit = _lru_conv_gates(xb, cw_ref, cb_ref, gw_ref, xbuf)
        g_b, u_b = gate_up(hn, mid, tf)
        h = jnp.concatenate([(_silu(g_a) * u_a).astype(BF16), (_silu(g_b) * u_b).astype(BF16)], axis=1)
        y_mlp = jnp.dot(h, wd_ref[...], preferred_element_type=F32)
        y_lru = _lru_recurrence(xc, r_logit, i_logit, gbr, gb_ref, lam_ref, hcar)
        x1 = x + jnp.dot(y_lru, wout_ref[...], preferred_element_type=F32)
        rows = pl.ds(pl.multiple_of(k * half, half), half)
        x1_sc[slot, rows, :] = x1
        hn_sc[slot, rows, :] = _rms(x1, g1_ref[...]).astype(BF16)
        return y_mlp

    @pl.when(k == 0)
    def _():
        acc_ref[...] = both()

    @pl.when(k == 1)
    def _():
        y = both()
        o_ref[0] = x1_sc[prev] + acc_ref[...] + y


def _lru_mlp(x, g0, w_in, conv_w, conv_b, gate_w, gate_b, lam, w_out, g1, w_gu, w_down, *, tm):
    bsz, s, d = x.shape
    bw = d // N_LRU_BLOCKS
    dff = w_down.shape[0]
    tf = dff // 2
    half = tm // 2
    nt = s // tm
    n_half = s // half
    gw = jnp.concatenate([gate_w[0], gate_w[1]], axis=-1).astype(BF16)
    w_gu = w_gu.astype(BF16)

    def const(*shape):
        return pl.BlockSpec(shape, lambda b, t, k: (0,) * len(shape), pipeline_mode=pl.Buffered(1))

    return pl.pallas_call(
        _lru_mlp_kernel,
        grid=(bsz, nt + 1, 2),
        in_specs=[
            pl.BlockSpec((1, half, d), lambda b, t, k: (b, jnp.minimum(2 * t + k, n_half - 1), 0)),
            const(1, d), const(d, 2 * d), const(CONV_W, d), const(1, d),
            const(N_LRU_BLOCKS, bw, 2 * bw), const(2, d), const(1, d), const(d, d),
            const(1, d),
            pl.BlockSpec((d, tf), lambda b, t, k: (0, k)),
            pl.BlockSpec((d, tf), lambda b, t, k: (0, 2 + k)),
            pl.BlockSpec((tf, d), lambda b, t, k: (k, 0)),
        ],
        out_specs=pl.BlockSpec((1, tm, d), lambda b, t, k: (b, jnp.maximum(t - 1, 0), 0)),
        out_shape=jax.ShapeDtypeStruct((bsz, s, d), F32),
        scratch_shapes=[pltpu.VMEM((half + 2 * SUBLANES, d), F32), pltpu.VMEM((1, d), F32),
                        pltpu.VMEM((2, tm, d), F32), pltpu.VMEM((2, tm, d), BF16), pltpu.VMEM((tm, d), F32)],
        compiler_params=pltpu.CompilerParams(dimension_semantics=("parallel", "arbitrary", "arbitrary")),
        name="lru_mlp",
    )(x, g0.reshape(1, d), w_in.astype(BF16), conv_w, conv_b.reshape(1, d), gw, gate_b, lam.reshape(1, d),
      w_out.astype(BF16), g1.reshape(1, d), w_gu, w_gu, w_down.astype(BF16))


def _swiglu_chunk(x, wg, wu, wd):
    gate = jnp.dot(x, wg[...], preferred_element_type=F32)
    up = jnp.dot(x, wu[...], preferred_element_type=F32)
    h = (_silu(gate) * up).astype(BF16)
    return jnp.dot(h, wd[...], preferred_element_type=F32)


def _chunked_swiglu(k, nk, chunk, acc_ref, first, last):
    if nk == 1:
        first()
        last(chunk())
        return

    @pl.when(k == 0)
    def _():
        first()
        acc_ref[...] = chunk()

    if nk > 2:
        @pl.when((k > 0) & (k < nk - 1))
        def _():
            acc_ref[...] += chunk()

    @pl.when(k == nk - 1)
    def _():
        last(acc_ref[...] + chunk())


def _head_selectors():
    hg = N_HEADS * HEAD_GROUP
    head = jnp.arange(hg) // HEAD_GROUP
    lane = jnp.arange(hg) % HEAD_GROUP
    col = jnp.arange(LANES)
    def sel(base):
        return ((col[:, None] % N_HEADS == head[None, :]) & (col[:, None] < 3 * N_HEADS)
                & (lane[None, :] == base + col[:, None] // N_HEADS)).astype(BF16)
    sel_k = -sel(BIAS_K)
    sel_q = sel(BIAS_Q).T
    ones_k = ((lane >= BIAS_Q) & (lane < BIAS_Q + 3)).astype(F32).reshape(1, hg)
    ones_q = ((lane >= BIAS_K) & (lane < BIAS_K + 3)).astype(F32).reshape(hg, 1)
    feat_head = jnp.arange(N_HEADS * HEAD_DIM) // HEAD_DIM
    head_sum = (col[None, :] == feat_head[:, None]).astype(BF16)
    return sel_k, sel_q, ones_k, ones_q, head_sum


def _kv_kernel(x_ref, g_ref, wk_ref, wvT_ref, wf_ref, wfT_ref, bfr_ref, bfc_ref, selk_ref, onesk_ref,
               hsum_ref, ka_ref, vT_ref, c2r_ref, kn_ref, car_r, car_c):
    t = pl.program_id(1)
    tk = x_ref.shape[1]

    @pl.when(t == 0)
    def _():
        car_r[...] = jnp.zeros_like(car_r)
        car_c[...] = jnp.zeros_like(car_c)

    xn = _rms(x_ref[0], g_ref[...]).astype(BF16)
    vr = lax.dot_general(wvT_ref[...], xn, NT_DIMS, preferred_element_type=F32)
    tail_row = lax.broadcasted_iota(jnp.int32, (V_GROUP - HEAD_DIM, tk), 0)
    tail = jnp.where(tail_row == 0, 1.0, 0.0)
    vT = jnp.concatenate([blk for h in range(N_HEADS) for blk in (vr[h * HEAD_DIM:(h + 1) * HEAD_DIM], tail)], axis=0)
    vT_ref[0, 0] = vT.astype(BF16)
    kr = jnp.dot(xn, wk_ref[...], preferred_element_type=F32)

    kss = jnp.dot((kr * kr).astype(BF16), hsum_ref[...], preferred_element_type=F32)
    kn = jnp.sqrt(jnp.max(kss, axis=0, keepdims=True))
    kn_ref[0, 0] = jnp.broadcast_to(kn, (SUBLANES, LANES))

    ri = lax.broadcasted_iota(jnp.int32, (tk, tk), 0)
    ci = lax.broadcasted_iota(jnp.int32, (tk, tk), 1)
    upper = (ri <= ci).astype(BF16)
    lower = (ci <= ri).astype(BF16)

    lf_c = _log_sigmoid(jnp.dot(xn, wf_ref[...], preferred_element_type=F32) + bfr_ref[...])
    cs3 = jnp.dot(lower, jnp.concatenate(_split3(lf_c), axis=1), preferred_element_type=F32)
    cs_c = car_c[...] + (cs3[:, :LANES] + cs3[:, LANES:2 * LANES] + cs3[:, 2 * LANES:])
    car_c[...] = cs_c[tk - 1:tk, :]
    p0, p1, p2 = (p.astype(F32) for p in _split3(cs_c * LOG2E))
    lane = lax.broadcasted_iota(jnp.int32, (tk, LANES), 1)
    groups = []
    for pair in range(N_HEADS // 2):
        two = kr[:, pair * LANES:(pair + 1) * LANES]
        groups.append(jnp.where(lane < HEAD_DIM, two, 0.0))
        groups.append(jnp.where(lane < HEAD_DIM, pltpu.roll(two, HEAD_DIM, 1), 0.0))
    kf = jnp.concatenate(groups, axis=1)
    packed = jnp.where(lane < N_HEADS, p0,
                       jnp.where(lane < 2 * N_HEADS, pltpu.roll(p1, N_HEADS, 1),
                                 jnp.where(lane < 3 * N_HEADS, pltpu.roll(p2, 2 * N_HEADS, 1), 0.0)))
    bias = jnp.dot(packed.astype(BF16), selk_ref[...], preferred_element_type=F32)
    ka_ref[0] = (kf + bias + onesk_ref[...]).astype(BF16)

    lf_r = _log_sigmoid(lax.dot_general(wfT_ref[...], xn, NT_DIMS, preferred_element_type=F32) + bfc_ref[...])
    cs3 = jnp.dot(jnp.concatenate(_split3(lf_r), axis=0), upper, preferred_element_type=F32)
    cs_r = car_r[...] + (cs3[:LANES] + cs3[LANES:2 * LANES] + cs3[2 * LANES:])
    car_r[...] = cs_r[:, tk - 1:tk]
    c2r_ref[0] = cs_r * LOG2E


def _kv_proj(x, g, w_kvf, b_f, consts, *, tk):
    bsz, s, d = x.shape
    da = N_HEADS * HEAD_DIM
    hg = N_HEADS * HEAD_GROUP
    dv = N_HEADS * V_GROUP
    nk = s // tk
    sel_k, _, ones_k, _, head_sum = consts
    wk = w_kvf[:, :da].astype(BF16)
    wvT = w_kvf[:, da:2 * da].T.astype(BF16)
    wf = jnp.pad(w_kvf[:, 2 * da:], ((0, 0), (0, LANES - N_HEADS))).astype(BF16)
    bfr = jnp.pad(b_f, (0, LANES - N_HEADS)).reshape(1, LANES)
    const = lambda *shape: _const_spec(shape, 2)
    return pl.pallas_call(
        _kv_kernel,
        grid=(bsz, nk),
        in_specs=[
            pl.BlockSpec((1, tk, d), lambda b, t: (b, t, 0)),
            const(1, d), const(d, da), const(da, d), const(d, LANES), const(LANES, d),
            const(1, LANES), const(LANES, 1), const(LANES, hg), const(1, hg), const(da, LANES),
        ],
        out_specs=[
            pl.BlockSpec((1, tk, hg), lambda b, t: (b, t, 0)),
            pl.BlockSpec((1, 1, dv, tk), lambda b, t: (b, t, 0, 0)),
            pl.BlockSpec((1, LANES, tk), lambda b, t: (b, 0, t)),
            pl.BlockSpec((1, 1, SUBLANES, LANES), lambda b, t: (b, t, 0, 0)),
        ],
        out_shape=[
            jax.ShapeDtypeStruct((bsz, s, hg), BF16),
            jax.ShapeDtypeStruct((bsz, nk, dv, tk), BF16),
            jax.ShapeDtypeStruct((bsz, LANES, s), F32),
            jax.ShapeDtypeStruct((bsz, nk, SUBLANES, LANES), F32),
        ],
        scratch_shapes=[pltpu.VMEM((LANES, 1), F32), pltpu.VMEM((1, LANES), F32)],
        compiler_params=pltpu.CompilerParams(dimension_semantics=("parallel", "arbitrary")),
        name="kv_proj",
    )(x, g.reshape(1, d), wk, wvT, wf, wf.T, bfr, bfr.reshape(LANES, 1), sel_k, ones_k, head_sum)


def _q_kernel(x_ref, g_ref, wqT_ref, wg_ref, c2r_ref, selq_ref, onesq_ref, hsumT_ref, qa_ref, gate_ref, qn_ref):
    xn = _rms(x_ref[0], g_ref[...]).astype(BF16)
    gate_ref[0] = jax.nn.sigmoid(jnp.dot(xn, wg_ref[...], preferred_element_type=F32)).astype(BF16)
    qr = lax.dot_general(wqT_ref[...], xn, NT_DIMS, preferred_element_type=F32) * (LOG2E * HEAD_DIM ** -0.5)

    qss = jnp.dot(hsumT_ref[...], (qr * qr).astype(BF16), preferred_element_type=F32)
    qn = jnp.sqrt(jnp.max(qss, axis=1, keepdims=True))
    qn_ref[0, 0] = jnp.broadcast_to(qn, (LANES, LANES))

    tq = c2r_ref.shape[2]
    gap = jnp.zeros((HEAD_GROUP - HEAD_DIM, tq), F32)
    qf = jnp.concatenate([blk for h in range(N_HEADS) for blk in (qr[h * HEAD_DIM:(h + 1) * HEAD_DIM], gap)], axis=0)
    packed = jnp.concatenate([p[:N_HEADS] for p in _split3(c2r_ref[0])]
                             + [jnp.zeros((LANES - 3 * N_HEADS, tq), BF16)], axis=0)
    bias = jnp.dot(selq_ref[...], packed, preferred_element_type=F32)
    qa_ref[0] = (qf + bias + onesq_ref[...]).astype(BF16)


def _q_proj(x, g, w_qg, c2r, consts, *, tq):
    bsz, s, d = x.shape
    da = N_HEADS * HEAD_DIM
    hg = N_HEADS * HEAD_GROUP
    nq = s // tq
    _, sel_q, _, ones_q, head_sum = consts
    wqT = w_qg[:, :da].T.astype(BF16)
    const = lambda *shape: _const_spec(shape, 2)
    return pl.pallas_call(
        _q_kernel,
        grid=(bsz, nq),
        in_specs=[
            pl.BlockSpec((1, tq, d), lambda b, t: (b, t, 0)),
            const(1, d), const(da, d), const(d, da),
            pl.BlockSpec((1, LANES, tq), lambda b, t: (b, 0, t)),
            const(hg, LANES), const(hg, 1), const(LANES, da),
        ],
        out_specs=[
            pl.BlockSpec((1, hg, tq), lambda b, t: (b, 0, t)),
            pl.BlockSpec((1, tq, da), lambda b, t: (b, t, 0)),
            pl.BlockSpec((1, 1, LANES, LANES), lambda b, t: (b, t, 0, 0)),
        ],
        out_shape=[
            jax.ShapeDtypeStruct((bsz, hg, s), BF16),
            jax.ShapeDtypeStruct((bsz, s, da), BF16),
            jax.ShapeDtypeStruct((bsz, nq, LANES, LANES), F32),
        ],
        compiler_params=pltpu.CompilerParams(dimension_semantics=("parallel", "parallel")),
        name="q_proj",
    )(x, g.reshape(1, d), wqT, w_qg[:, da:].astype(BF16), c2r, sel_q, ones_q, head_sum.T)


def _attn_kernel(jlo_ref, qa_ref, qan_ref, ka_ref, vT_ref, gate_ref, o_ref, ready, m0, m1, acc0, acc1,
                 sa0, sa1, sb0, sb1, sc0, sc1, pa0, pa1, pb0, pb1):
    b = pl.program_id(0)
    hp = pl.program_id(1)
    i = pl.program_id(2)
    nq = pl.num_programs(2)
    tq = qa_ref.shape[2]
    tk = vT_ref.shape[3]
    base = (b * pl.num_programs(1) + hp) * nq
    j_lo = jlo_ref[base + i]
    j_lo_next = jlo_ref[base + jnp.minimum(i + 1, nq - 1)]
    m_refs, acc_refs = (m0, m1), (acc0, acc1)
    s_a, s_b, s_c = (sa0, sa1), (sb0, sb1), (sc0, sc1)
    p_a, p_b = (pa0, pa1), (pb0, pb1)

    def scores(q_ref, j, h):
        k_blk = ka_ref[0, pl.ds(pl.multiple_of(j * tk, tk), tk), h * HEAD_GROUP:(h + 1) * HEAD_GROUP]
        qT = q_ref[0, h * HEAD_GROUP:(h + 1) * HEAD_GROUP, :]
        return jnp.dot(k_blk, qT, preferred_element_type=F32)

    def values(j, h, p):
        vT = vT_ref[0, j, h * V_GROUP:(h + 1) * V_GROUP, :]
        return jnp.dot(vT, p, preferred_element_type=F32)

    def softmax_tile(h, s, masked):
        if masked:
            kv_pos = lax.broadcasted_iota(jnp.int32, (tk, tq), 0)
            q_pos = lax.broadcasted_iota(jnp.int32, (tk, tq), 1)
            s = jnp.where(kv_pos <= q_pos, s, NEG_BIG)
        m_old = m_refs[h][...]
        m_new = jnp.maximum(m_old, jnp.max(s, axis=0, keepdims=True))
        m_refs[h][...] = m_new
        return jnp.exp2(m_old - m_new), jnp.exp2(s - m_new).astype(BF16)

    def step(s_src, s_dst, p_prev, p_dst, j):
        for h in range(2):
            s_dst[h][...] = scores(qa_ref, j + 1, h)
        pv = None if p_prev is None else [values(j - 1, h, p_prev[h][...]) for h in range(2)]
        for h in range(2):
            alpha, p = softmax_tile(h, s_src[h][...], masked=False)
            p_dst[h][...] = p
            if pv is not None:
                acc_refs[h][...] = alpha * (acc_refs[h][...] + pv[h])

    def finish(s_src, p_prev, prefetch):
        if prefetch:
            for h in range(2):
                s_c[h][...] = scores(qan_ref, j_lo_next, h)
            ready[0] = (i + 1 < nq).astype(jnp.int32)
        else:
            ready[0] = jnp.int32(0)
        pv = None if p_prev is None else [values(i - 1, h, p_prev[h][...]) for h in range(2)]
        outs = []
        for h in range(2):
            alpha, p = softmax_tile(h, s_src[h][...], masked=True)
            acc = values(i, h, p)
            if pv is not None:
                acc = acc + alpha * (acc_refs[h][...] + pv[h])
            outs.append(acc[0:HEAD_DIM, :] / acc[HEAD_DIM:HEAD_DIM + 1, :])
        oT = jnp.concatenate(outs, axis=0)
        o_ref[0] = (oT.T * gate_ref[0].astype(F32)).astype(BF16)

    @pl.when(i == 0)
    def _():
        ready[0] = jnp.int32(0)

    @pl.when(ready[0] == 0)
    def _():
        for h in range(2):
            s_c[h][...] = scores(qa_ref, j_lo, h)

    for h in range(2):
        m_refs[h][...] = jnp.full((1, tq), NEG_BIG, F32)
        acc_refs[h][...] = jnp.zeros((V_GROUP, tq), F32)

    n_full = i - j_lo

    @pl.when(n_full == 0)
    def _():
        finish(s_c, None, prefetch=False)

    @pl.when(n_full > 0)
    def _():
        step(s_c, s_a, None, p_a, j_lo)
        rest = n_full - 1

        def body(n, carry):
            j = j_lo + 1 + 2 * n
            step(s_a, s_b, p_a, p_b, j)
            step(s_b, s_a, p_b, p_a, j + 1)
            return carry

        lax.fori_loop(0, rest // 2, body, 0)

        @pl.when(rest % 2 == 1)
        def _():
            step(s_a, s_b, p_a, p_b, i - 1)
            finish(s_b, p_b, prefetch=True)

        @pl.when(rest % 2 == 0)
        def _():
            finish(s_a, p_a, prefetch=True)


def _permute_heads(w_kvf, b_f, w_qg, w_o):
    da = N_HEADS * HEAD_DIM
    perm = jnp.argsort(b_f)

    def by_head(w, axis):
        shape = w.shape
        split = shape[:axis] + (N_HEADS, HEAD_DIM) + shape[axis + 1:]
        return jnp.take(w.reshape(split), perm, axis=axis).reshape(shape)

    w_kvf_p = jnp.concatenate([by_head(w_kvf[:, :da], 1), by_head(w_kvf[:, da:2 * da], 1), w_kvf[:, 2 * da:][:, perm]],
                              axis=1)
    w_qg_p = jnp.concatenate([by_head(w_qg[:, :da], 1), by_head(w_qg[:, da:], 1)], axis=1)
    return w_kvf_p, b_f[perm], w_qg_p, by_head(w_o, 0)


def _skip_table(qn, kn, c2r, tq):
    bsz, nq = qn.shape[0], qn.shape[1]
    qn = qn[:, :, :N_HEADS, 0].transpose(0, 2, 1) * NORM_SLACK
    kn = kn[:, :, 0, :N_HEADS].transpose(0, 2, 1) * NORM_SLACK
    c2 = c2r[:, :N_HEADS, :]
    c_first = c2[:, :, ::tq]
    c_last = c2[:, :, tq - 1::tq]
    upper = qn[..., :, None] * kn[..., None, :] + c_first[..., :, None] - c_last[..., None, :]
    lower = -(qn * kn)[..., :, None]
    skip = (upper - lower) < -SKIP_GAP_LOG2
    nk = skip.shape[-1]
    jlo = jnp.min(jnp.where(skip, nk, jnp.arange(nk, dtype=jnp.int32)), axis=-1)
    jlo = jnp.minimum(jlo, jnp.arange(nq, dtype=jnp.int32))
    jlo = jnp.min(jlo.reshape(bsz, N_HEADS // 2, 2, nq), axis=2)
    return jlo.reshape(-1).astype(jnp.int32)


def _attention(jlo, qa, ka, vT, gate, *, tq):
    bsz, hg, s = qa.shape
    nk, tk = vT.shape[1], vT.shape[3]
    da = N_HEADS * HEAD_DIM
    assert tq == tk
    pair = 2 * HEAD_GROUP
    hw = 2 * HEAD_DIM
    nq = s // tq
    grid_spec = pltpu.PrefetchScalarGridSpec(
        num_scalar_prefetch=1,
        grid=(bsz, N_HEADS // 2, nq),
        in_specs=[
            pl.BlockSpec((1, pair, tq), lambda b, hp, i, jlo: (b, hp, i)),
            pl.BlockSpec((1, pair, tq), lambda b, hp, i, jlo: (b, hp, jnp.minimum(i + 1, nq - 1))),
            pl.BlockSpec((1, s, pair), lambda b, hp, i, jlo: (b, 0, hp)),
            pl.BlockSpec((1, nk, 2 * V_GROUP, tk), lambda b, hp, i, jlo: (b, 0, hp, 0)),
            pl.BlockSpec((1, tq, hw), lambda b, hp, i, jlo: (b, i, hp)),
        ],
        out_specs=pl.BlockSpec((1, tq, hw), lambda b, hp, i, jlo: (b, i, hp)),
        scratch_shapes=([pltpu.SMEM((1,), jnp.int32)]
                        + [pltpu.VMEM((1, tq), F32)] * 2 + [pltpu.VMEM((V_GROUP, tq), F32)] * 2
                        + [pltpu.VMEM((tk, tq), F32)] * 6 + [pltpu.VMEM((tk, tq), BF16)] * 4),
    )
    return pl.pallas_call(
        _attn_kernel,
        grid_spec=grid_spec,
        out_shape=jax.ShapeDtypeStruct((bsz, s, da), BF16),
        compiler_params=pltpu.CompilerParams(dimension_semantics=("parallel", "parallel", "arbitrary")),
        name="fox_attention",
    )(jlo, qa, qa, ka, vT, gate)


def _pack_bf16_pairs(x):
    half = x.shape[1] // 2
    lo = lax.bitcast_convert_type(x[:, :half].astype(BF16).astype(F32), jnp.uint32)
    hi = lax.bitcast_convert_type(x[:, half:].astype(BF16).astype(F32), jnp.uint32)
    word = (lo >> 16) | (hi & jnp.uint32(0xFFFF0000))
    return lax.bitcast_convert_type(word, jnp.int32)


def _unpack_bf16_pairs(w):
    u = lax.bitcast_convert_type(w, jnp.uint32)
    lo = lax.bitcast_convert_type(u << 16, F32)
    hi = lax.bitcast_convert_type(u & jnp.uint32(0xFFFF0000), F32)
    return jnp.concatenate([lo, hi], axis=1)


def _store_planes(ref, words):
    q = words.shape[1] // 2
    ref[0] = words[:, :q]
    ref[1] = words[:, q:]


def _load_planes(ref, base=0):
    return jnp.concatenate([ref[base], ref[base + 1]], axis=1)


def _router_kernel(x_ref, o_ref, wo_ref, g_ref, rtT_ref, x2_ref, hnp_ref, route_ref):
    tm = x_ref.shape[0]
    x2 = x_ref[...] + jnp.dot(o_ref[...], wo_ref[...], preferred_element_type=F32)
    x2_ref[...] = x2
    hn = _rms(x2, g_ref[...])
    _store_planes(hnp_ref, _pack_bf16_pairs(hn))
    h0, h1, _ = _split3(hn)
    r0, r1, _ = _split3(rtT_ref[...])
    lt = lax.dot_general(jnp.concatenate([r0, r1, r0], axis=1), jnp.concatenate([h0, h0, h1], axis=1), NT_DIMS,
                         preferred_element_type=F32)[:N_EXPERTS]
    row = lax.broadcasted_iota(jnp.int32, (N_EXPERTS, tm), 0)
    m1 = jnp.max(lt, axis=0, keepdims=True)
    i1 = jnp.min(jnp.where(lt == m1, row, N_EXPERTS), axis=0, keepdims=True)
    rest = jnp.where(row == i1, -jnp.inf, lt)
    m2 = jnp.max(rest, axis=0, keepdims=True)
    i2 = jnp.min(jnp.where(rest == m2, row, N_EXPERTS), axis=0, keepdims=True)
    e2 = jnp.exp(m2 - m1)
    w1 = 1.0 / (1.0 + e2)
    w2 = e2 / (1.0 + e2)
    route_ref[...] = jnp.where(row == 0, i1.astype(F32), jnp.where(row == 1, i2.astype(F32),
                               jnp.where(row == 2, w1, jnp.where(row == 3, w2, 0.0))))


def _router(x2d, o2d, w_o, g, router, *, tm):
    t, d = x2d.shape
    da = o2d.shape[1]
    ne = router.shape[1]
    rtT = jnp.pad(router.T, ((0, 2 * SUBLANES - ne), (0, 0)))
    return pl.pallas_call(
        _router_kernel,
        grid=(t // tm,),
        in_specs=[
            pl.BlockSpec((tm, d), lambda i: (i, 0)),
            pl.BlockSpec((tm, da), lambda i: (i, 0)),
            pl.BlockSpec((da, d), lambda i: (0, 0)),
            pl.BlockSpec((1, d), lambda i: (0, 0)),
            pl.BlockSpec((2 * SUBLANES, d), lambda i: (0, 0)),
        ],
        out_specs=[pl.BlockSpec((tm, d), lambda i: (i, 0)), pl.BlockSpec((2, tm, d // 4), lambda i: (0, i, 0)),
                   pl.BlockSpec((ne, tm), lambda i: (0, i))],
        out_shape=[jax.ShapeDtypeStruct((t, d), F32), jax.ShapeDtypeStruct((2, t, d // 4), jnp.int32),
                   jax.ShapeDtypeStruct((ne, t), F32)],
        compiler_params=pltpu.CompilerParams(dimension_semantics=("parallel",)),
        name="attn_out_moe_router",
    )(x2d, o2d, w_o.astype(BF16), g.reshape(1, d), rtT)


def _routing_tables(route, ts):
    ne = N_EXPERTS
    t = route.shape[1]
    e_pair = route[0:2].astype(jnp.int32).reshape(-1)
    n_tiles = (2 * t) // ts + ne
    n_slots = n_tiles * ts
    onehot = (e_pair[:, None] == jnp.arange(ne, dtype=jnp.int32)[None, :]).astype(jnp.int32)
    csum = jnp.cumsum(onehot, axis=0)
    rank = jnp.sum(onehot * csum, axis=1) - 1
    tiles = (csum[-1] + ts - 1) // ts
    tile_end = jnp.cumsum(tiles)
    tile_start = tile_end - tiles
    slot = (jnp.sum(onehot * tile_start[None, :], axis=1) * ts + rank).astype(jnp.int32)
    n_used = tile_end[-1]
    tile_ids = jnp.arange(n_tiles, dtype=jnp.int32)
    tile_expert = jnp.sum((tile_ids[:, None] >= tile_end[None, :]).astype(jnp.int32), axis=1)
    last_expert = jnp.sum(((n_used - 1) >= tile_end).astype(jnp.int32))
    tile_expert = jnp.minimum(tile_expert, last_expert).astype(jnp.int32)
    plane_rows = slot.reshape(2, 1, t) + (jnp.arange(2, dtype=jnp.int32) * n_slots).reshape(1, 2, 1)
    return plane_rows.reshape(-1), tile_expert, n_used.reshape(1).astype(jnp.int32), n_slots


def _sc_mesh():
    return plsc.VectorSubcoreMesh(core_axis_name="core", subcore_axis_name="subcore")


def _scatter_rows(table, idx, n_out):
    m = idx.shape[0]
    n, d = table.shape
    assert m % SC_ROW_WINDOW == 0 and n % SC_ROW_WINDOW == 0
    n_src_blocks = n // SC_ROW_WINDOW

    @pl.kernel(out_type=jax.ShapeDtypeStruct((n_out, d), table.dtype), mesh=_sc_mesh(), scratch_types=[])
    def scatter_kernel(x_hbm, i_hbm, o_hbm):
        def body(x_vmem, i_vmem):
            pltpu.sync_copy(x_vmem, o_hbm.at[i_vmem.at[0]])

        pltpu.emit_pipeline(
            body,
            grid=(m // SC_ROW_WINDOW,),
            in_specs=[pl.BlockSpec((SC_ROW_WINDOW, d), index_map=lambda i: (i % n_src_blocks, 0)),
                      pl.BlockSpec((1, SC_ROW_WINDOW), index_map=lambda i: (0, i))],
            out_specs=[],
            core_axis_name=("core", "subcore"),
            dimension_semantics=(pltpu.PARALLEL,),
        )(x_hbm, i_hbm)

    return scatter_kernel(table, idx.reshape(1, m))


def _gather_rows(table, idx):
    m = idx.shape[0]
    d = table.shape[1]
    assert m % SC_ROW_WINDOW == 0

    @pl.kernel(out_type=jax.ShapeDtypeStruct((m, d), table.dtype), mesh=_sc_mesh())
    def gather_kernel(x_hbm, i_hbm, o_hbm):
        def body(i_vmem, o_vmem):
            pltpu.sync_copy(x_hbm.at[i_vmem.at[0]], o_vmem)

        pltpu.emit_pipeline(
            body,
            grid=(m // SC_ROW_WINDOW,),
            in_specs=[pl.BlockSpec((1, SC_ROW_WINDOW), index_map=lambda i: (0, i))],
            out_specs=[pl.BlockSpec((SC_ROW_WINDOW, d), index_map=lambda i: (i, 0))],
            core_axis_name=("core", "subcore"),
            dimension_semantics=(pltpu.PARALLEL,),
        )(i_hbm, o_hbm)

    return gather_kernel(table, idx.reshape(1, m))


def _expert_kernel(nk, te_ref, nu_ref, xs_ref, wg_ref, wu_ref, wd_ref, ys_ref, x_sc, acc_ref):
    i = pl.program_id(0)
    k = pl.program_id(1)
    used = i < nu_ref[0]

    def first():
        x_sc[...] = _unpack_bf16_pairs(_load_planes(xs_ref)).astype(BF16)

    def last(total):
        _store_planes(ys_ref, _pack_bf16_pairs(total))

    @pl.when(used)
    def _():
        _chunked_swiglu(k, nk, lambda: _swiglu_chunk(x_sc[...], wg_ref.at[0], wu_ref.at[0], wd_ref.at[0]), acc_ref,
                        first, last)

    @pl.when(jnp.logical_not(used) & (k == nk - 1))
    def _():
        ys_ref[...] = jnp.zeros_like(ys_ref)


def _experts(xs, tile_expert, n_used, w_gu, w_down, *, ts, tf):
    _, n_slots, quarter = xs.shape
    d = 4 * quarter
    dexp = w_down.shape[1]
    nk = dexp // tf
    w_gu = w_gu.astype(BF16)
    grid_spec = pltpu.PrefetchScalarGridSpec(
        num_scalar_prefetch=2,
        grid=(n_slots // ts, nk),
        in_specs=[
            pl.BlockSpec((2, ts, quarter), lambda i, k, te, nu: (0, i, 0)),
            pl.BlockSpec((1, d, tf), lambda i, k, te, nu: (te[i], 0, jnp.where(i < nu[0], k, 0))),
            pl.BlockSpec((1, d, tf), lambda i, k, te, nu: (te[i], 0, nk + jnp.where(i < nu[0], k, 0))),
            pl.BlockSpec((1, tf, d), lambda i, k, te, nu: (te[i], jnp.where(i < nu[0], k, 0), 0)),
        ],
        out_specs=pl.BlockSpec((2, ts, quarter), lambda i, k, te, nu: (0, i, 0)),
        scratch_shapes=[pltpu.VMEM((ts, d), BF16), pltpu.VMEM((ts, d), F32)],
    )
    return pl.pallas_call(
        functools.partial(_expert_kernel, nk),
        grid_spec=grid_spec,
        out_shape=jax.ShapeDtypeStruct((2, n_slots, quarter), jnp.int32),
        compiler_params=pltpu.CompilerParams(dimension_semantics=("arbitrary", "arbitrary")),
        name="moe_experts",
    )(tile_expert, n_used, xs, w_gu, w_gu, w_down.astype(BF16))


def _combine_kernel(x_ref, y_ref, rt_ref, gf_ref, o_ref):
    rt = rt_ref[...]
    moe = (rt[:, 2:3] * _unpack_bf16_pairs(_load_planes(y_ref, 0))
           + rt[:, 3:4] * _unpack_bf16_pairs(_load_planes(y_ref, 2)))
    o_ref[...] = _rms(x_ref[...] + moe, gf_ref[...])


def _combine(x2d, y4, route_t, g_final, *, tm):
    t, d = x2d.shape
    ne = route_t.shape[1]
    return pl.pallas_call(
        _combine_kernel,
        grid=(t // tm,),
        in_specs=[
            pl.BlockSpec((tm, d), lambda i: (i, 0)),
            pl.BlockSpec((4, tm, d // 4), lambda i: (0, i, 0)),
            pl.BlockSpec((tm, ne), lambda i: (i, 0)),
            pl.BlockSpec((1, d), lambda i: (0, 0)),
        ],
        out_specs=pl.BlockSpec((tm, d), lambda i: (i, 0)),
        out_shape=jax.ShapeDtypeStruct((t, d), F32),
        compiler_params=pltpu.CompilerParams(dimension_semantics=("parallel",)),
        name="moe_combine",
    )(x2d, y4, route_t, g_final.reshape(1, d))


def _attn_out_moe(x1, o2d, w_o, g, router, w_gu, w_down, g_final, *, tm, ts, tf):
    t, d = x1.shape
    q = d // 4
    x2d, hn_planes, route = _router(x1, o2d, w_o, g, router, tm=tm)
    plane_rows, tile_expert, n_used, n_slots = _routing_tables(route, ts)
    xs = _scatter_rows(hn_planes.reshape(2 * t, q), plane_rows, 2 * n_slots)
    ys = _experts(xs.reshape(2, n_slots, q), tile_expert, n_used, w_gu, w_down, ts=ts, tf=tf)
    y4 = _gather_rows(ys.reshape(2 * n_slots, q), plane_rows)
    return _combine(x2d, y4.reshape(4, t, q), route.T, g_final, tm=tm)


ROW_TILE = 512


def _tile(n):
    t = min(n, ROW_TILE)
    assert n % t == 0 and t % (2 * LANES) == 0
    return t


def kernel(x, norm_g, lru_w_in, lru_conv_w, lru_conv_b, lru_gate_w, lru_gate_b, lru_lambda, lru_w_out,
           kv_norm_g, w_kvf, b_f, fox_w_qg, fox_w_o, mlp_w_gu, mlp_w_down, moe_router, moe_w_gu,
           moe_w_down, final_norm_g):
    bsz, s, d = x.shape
    t = bsz * s
    tm = _tile(t)
    ts = _tile(s)

    x = _lru_mlp(x, norm_g[0, 0], lru_w_in[0], lru_conv_w[0], lru_conv_b[0], lru_gate_w[0], lru_gate_b[0],
                 lru_lambda[0], lru_w_out[0], norm_g[0, 1], mlp_w_gu[0], mlp_w_down[0], tm=ts)

    consts = _head_selectors()
    w_kvf_p, b_f_p, w_qg_p, w_o_p = _permute_heads(w_kvf, b_f, fox_w_qg[0], fox_w_o[0])
    ka, vT, c2r, kn = _kv_proj(x, kv_norm_g, w_kvf_p, b_f_p, consts, tk=ts)
    qa, gate, qn = _q_proj(x, norm_g[1, 0], w_qg_p, c2r, consts, tq=ts)
    jlo = _skip_table(qn, kn, c2r, ts)
    o = _attention(jlo, qa, ka, vT, gate, tq=ts)
    dexp = moe_w_down.shape[2]
    out = _attn_out_moe(x.reshape(t, d), o.reshape(t, -1), w_o_p, norm_g[1, 1], moe_router[0], moe_w_gu[0],
                        moe_w_down[0], final_norm_g, tm=tm, ts=tm, tf=dexp // 2)
    return out.reshape(bsz, s, d)
```

```python
import functools
import math

import jax
import jax.numpy as jnp
from jax import lax
from jax.experimental import pallas as pl
from jax.experimental.pallas import tpu as pltpu
from jax.experimental.pallas import tpu_sc as plsc

EPS = 1e-6
LRU_C = 8.0
N_LRU_BLOCKS = 8
CONV_W = 4
N_HEADS = 16
HEAD_DIM = 64
N_EXPERTS = 8
LANES = 128
SUBLANES = 8
HEAD_GROUP = 128
BIAS_Q = HEAD_DIM
BIAS_K = HEAD_DIM + 3
V_GROUP = 80
NEG_BIG = -1e30
LOG2E = math.log2(math.e)
SKIP_GAP_LOG2 = 152.0
NORM_SLACK = 1.01
SC_ROW_WINDOW = 128
Q_PER_STEP = 2

F32 = jnp.float32
BF16 = jnp.bfloat16
NT_DIMS = (((1,), (1,)), ((), ()))


def _rms(x, g):
    ms = jnp.mean(x * x, axis=-1, keepdims=True)
    return x * lax.rsqrt(ms + EPS) * g


def _log_sigmoid(x):
    return jnp.minimum(x, 0.0) - jnp.log1p(jnp.exp(-jnp.abs(x)))


def _gelu_tanh(x):
    c = 0.7978845608028654
    return 0.5 * x * (1.0 + jnp.tanh(c * (x + 0.044715 * (x * x * x))))


def _silu(x):
    return x * jax.nn.sigmoid(x)


def _split3(x):
    p0 = x.astype(BF16)
    r1 = x - p0.astype(F32)
    p1 = r1.astype(BF16)
    p2 = (r1 - p1.astype(F32)).astype(BF16)
    return p0, p1, p2


def _const_spec(shape, n_grid):
    zeros = (0,) * len(shape)
    if n_grid == 1:
        return pl.BlockSpec(shape, lambda i: zeros)
    if n_grid == 2:
        return pl.BlockSpec(shape, lambda i, j: zeros)
    return pl.BlockSpec(shape, lambda i, j, k: zeros)


def _lru_in_proj(x, g_ref, win_ref):
    d = x.shape[1]
    xn = _rms(x, g_ref[...]).astype(BF16)
    xg = jnp.dot(xn, win_ref[...], preferred_element_type=F32)
    return xg[:, :d], xg[:, d:]


def _lru_conv_gates(xb, cw_ref, cb_ref, gw_ref, xbuf):
    tt, d = xb.shape
    bw = d // N_LRU_BLOCKS
    xbuf[SUBLANES:SUBLANES + tt, :] = xb
    cw = cw_ref[...]
    xc = cb_ref[...] + cw[CONV_W - 1:CONV_W, :] * xb
    for j in range(CONV_W - 1):
        xc = xc + cw[j:j + 1, :] * xbuf[pl.ds(SUBLANES - (CONV_W - 1) + j, tt), :]
    xbuf[0:SUBLANES, :] = xbuf[tt:tt + SUBLANES, :]

    xcb = xc.astype(BF16)
    rl, il = [], []
    for n in range(N_LRU_BLOCKS):
        gl = jnp.dot(xcb[:, n * bw:(n + 1) * bw], gw_ref[n], preferred_element_type=F32)
        rl.append(gl[:, :bw])
        il.append(gl[:, bw:])
    return xc, jnp.concatenate(rl, axis=1), jnp.concatenate(il, axis=1)


def _lru_recurrence(xc, r_logit, i_logit, gbr, gb_ref, lam_ref, hcar):
    tt = xc.shape[0]
    gb = gb_ref[...]
    r = jax.nn.sigmoid(r_logit + gb[0:1, :])
    i = jax.nn.sigmoid(i_logit + gb[1:2, :])
    log_a = LRU_C * r * _log_sigmoid(lam_ref[...])
    a = jnp.exp(log_a)
    b = jnp.sqrt((1.0 - a) * (1.0 + a)) * (i * xc)

    row = lax.broadcasted_iota(jnp.int32, (tt, 1), 0)
    sh = 1
    while sh < tt:
        if sh < SUBLANES:
            keep = row >= sh
            a_sh = jnp.where(keep, pltpu.roll(a, sh, 0), 1.0)
            b_sh = jnp.where(keep, pltpu.roll(b, sh, 0), 0.0)
            b = a * b_sh + b
            a = a * a_sh
        else:
            b = jnp.concatenate([b[:sh], a[sh:] * b[:tt - sh] + b[sh:]], axis=0)
            a = jnp.concatenate([a[:sh], a[sh:] * a[:tt - sh]], axis=0)
        sh *= 2
    h = b + a * hcar[...]
    hcar[...] = h[tt - 1:tt, :]
    return (_gelu_tanh(gbr) * h).astype(BF16)


def _lru_mlp_kernel(x_ref, g0_ref, win_ref, cw_ref, cb_ref, gw_ref, gb_ref, lam_ref, wout_ref,
                    g1_ref, wg_ref, wu_ref, wd_ref, o_ref, xbuf, hcar, x1_sc, hn_sc, acc_ref):
    t = pl.program_id(1)
    k = pl.program_id(2)
    half = x_ref.shape[1]
    tf = wg_ref.shape[1]
    mid = (tf // (2 * LANES)) * LANES
    slot = t % 2
    prev = 1 - slot

    @pl.when((t == 0) & (k == 0))
    def _():
        xbuf[0:SUBLANES, :] = jnp.zeros((SUBLANES, xbuf.shape[1]), F32)
        hcar[...] = jnp.zeros_like(hcar)
        x1_sc[...] = jnp.zeros_like(x1_sc)
        hn_sc[...] = jnp.zeros_like(hn_sc)

    def gate_up(hn, lo, hi):
        return (jnp.dot(hn, wg_ref[:, lo:hi], preferred_element_type=F32),
                jnp.dot(hn, wu_ref[:, lo:hi], preferred_element_type=F32))

    def both():
        x = x_ref[0]
        hn = hn_sc[prev]
        xb, gbr = _lru_in_proj(x, g0_ref, win_ref)
        g_a, u_a = gate_up(hn, 0, mid)
        xc, r_logit, i_logit = _lru_conv_gates(xb, cw_ref, cb_ref, gw_ref, xbuf)
        g_b, u_b = gate_up(hn, mid, tf)
        h = jnp.concatenate([(_silu(g_a) * u_a).astype(BF16), (_silu(g_b) * u_b).astype(BF16)], axis=1)
        y_mlp = jnp.dot(h, wd_ref[...], preferred_element_type=F32)
        y_lru = _lru_recurrence(xc, r_logit, i_logit, gbr, gb_ref, lam_ref, hcar)
        x1 = x + jnp.dot(y_lru, wout_ref[...], preferred_element_type=F32)
        rows = pl.ds(pl.multiple_of(k * half, half), half)
        x1_sc[slot, rows, :] = x1
        hn_sc[slot, rows, :] = _rms(x1, g1_ref[...]).astype(BF16)
        return y_mlp

    @pl.when(k == 0)
    def _():
        acc_ref[...] = both()

    @pl.when(k == 1)
    def _():
        y = both()
        o_ref[0] = x1_sc[prev] + acc_ref[...] + y


def _lru_mlp(x, g0, w_in, conv_w, conv_b, gate_w, gate_b, lam, w_out, g1, w_gu, w_down, *, tm):
    bsz, s, d = x.shape
    bw = d // N_LRU_BLOCKS
    dff = w_down.shape[0]
    tf = dff // 2
    half = tm // 2
    nt = s // tm
    n_half = s // half
    gw = jnp.concatenate([gate_w[0], gate_w[1]], axis=-1).astype(BF16)
    w_gu = w_gu.astype(BF16)

    def const(*shape):
        return pl.BlockSpec(shape, lambda b, t, k: (0,) * len(shape), pipeline_mode=pl.Buffered(1))

    return pl.pallas_call(
        _lru_mlp_kernel,
        grid=(bsz, nt + 1, 2),
        in_specs=[
            pl.BlockSpec((1, half, d), lambda b, t, k: (b, jnp.minimum(2 * t + k, n_half - 1), 0)),
            const(1, d), const(d, 2 * d), const(CONV_W, d), const(1, d),
            const(N_LRU_BLOCKS, bw, 2 * bw), const(2, d), const(1, d), const(d, d),
            const(1, d),
            pl.BlockSpec((d, tf), lambda b, t, k: (0, k)),
            pl.BlockSpec((d, tf), lambda b, t, k: (0, 2 + k)),
            pl.BlockSpec((tf, d), lambda b, t, k: (k, 0)),
        ],
        out_specs=pl.BlockSpec((1, tm, d), lambda b, t, k: (b, jnp.maximum(t - 1, 0), 0)),
        out_shape=jax.ShapeDtypeStruct((bsz, s, d), F32),
        scratch_shapes=[pltpu.VMEM((half + 2 * SUBLANES, d), F32), pltpu.VMEM((1, d), F32),
                        pltpu.VMEM((2, tm, d), F32), pltpu.VMEM((2, tm, d), BF16), pltpu.VMEM((tm, d), F32)],
        compiler_params=pltpu.CompilerParams(dimension_semantics=("parallel", "arbitrary", "arbitrary")),
        name="lru_mlp",
    )(x, g0.reshape(1, d), w_in.astype(BF16), conv_w, conv_b.reshape(1, d), gw, gate_b, lam.reshape(1, d),
      w_out.astype(BF16), g1.reshape(1, d), w_gu, w_gu, w_down.astype(BF16))


def _swiglu_chunk(x, wg, wu, wd):
    gate = jnp.dot(x, wg[...], preferred_element_type=F32)
    up = jnp.dot(x, wu[...], preferred_element_type=F32)
    h = (_silu(gate) * up).astype(BF16)
    return jnp.dot(h, wd[...], preferred_element_type=F32)


def _chunked_swiglu(k, nk, chunk, acc_ref, first, last):
    if nk == 1:
        first()
        last(chunk())
        return

    @pl.when(k == 0)
    def _():
        first()
        acc_ref[...] = chunk()

    if nk > 2:
        @pl.when((k > 0) & (k < nk - 1))
        def _():
            acc_ref[...] += chunk()

    @pl.when(k == nk - 1)
    def _():
        last(acc_ref[...] + chunk())


def _head_selectors():
    hg = N_HEADS * HEAD_GROUP
    head = jnp.arange(hg) // HEAD_GROUP
    lane = jnp.arange(hg) % HEAD_GROUP
    col = jnp.arange(LANES)
    def sel(base):
        return ((col[:, None] % N_HEADS == head[None, :]) & (col[:, None] < 3 * N_HEADS)
                & (lane[None, :] == base + col[:, None] // N_HEADS)).astype(BF16)
    sel_k = -sel(BIAS_K)
    sel_q = sel(BIAS_Q).T
    ones_k = ((lane >= BIAS_Q) & (lane < BIAS_Q + 3)).astype(F32).reshape(1, hg)
    ones_q = ((lane >= BIAS_K) & (lane < BIAS_K + 3)).astype(F32).reshape(hg, 1)
    feat_head = jnp.arange(N_HEADS * HEAD_DIM) // HEAD_DIM
    head_sum = (col[None, :] == feat_head[:, None]).astype(BF16)
    return sel_k, sel_q, ones_k, ones_q, head_sum


def _kv_kernel(x_ref, g_ref, wk_ref, wvT_ref, wf_ref, wfT_ref, bfr_ref, bfc_ref, selk_ref, onesk_ref,
               hsum_ref, ka_ref, vT_ref, c2r_ref, kn_ref, car_r, car_c):
    t = pl.program_id(1)
    tk = x_ref.shape[1]

    @pl.when(t == 0)
    def _():
        car_r[...] = jnp.zeros_like(car_r)
        car_c[...] = jnp.zeros_like(car_c)

    xn = _rms(x_ref[0], g_ref[...]).astype(BF16)
    vr = lax.dot_general(wvT_ref[...], xn, NT_DIMS, preferred_element_type=F32)
    tail_row = lax.broadcasted_iota(jnp.int32, (V_GROUP - HEAD_DIM, tk), 0)
    tail = jnp.where(tail_row == 0, 1.0, 0.0)
    vT = jnp.concatenate([blk for h in range(N_HEADS) for blk in (vr[h * HEAD_DIM:(h + 1) * HEAD_DIM], tail)], axis=0)
    vT_ref[0, 0] = vT.astype(BF16)
    kr = jnp.dot(xn, wk_ref[...], preferred_element_type=F32)

    kss = jnp.dot((kr * kr).astype(BF16), hsum_ref[...], preferred_element_type=F32)
    kn = jnp.sqrt(jnp.max(kss, axis=0, keepdims=True))
    kn_ref[0, 0] = jnp.broadcast_to(kn, (SUBLANES, LANES))

    ri = lax.broadcasted_iota(jnp.int32, (tk, tk), 0)
    ci = lax.broadcasted_iota(jnp.int32, (tk, tk), 1)
    upper = (ri <= ci).astype(BF16)
    lower = (ci <= ri).astype(BF16)

    lf_c = _log_sigmoid(jnp.dot(xn, wf_ref[...], preferred_element_type=F32) + bfr_ref[...])
    cs3 = jnp.dot(lower, jnp.concatenate(_split3(lf_c), axis=1), preferred_element_type=F32)
    cs_c = car_c[...] + (cs3[:, :LANES] + cs3[:, LANES:2 * LANES] + cs3[:, 2 * LANES:])
    car_c[...] = cs_c[tk - 1:tk, :]
    p0, p1, p2 = (p.astype(F32) for p in _split3(cs_c * LOG2E))
    lane = lax.broadcasted_iota(jnp.int32, (tk, LANES), 1)
    groups = []
    for pair in range(N_HEADS // 2):
        two = kr[:, pair * LANES:(pair + 1) * LANES]
        groups.append(jnp.where(lane < HEAD_DIM, two, 0.0))
        groups.append(jnp.where(lane < HEAD_DIM, pltpu.roll(two, HEAD_DIM, 1), 0.0))
    kf = jnp.concatenate(groups, axis=1)
    packed = jnp.where(lane < N_HEADS, p0,
                       jnp.where(lane < 2 * N_HEADS, pltpu.roll(p1, N_HEADS, 1),
                                 jnp.where(lane < 3 * N_HEADS, pltpu.roll(p2, 2 * N_HEADS, 1), 0.0)))
    bias = jnp.dot(packed.astype(BF16), selk_ref[...], preferred_element_type=F32)
    ka_ref[0] = (kf + bias + onesk_ref[...]).astype(BF16)

    lf_r = _log_sigmoid(lax.dot_general(wfT_ref[...], xn, NT_DIMS, preferred_element_type=F32) + bfc_ref[...])
    cs3 = jnp.dot(jnp.concatenate(_split3(lf_r), axis=0), upper, preferred_element_type=F32)
    cs_r = car_r[...] + (cs3[:LANES] + cs3[LANES:2 * LANES] + cs3[2 * LANES:])
    car_r[...] = cs_r[:, tk - 1:tk]
    c2r_ref[0] = cs_r * LOG2E


def _kv_proj(x, g, w_kvf, b_f, consts, *, tk):
    bsz, s, d = x.shape
    da = N_HEADS * HEAD_DIM
    hg = N_HEADS * HEAD_GROUP
    dv = N_HEADS * V_GROUP
    nk = s // tk
    sel_k, _, ones_k, _, head_sum = consts
    wk = w_kvf[:, :da].astype(BF16)
    wvT = w_kvf[:, da:2 * da].T.astype(BF16)
    wf = jnp.pad(w_kvf[:, 2 * da:], ((0, 0), (0, LANES - N_HEADS))).astype(BF16)
    bfr = jnp.pad(b_f, (0, LANES - N_HEADS)).reshape(1, LANES)
    const = lambda *shape: _const_spec(shape, 2)
    return pl.pallas_call(
        _kv_kernel,
        grid=(bsz, nk),
        in_specs=[
            pl.BlockSpec((1, tk, d), lambda b, t: (b, t, 0)),
            const(1, d), const(d, da), const(da, d), const(d, LANES), const(LANES, d),
            const(1, LANES), const(LANES, 1), const(LANES, hg), const(1, hg), const(da, LANES),
        ],
        out_specs=[
            pl.BlockSpec((1, tk, hg), lambda b, t: (b, t, 0)),
            pl.BlockSpec((1, 1, dv, tk), lambda b, t: (b, t, 0, 0)),
            pl.BlockSpec((1, LANES, tk), lambda b, t: (b, 0, t)),
            pl.BlockSpec((1, 1, SUBLANES, LANES), lambda b, t: (b, t, 0, 0)),
        ],
        out_shape=[
            jax.ShapeDtypeStruct((bsz, s, hg), BF16),
            jax.ShapeDtypeStruct((bsz, nk, dv, tk), BF16),
            jax.ShapeDtypeStruct((bsz, LANES, s), F32),
            jax.ShapeDtypeStruct((bsz, nk, SUBLANES, LANES), F32),
        ],
        scratch_shapes=[pltpu.VMEM((LANES, 1), F32), pltpu.VMEM((1, LANES), F32)],
        compiler_params=pltpu.CompilerParams(dimension_semantics=("parallel", "arbitrary")),
        name="kv_proj",
    )(x, g.reshape(1, d), wk, wvT, wf, wf.T, bfr, bfr.reshape(LANES, 1), sel_k, ones_k, head_sum)


def _q_kernel(x_ref, g_ref, wqT_ref, wg_ref, c2r_ref, selq_ref, onesq_ref, hsumT_ref, qa_ref, gate_ref, qn_ref):
    xn = _rms(x_ref[0], g_ref[...]).astype(BF16)
    gate_ref[0] = jax.nn.sigmoid(jnp.dot(xn, wg_ref[...], preferred_element_type=F32)).astype(BF16)
    qr = lax.dot_general(wqT_ref[...], xn, NT_DIMS, preferred_element_type=F32) * (LOG2E * HEAD_DIM ** -0.5)

    qss = jnp.dot(hsumT_ref[...], (qr * qr).astype(BF16), preferred_element_type=F32)
    qn = jnp.sqrt(jnp.max(qss, axis=1, keepdims=True))
    qn_ref[0, 0] = jnp.broadcast_to(qn, (LANES, LANES))

    tq = c2r_ref.shape[2]
    gap = jnp.zeros((HEAD_GROUP - HEAD_DIM, tq), F32)
    qf = jnp.concatenate([blk for h in range(N_HEADS) for blk in (qr[h * HEAD_DIM:(h + 1) * HEAD_DIM], gap)], axis=0)
    packed = jnp.concatenate([p[:N_HEADS] for p in _split3(c2r_ref[0])]
                             + [jnp.zeros((LANES - 3 * N_HEADS, tq), BF16)], axis=0)
    bias = jnp.dot(selq_ref[...], packed, preferred_element_type=F32)
    qa_ref[0] = (qf + bias + onesq_ref[...]).astype(BF16)


def _q_proj(x, g, w_qg, c2r, consts, *, tq):
    bsz, s, d = x.shape
    da = N_HEADS * HEAD_DIM
    hg = N_HEADS * HEAD_GROUP
    nq = s // tq
    _, sel_q, _, ones_q, head_sum = consts
    wqT = w_qg[:, :da].T.astype(BF16)
    const = lambda *shape: _const_spec(shape, 2)
    return pl.pallas_call(
        _q_kernel,
        grid=(bsz, nq),
        in_specs=[
            pl.BlockSpec((1, tq, d), lambda b, t: (b, t, 0)),
            const(1, d), const(da, d), const(d, da),
            pl.BlockSpec((1, LANES, tq), lambda b, t: (b, 0, t)),
            const(hg, LANES), const(hg, 1), const(LANES, da),
        ],
        out_specs=[
            pl.BlockSpec((1, hg, tq), lambda b, t: (b, 0, t)),
            pl.BlockSpec((1, tq, da), lambda b, t: (b, t, 0)),
            pl.BlockSpec((1, 1, LANES, LANES), lambda b, t: (b, t, 0, 0)),
        ],
        out_shape=[
            jax.ShapeDtypeStruct((bsz, hg, s), BF16),
            jax.ShapeDtypeStruct((bsz, s, da), BF16),
            jax.ShapeDtypeStruct((bsz, nq, LANES, LANES), F32),
        ],
        compiler_params=pltpu.CompilerParams(dimension_semantics=("parallel", "parallel")),
        name="q_proj",
    )(x, g.reshape(1, d), wqT, w_qg[:, da:].astype(BF16), c2r, sel_q, ones_q, head_sum.T)


def _attn_kernel(*refs):
    for sub in range(Q_PER_STEP):
        _attn_q_block(sub, *refs)


def _attn_q_block(sub, jlo_ref, qa_ref, qan_ref, ka_ref, vT_ref, gate_ref, o_ref, ready, m0, m1, acc0, acc1,
                  sa0, sa1, sb0, sb1, sc0, sc1, pa0, pa1, pb0, pb1):
    b = pl.program_id(0)
    hp = pl.program_id(1)
    i = pl.program_id(2) * Q_PER_STEP + sub
    nq = pl.num_programs(2) * Q_PER_STEP
    tq = qa_ref.shape[2] // Q_PER_STEP
    tk = vT_ref.shape[3]
    rows = slice(sub * tq, (sub + 1) * tq)
    next_q, next_sub = (qa_ref, sub + 1) if sub + 1 < Q_PER_STEP else (qan_ref, 0)
    base = (b * pl.num_programs(1) + hp) * nq
    j_lo = jlo_ref[base + i]
    j_lo_next = jlo_ref[base + jnp.minimum(i + 1, nq - 1)]
    m_refs, acc_refs = (m0, m1), (acc0, acc1)
    s_a, s_b, s_c = (sa0, sa1), (sb0, sb1), (sc0, sc1)
    p_a, p_b = (pa0, pa1), (pb0, pb1)

    def scores(q_ref, q_sub, j, h):
        k_blk = ka_ref[0, pl.ds(pl.multiple_of(j * tk, tk), tk), h * HEAD_GROUP:(h + 1) * HEAD_GROUP]
        qT = q_ref[0, h * HEAD_GROUP:(h + 1) * HEAD_GROUP, q_sub * tq:(q_sub + 1) * tq]
        return jnp.dot(k_blk, qT, preferred_element_type=F32)

    def values(j, h, p):
        vT = vT_ref[0, j, h * V_GROUP:(h + 1) * V_GROUP, :]
        return jnp.dot(vT, p, preferred_element_type=F32)

    def softmax_tile(h, s, masked):
        if masked:
            kv_pos = lax.broadcasted_iota(jnp.int32, (tk, tq), 0)
            q_pos = lax.broadcasted_iota(jnp.int32, (tk, tq), 1)
            s = jnp.where(kv_pos <= q_pos, s, NEG_BIG)
        m_old = m_refs[h][...]
        m_new = jnp.maximum(m_old, jnp.max(s, axis=0, keepdims=True))
        m_refs[h][...] = m_new
        return jnp.exp2(m_old - m_new), jnp.exp2(s - m_new).astype(BF16)

    def step(s_src, s_dst, p_prev, p_dst, j):
        for h in range(2):
            s_dst[h][...] = scores(qa_ref, sub, j + 1, h)
        pv = None if p_prev is None else [values(j - 1, h, p_prev[h][...]) for h in range(2)]
        for h in range(2):
            alpha, p = softmax_tile(h, s_src[h][...], masked=False)
            p_dst[h][...] = p
            if pv is not None:
                acc_refs[h][...] = alpha * (acc_refs[h][...] + pv[h])

    def finish(s_src, p_prev, prefetch):
        if prefetch:
            for h in range(2):
                s_c[h][...] = scores(next_q, next_sub, j_lo_next, h)
            ready[0] = (i + 1 < nq).astype(jnp.int32)
        else:
            ready[0] = jnp.int32(0)
        pv = None if p_prev is None else [values(i - 1, h, p_prev[h][...]) for h in range(2)]
        outs = []
        for h in range(2):
            alpha, p = softmax_tile(h, s_src[h][...], masked=True)
            acc = values(i, h, p)
            if pv is not None:
                acc = acc + alpha * (acc_refs[h][...] + pv[h])
            outs.append(acc[0:HEAD_DIM, :] / acc[HEAD_DIM:HEAD_DIM + 1, :])
        oT = jnp.concatenate(outs, axis=0)
        o_ref[0, rows, :] = (oT.T * gate_ref[0, rows, :].astype(F32)).astype(BF16)

    @pl.when(i == 0)
    def _():
        ready[0] = jnp.int32(0)

    @pl.when(ready[0] == 0)
    def _():
        for h in range(2):
            s_c[h][...] = scores(qa_ref, sub, j_lo, h)

    for h in range(2):
        m_refs[h][...] = jnp.full((1, tq), NEG_BIG, F32)
        acc_refs[h][...] = jnp.zeros((V_GROUP, tq), F32)

    n_full = i - j_lo

    @pl.when(n_full == 0)
    def _():
        finish(s_c, None, prefetch=False)

    @pl.when(n_full > 0)
    def _():
        step(s_c, s_a, None, p_a, j_lo)
        rest = n_full - 1

        def body(n, carry):
            j = j_lo + 1 + 2 * n
            step(s_a, s_b, p_a, p_b, j)
            step(s_b, s_a, p_b, p_a, j + 1)
            return carry

        lax.fori_loop(0, rest // 2, body, 0)

        @pl.when(rest % 2 == 1)
        def _():
            step(s_a, s_b, p_a, p_b, i - 1)
            finish(s_b, p_b, prefetch=True)

        @pl.when(rest % 2 == 0)
        def _():
            finish(s_a, p_a, prefetch=True)


def _permute_heads(w_kvf, b_f, w_qg, w_o):
    da = N_HEADS * HEAD_DIM
    perm = jnp.argsort(b_f)

    def by_head(w, axis):
        shape = w.shape
        split = shape[:axis] + (N_HEADS, HEAD_DIM) + shape[axis + 1:]
        return jnp.take(w.reshape(split), perm, axis=axis).reshape(shape)

    w_kvf_p = jnp.concatenate([by_head(w_kvf[:, :da], 1), by_head(w_kvf[:, da:2 * da], 1), w_kvf[:, 2 * da:][:, perm]],
                              axis=1)
    w_qg_p = jnp.concatenate([by_head(w_qg[:, :da], 1), by_head(w_qg[:, da:], 1)], axis=1)
    return w_kvf_p, b_f[perm], w_qg_p, by_head(w_o, 0)


def _skip_table(qn, kn, c2r, tq):
    bsz, nq = qn.shape[0], qn.shape[1]
    qn = qn[:, :, :N_HEADS, 0].transpose(0, 2, 1) * NORM_SLACK
    kn = kn[:, :, 0, :N_HEADS].transpose(0, 2, 1) * NORM_SLACK
    c2 = c2r[:, :N_HEADS, :]
    c_first = c2[:, :, ::tq]
    c_last = c2[:, :, tq - 1::tq]
    upper = qn[..., :, None] * kn[..., None, :] + c_first[..., :, None] - c_last[..., None, :]
    lower = -(qn * kn)[..., :, None]
    skip = (upper - lower) < -SKIP_GAP_LOG2
    nk = skip.shape[-1]
    jlo = jnp.min(jnp.where(skip, nk, jnp.arange(nk, dtype=jnp.int32)), axis=-1)
    jlo = jnp.minimum(jlo, jnp.arange(nq, dtype=jnp.int32))
    jlo = jnp.min(jlo.reshape(bsz, N_HEADS // 2, 2, nq), axis=2)
    return jlo.reshape(-1).astype(jnp.int32)


def _attention(jlo, qa, ka, vT, gate, *, tq):
    bsz, hg, s = qa.shape
    nk, tk = vT.shape[1], vT.shape[3]
    da = N_HEADS * HEAD_DIM
    assert tq == tk
    pair = 2 * HEAD_GROUP
    hw = 2 * HEAD_DIM
    tqs = Q_PER_STEP * tq
    assert s % tqs == 0
    n_steps = s // tqs
    grid_spec = pltpu.PrefetchScalarGridSpec(
        num_scalar_prefetch=1,
        grid=(bsz, N_HEADS // 2, n_steps),
        in_specs=[
            pl.BlockSpec((1, pair, tqs), lambda b, hp, i, jlo: (b, hp, i)),
            pl.BlockSpec((1, pair, tqs), lambda b, hp, i, jlo: (b, hp, jnp.minimum(i + 1, n_steps - 1))),
            pl.BlockSpec((1, s, pair), lambda b, hp, i, jlo: (b, 0, hp)),
            pl.BlockSpec((1, nk, 2 * V_GROUP, tk), lambda b, hp, i, jlo: (b, 0, hp, 0)),
            pl.BlockSpec((1, tqs, hw), lambda b, hp, i, jlo: (b, i, hp)),
        ],
        out_specs=pl.BlockSpec((1, tqs, hw), lambda b, hp, i, jlo: (b, i, hp)),
        scratch_shapes=([pltpu.SMEM((1,), jnp.int32)]
                        + [pltpu.VMEM((1, tq), F32)] * 2 + [pltpu.VMEM((V_GROUP, tq), F32)] * 2
                        + [pltpu.VMEM((tk, tq), F32)] * 6 + [pltpu.VMEM((tk, tq), BF16)] * 4),
    )
    return pl.pallas_call(
        _attn_kernel,
        grid_spec=grid_spec,
        out_shape=jax.ShapeDtypeStruct((bsz, s, da), BF16),
        compiler_params=pltpu.CompilerParams(dimension_semantics=("parallel", "parallel", "arbitrary")),
        name="fox_attention",
    )(jlo, qa, qa, ka, vT, gate)


def _pack_bf16_pairs(x):
    half = x.shape[1] // 2
    lo = lax.bitcast_convert_type(x[:, :half].astype(BF16).astype(F32), jnp.uint32)
    hi = lax.bitcast_convert_type(x[:, half:].astype(BF16).astype(F32), jnp.uint32)
    word = (lo >> 16) | (hi & jnp.uint32(0xFFFF0000))
    return lax.bitcast_convert_type(word, jnp.int32)


def _unpack_bf16_pairs(w):
    u = lax.bitcast_convert_type(w, jnp.uint32)
    lo = lax.bitcast_convert_type(u << 16, F32)
    hi = lax.bitcast_convert_type(u & jnp.uint32(0xFFFF0000), F32)
    return jnp.concatenate([lo, hi], axis=1)


def _store_planes(ref, words):
    q = words.shape[1] // 2
    ref[0] = words[:, :q]
    ref[1] = words[:, q:]


def _load_planes(ref, base=0):
    return jnp.concatenate([ref[base], ref[base + 1]], axis=1)


def _router_kernel(x_ref, o_ref, wo_ref, g_ref, rtT_ref, x2_ref, hnp_ref, route_ref):
    tm = x_ref.shape[0]
    x2 = x_ref[...] + jnp.dot(o_ref[...], wo_ref[...], preferred_element_type=F32)
    x2_ref[...] = x2
    hn = _rms(x2, g_ref[...])
    _store_planes(hnp_ref, _pack_bf16_pairs(hn))
    h0, h1, _ = _split3(hn)
    r0, r1, _ = _split3(rtT_ref[...])
    lt = lax.dot_general(jnp.concatenate([r0, r1, r0], axis=1), jnp.concatenate([h0, h0, h1], axis=1), NT_DIMS,
                         preferred_element_type=F32)[:N_EXPERTS]
    row = lax.broadcasted_iota(jnp.int32, (N_EXPERTS, tm), 0)
    m1 = jnp.max(lt, axis=0, keepdims=True)
    i1 = jnp.min(jnp.where(lt == m1, row, N_EXPERTS), axis=0, keepdims=True)
    rest = jnp.where(row == i1, -jnp.inf, lt)
    m2 = jnp.max(rest, axis=0, keepdims=True)
    i2 = jnp.min(jnp.where(rest == m2, row, N_EXPERTS), axis=0, keepdims=True)
    e2 = jnp.exp(m2 - m1)
    w1 = 1.0 / (1.0 + e2)
    w2 = e2 / (1.0 + e2)
    route_ref[...] = jnp.where(row == 0, i1.astype(F32), jnp.where(row == 1, i2.astype(F32),
                               jnp.where(row == 2, w1, jnp.where(row == 3, w2, 0.0))))


def _router(x2d, o2d, w_o, g, router, *, tm):
    t, d = x2d.shape
    da = o2d.shape[1]
    ne = router.shape[1]
    rtT = jnp.pad(router.T, ((0, 2 * SUBLANES - ne), (0, 0)))
    return pl.pallas_call(
        _router_kernel,
        grid=(t // tm,),
        in_specs=[
            pl.BlockSpec((tm, d), lambda i: (i, 0)),
            pl.BlockSpec((tm, da), lambda i: (i, 0)),
            pl.BlockSpec((da, d), lambda i: (0, 0)),
            pl.BlockSpec((1, d), lambda i: (0, 0)),
            pl.BlockSpec((2 * SUBLANES, d), lambda i: (0, 0)),
        ],
        out_specs=[pl.BlockSpec((tm, d), lambda i: (i, 0)), pl.BlockSpec((2, tm, d // 4), lambda i: (0, i, 0)),
                   pl.BlockSpec((ne, tm), lambda i: (0, i))],
        out_shape=[jax.ShapeDtypeStruct((t, d), F32), jax.ShapeDtypeStruct((2, t, d // 4), jnp.int32),
                   jax.ShapeDtypeStruct((ne, t), F32)],
        compiler_params=pltpu.CompilerParams(dimension_semantics=("parallel",)),
        name="attn_out_moe_router",
    )(x2d, o2d, w_o.astype(BF16), g.reshape(1, d), rtT)


def _routing_tables(route, ts):
    ne = N_EXPERTS
    t = route.shape[1]
    e_pair = route[0:2].astype(jnp.int32).reshape(-1)
    n_tiles = (2 * t) // ts + ne
    n_slots = n_tiles * ts
    onehot = (e_pair[:, None] == jnp.arange(ne, dtype=jnp.int32)[None, :]).astype(jnp.int32)
    csum = jnp.cumsum(onehot, axis=0)
    rank = jnp.sum(onehot * csum, axis=1) - 1
    tiles = (csum[-1] + ts - 1) // ts
    tile_end = jnp.cumsum(tiles)
    tile_start = tile_end - tiles
    slot = (jnp.sum(onehot * tile_start[None, :], axis=1) * ts + rank).astype(jnp.int32)
    n_used = tile_end[-1]
    tile_ids = jnp.arange(n_tiles, dtype=jnp.int32)
    tile_expert = jnp.sum((tile_ids[:, None] >= tile_end[None, :]).astype(jnp.int32), axis=1)
    last_expert = jnp.sum(((n_used - 1) >= tile_end).astype(jnp.int32))
    tile_expert = jnp.minimum(tile_expert, last_expert).astype(jnp.int32)
    plane_rows = slot.reshape(2, 1, t) + (jnp.arange(2, dtype=jnp.int32) * n_slots).reshape(1, 2, 1)
    return plane_rows.reshape(-1), tile_expert, n_used.reshape(1).astype(jnp.int32), n_slots


def _sc_mesh():
    return plsc.VectorSubcoreMesh(core_axis_name="core", subcore_axis_name="subcore")


def _scatter_rows(table, idx, n_out):
    m = idx.shape[0]
    n, d = table.shape
    assert m % SC_ROW_WINDOW == 0 and n % SC_ROW_WINDOW == 0
    n_src_blocks = n // SC_ROW_WINDOW

    @pl.kernel(out_type=jax.ShapeDtypeStruct((n_out, d), table.dtype), mesh=_sc_mesh(), scratch_types=[])
    def scatter_kernel(x_hbm, i_hbm, o_hbm):
        def body(x_vmem, i_vmem):
            pltpu.sync_copy(x_vmem, o_hbm.at[i_vmem.at[0]])

        pltpu.emit_pipeline(
            body,
            grid=(m // SC_ROW_WINDOW,),
            in_specs=[pl.BlockSpec((SC_ROW_WINDOW, d), index_map=lambda i: (i % n_src_blocks, 0)),
                      pl.BlockSpec((1, SC_ROW_WINDOW), index_map=lambda i: (0, i))],
            out_specs=[],
            core_axis_name=("core", "subcore"),
            dimension_semantics=(pltpu.PARALLEL,),
        )(x_hbm, i_hbm)

    return scatter_kernel(table, idx.reshape(1, m))


def _gather_rows(table, idx):
    m = idx.shape[0]
    d = table.shape[1]
    assert m % SC_ROW_WINDOW == 0

    @pl.kernel(out_type=jax.ShapeDtypeStruct((m, d), table.dtype), mesh=_sc_mesh())
    def gather_kernel(x_hbm, i_hbm, o_hbm):
        def body(i_vmem, o_vmem):
            pltpu.sync_copy(x_hbm.at[i_vmem.at[0]], o_vmem)

        pltpu.emit_pipeline(
            body,
            grid=(m // SC_ROW_WINDOW,),
            in_specs=[pl.BlockSpec((1, SC_ROW_WINDOW), index_map=lambda i: (0, i))],
            out_specs=[pl.BlockSpec((SC_ROW_WINDOW, d), index_map=lambda i: (i, 0))],
            core_axis_name=("core", "subcore"),
            dimension_semantics=(pltpu.PARALLEL,),
        )(i_hbm, o_hbm)

    return gather_kernel(table, idx.reshape(1, m))


def _expert_kernel(nk, te_ref, nu_ref, xs_ref, wg_ref, wu_ref, wd_ref, ys_ref, x_sc, acc_ref):
    i = pl.program_id(0)
    k = pl.program_id(1)
    used = i < nu_ref[0]

    def first():
        x_sc[...] = _unpack_bf16_pairs(_load_planes(xs_ref)).astype(BF16)

    def last(total):
        _store_planes(ys_ref, _pack_bf16_pairs(total))

    @pl.when(used)
    def _():
        _chunked_swiglu(k, nk, lambda: _swiglu_chunk(x_sc[...], wg_ref.at[0], wu_ref.at[0], wd_ref.at[0]), acc_ref,
                        first, last)

    @pl.when(jnp.logical_not(used) & (k == nk - 1))
    def _():
        ys_ref[...] = jnp.zeros_like(ys_ref)


def _experts(xs, tile_expert, n_used, w_gu, w_down, *, ts, tf):
    _, n_slots, quarter = xs.shape
    d = 4 * quarter
    dexp = w_down.shape[1]
    nk = dexp // tf
    w_gu = w_gu.astype(BF16)
    grid_spec = pltpu.PrefetchScalarGridSpec(
        num_scalar_prefetch=2,
        grid=(n_slots // ts, nk),
        in_specs=[
            pl.BlockSpec((2, ts, quarter), lambda i, k, te, nu: (0, i, 0)),
            pl.BlockSpec((1, d, tf), lambda i, k, te, nu: (te[i], 0, jnp.where(i < nu[0], k, 0))),
            pl.BlockSpec((1, d, tf), lambda i, k, te, nu: (te[i], 0, nk + jnp.where(i < nu[0], k, 0))),
            pl.BlockSpec((1, tf, d), lambda i, k, te, nu: (te[i], jnp.where(i < nu[0], k, 0), 0)),
        ],
        out_specs=pl.BlockSpec((2, ts, quarter), lambda i, k, te, nu: (0, i, 0)),
        scratch_shapes=[pltpu.VMEM((ts, d), BF16), pltpu.VMEM((ts, d), F32)],
    )
    return pl.pallas_call(
        functools.partial(_expert_kernel, nk),
        grid_spec=grid_spec,
        out_shape=jax.ShapeDtypeStruct((2, n_slots, quarter), jnp.int32),
        compiler_params=pltpu.CompilerParams(dimension_semantics=("arbitrary", "arbitrary")),
        name="moe_experts",
    )(tile_expert, n_used, xs, w_gu, w_gu, w_down.astype(BF16))


def _combine_kernel(x_ref, y_ref, rt_ref, gf_ref, o_ref):
    rt = rt_ref[...]
    moe = (rt[:, 2:3] * _unpack_bf16_pairs(_load_planes(y_ref, 0))
           + rt[:, 3:4] * _unpack_bf16_pairs(_load_planes(y_ref, 2)))
    o_ref[...] = _rms(x_ref[...] + moe, gf_ref[...])


def _combine(x2d, y4, route_t, g_final, *, tm):
    t, d = x2d.shape
    ne = route_t.shape[1]
    return pl.pallas_call(
        _combine_kernel,
        grid=(t // tm,),
        in_specs=[
            pl.BlockSpec((tm, d), lambda i: (i, 0)),
            pl.BlockSpec((4, tm, d // 4), lambda i: (0, i, 0)),
            pl.BlockSpec((tm, ne), lambda i: (i, 0)),
            pl.BlockSpec((1, d), lambda i: (0, 0)),
        ],
        out_specs=pl.BlockSpec((tm, d), lambda i: (i, 0)),
        out_shape=jax.ShapeDtypeStruct((t, d), F32),
        compiler_params=pltpu.CompilerParams(dimension_semantics=("parallel",)),
        name="moe_combine",
    )(x2d, y4, route_t, g_final.reshape(1, d))


def _attn_out_moe(x1, o2d, w_o, g, router, w_gu, w_down, g_final, *, tm, ts, tf):
    t, d = x1.shape
    q = d // 4
    x2d, hn_planes, route = _router(x1, o2d, w_o, g, router, tm=tm)
    plane_rows, tile_expert, n_used, n_slots = _routing_tables(route, ts)
    xs = _scatter_rows(hn_planes.reshape(2 * t, q), plane_rows, 2 * n_slots)
    ys = _experts(xs.reshape(2, n_slots, q), tile_expert, n_used, w_gu, w_down, ts=ts, tf=tf)
    y4 = _gather_rows(ys.reshape(2 * n_slots, q), plane_rows)
    return _combine(x2d, y4.reshape(4, t, q), route.T, g_final, tm=tm)


ROW_TILE = 512


def _tile(n):
    t = min(n, ROW_TILE)
    assert n % t == 0 and t % (2 * LANES) == 0
    return t


def kernel(x, norm_g, lru_w_in, lru_conv_w, lru_conv_b, lru_gate_w, lru_gate_b, lru_lambda, lru_w_out,
           kv_norm_g, w_kvf, b_f, fox_w_qg, fox_w_o, mlp_w_gu, mlp_w_down, moe_router, moe_w_gu,
           moe_w_down, final_norm_g):
    bsz, s, d = x.shape
    t = bsz * s
    tm = _tile(t)
    ts = _tile(s)

    x = _lru_mlp(x, norm_g[0, 0], lru_w_in[0], lru_conv_w[0], lru_conv_b[0], lru_gate_w[0], lru_gate_b[0],
                 lru_lambda[0], lru_w_out[0], norm_g[0, 1], mlp_w_gu[0], mlp_w_down[0], tm=ts)

    consts = _head_selectors()
    w_kvf_p, b_f_p, w_qg_p, w_o_p = _permute_heads(w_kvf, b_f, fox_w_qg[0], fox_w_o[0])
    ka, vT, c2r, kn = _kv_proj(x, kv_norm_g, w_kvf_p, b_f_p, consts, tk=ts)
    qa, gate, qn = _q_proj(x, norm_g[1, 0], w_qg_p, c2r, consts, tq=ts)
    jlo = _skip_table(qn, kn, c2r, ts)
    o = _attention(jlo, qa, ka, vT, gate, tq=ts)
    dexp = moe_w_down.shape[2]
    out = _attn_out_moe(x.reshape(t, d), o.reshape(t, -1), w_o_p, norm_g[1, 1], moe_router[0], moe_w_gu[0],
                        moe_w_down[0], final_norm_g, tm=tm, ts=tm, tf=dexp // 2)
    return out.reshape(bsz, s, d)
```

```python
import functools
import math

import jax
import jax.numpy as jnp
from jax import lax
from jax.experimental import pallas as pl
from jax.experimental.pallas import tpu as pltpu
from jax.experimental.pallas import tpu_sc as plsc

EPS = 1e-6
LRU_C = 8.0
N_LRU_BLOCKS = 8
CONV_W = 4
N_HEADS = 16
HEAD_DIM = 64
N_EXPERTS = 8
LANES = 128
SUBLANES = 8
HEAD_GROUP = 128
BIAS_Q = HEAD_DIM
BIAS_K = HEAD_DIM + 3
V_GROUP = 80
NEG_BIG = -1e30
LOG2E = math.log2(math.e)
SKIP_GAP_LOG2 = 152.0
NORM_SLACK = 1.01
SC_ROW_WINDOW = 128
Q_PER_STEP = 4

F32 = jnp.float32
BF16 = jnp.bfloat16
NT_DIMS = (((1,), (1,)), ((), ()))


def _rms(x, g):
    ms = jnp.mean(x * x, axis=-1, keepdims=True)
    return x * lax.rsqrt(ms + EPS) * g


def _log_sigmoid(x):
    return jnp.minimum(x, 0.0) - jnp.log1p(jnp.exp(-jnp.abs(x)))


def _gelu_tanh(x):
    c = 0.7978845608028654
    return 0.5 * x * (1.0 + jnp.tanh(c * (x + 0.044715 * (x * x * x))))


def _silu(x):
    return x * jax.nn.sigmoid(x)


def _split3(x):
    p0 = x.astype(BF16)
    r1 = x - p0.astype(F32)
    p1 = r1.astype(BF16)
    p2 = (r1 - p1.astype(F32)).astype(BF16)
    return p0, p1, p2


def _const_spec(shape, n_grid):
    zeros = (0,) * len(shape)
    if n_grid == 1:
        return pl.BlockSpec(shape, lambda i: zeros)
    if n_grid == 2:
        return pl.BlockSpec(shape, lambda i, j: zeros)
    return pl.BlockSpec(shape, lambda i, j, k: zeros)


def _lru_in_proj(x, g_ref, win_ref):
    d = x.shape[1]
    xn = _rms(x, g_ref[...]).astype(BF16)
    xg = jnp.dot(xn, win_ref[...], preferred_element_type=F32)
    return xg[:, :d], xg[:, d:]


def _lru_conv_gates(xb, cw_ref, cb_ref, gw_ref, xbuf):
    tt, d = xb.shape
    bw = d // N_LRU_BLOCKS
    xbuf[SUBLANES:SUBLANES + tt, :] = xb
    cw = cw_ref[...]
    xc = cb_ref[...] + cw[CONV_W - 1:CONV_W, :] * xb
    for j in range(CONV_W - 1):
        xc = xc + cw[j:j + 1, :] * xbuf[pl.ds(SUBLANES - (CONV_W - 1) + j, tt), :]
    xbuf[0:SUBLANES, :] = xbuf[tt:tt + SUBLANES, :]

    xcb = xc.astype(BF16)
    rl, il = [], []
    for n in range(N_LRU_BLOCKS):
        gl = jnp.dot(xcb[:, n * bw:(n + 1) * bw], gw_ref[n], preferred_element_type=F32)
        rl.append(gl[:, :bw])
        il.append(gl[:, bw:])
    return xc, jnp.concatenate(rl, axis=1), jnp.concatenate(il, axis=1)


def _lru_recurrence(xc, r_logit, i_logit, gbr, gb_ref, lam_ref, hcar):
    tt = xc.shape[0]
    gb = gb_ref[...]
    r = jax.nn.sigmoid(r_logit + gb[0:1, :])
    i = jax.nn.sigmoid(i_logit + gb[1:2, :])
    log_a = LRU_C * r * _log_sigmoid(lam_ref[...])
    a = jnp.exp(log_a)
    b = jnp.sqrt((1.0 - a) * (1.0 + a)) * (i * xc)

    row = lax.broadcasted_iota(jnp.int32, (tt, 1), 0)
    sh = 1
    while sh < tt:
        if sh < SUBLANES:
            keep = row >= sh
            a_sh = jnp.where(keep, pltpu.roll(a, sh, 0), 1.0)
            b_sh = jnp.where(keep, pltpu.roll(b, sh, 0), 0.0)
            b = a * b_sh + b
            a = a * a_sh
        else:
            b = jnp.concatenate([b[:sh], a[sh:] * b[:tt - sh] + b[sh:]], axis=0)
            a = jnp.concatenate([a[:sh], a[sh:] * a[:tt - sh]], axis=0)
        sh *= 2
    h = b + a * hcar[...]
    hcar[...] = h[tt - 1:tt, :]
    return (_gelu_tanh(gbr) * h).astype(BF16)


def _lru_mlp_kernel(x_ref, g0_ref, win_ref, cw_ref, cb_ref, gw_ref, gb_ref, lam_ref, wout_ref,
                    g1_ref, wg_ref, wu_ref, wd_ref, o_ref, xbuf, hcar, x1_sc, hn_sc, acc_ref):
    t = pl.program_id(1)
    k = pl.program_id(2)
    half = x_ref.shape[1]
    tf = wg_ref.shape[1]
    mid = (tf // (2 * LANES)) * LANES
    slot = t % 2
    prev = 1 - slot

    @pl.when((t == 0) & (k == 0))
    def _():
        xbuf[0:SUBLANES, :] = jnp.zeros((SUBLANES, xbuf.shape[1]), F32)
        hcar[...] = jnp.zeros_like(hcar)
        x1_sc[...] = jnp.zeros_like(x1_sc)
        hn_sc[...] = jnp.zeros_like(hn_sc)

    def gate_up(hn, lo, hi):
        return (jnp.dot(hn, wg_ref[:, lo:hi], preferred_element_type=F32),
                jnp.dot(hn, wu_ref[:, lo:hi], preferred_element_type=F32))

    def both():
        x = x_ref[0]
        hn = hn_sc[prev]
        xb, gbr = _lru_in_proj(x, g0_ref, win_ref)
        g_a, u_a = gate_up(hn, 0, mid)
        xc, r_logit, i_logit = _lru_conv_gates(xb, cw_ref, cb_ref, gw_ref, xbuf)
        g_b, u_b = gate_up(hn, mid, tf)
        h = jnp.concatenate([(_silu(g_a) * u_a).astype(BF16), (_silu(g_b) * u_b).astype(BF16)], axis=1)
        y_mlp = jnp.dot(h, wd_ref[...], preferred_element_type=F32)
        y_lru = _lru_recurrence(xc, r_logit, i_logit, gbr, gb_ref, lam_ref, hcar)
        x1 = x + jnp.dot(y_lru, wout_ref[...], preferred_element_type=F32)
        rows = pl.ds(pl.multiple_of(k * half, half), half)
        x1_sc[slot, rows, :] = x1
        hn_sc[slot, rows, :] = _rms(x1, g1_ref[...]).astype(BF16)
        return y_mlp

    @pl.when(k == 0)
    def _():
        acc_ref[...] = both()

    @pl.when(k == 1)
    def _():
        y = both()
        o_ref[0] = x1_sc[prev] + acc_ref[...] + y


def _lru_mlp(x, g0, w_in, conv_w, conv_b, gate_w, gate_b, lam, w_out, g1, w_gu, w_down, *, tm):
    bsz, s, d = x.shape
    bw = d // N_LRU_BLOCKS
    dff = w_down.shape[0]
    tf = dff // 2
    half = tm // 2
    nt = s // tm
    n_half = s // half
    gw = jnp.concatenate([gate_w[0], gate_w[1]], axis=-1).astype(BF16)
    w_gu = w_gu.astype(BF16)

    def const(*shape):
        return pl.BlockSpec(shape, lambda b, t, k: (0,) * len(shape), pipeline_mode=pl.Buffered(1))

    return pl.pallas_call(
        _lru_mlp_kernel,
        grid=(bsz, nt + 1, 2),
        in_specs=[
            pl.BlockSpec((1, half, d), lambda b, t, k: (b, jnp.minimum(2 * t + k, n_half - 1), 0)),
            const(1, d), const(d, 2 * d), const(CONV_W, d), const(1, d),
            const(N_LRU_BLOCKS, bw, 2 * bw), const(2, d), const(1, d), const(d, d),
            const(1, d),
            pl.BlockSpec((d, tf), lambda b, t, k: (0, k)),
            pl.BlockSpec((d, tf), lambda b, t, k: (0, 2 + k)),
            pl.BlockSpec((tf, d), lambda b, t, k: (k, 0)),
        ],
        out_specs=pl.BlockSpec((1, tm, d), lambda b, t, k: (b, jnp.maximum(t - 1, 0), 0)),
        out_shape=jax.ShapeDtypeStruct((bsz, s, d), F32),
        scratch_shapes=[pltpu.VMEM((half + 2 * SUBLANES, d), F32), pltpu.VMEM((1, d), F32),
                        pltpu.VMEM((2, tm, d), F32), pltpu.VMEM((2, tm, d), BF16), pltpu.VMEM((tm, d), F32)],
        compiler_params=pltpu.CompilerParams(dimension_semantics=("parallel", "arbitrary", "arbitrary")),
        name="lru_mlp",
    )(x, g0.reshape(1, d), w_in.astype(BF16), conv_w, conv_b.reshape(1, d), gw, gate_b, lam.reshape(1, d),
      w_out.astype(BF16), g1.reshape(1, d), w_gu, w_gu, w_down.astype(BF16))


def _swiglu_chunk(x, wg, wu, wd):
    gate = jnp.dot(x, wg[...], preferred_element_type=F32)
    up = jnp.dot(x, wu[...], preferred_element_type=F32)
    h = (_silu(gate) * up).astype(BF16)
    return jnp.dot(h, wd[...], preferred_element_type=F32)


def _chunked_swiglu(k, nk, chunk, acc_ref, first, last):
    if nk == 1:
        first()
        last(chunk())
        return

    @pl.when(k == 0)
    def _():
        first()
        acc_ref[...] = chunk()

    if nk > 2:
        @pl.when((k > 0) & (k < nk - 1))
        def _():
            acc_ref[...] += chunk()

    @pl.when(k == nk - 1)
    def _():
        last(acc_ref[...] + chunk())


def _head_selectors():
    hg = N_HEADS * HEAD_GROUP
    head = jnp.arange(hg) // HEAD_GROUP
    lane = jnp.arange(hg) % HEAD_GROUP
    col = jnp.arange(LANES)
    def sel(base):
        return ((col[:, None] % N_HEADS == head[None, :]) & (col[:, None] < 3 * N_HEADS)
                & (lane[None, :] == base + col[:, None] // N_HEADS)).astype(BF16)
    sel_k = -sel(BIAS_K)
    sel_q = sel(BIAS_Q).T
    ones_k = ((lane >= BIAS_Q) & (lane < BIAS_Q + 3)).astype(F32).reshape(1, hg)
    ones_q = ((lane >= BIAS_K) & (lane < BIAS_K + 3)).astype(F32).reshape(hg, 1)
    feat_head = jnp.arange(N_HEADS * HEAD_DIM) // HEAD_DIM
    head_sum = (col[None, :] == feat_head[:, None]).astype(BF16)
    return sel_k, sel_q, ones_k, ones_q, head_sum


def _kv_kernel(x_ref, g_ref, wk_ref, wvT_ref, wf_ref, wfT_ref, bfr_ref, bfc_ref, selk_ref, onesk_ref,
               hsum_ref, ka_ref, vT_ref, c2r_ref, kn_ref, car_r, car_c):
    t = pl.program_id(1)
    tk = x_ref.shape[1]

    @pl.when(t == 0)
    def _():
        car_r[...] = jnp.zeros_like(car_r)
        car_c[...] = jnp.zeros_like(car_c)

    xn = _rms(x_ref[0], g_ref[...]).astype(BF16)
    vr = lax.dot_general(wvT_ref[...], xn, NT_DIMS, preferred_element_type=F32)
    tail_row = lax.broadcasted_iota(jnp.int32, (V_GROUP - HEAD_DIM, tk), 0)
    tail = jnp.where(tail_row == 0, 1.0, 0.0)
    vT = jnp.concatenate([blk for h in range(N_HEADS) for blk in (vr[h * HEAD_DIM:(h + 1) * HEAD_DIM], tail)], axis=0)
    vT_ref[0, 0] = vT.astype(BF16)
    kr = jnp.dot(xn, wk_ref[...], preferred_element_type=F32)

    kss = jnp.dot((kr * kr).astype(BF16), hsum_ref[...], preferred_element_type=F32)
    kn = jnp.sqrt(jnp.max(kss, axis=0, keepdims=True))
    kn_ref[0, 0] = jnp.broadcast_to(kn, (SUBLANES, LANES))

    ri = lax.broadcasted_iota(jnp.int32, (tk, tk), 0)
    ci = lax.broadcasted_iota(jnp.int32, (tk, tk), 1)
    upper = (ri <= ci).astype(BF16)
    lower = (ci <= ri).astype(BF16)

    lf_c = _log_sigmoid(jnp.dot(xn, wf_ref[...], preferred_element_type=F32) + bfr_ref[...])
    cs3 = jnp.dot(lower, jnp.concatenate(_split3(lf_c), axis=1), preferred_element_type=F32)
    cs_c = car_c[...] + (cs3[:, :LANES] + cs3[:, LANES:2 * LANES] + cs3[:, 2 * LANES:])
    car_c[...] = cs_c[tk - 1:tk, :]
    p0, p1, p2 = (p.astype(F32) for p in _split3(cs_c * LOG2E))
    lane = lax.broadcasted_iota(jnp.int32, (tk, LANES), 1)
    groups = []
    for pair in range(N_HEADS // 2):
        two = kr[:, pair * LANES:(pair + 1) * LANES]
        groups.append(jnp.where(lane < HEAD_DIM, two, 0.0))
        groups.append(jnp.where(lane < HEAD_DIM, pltpu.roll(two, HEAD_DIM, 1), 0.0))
    kf = jnp.concatenate(groups, axis=1)
    packed = jnp.where(lane < N_HEADS, p0,
                       jnp.where(lane < 2 * N_HEADS, pltpu.roll(p1, N_HEADS, 1),
                                 jnp.where(lane < 3 * N_HEADS, pltpu.roll(p2, 2 * N_HEADS, 1), 0.0)))
    bias = jnp.dot(packed.astype(BF16), selk_ref[...], preferred_element_type=F32)
    ka_ref[0] = (kf + bias + onesk_ref[...]).astype(BF16)

    lf_r = _log_sigmoid(lax.dot_general(wfT_ref[...], xn, NT_DIMS, preferred_element_type=F32) + bfc_ref[...])
    cs3 = jnp.dot(jnp.concatenate(_split3(lf_r), axis=0), upper, preferred_element_type=F32)
    cs_r = car_r[...] + (cs3[:LANES] + cs3[LANES:2 * LANES] + cs3[2 * LANES:])
    car_r[...] = cs_r[:, tk - 1:tk]
    c2r_ref[0] = cs_r * LOG2E


def _kv_proj(x, g, w_kvf, b_f, consts, *, tk):
    bsz, s, d = x.shape
    da = N_HEADS * HEAD_DIM
    hg = N_HEADS * HEAD_GROUP
    dv = N_HEADS * V_GROUP
    nk = s // tk
    sel_k, _, ones_k, _, head_sum = consts
    wk = w_kvf[:, :da].astype(BF16)
    wvT = w_kvf[:, da:2 * da].T.astype(BF16)
    wf = jnp.pad(w_kvf[:, 2 * da:], ((0, 0), (0, LANES - N_HEADS))).astype(BF16)
    bfr = jnp.pad(b_f, (0, LANES - N_HEADS)).reshape(1, LANES)
    const = lambda *shape: _const_spec(shape, 2)
    return pl.pallas_call(
        _kv_kernel,
        grid=(bsz, nk),
        in_specs=[
            pl.BlockSpec((1, tk, d), lambda b, t: (b, t, 0)),
            const(1, d), const(d, da), const(da, d), const(d, LANES), const(LANES, d),
            const(1, LANES), const(LANES, 1), const(LANES, hg), const(1, hg), const(da, LANES),
        ],
        out_specs=[
            pl.BlockSpec((1, tk, hg), lambda b, t: (b, t, 0)),
            pl.BlockSpec((1, 1, dv, tk), lambda b, t: (b, t, 0, 0)),
            pl.BlockSpec((1, LANES, tk), lambda b, t: (b, 0, t)),
            pl.BlockSpec((1, 1, SUBLANES, LANES), lambda b, t: (b, t, 0, 0)),
        ],
        out_shape=[
            jax.ShapeDtypeStruct((bsz, s, hg), BF16),
            jax.ShapeDtypeStruct((bsz, nk, dv, tk), BF16),
            jax.ShapeDtypeStruct((bsz, LANES, s), F32),
            jax.ShapeDtypeStruct((bsz, nk, SUBLANES, LANES), F32),
        ],
        scratch_shapes=[pltpu.VMEM((LANES, 1), F32), pltpu.VMEM((1, LANES), F32)],
        compiler_params=pltpu.CompilerParams(dimension_semantics=("parallel", "arbitrary")),
        name="kv_proj",
    )(x, g.reshape(1, d), wk, wvT, wf, wf.T, bfr, bfr.reshape(LANES, 1), sel_k, ones_k, head_sum)


def _q_kernel(x_ref, g_ref, wqT_ref, wg_ref, c2r_ref, selq_ref, onesq_ref, hsumT_ref, qa_ref, gate_ref, qn_ref):
    xn = _rms(x_ref[0], g_ref[...]).astype(BF16)
    gate_ref[0] = jax.nn.sigmoid(jnp.dot(xn, wg_ref[...], preferred_element_type=F32)).astype(BF16)
    qr = lax.dot_general(wqT_ref[...], xn, NT_DIMS, preferred_element_type=F32) * (LOG2E * HEAD_DIM ** -0.5)

    qss = jnp.dot(hsumT_ref[...], (qr * qr).astype(BF16), preferred_element_type=F32)
    qn = jnp.sqrt(jnp.max(qss, axis=1, keepdims=True))
    qn_ref[0, 0] = jnp.broadcast_to(qn, (LANES, LANES))

    tq = c2r_ref.shape[2]
    gap = jnp.zeros((HEAD_GROUP - HEAD_DIM, tq), F32)
    qf = jnp.concatenate([blk for h in range(N_HEADS) for blk in (qr[h * HEAD_DIM:(h + 1) * HEAD_DIM], gap)], axis=0)
    packed = jnp.concatenate([p[:N_HEADS] for p in _split3(c2r_ref[0])]
                             + [jnp.zeros((LANES - 3 * N_HEADS, tq), BF16)], axis=0)
    bias = jnp.dot(selq_ref[...], packed, preferred_element_type=F32)
    qa_ref[0] = (qf + bias + onesq_ref[...]).astype(BF16)


def _q_proj(x, g, w_qg, c2r, consts, *, tq):
    bsz, s, d = x.shape
    da = N_HEADS * HEAD_DIM
    hg = N_HEADS * HEAD_GROUP
    nq = s // tq
    _, sel_q, _, ones_q, head_sum = consts
    wqT = w_qg[:, :da].T.astype(BF16)
    const = lambda *shape: _const_spec(shape, 2)
    return pl.pallas_call(
        _q_kernel,
        grid=(bsz, nq),
        in_specs=[
            pl.BlockSpec((1, tq, d), lambda b, t: (b, t, 0)),
            const(1, d), const(da, d), const(d, da),
            pl.BlockSpec((1, LANES, tq), lambda b, t: (b, 0, t)),
            const(hg, LANES), const(hg, 1), const(LANES, da),
        ],
        out_specs=[
            pl.BlockSpec((1, hg, tq), lambda b, t: (b, 0, t)),
            pl.BlockSpec((1, tq, da), lambda b, t: (b, t, 0)),
            pl.BlockSpec((1, 1, LANES, LANES), lambda b, t: (b, t, 0, 0)),
        ],
        out_shape=[
            jax.ShapeDtypeStruct((bsz, hg, s), BF16),
            jax.ShapeDtypeStruct((bsz, s, da), BF16),
            jax.ShapeDtypeStruct((bsz, nq, LANES, LANES), F32),
        ],
        compiler_params=pltpu.CompilerParams(dimension_semantics=("parallel", "parallel")),
        name="q_proj",
    )(x, g.reshape(1, d), wqT, w_qg[:, da:].astype(BF16), c2r, sel_q, ones_q, head_sum.T)


def _attn_kernel(*refs):
    for sub in range(Q_PER_STEP):
        _attn_q_block(sub, *refs)


def _attn_q_block(sub, jlo_ref, qa_ref, qan_ref, ka_ref, vT_ref, gate_ref, o_ref, ready, m0, m1, acc0, acc1,
                  sa0, sa1, sb0, sb1, sc0, sc1, pa0, pa1, pb0, pb1):
    b = pl.program_id(0)
    hp = pl.program_id(1)
    i = pl.program_id(2) * Q_PER_STEP + sub
    nq = pl.num_programs(2) * Q_PER_STEP
    tq = qa_ref.shape[2] // Q_PER_STEP
    tk = vT_ref.shape[3]
    rows = slice(sub * tq, (sub + 1) * tq)
    next_q, next_sub = (qa_ref, sub + 1) if sub + 1 < Q_PER_STEP else (qan_ref, 0)
    base = (b * pl.num_programs(1) + hp) * nq
    j_lo = jlo_ref[base + i]
    j_lo_next = jlo_ref[base + jnp.minimum(i + 1, nq - 1)]
    m_refs, acc_refs = (m0, m1), (acc0, acc1)
    s_a, s_b, s_c = (sa0, sa1), (sb0, sb1), (sc0, sc1)
    p_a, p_b = (pa0, pa1), (pb0, pb1)

    def scores(q_ref, q_sub, j, h):
        k_blk = ka_ref[0, pl.ds(pl.multiple_of(j * tk, tk), tk), h * HEAD_GROUP:(h + 1) * HEAD_GROUP]
        qT = q_ref[0, h * HEAD_GROUP:(h + 1) * HEAD_GROUP, q_sub * tq:(q_sub + 1) * tq]
        return jnp.dot(k_blk, qT, preferred_element_type=F32)

    def values(j, h, p):
        vT = vT_ref[0, j, h * V_GROUP:(h + 1) * V_GROUP, :]
        return jnp.dot(vT, p, preferred_element_type=F32)

    def softmax_tile(h, s, masked):
        if masked:
            kv_pos = lax.broadcasted_iota(jnp.int32, (tk, tq), 0)
            q_pos = lax.broadcasted_iota(jnp.int32, (tk, tq), 1)
            s = jnp.where(kv_pos <= q_pos, s, NEG_BIG)
        m_old = m_refs[h][...]
        m_new = jnp.maximum(m_old, jnp.max(s, axis=0, keepdims=True))
        m_refs[h][...] = m_new
        return jnp.exp2(m_old - m_new), jnp.exp2(s - m_new).astype(BF16)

    def step(s_src, s_dst, p_prev, p_dst, j):
        for h in range(2):
            s_dst[h][...] = scores(qa_ref, sub, j + 1, h)
        pv = None if p_prev is None else [values(j - 1, h, p_prev[h][...]) for h in range(2)]
        for h in range(2):
            alpha, p = softmax_tile(h, s_src[h][...], masked=False)
            p_dst[h][...] = p
            if pv is not None:
                acc_refs[h][...] = alpha * (acc_refs[h][...] + pv[h])

    def finish(s_src, p_prev, prefetch):
        if prefetch:
            for h in range(2):
                s_c[h][...] = scores(next_q, next_sub, j_lo_next, h)
            ready[0] = (i + 1 < nq).astype(jnp.int32)
        else:
            ready[0] = jnp.int32(0)
        pv = None if p_prev is None else [values(i - 1, h, p_prev[h][...]) for h in range(2)]
        outs = []
        for h in range(2):
            alpha, p = softmax_tile(h, s_src[h][...], masked=True)
            acc = values(i, h, p)
            if pv is not None:
                acc = acc + alpha * (acc_refs[h][...] + pv[h])
            outs.append(acc[0:HEAD_DIM, :] / acc[HEAD_DIM:HEAD_DIM + 1, :])
        oT = jnp.concatenate(outs, axis=0)
        o_ref[0, rows, :] = (oT.T * gate_ref[0, rows, :].astype(F32)).astype(BF16)

    @pl.when(i == 0)
    def _():
        ready[0] = jnp.int32(0)

    @pl.when(ready[0] == 0)
    def _():
        for h in range(2):
            s_c[h][...] = scores(qa_ref, sub, j_lo, h)

    for h in range(2):
        m_refs[h][...] = jnp.full((1, tq), NEG_BIG, F32)
        acc_refs[h][...] = jnp.zeros((V_GROUP, tq), F32)

    n_full = i - j_lo

    @pl.when(n_full == 0)
    def _():
        finish(s_c, None, prefetch=False)

    @pl.when(n_full > 0)
    def _():
        step(s_c, s_a, None, p_a, j_lo)
        rest = n_full - 1

        def body(n, carry):
            j = j_lo + 1 + 2 * n
            step(s_a, s_b, p_a, p_b, j)
            step(s_b, s_a, p_b, p_a, j + 1)
            return carry

        lax.fori_loop(0, rest // 2, body, 0)

        @pl.when(rest % 2 == 1)
        def _():
            step(s_a, s_b, p_a, p_b, i - 1)
            finish(s_b, p_b, prefetch=True)

        @pl.when(rest % 2 == 0)
        def _():
            finish(s_a, p_a, prefetch=True)


def _permute_heads(w_kvf, b_f, w_qg, w_o):
    da = N_HEADS * HEAD_DIM
    perm = jnp.argsort(b_f)

    def by_head(w, axis):
        shape = w.shape
        split = shape[:axis] + (N_HEADS, HEAD_DIM) + shape[axis + 1:]
        return jnp.take(w.reshape(split), perm, axis=axis).reshape(shape)

    w_kvf_p = jnp.concatenate([by_head(w_kvf[:, :da], 1), by_head(w_kvf[:, da:2 * da], 1), w_kvf[:, 2 * da:][:, perm]],
                              axis=1)
    w_qg_p = jnp.concatenate([by_head(w_qg[:, :da], 1), by_head(w_qg[:, da:], 1)], axis=1)
    return w_kvf_p, b_f[perm], w_qg_p, by_head(w_o, 0)


def _skip_table(qn, kn, c2r, tq):
    bsz, nq = qn.shape[0], qn.shape[1]
    qn = qn[:, :, :N_HEADS, 0].transpose(0, 2, 1) * NORM_SLACK
    kn = kn[:, :, 0, :N_HEADS].transpose(0, 2, 1) * NORM_SLACK
    c2 = c2r[:, :N_HEADS, :]
    c_first = c2[:, :, ::tq]
    c_last = c2[:, :, tq - 1::tq]
    upper = qn[..., :, None] * kn[..., None, :] + c_first[..., :, None] - c_last[..., None, :]
    lower = -(qn * kn)[..., :, None]
    skip = (upper - lower) < -SKIP_GAP_LOG2
    nk = skip.shape[-1]
    jlo = jnp.min(jnp.where(skip, nk, jnp.arange(nk, dtype=jnp.int32)), axis=-1)
    jlo = jnp.minimum(jlo, jnp.arange(nq, dtype=jnp.int32))
    jlo = jnp.min(jlo.reshape(bsz, N_HEADS // 2, 2, nq), axis=2)
    return jlo.reshape(-1).astype(jnp.int32)


def _attention(jlo, qa, ka, vT, gate, *, tq):
    bsz, hg, s = qa.shape
    nk, tk = vT.shape[1], vT.shape[3]
    da = N_HEADS * HEAD_DIM
    assert tq == tk
    pair = 2 * HEAD_GROUP
    hw = 2 * HEAD_DIM
    tqs = Q_PER_STEP * tq
    assert s % tqs == 0
    n_steps = s // tqs
    grid_spec = pltpu.PrefetchScalarGridSpec(
        num_scalar_prefetch=1,
        grid=(bsz, N_HEADS // 2, n_steps),
        in_specs=[
            pl.BlockSpec((1, pair, tqs), lambda b, hp, i, jlo: (b, hp, i)),
            pl.BlockSpec((1, pair, tqs), lambda b, hp, i, jlo: (b, hp, jnp.minimum(i + 1, n_steps - 1))),
            pl.BlockSpec((1, s, pair), lambda b, hp, i, jlo: (b, 0, hp)),
            pl.BlockSpec((1, nk, 2 * V_GROUP, tk), lambda b, hp, i, jlo: (b, 0, hp, 0)),
            pl.BlockSpec((1, tqs, hw), lambda b, hp, i, jlo: (b, i, hp)),
        ],
        out_specs=pl.BlockSpec((1, tqs, hw), lambda b, hp, i, jlo: (b, i, hp)),
        scratch_shapes=([pltpu.SMEM((1,), jnp.int32)]
                        + [pltpu.VMEM((1, tq), F32)] * 2 + [pltpu.VMEM((V_GROUP, tq), F32)] * 2
                        + [pltpu.VMEM((tk, tq), F32)] * 6 + [pltpu.VMEM((tk, tq), BF16)] * 4),
    )
    return pl.pallas_call(
        _attn_kernel,
        grid_spec=grid_spec,
        out_shape=jax.ShapeDtypeStruct((bsz, s, da), BF16),
        compiler_params=pltpu.CompilerParams(dimension_semantics=("parallel", "parallel", "arbitrary")),
        name="fox_attention",
    )(jlo, qa, qa, ka, vT, gate)


def _pack_bf16_pairs(x):
    half = x.shape[1] // 2
    lo = lax.bitcast_convert_type(x[:, :half].astype(BF16).astype(F32), jnp.uint32)
    hi = lax.bitcast_convert_type(x[:, half:].astype(BF16).astype(F32), jnp.uint32)
    word = (lo >> 16) | (hi & jnp.uint32(0xFFFF0000))
    return lax.bitcast_convert_type(word, jnp.int32)


def _unpack_bf16_pairs(w):
    u = lax.bitcast_convert_type(w, jnp.uint32)
    lo = lax.bitcast_convert_type(u << 16, F32)
    hi = lax.bitcast_convert_type(u & jnp.uint32(0xFFFF0000), F32)
    return jnp.concatenate([lo, hi], axis=1)


def _store_planes(ref, words):
    q = words.shape[1] // 2
    ref[0] = words[:, :q]
    ref[1] = words[:, q:]


def _load_planes(ref, base=0):
    return jnp.concatenate([ref[base], ref[base + 1]], axis=1)


def _router_kernel(x_ref, o_ref, wo_ref, g_ref, rtT_ref, x2_ref, hnp_ref, route_ref):
    tm = x_ref.shape[0]
    x2 = x_ref[...] + jnp.dot(o_ref[...], wo_ref[...], preferred_element_type=F32)
    x2_ref[...] = x2
    hn = _rms(x2, g_ref[...])
    _store_planes(hnp_ref, _pack_bf16_pairs(hn))
    h0, h1, _ = _split3(hn)
    r0, r1, _ = _split3(rtT_ref[...])
    lt = lax.dot_general(jnp.concatenate([r0, r1, r0], axis=1), jnp.concatenate([h0, h0, h1], axis=1), NT_DIMS,
                         preferred_element_type=F32)[:N_EXPERTS]
    row = lax.broadcasted_iota(jnp.int32, (N_EXPERTS, tm), 0)
    m1 = jnp.max(lt, axis=0, keepdims=True)
    i1 = jnp.min(jnp.where(lt == m1, row, N_EXPERTS), axis=0, keepdims=True)
    rest = jnp.where(row == i1, -jnp.inf, lt)
    m2 = jnp.max(rest, axis=0, keepdims=True)
    i2 = jnp.min(jnp.where(rest == m2, row, N_EXPERTS), axis=0, keepdims=True)
    e2 = jnp.exp(m2 - m1)
    w1 = 1.0 / (1.0 + e2)
    w2 = e2 / (1.0 + e2)
    route_ref[...] = jnp.where(row == 0, i1.astype(F32), jnp.where(row == 1, i2.astype(F32),
                               jnp.where(row == 2, w1, jnp.where(row == 3, w2, 0.0))))


def _router(x2d, o2d, w_o, g, router, *, tm):
    t, d = x2d.shape
    da = o2d.shape[1]
    ne = router.shape[1]
    rtT = jnp.pad(router.T, ((0, 2 * SUBLANES - ne), (0, 0)))
    return pl.pallas_call(
        _router_kernel,
        grid=(t // tm,),
        in_specs=[
            pl.BlockSpec((tm, d), lambda i: (i, 0)),
            pl.BlockSpec((tm, da), lambda i: (i, 0)),
            pl.BlockSpec((da, d), lambda i: (0, 0)),
            pl.BlockSpec((1, d), lambda i: (0, 0)),
            pl.BlockSpec((2 * SUBLANES, d), lambda i: (0, 0)),
        ],
        out_specs=[pl.BlockSpec((tm, d), lambda i: (i, 0)), pl.BlockSpec((2, tm, d // 4), lambda i: (0, i, 0)),
                   pl.BlockSpec((ne, tm), lambda i: (0, i))],
        out_shape=[jax.ShapeDtypeStruct((t, d), F32), jax.ShapeDtypeStruct((2, t, d // 4), jnp.int32),
                   jax.ShapeDtypeStruct((ne, t), F32)],
        compiler_params=pltpu.CompilerParams(dimension_semantics=("parallel",)),
        name="attn_out_moe_router",
    )(x2d, o2d, w_o.astype(BF16), g.reshape(1, d), rtT)


def _routing_tables(route, ts):
    ne = N_EXPERTS
    t = route.shape[1]
    e_pair = route[0:2].astype(jnp.int32).reshape(-1)
    n_tiles = (2 * t) // ts + ne
    n_slots = n_tiles * ts
    onehot = (e_pair[:, None] == jnp.arange(ne, dtype=jnp.int32)[None, :]).astype(jnp.int32)
    csum = jnp.cumsum(onehot, axis=0)
    rank = jnp.sum(onehot * csum, axis=1) - 1
    tiles = (csum[-1] + ts - 1) // ts
    tile_end = jnp.cumsum(tiles)
    tile_start = tile_end - tiles
    slot = (jnp.sum(onehot * tile_start[None, :], axis=1) * ts + rank).astype(jnp.int32)
    n_used = tile_end[-1]
    tile_ids = jnp.arange(n_tiles, dtype=jnp.int32)
    tile_expert = jnp.sum((tile_ids[:, None] >= tile_end[None, :]).astype(jnp.int32), axis=1)
    last_expert = jnp.sum(((n_used - 1) >= tile_end).astype(jnp.int32))
    tile_expert = jnp.minimum(tile_expert, last_expert).astype(jnp.int32)
    plane_rows = slot.reshape(2, 1, t) + (jnp.arange(2, dtype=jnp.int32) * n_slots).reshape(1, 2, 1)
    return plane_rows.reshape(-1), tile_expert, n_used.reshape(1).astype(jnp.int32), n_slots


def _sc_mesh():
    return plsc.VectorSubcoreMesh(core_axis_name="core", subcore_axis_name="subcore")


def _scatter_rows(table, idx, n_out):
    m = idx.shape[0]
    n, d = table.shape
    assert m % SC_ROW_WINDOW == 0 and n % SC_ROW_WINDOW == 0
    n_src_blocks = n // SC_ROW_WINDOW

    @pl.kernel(out_type=jax.ShapeDtypeStruct((n_out, d), table.dtype), mesh=_sc_mesh(), scratch_types=[])
    def scatter_kernel(x_hbm, i_hbm, o_hbm):
        def body(x_vmem, i_vmem):
            pltpu.sync_copy(x_vmem, o_hbm.at[i_vmem.at[0]])

        pltpu.emit_pipeline(
            body,
            grid=(m // SC_ROW_WINDOW,),
            in_specs=[pl.BlockSpec((SC_ROW_WINDOW, d), index_map=lambda i: (i % n_src_blocks, 0)),
                      pl.BlockSpec((1, SC_ROW_WINDOW), index_map=lambda i: (0, i))],
            out_specs=[],
            core_axis_name=("core", "subcore"),
            dimension_semantics=(pltpu.PARALLEL,),
        )(x_hbm, i_hbm)

    return scatter_kernel(table, idx.reshape(1, m))


def _gather_rows(table, idx):
    m = idx.shape[0]
    d = table.shape[1]
    assert m % SC_ROW_WINDOW == 0

    @pl.kernel(out_type=jax.ShapeDtypeStruct((m, d), table.dtype), mesh=_sc_mesh())
    def gather_kernel(x_hbm, i_hbm, o_hbm):
        def body(i_vmem, o_vmem):
            pltpu.sync_copy(x_hbm.at[i_vmem.at[0]], o_vmem)

        pltpu.emit_pipeline(
            body,
            grid=(m // SC_ROW_WINDOW,),
            in_specs=[pl.BlockSpec((1, SC_ROW_WINDOW), index_map=lambda i: (0, i))],
            out_specs=[pl.BlockSpec((SC_ROW_WINDOW, d), index_map=lambda i: (i, 0))],
            core_axis_name=("core", "subcore"),
            dimension_semantics=(pltpu.PARALLEL,),
        )(i_hbm, o_hbm)

    return gather_kernel(table, idx.reshape(1, m))


def _expert_kernel(nk, te_ref, nu_ref, xs_ref, wg_ref, wu_ref, wd_ref, ys_ref, x_sc, acc_ref):
    i = pl.program_id(0)
    k = pl.program_id(1)
    used = i < nu_ref[0]

    def first():
        x_sc[...] = _unpack_bf16_pairs(_load_planes(xs_ref)).astype(BF16)

    def last(total):
        _store_planes(ys_ref, _pack_bf16_pairs(total))

    @pl.when(used)
    def _():
        _chunked_swiglu(k, nk, lambda: _swiglu_chunk(x_sc[...], wg_ref.at[0], wu_ref.at[0], wd_ref.at[0]), acc_ref,
                        first, last)

    @pl.when(jnp.logical_not(used) & (k == nk - 1))
    def _():
        ys_ref[...] = jnp.zeros_like(ys_ref)


def _experts(xs, tile_expert, n_used, w_gu, w_down, *, ts, tf):
    _, n_slots, quarter = xs.shape
    d = 4 * quarter
    dexp = w_down.shape[1]
    nk = dexp // tf
    w_gu = w_gu.astype(BF16)
    grid_spec = pltpu.PrefetchScalarGridSpec(
        num_scalar_prefetch=2,
        grid=(n_slots // ts, nk),
        in_specs=[
            pl.BlockSpec((2, ts, quarter), lambda i, k, te, nu: (0, i, 0)),
            pl.BlockSpec((1, d, tf), lambda i, k, te, nu: (te[i], 0, jnp.where(i < nu[0], k, 0))),
            pl.BlockSpec((1, d, tf), lambda i, k, te, nu: (te[i], 0, nk + jnp.where(i < nu[0], k, 0))),
            pl.BlockSpec((1, tf, d), lambda i, k, te, nu: (te[i], jnp.where(i < nu[0], k, 0), 0)),
        ],
        out_specs=pl.BlockSpec((2, ts, quarter), lambda i, k, te, nu: (0, i, 0)),
        scratch_shapes=[pltpu.VMEM((ts, d), BF16), pltpu.VMEM((ts, d), F32)],
    )
    return pl.pallas_call(
        functools.partial(_expert_kernel, nk),
        grid_spec=grid_spec,
        out_shape=jax.ShapeDtypeStruct((2, n_slots, quarter), jnp.int32),
        compiler_params=pltpu.CompilerParams(dimension_semantics=("arbitrary", "arbitrary")),
        name="moe_experts",
    )(tile_expert, n_used, xs, w_gu, w_gu, w_down.astype(BF16))


def _combine_kernel(x_ref, y_ref, rt_ref, gf_ref, o_ref):
    rt = rt_ref[...]
    moe = (rt[:, 2:3] * _unpack_bf16_pairs(_load_planes(y_ref, 0))
           + rt[:, 3:4] * _unpack_bf16_pairs(_load_planes(y_ref, 2)))
    o_ref[...] = _rms(x_ref[...] + moe, gf_ref[...])


def _combine(x2d, y4, route_t, g_final, *, tm):
    t, d = x2d.shape
    ne = route_t.shape[1]
    return pl.pallas_call(
        _combine_kernel,
        grid=(t // tm,),
        in_specs=[
            pl.BlockSpec((tm, d), lambda i: (i, 0)),
            pl.BlockSpec((4, tm, d // 4), lambda i: (0, i, 0)),
            pl.BlockSpec((tm, ne), lambda i: (i, 0)),
            pl.BlockSpec((1, d), lambda i: (0, 0)),
        ],
        out_specs=pl.BlockSpec((tm, d), lambda i: (i, 0)),
        out_shape=jax.ShapeDtypeStruct((t, d), F32),
        compiler_params=pltpu.CompilerParams(dimension_semantics=("parallel",)),
        name="moe_combine",
    )(x2d, y4, route_t, g_final.reshape(1, d))


def _attn_out_moe(x1, o2d, w_o, g, router, w_gu, w_down, g_final, *, tm, ts, tf):
    t, d = x1.shape
    q = d // 4
    x2d, hn_planes, route = _router(x1, o2d, w_o, g, router, tm=tm)
    plane_rows, tile_expert, n_used, n_slots = _routing_tables(route, ts)
    xs = _scatter_rows(hn_planes.reshape(2 * t, q), plane_rows, 2 * n_slots)
    ys = _experts(xs.reshape(2, n_slots, q), tile_expert, n_used, w_gu, w_down, ts=ts, tf=tf)
    y4 = _gather_rows(ys.reshape(2 * n_slots, q), plane_rows)
    return _combine(x2d, y4.reshape(4, t, q), route.T, g_final, tm=tm)


ROW_TILE = 512


def _tile(n):
    t = min(n, ROW_TILE)
    assert n % t == 0 and t % (2 * LANES) == 0
    return t


def kernel(x, norm_g, lru_w_in, lru_conv_w, lru_conv_b, lru_gate_w, lru_gate_b, lru_lambda, lru_w_out,
           kv_norm_g, w_kvf, b_f, fox_w_qg, fox_w_o, mlp_w_gu, mlp_w_down, moe_router, moe_w_gu,
           moe_w_down, final_norm_g):
    bsz, s, d = x.shape
    t = bsz * s
    tm = _tile(t)
    ts = _tile(s)

    x = _lru_mlp(x, norm_g[0, 0], lru_w_in[0], lru_conv_w[0], lru_conv_b[0], lru_gate_w[0], lru_gate_b[0],
                 lru_lambda[0], lru_w_out[0], norm_g[0, 1], mlp_w_gu[0], mlp_w_down[0], tm=ts)

    consts = _head_selectors()
    w_kvf_p, b_f_p, w_qg_p, w_o_p = _permute_heads(w_kvf, b_f, fox_w_qg[0], fox_w_o[0])
    ka, vT, c2r, kn = _kv_proj(x, kv_norm_g, w_kvf_p, b_f_p, consts, tk=ts)
    qa, gate, qn = _q_proj(x, norm_g[1, 0], w_qg_p, c2r, consts, tq=ts)
    jlo = _skip_table(qn, kn, c2r, ts)
    o = _attention(jlo, qa, ka, vT, gate, tq=ts)
    dexp = moe_w_down.shape[2]
    out = _attn_out_moe(x.reshape(t, d), o.reshape(t, -1), w_o_p, norm_g[1, 1], moe_router[0], moe_w_gu[0],
                        moe_w_down[0], final_norm_g, tm=tm, ts=tm, tf=dexp // 2)
    return out.reshape(bsz, s, d)
```

```python
import functools
import math

import jax
import jax.numpy as jnp
from jax import lax
from jax.experimental import pallas as pl
from jax.experimental.pallas import tpu as pltpu
from jax.experimental.pallas import tpu_sc as plsc

EPS = 1e-6
LRU_C = 8.0
N_LRU_BLOCKS = 8
CONV_W = 4
N_HEADS = 16
HEAD_DIM = 64
N_EXPERTS = 8
LANES = 128
SUBLANES = 8
HEAD_GROUP = 128
BIAS_Q = HEAD_DIM
BIAS_K = HEAD_DIM + 3
V_GROUP = 80
NEG_BIG = -1e30
LOG2E = math.log2(math.e)
SKIP_GAP_LOG2 = 152.0
NORM_SLACK = 1.01
SC_ROW_WINDOW = 128
Q_PER_STEP = 2

F32 = jnp.float32
BF16 = jnp.bfloat16
NT_DIMS = (((1,), (1,)), ((), ()))


def _rms(x, g):
    ms = jnp.mean(x * x, axis=-1, keepdims=True)
    return x * lax.rsqrt(ms + EPS) * g


def _log_sigmoid(x):
    return jnp.minimum(x, 0.0) - jnp.log1p(jnp.exp(-jnp.abs(x)))


def _gelu_tanh(x):
    c = 0.7978845608028654
    return 0.5 * x * (1.0 + jnp.tanh(c * (x + 0.044715 * (x * x * x))))


def _silu(x):
    return x * jax.nn.sigmoid(x)


def _split3(x):
    p0 = x.astype(BF16)
    r1 = x - p0.astype(F32)
    p1 = r1.astype(BF16)
    p2 = (r1 - p1.astype(F32)).astype(BF16)
    return p0, p1, p2


def _const_spec(shape, n_grid):
    zeros = (0,) * len(shape)
    if n_grid == 1:
        return pl.BlockSpec(shape, lambda i: zeros)
    if n_grid == 2:
        return pl.BlockSpec(shape, lambda i, j: zeros)
    return pl.BlockSpec(shape, lambda i, j, k: zeros)


def _lru_in_proj(x, g_ref, win_ref):
    d = x.shape[1]
    xn = _rms(x, g_ref[...]).astype(BF16)
    xg = jnp.dot(xn, win_ref[...], preferred_element_type=F32)
    return xg[:, :d], xg[:, d:]


def _lru_conv_gates(xb, cw_ref, cb_ref, gw_ref, xbuf):
    tt, d = xb.shape
    bw = d // N_LRU_BLOCKS
    xbuf[SUBLANES:SUBLANES + tt, :] = xb
    cw = cw_ref[...]
    xc = cb_ref[...] + cw[CONV_W - 1:CONV_W, :] * xb
    for j in range(CONV_W - 1):
        xc = xc + cw[j:j + 1, :] * xbuf[pl.ds(SUBLANES - (CONV_W - 1) + j, tt), :]
    xbuf[0:SUBLANES, :] = xbuf[tt:tt + SUBLANES, :]

    xcb = xc.astype(BF16)
    rl, il = [], []
    for n in range(N_LRU_BLOCKS):
        gl = jnp.dot(xcb[:, n * bw:(n + 1) * bw], gw_ref[n], preferred_element_type=F32)
        rl.append(gl[:, :bw])
        il.append(gl[:, bw:])
    return xc, jnp.concatenate(rl, axis=1), jnp.concatenate(il, axis=1)


def _lru_recurrence(xc, r_logit, i_logit, gbr, gb_ref, lam_ref, hcar):
    tt = xc.shape[0]
    gb = gb_ref[...]
    r = jax.nn.sigmoid(r_logit + gb[0:1, :])
    i = jax.nn.sigmoid(i_logit + gb[1:2, :])
    log_a = LRU_C * r * _log_sigmoid(lam_ref[...])
    a = jnp.exp(log_a)
    b = jnp.sqrt((1.0 - a) * (1.0 + a)) * (i * xc)

    row = lax.broadcasted_iota(jnp.int32, (tt, 1), 0)
    sh = 1
    while sh < tt:
        if sh < SUBLANES:
            keep = row >= sh
            a_sh = jnp.where(keep, pltpu.roll(a, sh, 0), 1.0)
            b_sh = jnp.where(keep, pltpu.roll(b, sh, 0), 0.0)
            b = a * b_sh + b
            a = a * a_sh
        else:
            b = jnp.concatenate([b[:sh], a[sh:] * b[:tt - sh] + b[sh:]], axis=0)
            a = jnp.concatenate([a[:sh], a[sh:] * a[:tt - sh]], axis=0)
        sh *= 2
    h = b + a * hcar[...]
    hcar[...] = h[tt - 1:tt, :]
    return (_gelu_tanh(gbr) * h).astype(BF16)


def _lru_mlp_kernel(x_ref, g0_ref, win_ref, cw_ref, cb_ref, gw_ref, gb_ref, lam_ref, wout_ref,
                    g1_ref, wg_ref, wu_ref, wd_ref, o_ref, xbuf, hcar, x1_sc, hn_sc, acc_ref):
    t = pl.program_id(1)
    k = pl.program_id(2)
    half = x_ref.shape[1]
    tf = wg_ref.shape[1]
    mid = (tf // (2 * LANES)) * LANES
    slot = t % 2
    prev = 1 - slot

    @pl.when((t == 0) & (k == 0))
    def _():
        xbuf[0:SUBLANES, :] = jnp.zeros((SUBLANES, xbuf.shape[1]), F32)
        hcar[...] = jnp.zeros_like(hcar)

    def gate_up(hn, lo, hi):
        return (jnp.dot(hn, wg_ref[:, lo:hi], preferred_element_type=F32),
                jnp.dot(hn, wu_ref[:, lo:hi], preferred_element_type=F32))

    def run(lru, mlp):
        y_mlp = None
        if lru:
            x = x_ref[0]
            xb, gbr = _lru_in_proj(x, g0_ref, win_ref)
        if mlp:
            hn = hn_sc[prev]
            g_a, u_a = gate_up(hn, 0, mid)
        if lru:
            xc, r_logit, i_logit = _lru_conv_gates(xb, cw_ref, cb_ref, gw_ref, xbuf)
        if mlp:
            g_b, u_b = gate_up(hn, mid, tf)
            h = jnp.concatenate([(_silu(g_a) * u_a).astype(BF16), (_silu(g_b) * u_b).astype(BF16)], axis=1)
            y_mlp = jnp.dot(h, wd_ref[...], preferred_element_type=F32)
        if lru:
            y_lru = _lru_recurrence(xc, r_logit, i_logit, gbr, gb_ref, lam_ref, hcar)
            x1 = x + jnp.dot(y_lru, wout_ref[...], preferred_element_type=F32)
            rows = pl.ds(pl.multiple_of(k * half, half), half)
            x1_sc[slot, rows, :] = x1
            hn_sc[slot, rows, :] = _rms(x1, g1_ref[...]).astype(BF16)
        return y_mlp

    last_t = pl.num_programs(1) - 1
    middle = (t > 0) & (t < last_t)

    @pl.when(t == 0)
    def _():
        run(lru=True, mlp=False)

    @pl.when(middle & (k == 0))
    def _():
        acc_ref[...] = run(lru=True, mlp=True)

    @pl.when(middle & (k == 1))
    def _():
        y = run(lru=True, mlp=True)
        o_ref[0] = x1_sc[prev] + acc_ref[...] + y

    @pl.when((t == last_t) & (k == 0))
    def _():
        acc_ref[...] = run(lru=False, mlp=True)

    @pl.when((t == last_t) & (k == 1))
    def _():
        y = run(lru=False, mlp=True)
        o_ref[0] = x1_sc[prev] + acc_ref[...] + y


def _lru_mlp(x, g0, w_in, conv_w, conv_b, gate_w, gate_b, lam, w_out, g1, w_gu, w_down, *, tm):
    bsz, s, d = x.shape
    bw = d // N_LRU_BLOCKS
    dff = w_down.shape[0]
    tf = dff // 2
    half = tm // 2
    nt = s // tm
    n_half = s // half
    gw = jnp.concatenate([gate_w[0], gate_w[1]], axis=-1).astype(BF16)
    w_gu = w_gu.astype(BF16)

    def const(*shape):
        return pl.BlockSpec(shape, lambda b, t, k: (0,) * len(shape), pipeline_mode=pl.Buffered(1))

    return pl.pallas_call(
        _lru_mlp_kernel,
        grid=(bsz, nt + 1, 2),
        in_specs=[
            pl.BlockSpec((1, half, d), lambda b, t, k: (b, jnp.minimum(2 * t + k, n_half - 1), 0)),
            const(1, d), const(d, 2 * d), const(CONV_W, d), const(1, d),
            const(N_LRU_BLOCKS, bw, 2 * bw), const(2, d), const(1, d), const(d, d),
            const(1, d),
            pl.BlockSpec((d, tf), lambda b, t, k: (0, k)),
            pl.BlockSpec((d, tf), lambda b, t, k: (0, 2 + k)),
            pl.BlockSpec((tf, d), lambda b, t, k: (k, 0)),
        ],
        out_specs=pl.BlockSpec((1, tm, d), lambda b, t, k: (b, jnp.maximum(t - 1, 0), 0)),
        out_shape=jax.ShapeDtypeStruct((bsz, s, d), F32),
        scratch_shapes=[pltpu.VMEM((half + 2 * SUBLANES, d), F32), pltpu.VMEM((1, d), F32),
                        pltpu.VMEM((2, tm, d), F32), pltpu.VMEM((2, tm, d), BF16), pltpu.VMEM((tm, d), F32)],
        compiler_params=pltpu.CompilerParams(dimension_semantics=("parallel", "arbitrary", "arbitrary")),
        name="lru_mlp",
    )(x, g0.reshape(1, d), w_in.astype(BF16), conv_w, conv_b.reshape(1, d), gw, gate_b, lam.reshape(1, d),
      w_out.astype(BF16), g1.reshape(1, d), w_gu, w_gu, w_down.astype(BF16))


def _swiglu_chunk(x, wg, wu, wd):
    gate = jnp.dot(x, wg[...], preferred_element_type=F32)
    up = jnp.dot(x, wu[...], preferred_element_type=F32)
    h = (_silu(gate) * up).astype(BF16)
    return jnp.dot(h, wd[...], preferred_element_type=F32)


def _chunked_swiglu(k, nk, chunk, acc_ref, first, last):
    if nk == 1:
        first()
        last(chunk())
        return

    @pl.when(k == 0)
    def _():
        first()
        acc_ref[...] = chunk()

    if nk > 2:
        @pl.when((k > 0) & (k < nk - 1))
        def _():
            acc_ref[...] += chunk()

    @pl.when(k == nk - 1)
    def _():
        last(acc_ref[...] + chunk())


def _head_selectors():
    hg = N_HEADS * HEAD_GROUP
    head = jnp.arange(hg) // HEAD_GROUP
    lane = jnp.arange(hg) % HEAD_GROUP
    col = jnp.arange(LANES)
    def sel(base):
        return ((col[:, None] % N_HEADS == head[None, :]) & (col[:, None] < 3 * N_HEADS)
                & (lane[None, :] == base + col[:, None] // N_HEADS)).astype(BF16)
    sel_k = -sel(BIAS_K)
    sel_q = sel(BIAS_Q).T
    ones_k = ((lane >= BIAS_Q) & (lane < BIAS_Q + 3)).astype(F32).reshape(1, hg)
    ones_q = ((lane >= BIAS_K) & (lane < BIAS_K + 3)).astype(F32).reshape(hg, 1)
    feat_head = jnp.arange(N_HEADS * HEAD_DIM) // HEAD_DIM
    head_sum = (col[None, :] == feat_head[:, None]).astype(BF16)
    return sel_k, sel_q, ones_k, ones_q, head_sum


def _kv_kernel(x_ref, g_ref, wk_ref, wvT_ref, wf_ref, wfT_ref, bfr_ref, bfc_ref, selk_ref, onesk_ref,
               hsum_ref, ka_ref, vT_ref, c2r_ref, kn_ref, car_r, car_c):
    t = pl.program_id(1)
    tk = x_ref.shape[1]

    @pl.when(t == 0)
    def _():
        car_r[...] = jnp.zeros_like(car_r)
        car_c[...] = jnp.zeros_like(car_c)

    xn = _rms(x_ref[0], g_ref[...]).astype(BF16)
    vr = lax.dot_general(wvT_ref[...], xn, NT_DIMS, preferred_element_type=F32)
    tail_row = lax.broadcasted_iota(jnp.int32, (V_GROUP - HEAD_DIM, tk), 0)
    tail = jnp.where(tail_row == 0, 1.0, 0.0)
    vT = jnp.concatenate([blk for h in range(N_HEADS) for blk in (vr[h * HEAD_DIM:(h + 1) * HEAD_DIM], tail)], axis=0)
    vT_ref[0, 0] = vT.astype(BF16)
    kr = jnp.dot(xn, wk_ref[...], preferred_element_type=F32)

    kss = jnp.dot((kr * kr).astype(BF16), hsum_ref[...], preferred_element_type=F32)
    kn = jnp.sqrt(jnp.max(kss, axis=0, keepdims=True))
    kn_ref[0, 0] = jnp.broadcast_to(kn, (SUBLANES, LANES))

    ri = lax.broadcasted_iota(jnp.int32, (tk, tk), 0)
    ci = lax.broadcasted_iota(jnp.int32, (tk, tk), 1)
    upper = (ri <= ci).astype(BF16)
    lower = (ci <= ri).astype(BF16)

    lf_c = _log_sigmoid(jnp.dot(xn, wf_ref[...], preferred_element_type=F32) + bfr_ref[...])
    cs3 = jnp.dot(lower, jnp.concatenate(_split3(lf_c), axis=1), preferred_element_type=F32)
    cs_c = car_c[...] + (cs3[:, :LANES] + cs3[:, LANES:2 * LANES] + cs3[:, 2 * LANES:])
    car_c[...] = cs_c[tk - 1:tk, :]
    p0, p1, p2 = (p.astype(F32) for p in _split3(cs_c * LOG2E))
    lane = lax.broadcasted_iota(jnp.int32, (tk, LANES), 1)
    groups = []
    for pair in range(N_HEADS // 2):
        two = kr[:, pair * LANES:(pair + 1) * LANES]
        groups.append(jnp.where(lane < HEAD_DIM, two, 0.0))
        groups.append(jnp.where(lane < HEAD_DIM, pltpu.roll(two, HEAD_DIM, 1), 0.0))
    kf = jnp.concatenate(groups, axis=1)
    packed = jnp.where(lane < N_HEADS, p0,
                       jnp.where(lane < 2 * N_HEADS, pltpu.roll(p1, N_HEADS, 1),
                                 jnp.where(lane < 3 * N_HEADS, pltpu.roll(p2, 2 * N_HEADS, 1), 0.0)))
    bias = jnp.dot(packed.astype(BF16), selk_ref[...], preferred_element_type=F32)
    ka_ref[0] = (kf + bias + onesk_ref[...]).astype(BF16)

    lf_r = _log_sigmoid(lax.dot_general(wfT_ref[...], xn, NT_DIMS, preferred_element_type=F32) + bfc_ref[...])
    cs3 = jnp.dot(jnp.concatenate(_split3(lf_r), axis=0), upper, preferred_element_type=F32)
    cs_r = car_r[...] + (cs3[:LANES] + cs3[LANES:2 * LANES] + cs3[2 * LANES:])
    car_r[...] = cs_r[:, tk - 1:tk]
    c2r_ref[0] = cs_r * LOG2E


def _kv_proj(x, g, w_kvf, b_f, consts, *, tk):
    bsz, s, d = x.shape
    da = N_HEADS * HEAD_DIM
    hg = N_HEADS * HEAD_GROUP
    dv = N_HEADS * V_GROUP
    nk = s // tk
    sel_k, _, ones_k, _, head_sum = consts
    wk = w_kvf[:, :da].astype(BF16)
    wvT = w_kvf[:, da:2 * da].T.astype(BF16)
    wf = jnp.pad(w_kvf[:, 2 * da:], ((0, 0), (0, LANES - N_HEADS))).astype(BF16)
    bfr = jnp.pad(b_f, (0, LANES - N_HEADS)).reshape(1, LANES)
    const = lambda *shape: _const_spec(shape, 2)
    return pl.pallas_call(
        _kv_kernel,
        grid=(bsz, nk),
        in_specs=[
            pl.BlockSpec((1, tk, d), lambda b, t: (b, t, 0)),
            const(1, d), const(d, da), const(da, d), const(d, LANES), const(LANES, d),
            const(1, LANES), const(LANES, 1), const(LANES, hg), const(1, hg), const(da, LANES),
        ],
        out_specs=[
            pl.BlockSpec((1, tk, hg), lambda b, t: (b, t, 0)),
            pl.BlockSpec((1, 1, dv, tk), lambda b, t: (b, t, 0, 0)),
            pl.BlockSpec((1, LANES, tk), lambda b, t: (b, 0, t)),
            pl.BlockSpec((1, 1, SUBLANES, LANES), lambda b, t: (b, t, 0, 0)),
        ],
        out_shape=[
            jax.ShapeDtypeStruct((bsz, s, hg), BF16),
            jax.ShapeDtypeStruct((bsz, nk, dv, tk), BF16),
            jax.ShapeDtypeStruct((bsz, LANES, s), F32),
            jax.ShapeDtypeStruct((bsz, nk, SUBLANES, LANES), F32),
        ],
        scratch_shapes=[pltpu.VMEM((LANES, 1), F32), pltpu.VMEM((1, LANES), F32)],
        compiler_params=pltpu.CompilerParams(dimension_semantics=("parallel", "arbitrary")),
        name="kv_proj",
    )(x, g.reshape(1, d), wk, wvT, wf, wf.T, bfr, bfr.reshape(LANES, 1), sel_k, ones_k, head_sum)


def _q_kernel(x_ref, g_ref, wqT_ref, wg_ref, c2r_ref, selq_ref, onesq_ref, hsumT_ref, qa_ref, gate_ref, qn_ref):
    xn = _rms(x_ref[0], g_ref[...]).astype(BF16)
    gate_ref[0] = jax.nn.sigmoid(jnp.dot(xn, wg_ref[...], preferred_element_type=F32)).astype(BF16)
    qr = lax.dot_general(wqT_ref[...], xn, NT_DIMS, preferred_element_type=F32) * (LOG2E * HEAD_DIM ** -0.5)

    qss = jnp.dot(hsumT_ref[...], (qr * qr).astype(BF16), preferred_element_type=F32)
    qn = jnp.sqrt(jnp.max(qss, axis=1, keepdims=True))
    qn_ref[0, 0] = jnp.broadcast_to(qn, (LANES, LANES))

    tq = c2r_ref.shape[2]
    gap = jnp.zeros((HEAD_GROUP - HEAD_DIM, tq), F32)
    qf = jnp.concatenate([blk for h in range(N_HEADS) for blk in (qr[h * HEAD_DIM:(h + 1) * HEAD_DIM], gap)], axis=0)
    packed = jnp.concatenate([p[:N_HEADS] for p in _split3(c2r_ref[0])]
                             + [jnp.zeros((LANES - 3 * N_HEADS, tq), BF16)], axis=0)
    bias = jnp.dot(selq_ref[...], packed, preferred_element_type=F32)
    qa_ref[0] = (qf + bias + onesq_ref[...]).astype(BF16)


def _q_proj(x, g, w_qg, c2r, consts, *, tq):
    bsz, s, d = x.shape
    da = N_HEADS * HEAD_DIM
    hg = N_HEADS * HEAD_GROUP
    nq = s // tq
    _, sel_q, _, ones_q, head_sum = consts
    wqT = w_qg[:, :da].T.astype(BF16)
    const = lambda *shape: _const_spec(shape, 2)
    return pl.pallas_call(
        _q_kernel,
        grid=(bsz, nq),
        in_specs=[
            pl.BlockSpec((1, tq, d), lambda b, t: (b, t, 0)),
            const(1, d), const(da, d), const(d, da),
            pl.BlockSpec((1, LANES, tq), lambda b, t: (b, 0, t)),
            const(hg, LANES), const(hg, 1), const(LANES, da),
        ],
        out_specs=[
            pl.BlockSpec((1, hg, tq), lambda b, t: (b, 0, t)),
            pl.BlockSpec((1, tq, da), lambda b, t: (b, t, 0)),
            pl.BlockSpec((1, 1, LANES, LANES), lambda b, t: (b, t, 0, 0)),
        ],
        out_shape=[
            jax.ShapeDtypeStruct((bsz, hg, s), BF16),
            jax.ShapeDtypeStruct((bsz, s, da), BF16),
            jax.ShapeDtypeStruct((bsz, nq, LANES, LANES), F32),
        ],
        compiler_params=pltpu.CompilerParams(dimension_semantics=("parallel", "parallel")),
        name="q_proj",
    )(x, g.reshape(1, d), wqT, w_qg[:, da:].astype(BF16), c2r, sel_q, ones_q, head_sum.T)


def _attn_kernel(*refs):
    for sub in range(Q_PER_STEP):
        _attn_q_block(sub, *refs)


def _attn_q_block(sub, jlo_ref, qa_ref, qan_ref, ka_ref, vT_ref, gate_ref, o_ref, ready, m0, m1, acc0, acc1,
                  sa0, sa1, sb0, sb1, sc0, sc1, pa0, pa1, pb0, pb1):
    b = pl.program_id(0)
    hp = pl.program_id(1)
    i = pl.program_id(2) * Q_PER_STEP + sub
    nq = pl.num_programs(2) * Q_PER_STEP
    tq = qa_ref.shape[2] // Q_PER_STEP
    tk = vT_ref.shape[3]
    rows = slice(sub * tq, (sub + 1) * tq)
    next_q, next_sub = (qa_ref, sub + 1) if sub + 1 < Q_PER_STEP else (qan_ref, 0)
    base = (b * pl.num_programs(1) + hp) * nq
    j_lo = jlo_ref[base + i]
    j_lo_next = jlo_ref[base + jnp.minimum(i + 1, nq - 1)]
    m_refs, acc_refs = (m0, m1), (acc0, acc1)
    s_a, s_b, s_c = (sa0, sa1), (sb0, sb1), (sc0, sc1)
    p_a, p_b = (pa0, pa1), (pb0, pb1)

    def scores(q_ref, q_sub, j, h):
        k_blk = ka_ref[0, pl.ds(pl.multiple_of(j * tk, tk), tk), h * HEAD_GROUP:(h + 1) * HEAD_GROUP]
        qT = q_ref[0, h * HEAD_GROUP:(h + 1) * HEAD_GROUP, q_sub * tq:(q_sub + 1) * tq]
        return jnp.dot(k_blk, qT, preferred_element_type=F32)

    def values(j, h, p):
        vT = vT_ref[0, j, h * V_GROUP:(h + 1) * V_GROUP, :]
        return jnp.dot(vT, p, preferred_element_type=F32)

    def softmax_tile(h, s, masked):
        if masked:
            kv_pos = lax.broadcasted_iota(jnp.int32, (tk, tq), 0)
            q_pos = lax.broadcasted_iota(jnp.int32, (tk, tq), 1)
            s = jnp.where(kv_pos <= q_pos, s, NEG_BIG)
        m_old = m_refs[h][...]
        m_new = jnp.maximum(m_old, jnp.max(s, axis=0, keepdims=True))
        m_refs[h][...] = m_new
        return jnp.exp2(m_old - m_new), jnp.exp2(s - m_new).astype(BF16)

    def step(s_src, s_dst, p_prev, p_dst, j):
        for h in range(2):
            s_dst[h][...] = scores(qa_ref, sub, j + 1, h)
        pv = None if p_prev is None else [values(j - 1, h, p_prev[h][...]) for h in range(2)]
        for h in range(2):
            alpha, p = softmax_tile(h, s_src[h][...], masked=False)
            p_dst[h][...] = p
            if pv is not None:
                acc_refs[h][...] = alpha * (acc_refs[h][...] + pv[h])

    def finish(s_src, p_prev, prefetch):
        if prefetch:
            for h in range(2):
                s_c[h][...] = scores(next_q, next_sub, j_lo_next, h)
            ready[0] = (i + 1 < nq).astype(jnp.int32)
        else:
            ready[0] = jnp.int32(0)
        pv = None if p_prev is None else [values(i - 1, h, p_prev[h][...]) for h in range(2)]
        outs = []
        for h in range(2):
            alpha, p = softmax_tile(h, s_src[h][...], masked=True)
            acc = values(i, h, p)
            if pv is not None:
                acc = acc + alpha * (acc_refs[h][...] + pv[h])
            outs.append(acc[0:HEAD_DIM, :] / acc[HEAD_DIM:HEAD_DIM + 1, :])
        oT = jnp.concatenate(outs, axis=0)
        o_ref[0, rows, :] = (oT.T * gate_ref[0, rows, :].astype(F32)).astype(BF16)

    @pl.when(i == 0)
    def _():
        ready[0] = jnp.int32(0)

    @pl.when(ready[0] == 0)
    def _():
        for h in range(2):
            s_c[h][...] = scores(qa_ref, sub, j_lo, h)

    for h in range(2):
        m_refs[h][...] = jnp.full((1, tq), NEG_BIG, F32)
        acc_refs[h][...] = jnp.zeros((V_GROUP, tq), F32)

    n_full = i - j_lo

    @pl.when(n_full == 0)
    def _():
        finish(s_c, None, prefetch=False)

    @pl.when(n_full > 0)
    def _():
        step(s_c, s_a, None, p_a, j_lo)
        rest = n_full - 1

        def body(n, carry):
            j = j_lo + 1 + 2 * n
            step(s_a, s_b, p_a, p_b, j)
            step(s_b, s_a, p_b, p_a, j + 1)
            return carry

        lax.fori_loop(0, rest // 2, body, 0)

        @pl.when(rest % 2 == 1)
        def _():
            step(s_a, s_b, p_a, p_b, i - 1)
            finish(s_b, p_b, prefetch=True)

        @pl.when(rest % 2 == 0)
        def _():
            finish(s_a, p_a, prefetch=True)


def _permute_heads(w_kvf, b_f, w_qg, w_o):
    da = N_HEADS * HEAD_DIM
    perm = jnp.argsort(b_f)

    def by_head(w, axis):
        shape = w.shape
        split = shape[:axis] + (N_HEADS, HEAD_DIM) + shape[axis + 1:]
        return jnp.take(w.reshape(split), perm, axis=axis).reshape(shape)

    w_kvf_p = jnp.concatenate([by_head(w_kvf[:, :da], 1), by_head(w_kvf[:, da:2 * da], 1), w_kvf[:, 2 * da:][:, perm]],
                              axis=1)
    w_qg_p = jnp.concatenate([by_head(w_qg[:, :da], 1), by_head(w_qg[:, da:], 1)], axis=1)
    return w_kvf_p, b_f[perm], w_qg_p, by_head(w_o, 0)


def _skip_table(qn, kn, c2r, tq):
    bsz, nq = qn.shape[0], qn.shape[1]
    qn = qn[:, :, :N_HEADS, 0].transpose(0, 2, 1) * NORM_SLACK
    kn = kn[:, :, 0, :N_HEADS].transpose(0, 2, 1) * NORM_SLACK
    c2 = c2r[:, :N_HEADS, :]
    c_first = c2[:, :, ::tq]
    c_last = c2[:, :, tq - 1::tq]
    upper = qn[..., :, None] * kn[..., None, :] + c_first[..., :, None] - c_last[..., None, :]
    lower = -(qn * kn)[..., :, None]
    skip = (upper - lower) < -SKIP_GAP_LOG2
    nk = skip.shape[-1]
    jlo = jnp.min(jnp.where(skip, nk, jnp.arange(nk, dtype=jnp.int32)), axis=-1)
    jlo = jnp.minimum(jlo, jnp.arange(nq, dtype=jnp.int32))
    jlo = jnp.min(jlo.reshape(bsz, N_HEADS // 2, 2, nq), axis=2)
    return jlo.reshape(-1).astype(jnp.int32)


def _attention(jlo, qa, ka, vT, gate, *, tq):
    bsz, hg, s = qa.shape
    nk, tk = vT.shape[1], vT.shape[3]
    da = N_HEADS * HEAD_DIM
    assert tq == tk
    pair = 2 * HEAD_GROUP
    hw = 2 * HEAD_DIM
    tqs = Q_PER_STEP * tq
    assert s % tqs == 0
    n_steps = s // tqs
    grid_spec = pltpu.PrefetchScalarGridSpec(
        num_scalar_prefetch=1,
        grid=(bsz, N_HEADS // 2, n_steps),
        in_specs=[
            pl.BlockSpec((1, pair, tqs), lambda b, hp, i, jlo: (b, hp, i)),
            pl.BlockSpec((1, pair, tqs), lambda b, hp, i, jlo: (b, hp, jnp.minimum(i + 1, n_steps - 1))),
            pl.BlockSpec((1, s, pair), lambda b, hp, i, jlo: (b, 0, hp)),
            pl.BlockSpec((1, nk, 2 * V_GROUP, tk), lambda b, hp, i, jlo: (b, 0, hp, 0)),
            pl.BlockSpec((1, tqs, hw), lambda b, hp, i, jlo: (b, i, hp)),
        ],
        out_specs=pl.BlockSpec((1, tqs, hw), lambda b, hp, i, jlo: (b, i, hp)),
        scratch_shapes=([pltpu.SMEM((1,), jnp.int32)]
                        + [pltpu.VMEM((1, tq), F32)] * 2 + [pltpu.VMEM((V_GROUP, tq), F32)] * 2
                        + [pltpu.VMEM((tk, tq), F32)] * 6 + [pltpu.VMEM((tk, tq), BF16)] * 4),
    )
    return pl.pallas_call(
        _attn_kernel,
        grid_spec=grid_spec,
        out_shape=jax.ShapeDtypeStruct((bsz, s, da), BF16),
        compiler_params=pltpu.CompilerParams(dimension_semantics=("parallel", "parallel", "arbitrary")),
        name="fox_attention",
    )(jlo, qa, qa, ka, vT, gate)


def _pack_bf16_pairs(x):
    half = x.shape[1] // 2
    lo = lax.bitcast_convert_type(x[:, :half].astype(BF16).astype(F32), jnp.uint32)
    hi = lax.bitcast_convert_type(x[:, half:].astype(BF16).astype(F32), jnp.uint32)
    word = (lo >> 16) | (hi & jnp.uint32(0xFFFF0000))
    return lax.bitcast_convert_type(word, jnp.int32)


def _unpack_bf16_pairs(w):
    u = lax.bitcast_convert_type(w, jnp.uint32)
    lo = lax.bitcast_convert_type(u << 16, F32)
    hi = lax.bitcast_convert_type(u & jnp.uint32(0xFFFF0000), F32)
    return jnp.concatenate([lo, hi], axis=1)


def _store_planes(ref, words):
    q = words.shape[1] // 2
    ref[0] = words[:, :q]
    ref[1] = words[:, q:]


def _load_planes(ref, base=0):
    return jnp.concatenate([ref[base], ref[base + 1]], axis=1)


def _router_kernel(x_ref, o_ref, wo_ref, g_ref, rtT_ref, x2_ref, hnp_ref, route_ref):
    tm = x_ref.shape[0]
    x2 = x_ref[...] + jnp.dot(o_ref[...], wo_ref[...], preferred_element_type=F32)
    x2_ref[...] = x2
    hn = _rms(x2, g_ref[...])
    _store_planes(hnp_ref, _pack_bf16_pairs(hn))
    h0, h1, _ = _split3(hn)
    r0, r1, _ = _split3(rtT_ref[...])
    lt = lax.dot_general(jnp.concatenate([r0, r1, r0], axis=1), jnp.concatenate([h0, h0, h1], axis=1), NT_DIMS,
                         preferred_element_type=F32)[:N_EXPERTS]
    row = lax.broadcasted_iota(jnp.int32, (N_EXPERTS, tm), 0)
    m1 = jnp.max(lt, axis=0, keepdims=True)
    i1 = jnp.min(jnp.where(lt == m1, row, N_EXPERTS), axis=0, keepdims=True)
    rest = jnp.where(row == i1, -jnp.inf, lt)
    m2 = jnp.max(rest, axis=0, keepdims=True)
    i2 = jnp.min(jnp.where(rest == m2, row, N_EXPERTS), axis=0, keepdims=True)
    e2 = jnp.exp(m2 - m1)
    w1 = 1.0 / (1.0 + e2)
    w2 = e2 / (1.0 + e2)
    route_ref[...] = jnp.where(row == 0, i1.astype(F32), jnp.where(row == 1, i2.astype(F32),
                               jnp.where(row == 2, w1, jnp.where(row == 3, w2, 0.0))))


def _router(x2d, o2d, w_o, g, router, *, tm):
    t, d = x2d.shape
    da = o2d.shape[1]
    ne = router.shape[1]
    rtT = jnp.pad(router.T, ((0, 2 * SUBLANES - ne), (0, 0)))
    return pl.pallas_call(
        _router_kernel,
        grid=(t // tm,),
        in_specs=[
            pl.BlockSpec((tm, d), lambda i: (i, 0)),
            pl.BlockSpec((tm, da), lambda i: (i, 0)),
            pl.BlockSpec((da, d), lambda i: (0, 0)),
            pl.BlockSpec((1, d), lambda i: (0, 0)),
            pl.BlockSpec((2 * SUBLANES, d), lambda i: (0, 0)),
        ],
        out_specs=[pl.BlockSpec((tm, d), lambda i: (i, 0)), pl.BlockSpec((2, tm, d // 4), lambda i: (0, i, 0)),
                   pl.BlockSpec((ne, tm), lambda i: (0, i))],
        out_shape=[jax.ShapeDtypeStruct((t, d), F32), jax.ShapeDtypeStruct((2, t, d // 4), jnp.int32),
                   jax.ShapeDtypeStruct((ne, t), F32)],
        compiler_params=pltpu.CompilerParams(dimension_semantics=("parallel",)),
        name="attn_out_moe_router",
    )(x2d, o2d, w_o.astype(BF16), g.reshape(1, d), rtT)


def _routing_tables(route, ts):
    ne = N_EXPERTS
    t = route.shape[1]
    e_pair = route[0:2].astype(jnp.int32).reshape(-1)
    n_tiles = (2 * t) // ts + ne
    n_slots = n_tiles * ts
    onehot = (e_pair[:, None] == jnp.arange(ne, dtype=jnp.int32)[None, :]).astype(jnp.int32)
    csum = jnp.cumsum(onehot, axis=0)
    rank = jnp.sum(onehot * csum, axis=1) - 1
    tiles = (csum[-1] + ts - 1) // ts
    tile_end = jnp.cumsum(tiles)
    tile_start = tile_end - tiles
    slot = (jnp.sum(onehot * tile_start[None, :], axis=1) * ts + rank).astype(jnp.int32)
    n_used = tile_end[-1]
    tile_ids = jnp.arange(n_tiles, dtype=jnp.int32)
    tile_expert = jnp.sum((tile_ids[:, None] >= tile_end[None, :]).astype(jnp.int32), axis=1)
    last_expert = jnp.sum(((n_used - 1) >= tile_end).astype(jnp.int32))
    tile_expert = jnp.minimum(tile_expert, last_expert).astype(jnp.int32)
    plane_rows = slot.reshape(2, 1, t) + (jnp.arange(2, dtype=jnp.int32) * n_slots).reshape(1, 2, 1)
    return plane_rows.reshape(-1), tile_expert, n_used.reshape(1).astype(jnp.int32), n_slots


def _sc_mesh():
    return plsc.VectorSubcoreMesh(core_axis_name="core", subcore_axis_name="subcore")


def _scatter_rows(table, idx, n_out):
    m = idx.shape[0]
    n, d = table.shape
    assert m % SC_ROW_WINDOW == 0 and n % SC_ROW_WINDOW == 0
    n_src_blocks = n // SC_ROW_WINDOW

    @pl.kernel(out_type=jax.ShapeDtypeStruct((n_out, d), table.dtype), mesh=_sc_mesh(), scratch_types=[])
    def scatter_kernel(x_hbm, i_hbm, o_hbm):
        def body(x_vmem, i_vmem):
            pltpu.sync_copy(x_vmem, o_hbm.at[i_vmem.at[0]])

        pltpu.emit_pipeline(
            body,
            grid=(m // SC_ROW_WINDOW,),
            in_specs=[pl.BlockSpec((SC_ROW_WINDOW, d), index_map=lambda i: (i % n_src_blocks, 0)),
                      pl.BlockSpec((1, SC_ROW_WINDOW), index_map=lambda i: (0, i))],
            out_specs=[],
            core_axis_name=("core", "subcore"),
            dimension_semantics=(pltpu.PARALLEL,),
        )(x_hbm, i_hbm)

    return scatter_kernel(table, idx.reshape(1, m))


def _gather_rows(table, idx):
    m = idx.shape[0]
    d = table.shape[1]
    assert m % SC_ROW_WINDOW == 0

    @pl.kernel(out_type=jax.ShapeDtypeStruct((m, d), table.dtype), mesh=_sc_mesh())
    def gather_kernel(x_hbm, i_hbm, o_hbm):
        def body(i_vmem, o_vmem):
            pltpu.sync_copy(x_hbm.at[i_vmem.at[0]], o_vmem)

        pltpu.emit_pipeline(
            body,
            grid=(m // SC_ROW_WINDOW,),
            in_specs=[pl.BlockSpec((1, SC_ROW_WINDOW), index_map=lambda i: (0, i))],
            out_specs=[pl.BlockSpec((SC_ROW_WINDOW, d), index_map=lambda i: (i, 0))],
            core_axis_name=("core", "subcore"),
            dimension_semantics=(pltpu.PARALLEL,),
        )(i_hbm, o_hbm)

    return gather_kernel(table, idx.reshape(1, m))


def _expert_kernel(nk, te_ref, nu_ref, xs_ref, wg_ref, wu_ref, wd_ref, ys_ref, x_sc, acc_ref):
    i = pl.program_id(0)
    k = pl.program_id(1)
    used = i < nu_ref[0]

    def first():
        x_sc[...] = _unpack_bf16_pairs(_load_planes(xs_ref)).astype(BF16)

    def last(total):
        _store_planes(ys_ref, _pack_bf16_pairs(total))

    @pl.when(used)
    def _():
        _chunked_swiglu(k, nk, lambda: _swiglu_chunk(x_sc[...], wg_ref.at[0], wu_ref.at[0], wd_ref.at[0]), acc_ref,
                        first, last)

    @pl.when(jnp.logical_not(used) & (k == nk - 1))
    def _():
        ys_ref[...] = jnp.zeros_like(ys_ref)


def _experts(xs, tile_expert, n_used, w_gu, w_down, *, ts, tf):
    _, n_slots, quarter = xs.shape
    d = 4 * quarter
    dexp = w_down.shape[1]
    nk = dexp // tf
    w_gu = w_gu.astype(BF16)
    grid_spec = pltpu.PrefetchScalarGridSpec(
        num_scalar_prefetch=2,
        grid=(n_slots // ts, nk),
        in_specs=[
            pl.BlockSpec((2, ts, quarter), lambda i, k, te, nu: (0, i, 0)),
            pl.BlockSpec((1, d, tf), lambda i, k, te, nu: (te[i], 0, jnp.where(i < nu[0], k, 0))),
            pl.BlockSpec((1, d, tf), lambda i, k, te, nu: (te[i], 0, nk + jnp.where(i < nu[0], k, 0))),
            pl.BlockSpec((1, tf, d), lambda i, k, te, nu: (te[i], jnp.where(i < nu[0], k, 0), 0)),
        ],
        out_specs=pl.BlockSpec((2, ts, quarter), lambda i, k, te, nu: (0, i, 0)),
        scratch_shapes=[pltpu.VMEM((ts, d), BF16), pltpu.VMEM((ts, d), F32)],
    )
    return pl.pallas_call(
        functools.partial(_expert_kernel, nk),
        grid_spec=grid_spec,
        out_shape=jax.ShapeDtypeStruct((2, n_slots, quarter), jnp.int32),
        compiler_params=pltpu.CompilerParams(dimension_semantics=("arbitrary", "arbitrary")),
        name="moe_experts",
    )(tile_expert, n_used, xs, w_gu, w_gu, w_down.astype(BF16))


def _combine_kernel(x_ref, y_ref, rt_ref, gf_ref, o_ref):
    rt = rt_ref[...]
    moe = (rt[:, 2:3] * _unpack_bf16_pairs(_load_planes(y_ref, 0))
           + rt[:, 3:4] * _unpack_bf16_pairs(_load_planes(y_ref, 2)))
    o_ref[...] = _rms(x_ref[...] + moe, gf_ref[...])


def _combine(x2d, y4, route_t, g_final, *, tm):
    t, d = x2d.shape
    ne = route_t.shape[1]
    return pl.pallas_call(
        _combine_kernel,
        grid=(t // tm,),
        in_specs=[
            pl.BlockSpec((tm, d), lambda i: (i, 0)),
            pl.BlockSpec((4, tm, d // 4), lambda i: (0, i, 0)),
            pl.BlockSpec((tm, ne), lambda i: (i, 0)),
            pl.BlockSpec((1, d), lambda i: (0, 0)),
        ],
        out_specs=pl.BlockSpec((tm, d), lambda i: (i, 0)),
        out_shape=jax.ShapeDtypeStruct((t, d), F32),
        compiler_params=pltpu.CompilerParams(dimension_semantics=("parallel",)),
        name="moe_combine",
    )(x2d, y4, route_t, g_final.reshape(1, d))


def _attn_out_moe(x1, o2d, w_o, g, router, w_gu, w_down, g_final, *, tm, ts, tf):
    t, d = x1.shape
    q = d // 4
    x2d, hn_planes, route = _router(x1, o2d, w_o, g, router, tm=tm)
    plane_rows, tile_expert, n_used, n_slots = _routing_tables(route, ts)
    xs = _scatter_rows(hn_planes.reshape(2 * t, q), plane_rows, 2 * n_slots)
    ys = _experts(xs.reshape(2, n_slots, q), tile_expert, n_used, w_gu, w_down, ts=ts, tf=tf)
    y4 = _gather_rows(ys.reshape(2 * n_slots, q), plane_rows)
    return _combine(x2d, y4.reshape(4, t, q), route.T, g_final, tm=tm)


ROW_TILE = 512


def _tile(n):
    t = min(n, ROW_TILE)
    assert n % t == 0 and t % (2 * LANES) == 0
    return t


def kernel(x, norm_g, lru_w_in, lru_conv_w, lru_conv_b, lru_gate_w, lru_gate_b, lru_lambda, lru_w_out,
           kv_norm_g, w_kvf, b_f, fox_w_qg, fox_w_o, mlp_w_gu, mlp_w_down, moe_router, moe_w_gu,
           moe_w_down, final_norm_g):
    bsz, s, d = x.shape
    t = bsz * s
    tm = _tile(t)
    ts = _tile(s)

    x = _lru_mlp(x, norm_g[0, 0], lru_w_in[0], lru_conv_w[0], lru_conv_b[0], lru_gate_w[0], lru_gate_b[0],
                 lru_lambda[0], lru_w_out[0], norm_g[0, 1], mlp_w_gu[0], mlp_w_down[0], tm=ts)

    consts = _head_selectors()
    w_kvf_p, b_f_p, w_qg_p, w_o_p = _permute_heads(w_kvf, b_f, fox_w_qg[0], fox_w_o[0])
    ka, vT, c2r, kn = _kv_proj(x, kv_norm_g, w_kvf_p, b_f_p, consts, tk=ts)
    qa, gate, qn = _q_proj(x, norm_g[1, 0], w_qg_p, c2r, consts, tq=ts)
    jlo = _skip_table(qn, kn, c2r, ts)
    o = _attention(jlo, qa, ka, vT, gate, tq=ts)
    dexp = moe_w_down.shape[2]
    out = _attn_out_moe(x.reshape(t, d), o.reshape(t, -1), w_o_p, norm_g[1, 1], moe_router[0], moe_w_gu[0],
                        moe_w_down[0], final_norm_g, tm=tm, ts=tm, tf=dexp // 2)
    return out.reshape(bsz, s, d)
```
